```python
import math
import jax
import jax.numpy as jnp
from jax import lax
import numpy as np

D_MODEL = 2048
BATCH = 4
SEQ = 4096
DEPTH = 2

CHUNK = 64
N_BRANCH = 4
BRANCH = 512
CONV_W = 3
GLA_HEADS = 4
GLA_DK = 64
GLA_DV = 128
GLA_KEY = GLA_HEADS * GLA_DK
GLA_LOW_RANK = 16
GLA_LOGIT_NORM = 16.0
DIFF_HEADS = 4
DIFF_D = 64
Q_BLOCK = 128
RWKV_HEADS = 8
RWKV_HD = 64
RWKV_DECAY_LORA = 96
RWKV_AAA_LORA = 96
RWKV_MV_LORA = 64
RWKV_GATE_LORA = 256
D_FF = -(-8 * D_MODEL // (3 * 256)) * 256
RMS_EPS = 1e-6
HEAD_EPS = 1e-5
RWKV_GN_EPS = 64e-5
NEG_INF = -1e30

CONV_SIZES = (BRANCH, BRANCH, BRANCH)
GLA_SIZES = (GLA_KEY, GLA_KEY, GLA_HEADS * GLA_DV, BRANCH, GLA_LOW_RANK)
DIFF_SIZES = (DIFF_HEADS * 2 * DIFF_D,) * 3
RWKV_SIZES = (BRANCH, BRANCH, BRANCH, RWKV_DECAY_LORA, RWKV_AAA_LORA, RWKV_GATE_LORA)
GATE_SIZE = N_BRANCH * D_MODEL
TOP_SIZES = (sum(CONV_SIZES), sum(GLA_SIZES), sum(DIFF_SIZES), sum(RWKV_SIZES), GATE_SIZE)
N_IN = sum(TOP_SIZES)

kernel_name = 'hybrid_chunk_causal_gated_merge'


def _offsets(sizes):
    return [int(s) for s in np.cumsum(sizes)[:-1]]


def rms_norm(x, gain, eps=RMS_EPS):
    xf = x.astype(jnp.float32)
    y = xf * lax.rsqrt(jnp.mean(xf * xf, axis=-1, keepdims=True) + eps)
    return (y * gain.astype(jnp.float32)).astype(x.dtype)


def head_rms(o, gain, eps=HEAD_EPS):
    return o * lax.rsqrt(jnp.mean(o * o, axis=-1, keepdims=True) + eps) * gain.astype(jnp.float32)


def alibi_slopes(n_heads):
    return 2.0 ** (-8.0 * jnp.arange(1, n_heads + 1, dtype=jnp.float32) / n_heads)


def short_conv_mixer(zc, conv_w):
    gate_b, gate_c, u = jnp.split(zc, _offsets(CONV_SIZES), axis=-1)
    y = lax.conv_general_dilated(gate_c * u, conv_w[:, None, :], window_strides=(1,),
                                 padding=[(CONV_W - 1, 0)],
                                 dimension_numbers=('NWC', 'WIO', 'NWC'),
                                 feature_group_count=BRANCH)
    return gate_b * y


def gla_mixer(zg, w_a2, b_a, head_gain):
    dt = zg.dtype
    bsz, seq = zg.shape[0], zg.shape[1]
    n_chunks = seq // CHUNK
    q, k, v, g, w_lr = jnp.split(zg.astype(jnp.float32), _offsets(GLA_SIZES), axis=-1)
    log_a = jax.nn.log_sigmoid(w_lr @ w_a2 + b_a) / GLA_LOGIT_NORM

    def chunks(t, d):
        return t.reshape(bsz, n_chunks, CHUNK, GLA_HEADS, d).transpose(1, 0, 3, 2, 4)

    qc = chunks(q * GLA_DK ** -0.5, GLA_DK)
    kc = chunks(k, GLA_DK)
    vc = chunks(v, GLA_DV)
    ac = chunks(log_a, GLA_DK)
    causal = jnp.tril(jnp.ones((CHUNK, CHUNK), dtype=bool))[:, :, None]

    def step(state, inp):
        qi, ki, vi, ai = inp
        cum = jnp.cumsum(ai, axis=2)
        rel = cum[:, :, :, None, :] - cum[:, :, None, :, :]
        decay = jnp.where(causal, jnp.exp(jnp.minimum(rel, 0.0)), 0.0)
        scores = jnp.einsum('bhtd,bhtsd,bhsd->bhts', qi, decay, ki)
        out = scores @ vi + (qi * jnp.exp(cum)) @ state
        last = cum[:, :, -1:, :]
        state = (jnp.exp(last[:, :, 0, :, None]) * state
                 + jnp.einsum('bhsd,bhse->bhde', ki * jnp.exp(last - cum), vi))
        return state, out

    state0 = jnp.zeros((bsz, GLA_HEADS, GLA_DK, GLA_DV), jnp.float32)
    _, o = lax.scan(step, state0, (qc, kc, vc, ac))
    o = o.transpose(1, 0, 3, 2, 4).reshape(bsz, seq, GLA_HEADS, GLA_DV)
    o = head_rms(o, head_gain).reshape(bsz, seq, BRANCH) * jax.nn.silu(g)
    return o.astype(dt)


def diff_attention_mixer(zd, lq1, lk1, lq2, lk2, head_gain, lambda_init):
    dt = zd.dtype
    bsz, seq = zd.shape[0], zd.shape[1]
    n_blocks = seq // Q_BLOCK
    f32 = jnp.float32
    q, k, v = jnp.split(zd.astype(f32), _offsets(DIFF_SIZES), axis=-1)
    q = q.reshape(bsz, seq, DIFF_HEADS, 2, DIFF_D) * DIFF_D ** -0.5
    k = k.reshape(bsz, seq, DIFF_HEADS, 2, DIFF_D)
    v = v.reshape(bsz, seq, DIFF_HEADS, 2 * DIFF_D)
    lam = (jnp.exp(jnp.sum(lq1.astype(f32) * lk1.astype(f32)))
           - jnp.exp(jnp.sum(lq2.astype(f32) * lk2.astype(f32))) + lambda_init)
    slopes = alibi_slopes(DIFF_HEADS)[:, None, None]
    k_pos = jnp.arange(seq)
    q_blocks = q.reshape(bsz, n_blocks, Q_BLOCK, DIFF_HEADS, 2, DIFF_D).transpose(1, 0, 2, 3, 4, 5)

    def block(args):
        qi, bi = args
        q_pos = bi * Q_BLOCK + jnp.arange(Q_BLOCK)
        s = jnp.einsum('bqhmd,bkhmd->bhmqk', qi, k)
        dist = jnp.abs(q_pos[:, None] - k_pos[None, :]).astype(f32)
        visible = (k_pos[None, :] // CHUNK) <= (q_pos[:, None] // CHUNK)
        bias = jnp.where(visible[None], -slopes * dist[None], NEG_INF)
        p = jax.nn.softmax(s + bias[None, :, None], axis=-1)
        attn = p[:, :, 0] - lam * p[:, :, 1]
        return jnp.einsum('bhqk,bkhe->bqhe', attn, v)

    o = lax.map(block, (q_blocks, jnp.arange(n_blocks)))
    o = o.transpose(1, 0, 2, 3, 4).reshape(bsz, seq, DIFF_HEADS, 2 * DIFF_D)
    o = head_rms(o, head_gain) * (1.0 - lambda_init)
    return o.reshape(bsz, seq, BRANCH).astype(dt)


def rwkv7_mixer(zr, mu, w0, w2, a0, a2, g2, k_k, k_a, r_k, ln_w, ln_b, v_first, vres):
    dt = zr.dtype
    bsz, seq = zr.shape[0], zr.shape[1]
    zr = zr.astype(jnp.float32)
    prev = jnp.pad(zr, ((0, 0), (1, 0), (0, 0)))[:, :-1]
    zm = zr + (prev - zr) * mu
    r, k, v, w_lr, a_lr, g_lr = jnp.split(zm, _offsets(RWKV_SIZES), axis=-1)
    log_w = -jax.nn.softplus(-(w0 + jnp.tanh(w_lr) @ w2)) - 0.5
    decay = jnp.exp(-jnp.exp(log_w))
    a = jax.nn.sigmoid(a0 + a_lr @ a2)
    g = jax.nn.sigmoid(g_lr) @ g2
    if vres is None:
        v_first = v
    else:
        v0, v1, v2 = vres
        v = v + (v_first - v) * jax.nn.sigmoid(v0 + (v @ v1) @ v2)

    def heads(t):
        return t.reshape(bsz, seq, RWKV_HEADS, RWKV_HD)

    kk = heads(k * k_k)
    kk = kk / jnp.maximum(jnp.sqrt(jnp.sum(kk * kk, axis=-1, keepdims=True)), 1e-12)
    k = k * (1.0 + (a - 1.0) * k_a)
    rh, kh, vh, wh, ah = heads(r), heads(k), heads(v), heads(decay), heads(a)

    def step(state, inp):
        r_t, w_t, k_t, v_t, kk_t, b_t = inp
        sa = jnp.einsum('bhvk,bhk->bhv', state, -kk_t)
        state = (state * w_t[:, :, None, :] + sa[..., None] * b_t[:, :, None, :]
                 + v_t[..., None] * k_t[:, :, None, :])
        return state, jnp.einsum('bhvk,bhk->bhv', state, r_t)

    xs = tuple(jnp.swapaxes(t, 0, 1) for t in (rh, wh, kh, vh, kk, kk * ah))
    state0 = jnp.zeros((bsz, RWKV_HEADS, RWKV_HD, RWKV_HD), jnp.float32)
    _, o = lax.scan(step, state0, xs)
    o = jnp.swapaxes(o, 0, 1)
    mean = jnp.mean(o, axis=-1, keepdims=True)
    var = jnp.mean(jnp.square(o - mean), axis=-1, keepdims=True)
    o = ((o - mean) * lax.rsqrt(var + RWKV_GN_EPS)).reshape(bsz, seq, BRANCH) * ln_w + ln_b
    bonus = jnp.sum(rh * kh * r_k.reshape(RWKV_HEADS, RWKV_HD), axis=-1, keepdims=True) * vh
    out = (o + bonus.reshape(bsz, seq, BRANCH)) * g
    return out.astype(dt), v_first


def setup_inputs(seed: int = 0) -> dict:
    key = jax.random.key(seed)
    ks = iter(jax.random.split(key, 48))
    f32 = jnp.float32
    L = DEPTH

    def nrm(shape, scale):
        return jax.random.normal(next(ks), shape, f32) * scale

    def gain(shape):
        return 1.0 + nrm(shape, 0.02)

    return {
        'x': nrm((BATCH, SEQ, D_MODEL), 1.0),
        'norm_mix_pre': gain((L, D_MODEL)),
        'w_in': nrm((L, D_MODEL, N_IN), D_MODEL ** -0.5),
        'conv_w': nrm((L, CONV_W, BRANCH), CONV_W ** -0.5),
        'gla_wa2': nrm((L, GLA_LOW_RANK, GLA_KEY), GLA_LOW_RANK ** -0.5),
        'gla_ba': nrm((L, GLA_KEY), 0.01),
        'gla_norm': gain((L, GLA_DV)),
        'diff_lq1': nrm((L, DIFF_D), 0.1),
        'diff_lk1': nrm((L, DIFF_D), 0.1),
        'diff_lq2': nrm((L, DIFF_D), 0.1),
        'diff_lk2': nrm((L, DIFF_D), 0.1),
        'diff_norm': gain((L, 2 * DIFF_D)),
        'rw_mu': jax.random.uniform(next(ks), (L, sum(RWKV_SIZES)), f32),
        'rw_w0': jax.random.uniform(next(ks), (L, BRANCH), f32, -6.0, -1.0),
        'rw_w2': nrm((L, RWKV_DECAY_LORA, BRANCH), 0.1 * RWKV_DECAY_LORA ** -0.5),
        'rw_a0': nrm((L, BRANCH), 0.1),
        'rw_a2': nrm((L, RWKV_AAA_LORA, BRANCH), 0.5 * RWKV_AAA_LORA ** -0.5),
        'rw_g2': nrm((L, RWKV_GATE_LORA, BRANCH), RWKV_GATE_LORA ** -0.5),
        'rw_kk': 0.85 + nrm((L, BRANCH), 0.02),
        'rw_ka': gain((L, BRANCH)),
        'rw_rk': nrm((L, BRANCH), 0.1),
        'rw_lnw': gain((L, BRANCH)),
        'rw_lnb': nrm((L, BRANCH), 0.01),
        'rw_v0': 1.0 + nrm((L - 1, BRANCH), 0.1),
        'rw_v1': nrm((L - 1, BRANCH, RWKV_MV_LORA), BRANCH ** -0.5),
        'rw_v2': nrm((L - 1, RWKV_MV_LORA, BRANCH), 0.5 * RWKV_MV_LORA ** -0.5),
        'w_branch': nrm((L, N_BRANCH, BRANCH, D_MODEL), BRANCH ** -0.5),
        'w_out': nrm((L, D_MODEL, D_MODEL), D_MODEL ** -0.5),
        'norm_mix_post': gain((L, D_MODEL)),
        'norm_ffn_pre': gain((L, D_MODEL)),
        'w_gate': nrm((L, D_MODEL, D_FF), D_MODEL ** -0.5),
        'w_up': nrm((L, D_MODEL, D_FF), D_MODEL ** -0.5),
        'w_down': nrm((L, D_FF, D_MODEL), D_FF ** -0.5),
        'norm_ffn_post': gain((L, D_MODEL)),
    }


def reference(x, norm_mix_pre, w_in, conv_w, gla_wa2, gla_ba, gla_norm, diff_lq1, diff_lk1,
              diff_lq2, diff_lk2, diff_norm, rw_mu, rw_w0, rw_w2, rw_a0, rw_a2, rw_g2, rw_kk,
              rw_ka, rw_rk, rw_lnw, rw_lnb, rw_v0, rw_v1, rw_v2, w_branch, w_out, norm_mix_post,
              norm_ffn_pre, w_gate, w_up, w_down, norm_ffn_post):
    bsz, seq = x.shape[0], x.shape[1]
    v_first = None
    for l in range(DEPTH):
        h = rms_norm(x, norm_mix_pre[l])
        z = h @ w_in[l]
        z_conv, z_gla, z_diff, z_rwkv, z_gate = jnp.split(z, _offsets(TOP_SIZES), axis=-1)
        lambda_init = 0.8 - 0.6 * math.exp(-0.3 * l)
        vres = None if l == 0 else (rw_v0[l - 1], rw_v1[l - 1], rw_v2[l - 1])
        o_conv = short_conv_mixer(z_conv, conv_w[l])
        o_gla = gla_mixer(z_gla, gla_wa2[l], gla_ba[l], gla_norm[l])
        o_diff = diff_attention_mixer(z_diff, diff_lq1[l], diff_lk1[l], diff_lq2[l], diff_lk2[l],
                                      diff_norm[l], lambda_init)
        o_rwkv, v_first = rwkv7_mixer(z_rwkv, rw_mu[l], rw_w0[l], rw_w2[l], rw_a0[l], rw_a2[l],
                                      rw_g2[l], rw_kk[l], rw_ka[l], rw_rk[l], rw_lnw[l], rw_lnb[l],
                                      v_first, vres)
        gates = jax.nn.sigmoid(z_gate.reshape(bsz, seq, N_BRANCH, D_MODEL))
        merged = gates[:, :, 0] * (o_conv @ w_branch[l, 0])
        for n, o_n in ((1, o_gla), (2, o_diff), (3, o_rwkv)):
            merged = merged + gates[:, :, n] * (o_n @ w_branch[l, n])
        x = x + rms_norm(merged @ w_out[l], norm_mix_post[l])
        h = rms_norm(x, norm_ffn_pre[l])
        ffn = (jax.nn.silu(h @ w_gate[l]) * (h @ w_up[l])) @ w_down[l]
        x = x + rms_norm(ffn, norm_ffn_post[l])
    return x
```

```python
import functools
import math

import jax
import jax.numpy as jnp
from jax import lax
from jax.experimental import pallas as pl
from jax.experimental.pallas import tpu as pltpu

F32 = jnp.float32
BF16 = jnp.bfloat16

D_MODEL = 2048
CHUNK = 64
N_BRANCH = 4
BRANCH = 512
GLA_HEADS = 4
GLA_DK = 64
GLA_DV = 128
GLA_KEY = GLA_HEADS * GLA_DK
GLA_LOW_RANK = 16
GLA_LOGIT_NORM = 16.0
DIFF_HEADS = 4
DIFF_D = 64
RWKV_HEADS = 8
RWKV_HD = 64
RWKV_DECAY_LORA = 96
RWKV_AAA_LORA = 96
RWKV_MV_LORA = 64
RWKV_GATE_LORA = 256
D_FF = 5632
RMS_EPS = 1e-6
HEAD_EPS = 1e-5
RWKV_GN_EPS = 64e-5
NEG_INF = -1e30

COL_CONV = 0
COL_GLA = 1536
COL_DIFF = 3072
COL_RWKV = 4608
COL_RW_G = 6144
COL_RW_W = 6400
COL_RW_A = 6528
GLA_WLR_LANE = RWKV_AAA_LORA
N_MIX = 6656
LANE = 128
HEAD_GROUP = 4
GROUP_W = HEAD_GROUP * 64

VMEM_LIMIT = 56 * 1024 * 1024


def _tile(n, pref):
    t = min(n, pref)
    while n % t:
        t -= 8
    return t


def _cparams(sem):
    return pltpu.CompilerParams(dimension_semantics=sem, vmem_limit_bytes=VMEM_LIMIT)


def _bdot(a, b):
    return jnp.dot(a.astype(BF16), b.astype(BF16), preferred_element_type=F32)


def _bdot_nt(a, b):
    return lax.dot_general(a.astype(BF16), b.astype(BF16), (((1,), (1,)), ((), ())),
                           preferred_element_type=F32)


def _bdot_tn(a, b):
    return lax.dot_general(a.astype(BF16), b.astype(BF16), (((0,), (0,)), ((), ())),
                           preferred_element_type=F32)


def _split_dot(x, ones_bf16):
    hi = x.astype(BF16)
    lo = (x - hi.astype(F32)).astype(BF16)
    return (jnp.dot(hi, ones_bf16, preferred_element_type=F32)
            + jnp.dot(lo, ones_bf16, preferred_element_type=F32))


def _split_dot_left(ones_bf16, x):
    hi = x.astype(BF16)
    lo = (x - hi.astype(F32)).astype(BF16)
    return (jnp.dot(ones_bf16, hi, preferred_element_type=F32)
            + jnp.dot(ones_bf16, lo, preferred_element_type=F32))


def _sigmoid(x):
    return 1.0 / (1.0 + jnp.exp(-x))


def _rms(x, gain, eps):
    return x * lax.rsqrt(jnp.mean(x * x, axis=-1, keepdims=True) + eps) * gain


def _iota2(shape, dim):
    return lax.broadcasted_iota(jnp.int32, shape, dim)


def _tril_incl(n):
    return (_iota2((n, n), 0) >= _iota2((n, n), 1)).astype(BF16)


def _head_ones(n, width):
    return ((_iota2((n, n), 0) // width) == (_iota2((n, n), 1) // width)).astype(BF16)


def _expand(x, col_group):
    rows = HEAD_GROUP * CHUNK
    xt = jnp.concatenate([x] * HEAD_GROUP, axis=0)
    keep = (_iota2((rows, x.shape[1]), 0) // CHUNK) == (_iota2((rows, x.shape[1]), 1) // col_group)
    return jnp.where(keep, xt, 0.0)


def _collapse(xe):
    out = xe[0:CHUNK]
    for h in range(1, HEAD_GROUP):
        out = out + xe[h * CHUNK:(h + 1) * CHUNK]
    return out


def _inproj_kernel(x_ref, g_ref, w_ref, z_ref, h_ref):
    @pl.when(pl.program_id(1) == 0)
    def _():
        h_ref[...] = _rms(x_ref[...], g_ref[...], RMS_EPS).astype(BF16)

    z_ref[...] = jnp.dot(h_ref[...], w_ref[...], preferred_element_type=F32)


def _inproj(x2, gain, w):
    t, n = x2.shape[0], w.shape[1]
    tm, tn = _tile(t, 1024), _tile(n, 512)
    return pl.pallas_call(
        _inproj_kernel,
        out_shape=(jax.ShapeDtypeStruct((t, n), F32), jax.ShapeDtypeStruct((t, D_MODEL), BF16)),
        grid=(t // tm, n // tn),
        in_specs=[pl.BlockSpec((tm, D_MODEL), lambda i, j: (i, 0)),
                  pl.BlockSpec((1, D_MODEL), lambda i, j: (0, 0)),
                  pl.BlockSpec((D_MODEL, tn), lambda i, j: (0, j))],
        out_specs=(pl.BlockSpec((tm, tn), lambda i, j: (i, j)),
                   pl.BlockSpec((tm, D_MODEL), lambda i, j: (i, 0))),
        compiler_params=_cparams(("parallel", "arbitrary")),
        name="inproj",
    )(x2, gain, w)


def _merge_kernel(h_ref, wg0, wg1, wg2, wg3, o0, o1, o2, o3, p_ref, out_ref):
    h = h_ref[...]
    acc = None
    for n, (wg, o) in enumerate(((wg0, o0), (wg1, o1), (wg2, o2), (wg3, o3))):
        gate = _sigmoid(jnp.dot(h, wg[...], preferred_element_type=F32))
        term = gate * jnp.dot(o[...], p_ref[n], preferred_element_type=F32)
        acc = term if acc is None else acc + term
    out_ref[...] = acc.astype(BF16)


def _merge(h, w_gate, outs, w_branch):
    t = h.shape[0]
    tm, tn = _tile(t, 1024), 512
    nj = D_MODEL // tn
    gate_specs = [pl.BlockSpec((D_MODEL, tn), functools.partial(lambda i, j, n: (0, n * nj + j), n=n))
                  for n in range(N_BRANCH)]
    o_specs = [pl.BlockSpec((tm, BRANCH), lambda i, j: (i, 0)) for _ in range(N_BRANCH)]
    return pl.pallas_call(
        _merge_kernel,
        out_shape=jax.ShapeDtypeStruct((t, D_MODEL), BF16),
        grid=(t // tm, nj),
        in_specs=[pl.BlockSpec((tm, D_MODEL), lambda i, j: (i, 0))] + gate_specs + o_specs
                 + [pl.BlockSpec((N_BRANCH, BRANCH, tn), lambda i, j: (0, 0, j))],
        out_specs=pl.BlockSpec((tm, tn), lambda i, j: (i, j)),
        compiler_params=_cparams(("parallel", "arbitrary")),
        name="merge",
    )(h, w_gate, w_gate, w_gate, w_gate, *outs, w_branch)


def _proj_norm_res_kernel(a_ref, w_ref, x_ref, g_ref, o_ref, acc_ref):
    k = pl.program_id(1)

    @pl.when(k == 0)
    def _():
        acc_ref[...] = jnp.zeros_like(acc_ref)

    acc_ref[...] += jnp.dot(a_ref[...], w_ref[...], preferred_element_type=F32)

    @pl.when(k == pl.num_programs(1) - 1)
    def _():
        o_ref[...] = x_ref[...] + _rms(acc_ref[...], g_ref[...], RMS_EPS)


def _proj_norm_res(a, w, x2, gain, tk):
    t, kdim = a.shape
    tm = _tile(t, 512)
    return pl.pallas_call(
        _proj_norm_res_kernel,
        out_shape=jax.ShapeDtypeStruct((t, D_MODEL), F32),
        grid=(t // tm, kdim // tk),
        in_specs=[pl.BlockSpec((tm, tk), lambda i, k: (i, k)),
                  pl.BlockSpec((tk, D_MODEL), lambda i, k: (k, 0)),
                  pl.BlockSpec((tm, D_MODEL), lambda i, k: (i, 0)),
                  pl.BlockSpec((1, D_MODEL), lambda i, k: (0, 0))],
        out_specs=pl.BlockSpec((tm, D_MODEL), lambda i, k: (i, 0)),
        scratch_shapes=[pltpu.VMEM((tm, D_MODEL), F32)],
        compiler_params=_cparams(("parallel", "arbitrary")),
        name="proj_norm_res",
    )(a, w, x2, gain)


def _ffn_up_kernel(x_ref, g_ref, wg_ref, wu_ref, a_ref, h_scr):
    @pl.when(pl.program_id(1) == 0)
    def _():
        h_scr[...] = _rms(x_ref[...], g_ref[...], RMS_EPS).astype(BF16)

    h = h_scr[...]
    gt = jnp.dot(h, wg_ref[...], preferred_element_type=F32)
    up = jnp.dot(h, wu_ref[...], preferred_element_type=F32)
    a_ref[...] = (gt * _sigmoid(gt) * up).astype(BF16)


def _ffn_up(x2, gain, wg, wu):
    t = x2.shape[0]
    tm, tn = _tile(t, 1024), 512
    return pl.pallas_call(
        _ffn_up_kernel,
        out_shape=jax.ShapeDtypeStruct((t, D_FF), BF16),
        grid=(t // tm, D_FF // tn),
        in_specs=[pl.BlockSpec((tm, D_MODEL), lambda i, j: (i, 0)),
                  pl.BlockSpec((1, D_MODEL), lambda i, j: (0, 0)),
                  pl.BlockSpec((D_MODEL, tn), lambda i, j: (0, j)),
                  pl.BlockSpec((D_MODEL, tn), lambda i, j: (0, j))],
        out_specs=pl.BlockSpec((tm, tn), lambda i, j: (i, j)),
        scratch_shapes=[pltpu.VMEM((tm, D_MODEL), BF16)],
        compiler_params=_cparams(("parallel", "arbitrary")),
        name="ffn_up",
    )(x2, gain, wg, wu)


def _conv_kernel(b_ref, c_ref, u_ref, w_ref, o_ref, carry):
    @pl.when(pl.program_id(1) == 0)
    def _():
        carry[...] = jnp.zeros_like(carry)

    cu = c_ref[0] * u_ref[0]
    ts = cu.shape[0]
    row = _iota2(cu.shape, 0)
    p1 = carry[7:8, :]
    p2 = carry[6:7, :]
    s1 = jnp.where(row == 0, p1, pltpu.roll(cu, 1, axis=0))
    s2 = jnp.where(row == 0, p2, jnp.where(row == 1, p1, pltpu.roll(cu, 2, axis=0)))
    w = w_ref[...]
    y = w[2:3, :] * cu + w[1:2, :] * s1 + w[0:1, :] * s2
    o_ref[0] = (b_ref[0] * y).astype(BF16)
    carry[...] = cu[ts - 8:, :]


def _conv_mixer(z3, conv_w):
    b, s = z3.shape[0], z3.shape[1]
    ts = _tile(s, 512)
    c0 = COL_CONV // BRANCH
    return pl.pallas_call(
        _conv_kernel,
        out_shape=jax.ShapeDtypeStruct((b, s, BRANCH), BF16),
        grid=(b, s // ts),
        in_specs=[pl.BlockSpec((1, ts, BRANCH), lambda i, j: (i, j, c0)),
                  pl.BlockSpec((1, ts, BRANCH), lambda i, j: (i, j, c0 + 1)),
                  pl.BlockSpec((1, ts, BRANCH), lambda i, j: (i, j, c0 + 2)),
                  pl.BlockSpec((3, BRANCH), lambda i, j: (0, 0))],
        out_specs=pl.BlockSpec((1, ts, BRANCH), lambda i, j: (i, j, 0)),
        scratch_shapes=[pltpu.VMEM((8, BRANCH), F32)],
        compiler_params=_cparams(("parallel", "arbitrary")),
        name="conv",
    )(z3, z3, z3, conv_w)


def _gla_kernel(q_ref, k_ref, v_ref, g_ref, wl_ref, wa2_ref, ba_ref, gn_ref, o_ref, state):
    @pl.when(pl.program_id(1) == 0)
    def _():
        state[...] = jnp.zeros_like(state)

    q = q_ref[0] * (GLA_DK ** -0.5)
    k = k_ref[0]
    v = v_ref[0]
    z = _bdot(wl_ref[0], wa2_ref[...]) + ba_ref[...]
    log_a = (jnp.minimum(z, 0.0) - jnp.log(1.0 + jnp.exp(-jnp.abs(z)))) * (1.0 / GLA_LOGIT_NORM)
    cum = _split_dot_left(_tril_incl(CHUNK), log_a)
    mid = cum[CHUNK // 2 - 1:CHUNK // 2, :]
    last = cum[CHUNK - 1:CHUNK, :]
    q_in = _expand(q * jnp.exp(cum - mid), GLA_DK)
    k_in = _expand(k * jnp.exp(mid - cum), GLA_DK)
    q_st = _expand(q * jnp.exp(cum), GLA_DK)
    k_st = _expand(k * jnp.exp(last - cum), GLA_DK)
    ve = _expand(v, GLA_DV)
    rows = HEAD_GROUP * CHUNK
    causal = (_iota2((rows, rows), 0) % CHUNK) >= (_iota2((rows, rows), 1) % CHUNK)
    scores = jnp.where(causal, _bdot_nt(q_in, k_in), 0.0)
    st = state[...]
    oe = _bdot(scores, ve) + _bdot_nt(q_st, st)
    state[...] = st * jnp.exp(last) + _bdot_tn(ve, k_st)
    o = _collapse(oe)
    g = g_ref[0]
    gate = g * _sigmoid(g)
    gn = gn_ref[...]
    outs = [_rms(o[:, h * GLA_DV:(h + 1) * GLA_DV], gn, HEAD_EPS) for h in range(GLA_HEADS)]
    o_ref[0] = (jnp.concatenate(outs, axis=1) * gate).astype(BF16)


def _gla_mixer(z3, wa2p, ba, gn):
    b, s = z3.shape[0], z3.shape[1]
    cq = COL_GLA // GLA_KEY
    cv = (COL_GLA + 2 * GLA_KEY) // BRANCH
    cw = COL_RW_A // LANE
    return pl.pallas_call(
        _gla_kernel,
        out_shape=jax.ShapeDtypeStruct((b, s, BRANCH), BF16),
        grid=(b, s // CHUNK),
        in_specs=[pl.BlockSpec((1, CHUNK, GLA_KEY), lambda i, j: (i, j, cq)),
                  pl.BlockSpec((1, CHUNK, GLA_KEY), lambda i, j: (i, j, cq + 1)),
                  pl.BlockSpec((1, CHUNK, BRANCH), lambda i, j: (i, j, cv)),
                  pl.BlockSpec((1, CHUNK, BRANCH), lambda i, j: (i, j, cv + 1)),
                  pl.BlockSpec((1, CHUNK, LANE), lambda i, j: (i, j, cw)),
                  pl.BlockSpec((LANE, GLA_KEY), lambda i, j: (0, 0)),
                  pl.BlockSpec((1, GLA_KEY), lambda i, j: (0, 0)),
                  pl.BlockSpec((1, GLA_DV), lambda i, j: (0, 0))],
        out_specs=pl.BlockSpec((1, CHUNK, BRANCH), lambda i, j: (i, j, 0)),
        scratch_shapes=[pltpu.VMEM((GLA_HEADS * GLA_DV, GLA_KEY), F32)],
        compiler_params=_cparams(("parallel", "arbitrary")),
        name="gla",
    )(z3, z3, z3, z3, z3, wa2p, ba, gn)


def _diff_kernel(q_ref, k_ref, v_ref, sl_ref, lq1, lk1, lq2, lk2, gn_ref, o_ref, m_scr, l_scr, acc_scr,
                 *, lambda_init, tq):
    qi = pl.program_id(2)
    slope = sl_ref[0, 0:1, 0:1]
    lane = _iota2((tq, 2 * DIFF_D), 1)
    q = q_ref[0] * (DIFF_D ** -0.5)
    q2 = jnp.concatenate([jnp.where(lane < DIFF_D, q, 0.0), jnp.where(lane < DIFF_D, 0.0, q)],
                         axis=0).astype(BF16)
    row = _iota2((2 * tq, tq), 0)
    q_pos = qi * tq + jnp.where(row >= tq, row - tq, row)
    col = _iota2((2 * tq, tq), 1)

    m_scr[...] = jnp.full_like(m_scr, NEG_INF)
    l_scr[...] = jnp.zeros_like(l_scr)
    acc_scr[...] = jnp.zeros_like(acc_scr)

    def step(j, diagonal):
        start = pl.multiple_of(j * tq, tq)
        kt = k_ref[0, pl.ds(start, tq), :].astype(BF16)
        vt = v_ref[0, pl.ds(start, tq), :].astype(BF16)
        s = lax.dot_general(q2, kt, (((1,), (1,)), ((), ())), preferred_element_type=F32)
        k_pos = j * tq + col
        if diagonal:
            visible = (k_pos // CHUNK) <= (q_pos // CHUNK)
            dist = jnp.abs(q_pos - k_pos).astype(F32)
            s = s + jnp.where(visible, -slope * dist, NEG_INF)
        else:
            s = s - slope * (q_pos - k_pos).astype(F32)
        m_old = m_scr[...]
        m_new = jnp.maximum(m_old, jnp.max(s, axis=-1, keepdims=True))
        alpha = jnp.exp(m_old - m_new)
        p = jnp.exp(s - m_new)
        l_scr[...] = alpha * l_scr[...] + jnp.sum(p, axis=-1, keepdims=True)
        acc_scr[...] = alpha * acc_scr[...] + jnp.dot(p.astype(BF16), vt, preferred_element_type=F32)
        m_scr[...] = m_new

    def body(j, carry):
        step(j, False)
        return carry

    lax.fori_loop(0, qi, body, 0)
    step(qi, True)

    f32 = F32
    lam = (jnp.exp(jnp.sum(lq1[...].astype(f32) * lk1[...].astype(f32), axis=-1, keepdims=True))
           - jnp.exp(jnp.sum(lq2[...].astype(f32) * lk2[...].astype(f32), axis=-1, keepdims=True))
           + lambda_init)
    on = acc_scr[...] / l_scr[...]
    o = on[:tq] - lam * on[tq:]
    o_ref[0] = (_rms(o, gn_ref[...], HEAD_EPS) * (1.0 - lambda_init)).astype(BF16)


def _diff_mixer(z3, slopes, lq1, lk1, lq2, lk2, gn, lambda_init):
    b, s = z3.shape[0], z3.shape[1]
    tq = _tile(s, 256)
    hw = 2 * DIFF_D
    cq = COL_DIFF // hw
    nh = DIFF_HEADS
    vec = pl.BlockSpec((1, DIFF_D), lambda i, h, j: (0, 0))
    return pl.pallas_call(
        functools.partial(_diff_kernel, lambda_init=lambda_init, tq=tq),
        out_shape=jax.ShapeDtypeStruct((b, s, BRANCH), BF16),
        grid=(b, nh, s // tq),
        in_specs=[pl.BlockSpec((1, tq, hw), lambda i, h, j: (i, j, cq + h)),
                  pl.BlockSpec((1, s, hw), lambda i, h, j: (i, 0, cq + nh + h)),
                  pl.BlockSpec((1, s, hw), lambda i, h, j: (i, 0, cq + 2 * nh + h)),
                  pl.BlockSpec((1, 8, LANE), lambda i, h, j: (h, 0, 0)),
                  vec, vec, vec, vec,
                  pl.BlockSpec((1, hw), lambda i, h, j: (0, 0))],
        out_specs=pl.BlockSpec((1, tq, hw), lambda i, h, j: (i, j, h)),
        scratch_shapes=[pltpu.VMEM((2 * tq, 1), F32), pltpu.VMEM((2 * tq, 1), F32),
                        pltpu.VMEM((2 * tq, hw), F32)],
        compiler_params=_cparams(("parallel", "parallel", "arbitrary")),
        name="diff_attn",
    )(z3, z3, z3, slopes, lq1, lk1, lq2, lk2, gn)


def _shift(x, prev_rows):
    row = _iota2(x.shape, 0)
    return jnp.where(row == 0, prev_rows[7:8, :], pltpu.roll(x, 1, axis=0))


def _rwkv_kernel(*refs, has_vres):
    if has_vres:
        (rkv_ref, gl_ref, wb_ref, ab_ref, vf_ref, mu_rkv, mu_g, mu_w, mu_a, w0, w2, a0, a2, g2, kk_s, ka_s,
         rk_s, lnw, lnb, v0, v1, v2, o_ref, p_rkv, p_g, p_w, p_a, state) = refs
    else:
        (rkv_ref, gl_ref, wb_ref, ab_ref, mu_rkv, mu_g, mu_w, mu_a, w0, w2, a0, a2, g2, kk_s, ka_s,
         rk_s, lnw, lnb, o_ref, vf_out, p_rkv, p_g, p_w, p_a, state) = refs

    @pl.when(pl.program_id(1) == 0)
    def _():
        for r in (p_rkv, p_g, p_w, p_a, state):
            r[...] = jnp.zeros_like(r)

    def mixed(x_ref, p_ref, mu_ref):
        x = x_ref[0]
        xm = x + (_shift(x, p_ref[...]) - x) * mu_ref[...]
        p_ref[...] = x[CHUNK - 8:, :]
        return xm

    rkv = mixed(rkv_ref, p_rkv, mu_rkv)
    g_lr = mixed(gl_ref, p_g, mu_g)
    w_lr = mixed(wb_ref, p_w, mu_w)
    a_lr = mixed(ab_ref, p_a, mu_a)
    r = rkv[:, 0:BRANCH]
    k = rkv[:, BRANCH:2 * BRANCH]
    v = rkv[:, 2 * BRANCH:3 * BRANCH]

    y = w0[...] + _bdot(jnp.tanh(w_lr), w2[...])
    softplus = jnp.maximum(-y, 0.0) + jnp.log(1.0 + jnp.exp(-jnp.abs(y)))
    lw = -jnp.exp(-softplus - 0.5)
    a = _sigmoid(a0[...] + _bdot(a_lr, a2[...]))
    gate = _bdot(_sigmoid(g_lr), g2[...])
    if has_vres:
        v = v + (vf_ref[0] - v) * _sigmoid(v0[...] + _bdot(_bdot(v, v1[...]), v2[...]))
    else:
        vf_out[0] = v

    ones = _head_ones(BRANCH, RWKV_HD)
    kk = k * kk_s[...]
    kk = kk / jnp.maximum(jnp.sqrt(_split_dot(kk * kk, ones)), 1e-12)
    k = k * (1.0 + (a - 1.0) * ka_s[...])
    bonus = _split_dot(r * k * rk_s[...], ones) * v

    cum = _split_dot_left(_tril_incl(CHUNK), lw)
    last = cum[CHUNK - 1:CHUNK, :]
    decay_in = jnp.exp(cum)
    a_t = -kk * jnp.exp(cum - lw)
    r_t = r * decay_in
    inv = jnp.exp(-cum)
    b_s = kk * a * inv
    k_s = k * inv
    to_end = jnp.exp(last - cum)
    b_e = kk * a * to_end
    k_e = k * to_end
    decay_all = jnp.exp(last)

    rows = HEAD_GROUP * CHUNK
    ri = _iota2((rows, rows), 0) % CHUNK
    ci = _iota2((rows, rows), 1) % CHUNK
    strict = ri > ci
    incl = ri >= ci
    eye = (_iota2((rows, rows), 0) == _iota2((rows, rows), 1)).astype(F32)

    outs = []
    for gi in range(RWKV_HEADS // HEAD_GROUP):
        sl = slice(gi * GROUP_W, (gi + 1) * GROUP_W)
        ae, re_, be, ke = (_expand(t[:, sl], RWKV_HD) for t in (a_t, r_t, b_s, k_s))
        bee, kee, ve = (_expand(t[:, sl], RWKV_HD) for t in (b_e, k_e, v))
        p = _bdot_nt(jnp.concatenate([ae, re_], axis=0), jnp.concatenate([be, ke], axis=0))
        n_ab = jnp.where(strict, p[:rows, :rows], 0.0)
        a_ak = jnp.where(strict, p[:rows, rows:], 0.0)
        a_rb = jnp.where(incl, p[rows:, :rows], 0.0)
        a_rk = jnp.where(incl, p[rows:, rows:], 0.0)
        t_inv = eye + n_ab
        n_pow = n_ab
        for _ in range(int(math.log2(CHUNK)) - 1):
            n_pow = _bdot(n_pow, n_pow)
            t_inv = t_inv + _bdot(t_inv, n_pow)
        av = _bdot(jnp.concatenate([a_ak, a_rk], axis=0), ve)
        y2 = _bdot(t_inv, jnp.concatenate([ae, av[:rows]], axis=1))
        st = state[gi]
        x2 = _bdot_nt(jnp.concatenate([y2[:, :GROUP_W], re_], axis=0), st)
        ue = x2[:rows] + y2[:, GROUP_W:]
        oe = x2[rows:] + _bdot(a_rb, ue) + av[rows:]
        state[gi] = (st * decay_all[:, sl]
                     + _bdot_tn(jnp.concatenate([ue, ve], axis=0), jnp.concatenate([bee, kee], axis=0)))
        outs.append(_collapse(oe))
    o = jnp.concatenate(outs, axis=1)

    mean = _split_dot(o, ones) * (1.0 / RWKV_HD)
    cen = o - mean
    var = _split_dot(cen * cen, ones) * (1.0 / RWKV_HD)
    o = cen * lax.rsqrt(var + RWKV_GN_EPS) * lnw[...] + lnb[...]
    o_ref[0] = ((o + bonus) * gate).astype(BF16)


def _rwkv_mixer(z3, v_first, p):
    b, s = z3.shape[0], z3.shape[1]
    has_vres = v_first is not None
    row = lambda w, c: pl.BlockSpec((1, CHUNK, w), lambda i, j: (i, j, c))
    full = lambda a: pl.BlockSpec(a.shape, lambda i, j: (0,) * a.ndim)
    ins = [z3, z3, z3, z3]
    specs = [row(3 * BRANCH, COL_RWKV // (3 * BRANCH)), row(RWKV_GATE_LORA, COL_RW_G // RWKV_GATE_LORA),
             row(LANE, COL_RW_W // LANE), row(LANE, COL_RW_A // LANE)]
    if has_vres:
        ins.append(v_first)
        specs.append(row(BRANCH, 0))
    names = ["mu_rkv", "mu_g", "mu_w", "mu_a", "w0", "w2", "a0", "a2", "g2", "kk", "ka", "rk", "lnw", "lnb"]
    if has_vres:
        names += ["v0", "v1", "v2"]
    for nme in names:
        ins.append(p[nme])
        specs.append(full(p[nme]))
    o_spec = pl.BlockSpec((1, CHUNK, BRANCH), lambda i, j: (i, j, 0))
    o_shape = jax.ShapeDtypeStruct((b, s, BRANCH), BF16)
    if has_vres:
        out_shape, out_specs = o_shape, o_spec
    else:
        out_shape = (o_shape, jax.ShapeDtypeStruct((b, s, BRANCH), F32))
        out_specs = (o_spec, pl.BlockSpec((1, CHUNK, BRANCH), lambda i, j: (i, j, 0)))
    res = pl.pallas_call(
        functools.partial(_rwkv_kernel, has_vres=has_vres),
        out_shape=out_shape,
        grid=(b, s // CHUNK),
        in_specs=specs,
        out_specs=out_specs,
        scratch_shapes=[pltpu.VMEM((8, 3 * BRANCH), F32), pltpu.VMEM((8, RWKV_GATE_LORA), F32),
                        pltpu.VMEM((8, LANE), F32), pltpu.VMEM((8, LANE), F32),
                        pltpu.VMEM((RWKV_HEADS // HEAD_GROUP, GROUP_W, GROUP_W), F32)],
        compiler_params=_cparams(("parallel", "arbitrary")),
        name="rwkv7",
    )(*ins)
    if has_vres:
        return res, v_first
    return res[0], res[1]


def _pad_rows(w, rows, offset=0):
    return jnp.zeros((rows, w.shape[1]), F32).at[offset:offset + w.shape[0]].set(w)


def _pack_w_in(w):
    off = 0
    conv = w[:, 0:1536]
    gla = w[:, 1536:3072]
    gla_lr = w[:, 3072:3088]
    diff = w[:, 3088:4624]
    rw_rkv = w[:, 4624:6160]
    rw_w = w[:, 6160:6256]
    rw_a = w[:, 6256:6352]
    rw_g = w[:, 6352:6608]
    z = lambda n: jnp.zeros((w.shape[0], n), w.dtype)
    del off
    packed = jnp.concatenate([conv, gla, diff, rw_rkv, rw_g, rw_w, z(32), rw_a, gla_lr, z(16)], axis=1)
    return packed.astype(BF16), w[:, 6608:].astype(BF16)


def _pad_lanes(v, width, offset=0):
    return jnp.zeros((1, width), F32).at[0, offset:offset + v.shape[0]].set(v)


def kernel(x, norm_mix_pre, w_in, conv_w, gla_wa2, gla_ba, gla_norm, diff_lq1, diff_lk1, diff_lq2, diff_lk2,
           diff_norm, rw_mu, rw_w0, rw_w2, rw_a0, rw_a2, rw_g2, rw_kk, rw_ka, rw_rk, rw_lnw, rw_lnb, rw_v0,
           rw_v1, rw_v2, w_branch, w_out, norm_mix_post, norm_ffn_pre, w_gate, w_up, w_down, norm_ffn_post):
    bsz, seq = x.shape[0], x.shape[1]
    depth = w_in.shape[0]
    t = bsz * seq
    x2 = x.reshape(t, D_MODEL)
    slopes = jnp.broadcast_to(
        (2.0 ** (-8.0 * jnp.arange(1, DIFF_HEADS + 1, dtype=F32) / DIFF_HEADS))[:, None, None],
        (DIFF_HEADS, 8, LANE))
    row = lambda v: v.reshape(1, -1)
    v_first = None
    for l in range(depth):
        w_mix, w_gates = _pack_w_in(w_in[l])
        z, h = _inproj(x2, row(norm_mix_pre[l]), w_mix)
        z3 = z.reshape(bsz, seq, N_MIX)
        lambda_init = 0.8 - 0.6 * math.exp(-0.3 * l)

        o_conv = _conv_mixer(z3, conv_w[l])
        o_gla = _gla_mixer(z3, _pad_rows(gla_wa2[l], LANE, GLA_WLR_LANE), row(gla_ba[l]), row(gla_norm[l]))
        o_diff = _diff_mixer(z3, slopes, row(diff_lq1[l]), row(diff_lk1[l]), row(diff_lq2[l]),
                             row(diff_lk2[l]), row(diff_norm[l]), lambda_init)
        mu = rw_mu[l]
        rp = {
            "mu_rkv": row(mu[0:1536]),
            "mu_w": _pad_lanes(mu[1536:1632], LANE),
            "mu_a": _pad_lanes(mu[1632:1728], LANE),
            "mu_g": row(mu[1728:1984]),
            "w0": row(rw_w0[l]), "w2": _pad_rows(rw_w2[l], LANE),
            "a0": row(rw_a0[l]), "a2": _pad_rows(rw_a2[l], LANE),
            "g2": rw_g2[l], "kk": row(rw_kk[l]), "ka": row(rw_ka[l]), "rk": row(rw_rk[l]),
            "lnw": row(rw_lnw[l]), "lnb": row(rw_lnb[l]),
        }
        if l > 0:
            rp.update(v0=row(rw_v0[l - 1]), v1=rw_v1[l - 1], v2=rw_v2[l - 1])
        o_rwkv, v_first = _rwkv_mixer(z3, v_first if l > 0 else None, rp)

        outs = [o.reshape(t, BRANCH) for o in (o_conv, o_gla, o_diff, o_rwkv)]
        merged = _merge(h, w_gates, outs, w_branch[l].astype(BF16))
        x2 = _proj_norm_res(merged, w_out[l].astype(BF16), x2, row(norm_mix_post[l]), D_MODEL)
        act = _ffn_up(x2, row(norm_ffn_pre[l]), w_gate[l].astype(BF16), w_up[l].astype(BF16))
        x2 = _proj_norm_res(act, w_down[l].astype(BF16), x2, row(norm_ffn_post[l]), D_FF // 4)
    return x2.reshape(bsz, seq, D_MODEL)
```

```python
import functools
import math

import jax
import jax.numpy as jnp
from jax import lax
from jax.experimental import pallas as pl
from jax.experimental.pallas import tpu as pltpu

F32 = jnp.float32
BF16 = jnp.bfloat16

D_MODEL = 2048
CHUNK = 64
N_BRANCH = 4
BRANCH = 512
GLA_HEADS = 4
GLA_DK = 64
GLA_DV = 128
GLA_KEY = GLA_HEADS * GLA_DK
GLA_LOW_RANK = 16
GLA_LOGIT_NORM = 16.0
DIFF_HEADS = 4
DIFF_D = 64
RWKV_HEADS = 8
RWKV_HD = 64
RWKV_DECAY_LORA = 96
RWKV_AAA_LORA = 96
RWKV_MV_LORA = 64
RWKV_GATE_LORA = 256
D_FF = 5632
RMS_EPS = 1e-6
HEAD_EPS = 1e-5
RWKV_GN_EPS = 64e-5
NEG_INF = -1e30
LOG2E = 1.4426950408889634

COL_CONV = 0
COL_GLA = 1536
COL_DIFF = 3072
COL_RWKV = 4608
COL_RW_G = 6144
COL_RW_W = 6400
COL_RW_A = 6528
GLA_WLR_LANE = RWKV_AAA_LORA
N_MIX = 6656
LANE = 128
HEAD_GROUP = 4
GROUP_W = HEAD_GROUP * 64

VMEM_LIMIT = 56 * 1024 * 1024


def _tile(n, pref):
    t = min(n, pref)
    while n % t:
        t -= 8
    return t


def _cparams(sem):
    return pltpu.CompilerParams(dimension_semantics=sem, vmem_limit_bytes=VMEM_LIMIT)


def _bdot(a, b):
    return jnp.dot(a.astype(BF16), b.astype(BF16), preferred_element_type=F32)


def _bdot_nt(a, b):
    return lax.dot_general(a.astype(BF16), b.astype(BF16), (((1,), (1,)), ((), ())),
                           preferred_element_type=F32)


def _bdot_tn(a, b):
    return lax.dot_general(a.astype(BF16), b.astype(BF16), (((0,), (0,)), ((), ())),
                           preferred_element_type=F32)


def _split_dot(x, ones_bf16):
    hi = x.astype(BF16)
    lo = (x - hi.astype(F32)).astype(BF16)
    return (jnp.dot(hi, ones_bf16, preferred_element_type=F32)
            + jnp.dot(lo, ones_bf16, preferred_element_type=F32))


def _split_dot_left(ones_bf16, x):
    hi = x.astype(BF16)
    lo = (x - hi.astype(F32)).astype(BF16)
    return (jnp.dot(ones_bf16, hi, preferred_element_type=F32)
            + jnp.dot(ones_bf16, lo, preferred_element_type=F32))


def _sigmoid(x):
    return 1.0 / (1.0 + jnp.exp(-x))


def _rms(x, gain, eps):
    return x * lax.rsqrt(jnp.mean(x * x, axis=-1, keepdims=True) + eps) * gain


def _iota2(shape, dim):
    return lax.broadcasted_iota(jnp.int32, shape, dim)


def _tril_incl(n):
    return (_iota2((n, n), 0) >= _iota2((n, n), 1)).astype(BF16)


def _head_ones(n, width):
    return ((_iota2((n, n), 0) // width) == (_iota2((n, n), 1) // width)).astype(BF16)


def _expand(x, col_group):
    rows = HEAD_GROUP * CHUNK
    xt = jnp.concatenate([x] * HEAD_GROUP, axis=0)
    keep = (_iota2((rows, x.shape[1]), 0) // CHUNK) == (_iota2((rows, x.shape[1]), 1) // col_group)
    return jnp.where(keep, xt, 0.0)


def _collapse(xe):
    out = xe[0:CHUNK]
    for h in range(1, HEAD_GROUP):
        out = out + xe[h * CHUNK:(h + 1) * CHUNK]
    return out


def _inproj_kernel(x_ref, g_ref, w_ref, z_ref, h_ref):
    @pl.when(pl.program_id(1) == 0)
    def _():
        h_ref[...] = _rms(x_ref[...], g_ref[...], RMS_EPS).astype(BF16)

    z_ref[...] = jnp.dot(h_ref[...], w_ref[...], preferred_element_type=F32)


def _inproj(x2, gain, w):
    t, n = x2.shape[0], w.shape[1]
    tm, tn = _tile(t, 1024), _tile(n, 512)
    return pl.pallas_call(
        _inproj_kernel,
        out_shape=(jax.ShapeDtypeStruct((t, n), F32), jax.ShapeDtypeStruct((t, D_MODEL), BF16)),
        grid=(t // tm, n // tn),
        in_specs=[pl.BlockSpec((tm, D_MODEL), lambda i, j: (i, 0)),
                  pl.BlockSpec((1, D_MODEL), lambda i, j: (0, 0)),
                  pl.BlockSpec((D_MODEL, tn), lambda i, j: (0, j))],
        out_specs=(pl.BlockSpec((tm, tn), lambda i, j: (i, j)),
                   pl.BlockSpec((tm, D_MODEL), lambda i, j: (i, 0))),
        compiler_params=_cparams(("parallel", "arbitrary")),
        name="inproj",
    )(x2, gain, w)


def _merge_kernel(h_ref, wg0, wg1, wg2, wg3, o0, o1, o2, o3, p_ref, out_ref):
    h = h_ref[...]
    acc = None
    for n, (wg, o) in enumerate(((wg0, o0), (wg1, o1), (wg2, o2), (wg3, o3))):
        gate = _sigmoid(jnp.dot(h, wg[...], preferred_element_type=F32))
        term = gate * jnp.dot(o[...], p_ref[n], preferred_element_type=F32)
        acc = term if acc is None else acc + term
    out_ref[...] = acc.astype(BF16)


def _merge(h, w_gate, outs, w_branch):
    t = h.shape[0]
    tm, tn = _tile(t, 1024), 512
    nj = D_MODEL // tn
    gate_specs = [pl.BlockSpec((D_MODEL, tn), functools.partial(lambda i, j, n: (0, n * nj + j), n=n))
                  for n in range(N_BRANCH)]
    o_specs = [pl.BlockSpec((tm, BRANCH), lambda i, j: (i, 0)) for _ in range(N_BRANCH)]
    return pl.pallas_call(
        _merge_kernel,
        out_shape=jax.ShapeDtypeStruct((t, D_MODEL), BF16),
        grid=(t // tm, nj),
        in_specs=[pl.BlockSpec((tm, D_MODEL), lambda i, j: (i, 0))] + gate_specs + o_specs
                 + [pl.BlockSpec((N_BRANCH, BRANCH, tn), lambda i, j: (0, 0, j))],
        out_specs=pl.BlockSpec((tm, tn), lambda i, j: (i, j)),
        compiler_params=_cparams(("parallel", "arbitrary")),
        name="merge",
    )(h, w_gate, w_gate, w_gate, w_gate, *outs, w_branch)


def _proj_norm_res_kernel(a_ref, w_ref, x_ref, g_ref, o_ref, acc_ref):
    k = pl.program_id(1)

    @pl.when(k == 0)
    def _():
        acc_ref[...] = jnp.zeros_like(acc_ref)

    acc_ref[...] += jnp.dot(a_ref[...], w_ref[...], preferred_element_type=F32)

    @pl.when(k == pl.num_programs(1) - 1)
    def _():
        o_ref[...] = x_ref[...] + _rms(acc_ref[...], g_ref[...], RMS_EPS)


def _proj_norm_res(a, w, x2, gain, tk):
    t, kdim = a.shape
    tm = _tile(t, 512)
    return pl.pallas_call(
        _proj_norm_res_kernel,
        out_shape=jax.ShapeDtypeStruct((t, D_MODEL), F32),
        grid=(t // tm, kdim // tk),
        in_specs=[pl.BlockSpec((tm, tk), lambda i, k: (i, k)),
                  pl.BlockSpec((tk, D_MODEL), lambda i, k: (k, 0)),
                  pl.BlockSpec((tm, D_MODEL), lambda i, k: (i, 0)),
                  pl.BlockSpec((1, D_MODEL), lambda i, k: (0, 0))],
        out_specs=pl.BlockSpec((tm, D_MODEL), lambda i, k: (i, 0)),
        scratch_shapes=[pltpu.VMEM((tm, D_MODEL), F32)],
        compiler_params=_cparams(("parallel", "arbitrary")),
        name="proj_norm_res",
    )(a, w, x2, gain)


def _ffn_up_kernel(x_ref, g_ref, wg_ref, wu_ref, a_ref, h_scr):
    @pl.when(pl.program_id(1) == 0)
    def _():
        h_scr[...] = _rms(x_ref[...], g_ref[...], RMS_EPS).astype(BF16)

    h = h_scr[...]
    gt = jnp.dot(h, wg_ref[...], preferred_element_type=F32)
    up = jnp.dot(h, wu_ref[...], preferred_element_type=F32)
    a_ref[...] = (gt * _sigmoid(gt) * up).astype(BF16)


def _ffn_up(x2, gain, wg, wu):
    t = x2.shape[0]
    tm, tn = _tile(t, 1024), 512
    return pl.pallas_call(
        _ffn_up_kernel,
        out_shape=jax.ShapeDtypeStruct((t, D_FF), BF16),
        grid=(t // tm, D_FF // tn),
        in_specs=[pl.BlockSpec((tm, D_MODEL), lambda i, j: (i, 0)),
                  pl.BlockSpec((1, D_MODEL), lambda i, j: (0, 0)),
                  pl.BlockSpec((D_MODEL, tn), lambda i, j: (0, j)),
                  pl.BlockSpec((D_MODEL, tn), lambda i, j: (0, j))],
        out_specs=pl.BlockSpec((tm, tn), lambda i, j: (i, j)),
        scratch_shapes=[pltpu.VMEM((tm, D_MODEL), BF16)],
        compiler_params=_cparams(("parallel", "arbitrary")),
        name="ffn_up",
    )(x2, gain, wg, wu)


def _conv_kernel(b_ref, c_ref, u_ref, w_ref, o_ref, carry):
    @pl.when(pl.program_id(1) == 0)
    def _():
        carry[...] = jnp.zeros_like(carry)

    cu = c_ref[0] * u_ref[0]
    ts = cu.shape[0]
    row = _iota2(cu.shape, 0)
    p1 = carry[7:8, :]
    p2 = carry[6:7, :]
    s1 = jnp.where(row == 0, p1, pltpu.roll(cu, 1, axis=0))
    s2 = jnp.where(row == 0, p2, jnp.where(row == 1, p1, pltpu.roll(cu, 2, axis=0)))
    w = w_ref[...]
    y = w[2:3, :] * cu + w[1:2, :] * s1 + w[0:1, :] * s2
    o_ref[0] = (b_ref[0] * y).astype(BF16)
    carry[...] = cu[ts - 8:, :]


def _conv_mixer(z3, conv_w):
    b, s = z3.shape[0], z3.shape[1]
    ts = _tile(s, 512)
    c0 = COL_CONV // BRANCH
    return pl.pallas_call(
        _conv_kernel,
        out_shape=jax.ShapeDtypeStruct((b, s, BRANCH), BF16),
        grid=(b, s // ts),
        in_specs=[pl.BlockSpec((1, ts, BRANCH), lambda i, j: (i, j, c0)),
                  pl.BlockSpec((1, ts, BRANCH), lambda i, j: (i, j, c0 + 1)),
                  pl.BlockSpec((1, ts, BRANCH), lambda i, j: (i, j, c0 + 2)),
                  pl.BlockSpec((3, BRANCH), lambda i, j: (0, 0))],
        out_specs=pl.BlockSpec((1, ts, BRANCH), lambda i, j: (i, j, 0)),
        scratch_shapes=[pltpu.VMEM((8, BRANCH), F32)],
        compiler_params=_cparams(("parallel", "arbitrary")),
        name="conv",
    )(z3, z3, z3, conv_w)


def _gla_kernel(q_ref, k_ref, v_ref, g_ref, wl_ref, wa2_ref, ba_ref, gn_ref, o_ref, state):
    @pl.when(pl.program_id(1) == 0)
    def _():
        state[...] = jnp.zeros_like(state)

    q = q_ref[0] * (GLA_DK ** -0.5)
    k = k_ref[0]
    v = v_ref[0]
    z = _bdot(wl_ref[0], wa2_ref[...]) + ba_ref[...]
    log_a = (jnp.minimum(z, 0.0) - jnp.log(1.0 + jnp.exp(-jnp.abs(z)))) * (1.0 / GLA_LOGIT_NORM)
    cum = _split_dot_left(_tril_incl(CHUNK), log_a)
    mid = cum[CHUNK // 2 - 1:CHUNK // 2, :]
    last = cum[CHUNK - 1:CHUNK, :]
    q_in = _expand(q * jnp.exp(cum - mid), GLA_DK)
    k_in = _expand(k * jnp.exp(mid - cum), GLA_DK)
    q_st = _expand(q * jnp.exp(cum), GLA_DK)
    k_st = _expand(k * jnp.exp(last - cum), GLA_DK)
    ve = _expand(v, GLA_DV)
    rows = HEAD_GROUP * CHUNK
    causal = (_iota2((rows, rows), 0) % CHUNK) >= (_iota2((rows, rows), 1) % CHUNK)
    scores = jnp.where(causal, _bdot_nt(q_in, k_in), 0.0)
    st = state[...]
    oe = _bdot(scores, ve) + _bdot_nt(q_st, st)
    state[...] = st * jnp.exp(last) + _bdot_tn(ve, k_st)
    o = _collapse(oe)
    g = g_ref[0]
    gate = g * _sigmoid(g)
    gn = gn_ref[...]
    outs = [_rms(o[:, h * GLA_DV:(h + 1) * GLA_DV], gn, HEAD_EPS) for h in range(GLA_HEADS)]
    o_ref[0] = (jnp.concatenate(outs, axis=1) * gate).astype(BF16)


def _gla_mixer(z3, wa2p, ba, gn):
    b, s = z3.shape[0], z3.shape[1]
    cq = COL_GLA // GLA_KEY
    cv = (COL_GLA + 2 * GLA_KEY) // BRANCH
    cw = COL_RW_A // LANE
    return pl.pallas_call(
        _gla_kernel,
        out_shape=jax.ShapeDtypeStruct((b, s, BRANCH), BF16),
        grid=(b, s // CHUNK),
        in_specs=[pl.BlockSpec((1, CHUNK, GLA_KEY), lambda i, j: (i, j, cq)),
                  pl.BlockSpec((1, CHUNK, GLA_KEY), lambda i, j: (i, j, cq + 1)),
                  pl.BlockSpec((1, CHUNK, BRANCH), lambda i, j: (i, j, cv)),
                  pl.BlockSpec((1, CHUNK, BRANCH), lambda i, j: (i, j, cv + 1)),
                  pl.BlockSpec((1, CHUNK, LANE), lambda i, j: (i, j, cw)),
                  pl.BlockSpec((LANE, GLA_KEY), lambda i, j: (0, 0)),
                  pl.BlockSpec((1, GLA_KEY), lambda i, j: (0, 0)),
                  pl.BlockSpec((1, GLA_DV), lambda i, j: (0, 0))],
        out_specs=pl.BlockSpec((1, CHUNK, BRANCH), lambda i, j: (i, j, 0)),
        scratch_shapes=[pltpu.VMEM((GLA_HEADS * GLA_DV, GLA_KEY), F32)],
        compiler_params=_cparams(("parallel", "arbitrary")),
        name="gla",
    )(z3, z3, z3, z3, z3, wa2p, ba, gn)


def _alibi_slope(h):
    return 2.0 ** (-8.0 * (h + 1) / DIFF_HEADS)


def _diff_kernel(q_ref, k_ref, v_ref, lq1, lk1, lq2, lk2, gn_ref, o_ref, ka_scr, vb_scr, qa_scr, acc_scr,
                 m_scr, l_scr, *, lambda_init, tq):
    qi = pl.program_id(1)
    hw = 2 * DIFF_D
    seq = k_ref.shape[1]
    lane = _iota2((tq, hw), 1)

    @pl.when(qi == 0)
    def _():
        key_row = _iota2((tq, hw), 0).astype(F32)
        for h in range(DIFF_HEADS):
            key_bias = (_alibi_slope(h) * LOG2E) * key_row
            hi = key_bias.astype(BF16).astype(F32)
            mid = (key_bias - hi).astype(BF16).astype(F32)
            lo = key_bias - hi - mid
            k_aug = jnp.where(lane == 0, hi, jnp.where(lane == 1, mid, jnp.where(lane == 2, lo, 0.0)))
            k_aug = k_aug.astype(BF16)
            for c in range(seq // tq):
                rows = slice(c * tq, (c + 1) * tq)
                ka_scr[h, rows, 0:hw] = k_ref[0, rows, h * hw:(h + 1) * hw].astype(BF16)
                ka_scr[h, rows, hw:2 * hw] = k_aug
        for c in range(seq // tq):
            rows = slice(c * tq, (c + 1) * tq)
            vb_scr[rows, :] = v_ref[0, rows, :].astype(BF16)

    lane2 = _iota2((2 * tq, hw), 1)
    ones_aug = (lane2 < 3).astype(F32)
    for h in range(DIFF_HEADS):
        q = q_ref[0, :, h * hw:(h + 1) * hw] * (DIFF_D ** -0.5 * LOG2E)
        q2 = jnp.concatenate([jnp.where(lane < DIFF_D, q, 0.0), jnp.where(lane < DIFF_D, 0.0, q)],
                             axis=0)
        qa_scr[h] = jnp.concatenate([q2, ones_aug], axis=1).astype(BF16)
    acc_scr[...] = jnp.zeros_like(acc_scr)
    m_scr[...] = jnp.full_like(m_scr, NEG_INF)
    l_scr[...] = jnp.zeros_like(l_scr)

    def update(scores, start, shifts):
        alphas, probs = [], []
        for h, s in enumerate(scores):
            m_old = m_scr[h]
            m_new = jnp.maximum(m_old, jnp.max(s, axis=0, keepdims=True))
            alpha = jnp.exp2(m_old - m_new)
            p = jnp.exp2(s - m_new)
            l_scr[h] = alpha * l_scr[h] + jnp.sum(p, axis=0, keepdims=True)
            m_scr[h] = m_new - shifts[h]
            alphas.append(alpha)
            probs.append(p.astype(BF16))
        pvs = [lax.dot_general(vb_scr[pl.ds(start, tq), h * hw:(h + 1) * hw], probs[h],
                               (((0,), (0,)), ((), ())), preferred_element_type=F32)
               for h in range(DIFF_HEADS)]
        for h in range(DIFF_HEADS):
            acc_scr[h] = alphas[h] * acc_scr[h] + pvs[h]

    def body(j, carry):
        start = pl.multiple_of(j * tq, tq)
        scores = [lax.dot_general(ka_scr[h, pl.ds(start, tq), :], qa_scr[h], (((1,), (1,)), ((), ())),
                                  preferred_element_type=F32) for h in range(DIFF_HEADS)]
        update(scores, start, [_alibi_slope(h) * LOG2E * tq for h in range(DIFF_HEADS)])
        return carry

    lax.fori_loop(0, qi, body, 0)

    start = pl.multiple_of(qi * tq, tq)
    k_row = _iota2((tq, 2 * tq), 0)
    col = _iota2((tq, 2 * tq), 1)
    q_row = jnp.where(col >= tq, col - tq, col)
    visible = (k_row // CHUNK) <= (q_row // CHUNK)
    rel = (q_row - jnp.abs(q_row - k_row)).astype(F32)
    lam = (jnp.exp(jnp.sum(lq1[...] * lk1[...], axis=-1, keepdims=True))
           - jnp.exp(jnp.sum(lq2[...] * lk2[...], axis=-1, keepdims=True))
           + lambda_init)
    scores = []
    for h in range(DIFF_HEADS):
        s = lax.dot_general(ka_scr[h, pl.ds(start, tq), 0:hw], qa_scr[h, :, 0:hw], (((1,), (1,)), ((), ())),
                            preferred_element_type=F32)
        scores.append(s + jnp.where(visible, (_alibi_slope(h) * LOG2E) * rel, NEG_INF))
    update(scores, start, [0.0] * DIFF_HEADS)
    for h in range(DIFF_HEADS):
        on = acc_scr[h] / l_scr[h]
        o = (on[:, :tq] - lam * on[:, tq:]).T
        o_ref[0, :, h * hw:(h + 1) * hw] = (_rms(o, gn_ref[...], HEAD_EPS) * (1.0 - lambda_init)).astype(BF16)


def _diff_mixer(z3, lq1, lk1, lq2, lk2, gn, lambda_init):
    b, s = z3.shape[0], z3.shape[1]
    tq = _tile(s, 256)
    hw = 2 * DIFF_D
    cq = COL_DIFF // BRANCH
    nh = DIFF_HEADS
    vec = pl.BlockSpec((1, DIFF_D), lambda i, j: (0, 0))
    return pl.pallas_call(
        functools.partial(_diff_kernel, lambda_init=lambda_init, tq=tq),
        out_shape=jax.ShapeDtypeStruct((b, s, BRANCH), BF16),
        grid=(b, s // tq),
        in_specs=[pl.BlockSpec((1, tq, BRANCH), lambda i, j: (i, j, cq)),
                  pl.BlockSpec((1, s, BRANCH), lambda i, j: (i, 0, cq + 1), pipeline_mode=pl.Buffered(1)),
                  pl.BlockSpec((1, s, BRANCH), lambda i, j: (i, 0, cq + 2), pipeline_mode=pl.Buffered(1)),
                  vec, vec, vec, vec,
                  pl.BlockSpec((1, hw), lambda i, j: (0, 0))],
        out_specs=pl.BlockSpec((1, tq, BRANCH), lambda i, j: (i, j, 0)),
        scratch_shapes=[pltpu.VMEM((nh, s, 2 * hw), BF16), pltpu.VMEM((s, BRANCH), BF16),
                        pltpu.VMEM((nh, 2 * tq, 2 * hw), BF16), pltpu.VMEM((nh, hw, 2 * tq), F32),
                        pltpu.VMEM((nh, 1, 2 * tq), F32), pltpu.VMEM((nh, 1, 2 * tq), F32)],
        compiler_params=_cparams(("parallel", "arbitrary")),
        name="diff_attn",
    )(z3, z3, z3, lq1, lk1, lq2, lk2, gn)


def _shift(x, prev_rows):
    row = _iota2(x.shape, 0)
    return jnp.where(row == 0, prev_rows[7:8, :], pltpu.roll(x, 1, axis=0))


def _rwkv_kernel(*refs, has_vres):
    if has_vres:
        (rkv_ref, gl_ref, wb_ref, ab_ref, vf_ref, mu_rkv, mu_g, mu_w, mu_a, w0, w2, a0, a2, g2, kk_s, ka_s,
         rk_s, lnw, lnb, v0, v1, v2, o_ref, p_rkv, p_g, p_w, p_a, state) = refs
    else:
        (rkv_ref, gl_ref, wb_ref, ab_ref, mu_rkv, mu_g, mu_w, mu_a, w0, w2, a0, a2, g2, kk_s, ka_s,
         rk_s, lnw, lnb, o_ref, vf_out, p_rkv, p_g, p_w, p_a, state) = refs

    @pl.when(pl.program_id(1) == 0)
    def _():
        for r in (p_rkv, p_g, p_w, p_a, state):
            r[...] = jnp.zeros_like(r)

    def mixed(x_ref, p_ref, mu_ref):
        x = x_ref[0]
        xm = x + (_shift(x, p_ref[...]) - x) * mu_ref[...]
        p_ref[...] = x[CHUNK - 8:, :]
        return xm

    rkv = mixed(rkv_ref, p_rkv, mu_rkv)
    g_lr = mixed(gl_ref, p_g, mu_g)
    w_lr = mixed(wb_ref, p_w, mu_w)
    a_lr = mixed(ab_ref, p_a, mu_a)
    r = rkv[:, 0:BRANCH]
    k = rkv[:, BRANCH:2 * BRANCH]
    v = rkv[:, 2 * BRANCH:3 * BRANCH]

    y = w0[...] + _bdot(jnp.tanh(w_lr), w2[...])
    softplus = jnp.maximum(-y, 0.0) + jnp.log(1.0 + jnp.exp(-jnp.abs(y)))
    lw = -jnp.exp(-softplus - 0.5)
    a = _sigmoid(a0[...] + _bdot(a_lr, a2[...]))
    gate = _bdot(_sigmoid(g_lr), g2[...])
    if has_vres:
        v = v + (vf_ref[0] - v) * _sigmoid(v0[...] + _bdot(_bdot(v, v1[...]), v2[...]))
    else:
        vf_out[0] = v

    ones = _head_ones(BRANCH, RWKV_HD)
    kk = k * kk_s[...]
    kk = kk / jnp.maximum(jnp.sqrt(_split_dot(kk * kk, ones)), 1e-12)
    k = k * (1.0 + (a - 1.0) * ka_s[...])
    bonus = _split_dot(r * k * rk_s[...], ones) * v

    cum = _split_dot_left(_tril_incl(CHUNK), lw)
    last = cum[CHUNK - 1:CHUNK, :]
    decay_in = jnp.exp(cum)
    a_t = -kk * jnp.exp(cum - lw)
    r_t = r * decay_in
    inv = jnp.exp(-cum)
    b_s = kk * a * inv
    k_s = k * inv
    to_end = jnp.exp(last - cum)
    b_e = kk * a * to_end
    k_e = k * to_end
    decay_all = jnp.exp(last)

    rows = HEAD_GROUP * CHUNK
    ri = _iota2((rows, rows), 0) % CHUNK
    ci = _iota2((rows, rows), 1) % CHUNK
    strict = ri > ci
    incl = ri >= ci
    eye = (_iota2((rows, rows), 0) == _iota2((rows, rows), 1)).astype(F32)

    outs = []
    for gi in range(RWKV_HEADS // HEAD_GROUP):
        sl = slice(gi * GROUP_W, (gi + 1) * GROUP_W)
        ae, re_, be, ke = (_expand(t[:, sl], RWKV_HD) for t in (a_t, r_t, b_s, k_s))
        bee, kee, ve = (_expand(t[:, sl], RWKV_HD) for t in (b_e, k_e, v))
        p = _bdot_nt(jnp.concatenate([ae, re_], axis=0), jnp.concatenate([be, ke], axis=0))
        n_ab = jnp.where(strict, p[:rows, :rows], 0.0)
        a_ak = jnp.where(strict, p[:rows, rows:], 0.0)
        a_rb = jnp.where(incl, p[rows:, :rows], 0.0)
        a_rk = jnp.where(incl, p[rows:, rows:], 0.0)
        t_inv = eye + n_ab
        n_pow = n_ab
        for _ in range(int(math.log2(CHUNK)) - 1):
            n_pow = _bdot(n_pow, n_pow)
            t_inv = t_inv + _bdot(t_inv, n_pow)
        av = _bdot(jnp.concatenate([a_ak, a_rk], axis=0), ve)
        y2 = _bdot(t_inv, jnp.concatenate([ae, av[:rows]], axis=1))
        st = state[gi]
        x2 = _bdot_nt(jnp.concatenate([y2[:, :GROUP_W], re_], axis=0), st)
        ue = x2[:rows] + y2[:, GROUP_W:]
        oe = x2[rows:] + _bdot(a_rb, ue) + av[rows:]
        state[gi] = (st * decay_all[:, sl]
                     + _bdot_tn(jnp.concatenate([ue, ve], axis=0), jnp.concatenate([bee, kee], axis=0)))
        outs.append(_collapse(oe))
    o = jnp.concatenate(outs, axis=1)

    mean = _split_dot(o, ones) * (1.0 / RWKV_HD)
    cen = o - mean
    var = _split_dot(cen * cen, ones) * (1.0 / RWKV_HD)
    o = cen * lax.rsqrt(var + RWKV_GN_EPS) * lnw[...] + lnb[...]
    o_ref[0] = ((o + bonus) * gate).astype(BF16)


def _rwkv_mixer(z3, v_first, p):
    b, s = z3.shape[0], z3.shape[1]
    has_vres = v_first is not None
    row = lambda w, c: pl.BlockSpec((1, CHUNK, w), lambda i, j: (i, j, c))
    full = lambda a: pl.BlockSpec(a.shape, lambda i, j: (0,) * a.ndim)
    ins = [z3, z3, z3, z3]
    specs = [row(3 * BRANCH, COL_RWKV // (3 * BRANCH)), row(RWKV_GATE_LORA, COL_RW_G // RWKV_GATE_LORA),
             row(LANE, COL_RW_W // LANE), row(LANE, COL_RW_A // LANE)]
    if has_vres:
        ins.append(v_first)
        specs.append(row(BRANCH, 0))
    names = ["mu_rkv", "mu_g", "mu_w", "mu_a", "w0", "w2", "a0", "a2", "g2", "kk", "ka", "rk", "lnw", "lnb"]
    if has_vres:
        names += ["v0", "v1", "v2"]
    for nme in names:
        ins.append(p[nme])
        specs.append(full(p[nme]))
    o_spec = pl.BlockSpec((1, CHUNK, BRANCH), lambda i, j: (i, j, 0))
    o_shape = jax.ShapeDtypeStruct((b, s, BRANCH), BF16)
    if has_vres:
        out_shape, out_specs = o_shape, o_spec
    else:
        out_shape = (o_shape, jax.ShapeDtypeStruct((b, s, BRANCH), F32))
        out_specs = (o_spec, pl.BlockSpec((1, CHUNK, BRANCH), lambda i, j: (i, j, 0)))
    res = pl.pallas_call(
        functools.partial(_rwkv_kernel, has_vres=has_vres),
        out_shape=out_shape,
        grid=(b, s // CHUNK),
        in_specs=specs,
        out_specs=out_specs,
        scratch_shapes=[pltpu.VMEM((8, 3 * BRANCH), F32), pltpu.VMEM((8, RWKV_GATE_LORA), F32),
                        pltpu.VMEM((8, LANE), F32), pltpu.VMEM((8, LANE), F32),
                        pltpu.VMEM((RWKV_HEADS // HEAD_GROUP, GROUP_W, GROUP_W), F32)],
        compiler_params=_cparams(("parallel", "arbitrary")),
        name="rwkv7",
    )(*ins)
    if has_vres:
        return res, v_first
    return res[0], res[1]


def _pad_rows(w, rows, offset=0):
    return jnp.zeros((rows, w.shape[1]), F32).at[offset:offset + w.shape[0]].set(w)


def _pack_w_in(w):
    off = 0
    conv = w[:, 0:1536]
    gla = w[:, 1536:3072]
    gla_lr = w[:, 3072:3088]
    diff = w[:, 3088:4624]
    rw_rkv = w[:, 4624:6160]
    rw_w = w[:, 6160:6256]
    rw_a = w[:, 6256:6352]
    rw_g = w[:, 6352:6608]
    z = lambda n: jnp.zeros((w.shape[0], n), w.dtype)
    del off
    packed = jnp.concatenate([conv, gla, diff, rw_rkv, rw_g, rw_w, z(32), rw_a, gla_lr, z(16)], axis=1)
    return packed.astype(BF16), w[:, 6608:].astype(BF16)


def _pad_lanes(v, width, offset=0):
    return jnp.zeros((1, width), F32).at[0, offset:offset + v.shape[0]].set(v)


def kernel(x, norm_mix_pre, w_in, conv_w, gla_wa2, gla_ba, gla_norm, diff_lq1, diff_lk1, diff_lq2, diff_lk2,
           diff_norm, rw_mu, rw_w0, rw_w2, rw_a0, rw_a2, rw_g2, rw_kk, rw_ka, rw_rk, rw_lnw, rw_lnb, rw_v0,
           rw_v1, rw_v2, w_branch, w_out, norm_mix_post, norm_ffn_pre, w_gate, w_up, w_down, norm_ffn_post):
    bsz, seq = x.shape[0], x.shape[1]
    depth = w_in.shape[0]
    t = bsz * seq
    x2 = x.reshape(t, D_MODEL)
    row = lambda v: v.reshape(1, -1)
    v_first = None
    for l in range(depth):
        w_mix, w_gates = _pack_w_in(w_in[l])
        z, h = _inproj(x2, row(norm_mix_pre[l]), w_mix)
        z3 = z.reshape(bsz, seq, N_MIX)
        lambda_init = 0.8 - 0.6 * math.exp(-0.3 * l)

        o_conv = _conv_mixer(z3, conv_w[l])
        o_gla = _gla_mixer(z3, _pad_rows(gla_wa2[l], LANE, GLA_WLR_LANE), row(gla_ba[l]), row(gla_norm[l]))
        o_diff = _diff_mixer(z3, row(diff_lq1[l]), row(diff_lk1[l]), row(diff_lq2[l]),
                             row(diff_lk2[l]), row(diff_norm[l]), lambda_init)
        mu = rw_mu[l]
        rp = {
            "mu_rkv": row(mu[0:1536]),
            "mu_w": _pad_lanes(mu[1536:1632], LANE),
            "mu_a": _pad_lanes(mu[1632:1728], LANE),
            "mu_g": row(mu[1728:1984]),
            "w0": row(rw_w0[l]), "w2": _pad_rows(rw_w2[l], LANE),
            "a0": row(rw_a0[l]), "a2": _pad_rows(rw_a2[l], LANE),
            "g2": rw_g2[l], "kk": row(rw_kk[l]), "ka": row(rw_ka[l]), "rk": row(rw_rk[l]),
            "lnw": row(rw_lnw[l]), "lnb": row(rw_lnb[l]),
        }
        if l > 0:
            rp.update(v0=row(rw_v0[l - 1]), v1=rw_v1[l - 1], v2=rw_v2[l - 1])
        o_rwkv, v_first = _rwkv_mixer(z3, v_first if l > 0 else None, rp)

        outs = [o.reshape(t, BRANCH) for o in (o_conv, o_gla, o_diff, o_rwkv)]
        merged = _merge(h, w_gates, outs, w_branch[l].astype(BF16))
        x2 = _proj_norm_res(merged, w_out[l].astype(BF16), x2, row(norm_mix_post[l]), D_MODEL)
        act = _ffn_up(x2, row(norm_ffn_pre[l]), w_gate[l].astype(BF16), w_up[l].astype(BF16))
        x2 = _proj_norm_res(act, w_down[l].astype(BF16), x2, row(norm_ffn_post[l]), D_FF // 4)
    return x2.reshape(bsz, seq, D_MODEL)
```

```python
import functools
import math

import jax
import jax.numpy as jnp
from jax import lax
from jax.experimental import pallas as pl
from jax.experimental.pallas import tpu as pltpu

F32 = jnp.float32
BF16 = jnp.bfloat16

D_MODEL = 2048
CHUNK = 64
N_BRANCH = 4
BRANCH = 512
GLA_HEADS = 4
GLA_DK = 64
GLA_DV = 128
GLA_KEY = GLA_HEADS * GLA_DK
GLA_LOW_RANK = 16
GLA_LOGIT_NORM = 16.0
DIFF_HEADS = 4
DIFF_D = 64
RWKV_HEADS = 8
RWKV_HD = 64
RWKV_DECAY_LORA = 96
RWKV_AAA_LORA = 96
RWKV_MV_LORA = 64
RWKV_GATE_LORA = 256
D_FF = 5632
RMS_EPS = 1e-6
HEAD_EPS = 1e-5
RWKV_GN_EPS = 64e-5
NEG_INF = -1e30
LOG2E = 1.4426950408889634

COL_CONV = 0
COL_GLA = 1536
COL_DIFF = 3072
COL_RWKV = 4608
COL_RW_G = 6144
COL_RW_W = 6400
COL_RW_A = 6528
GLA_WLR_LANE = RWKV_AAA_LORA
N_MIX = 6656
LANE = 128
HEAD_GROUP = 4
GROUP_W = HEAD_GROUP * 64
RWKV_SEQS_PER_STEP = 4

VMEM_LIMIT = 56 * 1024 * 1024


def _tile(n, pref):
    t = min(n, pref)
    while n % t:
        t -= 8
    return t


def _cparams(sem):
    return pltpu.CompilerParams(dimension_semantics=sem, vmem_limit_bytes=VMEM_LIMIT)


def _bdot(a, b):
    return jnp.dot(a.astype(BF16), b.astype(BF16), preferred_element_type=F32)


def _bdot_nt(a, b):
    return lax.dot_general(a.astype(BF16), b.astype(BF16), (((1,), (1,)), ((), ())),
                           preferred_element_type=F32)


def _bdot_tn(a, b):
    return lax.dot_general(a.astype(BF16), b.astype(BF16), (((0,), (0,)), ((), ())),
                           preferred_element_type=F32)


def _split_dot(x, ones_bf16):
    hi = x.astype(BF16)
    lo = (x - hi.astype(F32)).astype(BF16)
    return (jnp.dot(hi, ones_bf16, preferred_element_type=F32)
            + jnp.dot(lo, ones_bf16, preferred_element_type=F32))


def _split_dot_left(ones_bf16, x):
    hi = x.astype(BF16)
    lo = (x - hi.astype(F32)).astype(BF16)
    return (jnp.dot(ones_bf16, hi, preferred_element_type=F32)
            + jnp.dot(ones_bf16, lo, preferred_element_type=F32))


def _sigmoid(x):
    return 1.0 / (1.0 + jnp.exp(-x))


def _rms(x, gain, eps):
    return x * lax.rsqrt(jnp.mean(x * x, axis=-1, keepdims=True) + eps) * gain


def _iota2(shape, dim):
    return lax.broadcasted_iota(jnp.int32, shape, dim)


def _tril_incl(n):
    return (_iota2((n, n), 0) >= _iota2((n, n), 1)).astype(BF16)


def _head_ones(n, width):
    return ((_iota2((n, n), 0) // width) == (_iota2((n, n), 1) // width)).astype(BF16)


def _expand(x, col_group):
    rows = HEAD_GROUP * CHUNK
    xt = jnp.concatenate([x] * HEAD_GROUP, axis=0)
    keep = (_iota2((rows, x.shape[1]), 0) // CHUNK) == (_iota2((rows, x.shape[1]), 1) // col_group)
    return jnp.where(keep, xt, 0.0)


def _collapse(xe):
    out = xe[0:CHUNK]
    for h in range(1, HEAD_GROUP):
        out = out + xe[h * CHUNK:(h + 1) * CHUNK]
    return out


def _inproj_kernel(x_ref, g_ref, w_ref, z_ref, h_ref):
    @pl.when(pl.program_id(1) == 0)
    def _():
        h_ref[...] = _rms(x_ref[...], g_ref[...], RMS_EPS).astype(BF16)

    z_ref[...] = jnp.dot(h_ref[...], w_ref[...], preferred_element_type=F32)


def _inproj(x2, gain, w):
    t, n = x2.shape[0], w.shape[1]
    tm, tn = _tile(t, 1024), _tile(n, 512)
    return pl.pallas_call(
        _inproj_kernel,
        out_shape=(jax.ShapeDtypeStruct((t, n), F32), jax.ShapeDtypeStruct((t, D_MODEL), BF16)),
        grid=(t // tm, n // tn),
        in_specs=[pl.BlockSpec((tm, D_MODEL), lambda i, j: (i, 0)),
                  pl.BlockSpec((1, D_MODEL), lambda i, j: (0, 0)),
                  pl.BlockSpec((D_MODEL, tn), lambda i, j: (0, j))],
        out_specs=(pl.BlockSpec((tm, tn), lambda i, j: (i, j)),
                   pl.BlockSpec((tm, D_MODEL), lambda i, j: (i, 0))),
        compiler_params=_cparams(("parallel", "arbitrary")),
        name="inproj",
    )(x2, gain, w)


def _merge_kernel(h_ref, wg0, wg1, wg2, wg3, o0, o1, o2, o3, p_ref, out_ref):
    h = h_ref[...]
    acc = None
    for n, (wg, o) in enumerate(((wg0, o0), (wg1, o1), (wg2, o2), (wg3, o3))):
        gate = _sigmoid(jnp.dot(h, wg[...], preferred_element_type=F32))
        term = gate * jnp.dot(o[...], p_ref[n], preferred_element_type=F32)
        acc = term if acc is None else acc + term
    out_ref[...] = acc.astype(BF16)


def _merge(h, w_gate, outs, w_branch):
    t = h.shape[0]
    tm, tn = _tile(t, 1024), 512
    nj = D_MODEL // tn
    gate_specs = [pl.BlockSpec((D_MODEL, tn), functools.partial(lambda i, j, n: (0, n * nj + j), n=n))
                  for n in range(N_BRANCH)]
    o_specs = [pl.BlockSpec((tm, BRANCH), lambda i, j: (i, 0)) for _ in range(N_BRANCH)]
    return pl.pallas_call(
        _merge_kernel,
        out_shape=jax.ShapeDtypeStruct((t, D_MODEL), BF16),
        grid=(t // tm, nj),
        in_specs=[pl.BlockSpec((tm, D_MODEL), lambda i, j: (i, 0))] + gate_specs + o_specs
                 + [pl.BlockSpec((N_BRANCH, BRANCH, tn), lambda i, j: (0, 0, j))],
        out_specs=pl.BlockSpec((tm, tn), lambda i, j: (i, j)),
        compiler_params=_cparams(("parallel", "arbitrary")),
        name="merge",
    )(h, w_gate, w_gate, w_gate, w_gate, *outs, w_branch)


def _proj_norm_res_kernel(a_ref, w_ref, x_ref, g_ref, o_ref, acc_ref):
    k = pl.program_id(1)

    @pl.when(k == 0)
    def _():
        acc_ref[...] = jnp.zeros_like(acc_ref)

    acc_ref[...] += jnp.dot(a_ref[...], w_ref[...], preferred_element_type=F32)

    @pl.when(k == pl.num_programs(1) - 1)
    def _():
        o_ref[...] = x_ref[...] + _rms(acc_ref[...], g_ref[...], RMS_EPS)


def _proj_norm_res(a, w, x2, gain, tk):
    t, kdim = a.shape
    tm = _tile(t, 512)
    return pl.pallas_call(
        _proj_norm_res_kernel,
        out_shape=jax.ShapeDtypeStruct((t, D_MODEL), F32),
        grid=(t // tm, kdim // tk),
        in_specs=[pl.BlockSpec((tm, tk), lambda i, k: (i, k)),
                  pl.BlockSpec((tk, D_MODEL), lambda i, k: (k, 0)),
                  pl.BlockSpec((tm, D_MODEL), lambda i, k: (i, 0)),
                  pl.BlockSpec((1, D_MODEL), lambda i, k: (0, 0))],
        out_specs=pl.BlockSpec((tm, D_MODEL), lambda i, k: (i, 0)),
        scratch_shapes=[pltpu.VMEM((tm, D_MODEL), F32)],
        compiler_params=_cparams(("parallel", "arbitrary")),
        name="proj_norm_res",
    )(a, w, x2, gain)


def _ffn_up_kernel(x_ref, g_ref, wg_ref, wu_ref, a_ref, h_scr):
    @pl.when(pl.program_id(1) == 0)
    def _():
        h_scr[...] = _rms(x_ref[...], g_ref[...], RMS_EPS).astype(BF16)

    h = h_scr[...]
    gt = jnp.dot(h, wg_ref[...], preferred_element_type=F32)
    up = jnp.dot(h, wu_ref[...], preferred_element_type=F32)
    a_ref[...] = (gt * _sigmoid(gt) * up).astype(BF16)


def _ffn_up(x2, gain, wg, wu):
    t = x2.shape[0]
    tm, tn = _tile(t, 1024), 512
    return pl.pallas_call(
        _ffn_up_kernel,
        out_shape=jax.ShapeDtypeStruct((t, D_FF), BF16),
        grid=(t // tm, D_FF // tn),
        in_specs=[pl.BlockSpec((tm, D_MODEL), lambda i, j: (i, 0)),
                  pl.BlockSpec((1, D_MODEL), lambda i, j: (0, 0)),
                  pl.BlockSpec((D_MODEL, tn), lambda i, j: (0, j)),
                  pl.BlockSpec((D_MODEL, tn), lambda i, j: (0, j))],
        out_specs=pl.BlockSpec((tm, tn), lambda i, j: (i, j)),
        scratch_shapes=[pltpu.VMEM((tm, D_MODEL), BF16)],
        compiler_params=_cparams(("parallel", "arbitrary")),
        name="ffn_up",
    )(x2, gain, wg, wu)


def _conv_kernel(b_ref, c_ref, u_ref, w_ref, o_ref, carry):
    @pl.when(pl.program_id(1) == 0)
    def _():
        carry[...] = jnp.zeros_like(carry)

    cu = c_ref[0] * u_ref[0]
    ts = cu.shape[0]
    row = _iota2(cu.shape, 0)
    p1 = carry[7:8, :]
    p2 = carry[6:7, :]
    s1 = jnp.where(row == 0, p1, pltpu.roll(cu, 1, axis=0))
    s2 = jnp.where(row == 0, p2, jnp.where(row == 1, p1, pltpu.roll(cu, 2, axis=0)))
    w = w_ref[...]
    y = w[2:3, :] * cu + w[1:2, :] * s1 + w[0:1, :] * s2
    o_ref[0] = (b_ref[0] * y).astype(BF16)
    carry[...] = cu[ts - 8:, :]


def _conv_mixer(z3, conv_w):
    b, s = z3.shape[0], z3.shape[1]
    ts = _tile(s, 512)
    c0 = COL_CONV // BRANCH
    return pl.pallas_call(
        _conv_kernel,
        out_shape=jax.ShapeDtypeStruct((b, s, BRANCH), BF16),
        grid=(b, s // ts),
        in_specs=[pl.BlockSpec((1, ts, BRANCH), lambda i, j: (i, j, c0)),
                  pl.BlockSpec((1, ts, BRANCH), lambda i, j: (i, j, c0 + 1)),
                  pl.BlockSpec((1, ts, BRANCH), lambda i, j: (i, j, c0 + 2)),
                  pl.BlockSpec((3, BRANCH), lambda i, j: (0, 0))],
        out_specs=pl.BlockSpec((1, ts, BRANCH), lambda i, j: (i, j, 0)),
        scratch_shapes=[pltpu.VMEM((8, BRANCH), F32)],
        compiler_params=_cparams(("parallel", "arbitrary")),
        name="conv",
    )(z3, z3, z3, conv_w)


def _gla_kernel(q_ref, k_ref, v_ref, g_ref, wl_ref, wa2_ref, ba_ref, gn_ref, o_ref, state):
    @pl.when(pl.program_id(1) == 0)
    def _():
        state[...] = jnp.zeros_like(state)

    nb = q_ref.shape[0]
    rows = HEAD_GROUP * CHUNK
    causal = (_iota2((rows, rows), 0) % CHUNK) >= (_iota2((rows, rows), 1) % CHUNK)
    tril = _tril_incl(CHUNK)
    q_in, k_in, q_st, k_st, ve, dec = [], [], [], [], [], []
    for b in range(nb):
        q = q_ref[b] * (GLA_DK ** -0.5)
        k = k_ref[b]
        z = _bdot(wl_ref[b], wa2_ref[...]) + ba_ref[...]
        log_a = (jnp.minimum(z, 0.0) - jnp.log(1.0 + jnp.exp(-jnp.abs(z)))) * (1.0 / GLA_LOGIT_NORM)
        cum = _split_dot_left(tril, log_a)
        mid = cum[CHUNK // 2 - 1:CHUNK // 2, :]
        last = cum[CHUNK - 1:CHUNK, :]
        q_in.append(_expand(q * jnp.exp(cum - mid), GLA_DK).astype(BF16))
        k_in.append(_expand(k * jnp.exp(mid - cum), GLA_DK).astype(BF16))
        q_st.append(_expand(q * jnp.exp(cum), GLA_DK).astype(BF16))
        k_st.append(_expand(k * jnp.exp(last - cum), GLA_DK).astype(BF16))
        ve.append(_expand(v_ref[b], GLA_DV).astype(BF16))
        dec.append(jnp.exp(last))
    scores = [jnp.where(causal, _bdot_nt(q_in[b], k_in[b]), 0.0).astype(BF16) for b in range(nb)]
    st = [state[b] for b in range(nb)]
    oe = [_bdot(scores[b], ve[b]) + _bdot_nt(q_st[b], st[b]) for b in range(nb)]
    for b in range(nb):
        state[b] = st[b] * dec[b] + _bdot_tn(ve[b], k_st[b])
    gn = gn_ref[...]
    for b in range(nb):
        o = _collapse(oe[b])
        g = g_ref[b]
        gate = g * _sigmoid(g)
        outs = [_rms(o[:, h * GLA_DV:(h + 1) * GLA_DV], gn, HEAD_EPS) for h in range(GLA_HEADS)]
        o_ref[b] = (jnp.concatenate(outs, axis=1) * gate).astype(BF16)


def _gla_mixer(z3, wa2p, ba, gn):
    b, s = z3.shape[0], z3.shape[1]
    cq = COL_GLA // GLA_KEY
    cv = (COL_GLA + 2 * GLA_KEY) // BRANCH
    cw = COL_RW_A // LANE
    nb = RWKV_SEQS_PER_STEP if b % RWKV_SEQS_PER_STEP == 0 else 1
    return pl.pallas_call(
        _gla_kernel,
        out_shape=jax.ShapeDtypeStruct((b, s, BRANCH), BF16),
        grid=(b // nb, s // CHUNK),
        in_specs=[pl.BlockSpec((nb, CHUNK, GLA_KEY), lambda i, j: (i, j, cq)),
                  pl.BlockSpec((nb, CHUNK, GLA_KEY), lambda i, j: (i, j, cq + 1)),
                  pl.BlockSpec((nb, CHUNK, BRANCH), lambda i, j: (i, j, cv)),
                  pl.BlockSpec((nb, CHUNK, BRANCH), lambda i, j: (i, j, cv + 1)),
                  pl.BlockSpec((nb, CHUNK, LANE), lambda i, j: (i, j, cw)),
                  pl.BlockSpec((LANE, GLA_KEY), lambda i, j: (0, 0)),
                  pl.BlockSpec((1, GLA_KEY), lambda i, j: (0, 0)),
                  pl.BlockSpec((1, GLA_DV), lambda i, j: (0, 0))],
        out_specs=pl.BlockSpec((nb, CHUNK, BRANCH), lambda i, j: (i, j, 0)),
        scratch_shapes=[pltpu.VMEM((nb, GLA_HEADS * GLA_DV, GLA_KEY), F32)],
        compiler_params=_cparams(("parallel", "arbitrary")),
        name="gla",
    )(z3, z3, z3, z3, z3, wa2p, ba, gn)


def _alibi_slope(h):
    return 2.0 ** (-8.0 * (h + 1) / DIFF_HEADS)


def _diff_kernel(q_ref, k_ref, v_ref, lq1, lk1, lq2, lk2, gn_ref, o_ref, ka_scr, vb_scr, qa_scr, acc_scr,
                 m_scr, l_scr, *, lambda_init, tq):
    qi = pl.program_id(1)
    hw = 2 * DIFF_D
    seq = k_ref.shape[1]
    lane = _iota2((tq, hw), 1)

    @pl.when(qi == 0)
    def _():
        key_row = _iota2((tq, hw), 0).astype(F32)
        for h in range(DIFF_HEADS):
            key_bias = (_alibi_slope(h) * LOG2E) * key_row
            hi = key_bias.astype(BF16).astype(F32)
            mid = (key_bias - hi).astype(BF16).astype(F32)
            lo = key_bias - hi - mid
            k_aug = jnp.where(lane == 0, hi, jnp.where(lane == 1, mid, jnp.where(lane == 2, lo, 0.0)))
            k_aug = k_aug.astype(BF16)
            for c in range(seq // tq):
                rows = slice(c * tq, (c + 1) * tq)
                ka_scr[h, rows, 0:hw] = k_ref[0, rows, h * hw:(h + 1) * hw].astype(BF16)
                ka_scr[h, rows, hw:2 * hw] = k_aug
        for c in range(seq // tq):
            rows = slice(c * tq, (c + 1) * tq)
            vb_scr[rows, :] = v_ref[0, rows, :].astype(BF16)

    lane2 = _iota2((2 * tq, hw), 1)
    ones_aug = (lane2 < 3).astype(F32)
    for h in range(DIFF_HEADS):
        q = q_ref[0, :, h * hw:(h + 1) * hw] * (DIFF_D ** -0.5 * LOG2E)
        q2 = jnp.concatenate([jnp.where(lane < DIFF_D, q, 0.0), jnp.where(lane < DIFF_D, 0.0, q)],
                             axis=0)
        qa_scr[h] = jnp.concatenate([q2, ones_aug], axis=1).astype(BF16)
    acc_scr[...] = jnp.zeros_like(acc_scr)
    m_scr[...] = jnp.full_like(m_scr, NEG_INF)
    l_scr[...] = jnp.zeros_like(l_scr)

    def update(scores, start, shifts):
        alphas, probs = [], []
        for h, s in enumerate(scores):
            m_old = m_scr[h]
            m_new = jnp.maximum(m_old, jnp.max(s, axis=0, keepdims=True))
            alpha = jnp.exp2(m_old - m_new)
            p = jnp.exp2(s - m_new)
            l_scr[h] = alpha * l_scr[h] + jnp.sum(p, axis=0, keepdims=True)
            m_scr[h] = m_new - shifts[h]
            alphas.append(alpha)
            probs.append(p.astype(BF16))
        pvs = [lax.dot_general(vb_scr[pl.ds(start, tq), h * hw:(h + 1) * hw], probs[h],
                               (((0,), (0,)), ((), ())), preferred_element_type=F32)
               for h in range(DIFF_HEADS)]
        for h in range(DIFF_HEADS):
            acc_scr[h] = alphas[h] * acc_scr[h] + pvs[h]

    def body(j, carry):
        start = pl.multiple_of(j * tq, tq)
        scores = [lax.dot_general(ka_scr[h, pl.ds(start, tq), :], qa_scr[h], (((1,), (1,)), ((), ())),
                                  preferred_element_type=F32) for h in range(DIFF_HEADS)]
        update(scores, start, [_alibi_slope(h) * LOG2E * tq for h in range(DIFF_HEADS)])
        return carry

    lax.fori_loop(0, qi, body, 0)

    start = pl.multiple_of(qi * tq, tq)
    k_row = _iota2((tq, 2 * tq), 0)
    col = _iota2((tq, 2 * tq), 1)
    q_row = jnp.where(col >= tq, col - tq, col)
    visible = (k_row // CHUNK) <= (q_row // CHUNK)
    rel = (q_row - jnp.abs(q_row - k_row)).astype(F32)
    lam = (jnp.exp(jnp.sum(lq1[...] * lk1[...], axis=-1, keepdims=True))
           - jnp.exp(jnp.sum(lq2[...] * lk2[...], axis=-1, keepdims=True))
           + lambda_init)
    scores = []
    for h in range(DIFF_HEADS):
        s = lax.dot_general(ka_scr[h, pl.ds(start, tq), 0:hw], qa_scr[h, :, 0:hw], (((1,), (1,)), ((), ())),
                            preferred_element_type=F32)
        scores.append(s + jnp.where(visible, (_alibi_slope(h) * LOG2E) * rel, NEG_INF))
    update(scores, start, [0.0] * DIFF_HEADS)
    for h in range(DIFF_HEADS):
        on = acc_scr[h] / l_scr[h]
        o = (on[:, :tq] - lam * on[:, tq:]).T
        o_ref[0, :, h * hw:(h + 1) * hw] = (_rms(o, gn_ref[...], HEAD_EPS) * (1.0 - lambda_init)).astype(BF16)


def _diff_mixer(z3, lq1, lk1, lq2, lk2, gn, lambda_init):
    b, s = z3.shape[0], z3.shape[1]
    tq = _tile(s, 256)
    hw = 2 * DIFF_D
    cq = COL_DIFF // BRANCH
    nh = DIFF_HEADS
    vec = pl.BlockSpec((1, DIFF_D), lambda i, j: (0, 0))
    return pl.pallas_call(
        functools.partial(_diff_kernel, lambda_init=lambda_init, tq=tq),
        out_shape=jax.ShapeDtypeStruct((b, s, BRANCH), BF16),
        grid=(b, s // tq),
        in_specs=[pl.BlockSpec((1, tq, BRANCH), lambda i, j: (i, j, cq)),
                  pl.BlockSpec((1, s, BRANCH), lambda i, j: (i, 0, cq + 1), pipeline_mode=pl.Buffered(1)),
                  pl.BlockSpec((1, s, BRANCH), lambda i, j: (i, 0, cq + 2), pipeline_mode=pl.Buffered(1)),
                  vec, vec, vec, vec,
                  pl.BlockSpec((1, hw), lambda i, j: (0, 0))],
        out_specs=pl.BlockSpec((1, tq, BRANCH), lambda i, j: (i, j, 0)),
        scratch_shapes=[pltpu.VMEM((nh, s, 2 * hw), BF16), pltpu.VMEM((s, BRANCH), BF16),
                        pltpu.VMEM((nh, 2 * tq, 2 * hw), BF16), pltpu.VMEM((nh, hw, 2 * tq), F32),
                        pltpu.VMEM((nh, 1, 2 * tq), F32), pltpu.VMEM((nh, 1, 2 * tq), F32)],
        compiler_params=_cparams(("parallel", "arbitrary")),
        name="diff_attn",
    )(z3, z3, z3, lq1, lk1, lq2, lk2, gn)


def _shift(x, prev_rows):
    row = _iota2(x.shape, 0)
    return jnp.where(row == 0, prev_rows[7:8, :], pltpu.roll(x, 1, axis=0))


def _rwkv_kernel(*refs, has_vres, nb):
    if has_vres:
        (rkv_ref, gl_ref, wb_ref, ab_ref, vf_ref, mu_rkv, mu_g, mu_w, mu_a, w0, w2, a0, a2, g2, kk_s, ka_s,
         rk_s, lnw, lnb, v0, v1, v2, o_ref, p_rkv, p_g, p_w, p_a, state) = refs
    else:
        (rkv_ref, gl_ref, wb_ref, ab_ref, mu_rkv, mu_g, mu_w, mu_a, w0, w2, a0, a2, g2, kk_s, ka_s,
         rk_s, lnw, lnb, o_ref, vf_out, p_rkv, p_g, p_w, p_a, state) = refs

    @pl.when(pl.program_id(1) == 0)
    def _():
        for r in (p_rkv, p_g, p_w, p_a, state):
            r[...] = jnp.zeros_like(r)

    def stack(parts):
        return parts[0] if nb == 1 else jnp.concatenate(parts, axis=0)

    def mixed(x_ref, p_ref, mu_ref):
        parts = []
        for b in range(nb):
            x = x_ref[b]
            parts.append(x + (_shift(x, p_ref[b]) - x) * mu_ref[...])
            p_ref[b] = x[CHUNK - 8:, :]
        return stack(parts)

    rkv = mixed(rkv_ref, p_rkv, mu_rkv)
    g_lr = mixed(gl_ref, p_g, mu_g)
    w_lr = mixed(wb_ref, p_w, mu_w)
    a_lr = mixed(ab_ref, p_a, mu_a)
    r = rkv[:, 0:BRANCH]
    k = rkv[:, BRANCH:2 * BRANCH]
    v = rkv[:, 2 * BRANCH:3 * BRANCH]
    nrow = nb * CHUNK

    y = w0[...] + _bdot(jnp.tanh(w_lr), w2[...])
    softplus = jnp.maximum(-y, 0.0) + jnp.log(1.0 + jnp.exp(-jnp.abs(y)))
    lw = -jnp.exp(-softplus - 0.5)
    a = _sigmoid(a0[...] + _bdot(a_lr, a2[...]))
    gate = _bdot(_sigmoid(g_lr), g2[...])
    if has_vres:
        vf = stack([vf_ref[b] for b in range(nb)])
        v = v + (vf - v) * _sigmoid(v0[...] + _bdot(_bdot(v, v1[...]), v2[...]))
    else:
        for b in range(nb):
            vf_out[b] = v[b * CHUNK:(b + 1) * CHUNK]

    ones = _head_ones(BRANCH, RWKV_HD)
    kk = k * kk_s[...]
    k = k * (1.0 + (a - 1.0) * ka_s[...])
    sums = _split_dot(jnp.concatenate([kk * kk, r * k * rk_s[...]], axis=0), ones)
    kk = kk / jnp.maximum(jnp.sqrt(sums[:nrow]), 1e-12)
    bonus = sums[nrow:] * v

    ri, ci = _iota2((nrow, nrow), 0), _iota2((nrow, nrow), 1)
    tril = ((ri // CHUNK == ci // CHUNK) & (ri >= ci)).astype(BF16)
    cum = _split_dot_left(tril, lw)
    lasts = [cum[(b + 1) * CHUNK - 1:(b + 1) * CHUNK, :] for b in range(nb)]
    to_end = stack([jnp.exp(lasts[b] - cum[b * CHUNK:(b + 1) * CHUNK]) for b in range(nb)])
    decay_all = [jnp.exp(last) for last in lasts]
    a_t = -kk * jnp.exp(cum - lw)
    r_t = r * jnp.exp(cum)
    inv = jnp.exp(-cum)
    kka = kk * a
    b_s = kka * inv
    k_s = k * inv
    b_e = kka * to_end
    k_e = k * to_end

    rows = HEAD_GROUP * CHUNK
    ri = _iota2((rows, rows), 0) % CHUNK
    ci = _iota2((rows, rows), 1) % CHUNK
    strict = ri > ci
    incl = ri >= ci
    eye = (_iota2((rows, rows), 0) == _iota2((rows, rows), 1)).astype(F32)

    ngroup = RWKV_HEADS // HEAD_GROUP
    probs = [(b, gi) for b in range(nb) for gi in range(ngroup)]

    def ex(t):
        return [_expand(t[b * CHUNK:(b + 1) * CHUNK, gi * GROUP_W:(gi + 1) * GROUP_W], RWKV_HD).astype(BF16)
                for b, gi in probs]

    cat0 = lambda x, y_: jnp.concatenate([x, y_], axis=0)
    ae, re_, be, ke, bee, kee, ve = ex(a_t), ex(r_t), ex(b_s), ex(k_s), ex(b_e), ex(k_e), ex(v)
    pm = [_bdot_nt(cat0(ae[i], re_[i]), cat0(be[i], ke[i])) for i in range(len(probs))]
    n_ab = [jnp.where(strict, p[:rows, :rows], 0.0) for p in pm]
    a_ak = [jnp.where(strict, p[:rows, rows:], 0.0).astype(BF16) for p in pm]
    a_rb = [jnp.where(incl, p[rows:, :rows], 0.0).astype(BF16) for p in pm]
    a_rk = [jnp.where(incl, p[rows:, rows:], 0.0).astype(BF16) for p in pm]
    t_inv = [eye + n for n in n_ab]
    n_pow = [n.astype(BF16) for n in n_ab]
    for _ in range(int(math.log2(CHUNK)) - 1):
        n_pow = [_bdot(n, n).astype(BF16) for n in n_pow]
        t_inv = [t + _bdot(t, n) for t, n in zip(t_inv, n_pow)]
    av = [_bdot(cat0(a_ak[i], a_rk[i]), ve[i]) for i in range(len(probs))]
    y2 = [_bdot(t_inv[i], jnp.concatenate([ae[i], av[i][:rows].astype(BF16)], axis=1))
          for i in range(len(probs))]
    st = [state[b, gi] for b, gi in probs]
    x2 = [_bdot_nt(cat0(y2[i][:, :GROUP_W].astype(BF16), re_[i]), st[i]) for i in range(len(probs))]
    ue = [x2[i][:rows] + y2[i][:, GROUP_W:] for i in range(len(probs))]
    oe = [x2[i][rows:] + _bdot(a_rb[i], ue[i]) + av[i][rows:] for i in range(len(probs))]
    for i, (b, gi) in enumerate(probs):
        state[b, gi] = (st[i] * decay_all[b][:, gi * GROUP_W:(gi + 1) * GROUP_W]
                        + _bdot_tn(cat0(ue[i].astype(BF16), ve[i]), cat0(bee[i], kee[i])))
    o = stack([jnp.concatenate([_collapse(oe[b * ngroup + gi]) for gi in range(ngroup)], axis=1)
               for b in range(nb)])

    mean = _split_dot(o, ones) * (1.0 / RWKV_HD)
    cen = o - mean
    var = _split_dot(cen * cen, ones) * (1.0 / RWKV_HD)
    o = cen * lax.rsqrt(var + RWKV_GN_EPS) * lnw[...] + lnb[...]
    o = ((o + bonus) * gate).astype(BF16)
    for b in range(nb):
        o_ref[b] = o[b * CHUNK:(b + 1) * CHUNK]


def _rwkv_mixer(z3, v_first, p):
    b, s = z3.shape[0], z3.shape[1]
    has_vres = v_first is not None
    nb = RWKV_SEQS_PER_STEP if b % RWKV_SEQS_PER_STEP == 0 else 1
    row = lambda w, c: pl.BlockSpec((nb, CHUNK, w), lambda i, j: (i, j, c))
    full = lambda a: pl.BlockSpec(a.shape, lambda i, j: (0,) * a.ndim)
    ins = [z3, z3, z3, z3]
    specs = [row(3 * BRANCH, COL_RWKV // (3 * BRANCH)), row(RWKV_GATE_LORA, COL_RW_G // RWKV_GATE_LORA),
             row(LANE, COL_RW_W // LANE), row(LANE, COL_RW_A // LANE)]
    if has_vres:
        ins.append(v_first)
        specs.append(row(BRANCH, 0))
    names = ["mu_rkv", "mu_g", "mu_w", "mu_a", "w0", "w2", "a0", "a2", "g2", "kk", "ka", "rk", "lnw", "lnb"]
    if has_vres:
        names += ["v0", "v1", "v2"]
    for nme in names:
        ins.append(p[nme])
        specs.append(full(p[nme]))
    o_spec = pl.BlockSpec((nb, CHUNK, BRANCH), lambda i, j: (i, j, 0))
    o_shape = jax.ShapeDtypeStruct((b, s, BRANCH), BF16)
    if has_vres:
        out_shape, out_specs = o_shape, o_spec
    else:
        out_shape = (o_shape, jax.ShapeDtypeStruct((b, s, BRANCH), F32))
        out_specs = (o_spec, pl.BlockSpec((nb, CHUNK, BRANCH), lambda i, j: (i, j, 0)))
    res = pl.pallas_call(
        functools.partial(_rwkv_kernel, has_vres=has_vres, nb=nb),
        out_shape=out_shape,
        grid=(b // nb, s // CHUNK),
        in_specs=specs,
        out_specs=out_specs,
        scratch_shapes=[pltpu.VMEM((nb, 8, 3 * BRANCH), F32), pltpu.VMEM((nb, 8, RWKV_GATE_LORA), F32),
                        pltpu.VMEM((nb, 8, LANE), F32), pltpu.VMEM((nb, 8, LANE), F32),
                        pltpu.VMEM((nb, RWKV_HEADS // HEAD_GROUP, GROUP_W, GROUP_W), F32)],
        compiler_params=_cparams(("parallel", "arbitrary")),
        name="rwkv7",
    )(*ins)
    if has_vres:
        return res, v_first
    return res[0], res[1]


def _pad_rows(w, rows, offset=0):
    return jnp.zeros((rows, w.shape[1]), F32).at[offset:offset + w.shape[0]].set(w)


def _pack_w_in(w):
    off = 0
    conv = w[:, 0:1536]
    gla = w[:, 1536:3072]
    gla_lr = w[:, 3072:3088]
    diff = w[:, 3088:4624]
    rw_rkv = w[:, 4624:6160]
    rw_w = w[:, 6160:6256]
    rw_a = w[:, 6256:6352]
    rw_g = w[:, 6352:6608]
    z = lambda n: jnp.zeros((w.shape[0], n), w.dtype)
    del off
    packed = jnp.concatenate([conv, gla, diff, rw_rkv, rw_g, rw_w, z(32), rw_a, gla_lr, z(16)], axis=1)
    return packed.astype(BF16), w[:, 6608:].astype(BF16)


def _pad_lanes(v, width, offset=0):
    return jnp.zeros((1, width), F32).at[0, offset:offset + v.shape[0]].set(v)


def kernel(x, norm_mix_pre, w_in, conv_w, gla_wa2, gla_ba, gla_norm, diff_lq1, diff_lk1, diff_lq2, diff_lk2,
           diff_norm, rw_mu, rw_w0, rw_w2, rw_a0, rw_a2, rw_g2, rw_kk, rw_ka, rw_rk, rw_lnw, rw_lnb, rw_v0,
           rw_v1, rw_v2, w_branch, w_out, norm_mix_post, norm_ffn_pre, w_gate, w_up, w_down, norm_ffn_post):
    bsz, seq = x.shape[0], x.shape[1]
    depth = w_in.shape[0]
    t = bsz * seq
    x2 = x.reshape(t, D_MODEL)
    row = lambda v: v.reshape(1, -1)
    v_first = None
    for l in range(depth):
        w_mix, w_gates = _pack_w_in(w_in[l])
        z, h = _inproj(x2, row(norm_mix_pre[l]), w_mix)
        z3 = z.reshape(bsz, seq, N_MIX)
        lambda_init = 0.8 - 0.6 * math.exp(-0.3 * l)

        o_conv = _conv_mixer(z3, conv_w[l])
        o_gla = _gla_mixer(z3, _pad_rows(gla_wa2[l], LANE, GLA_WLR_LANE), row(gla_ba[l]), row(gla_norm[l]))
        o_diff = _diff_mixer(z3, row(diff_lq1[l]), row(diff_lk1[l]), row(diff_lq2[l]),
                             row(diff_lk2[l]), row(diff_norm[l]), lambda_init)
        mu = rw_mu[l]
        rp = {
            "mu_rkv": row(mu[0:1536]),
            "mu_w": _pad_lanes(mu[1536:1632], LANE),
            "mu_a": _pad_lanes(mu[1632:1728], LANE),
            "mu_g": row(mu[1728:1984]),
            "w0": row(rw_w0[l]), "w2": _pad_rows(rw_w2[l], LANE),
            "a0": row(rw_a0[l]), "a2": _pad_rows(rw_a2[l], LANE),
            "g2": rw_g2[l], "kk": row(rw_kk[l]), "ka": row(rw_ka[l]), "rk": row(rw_rk[l]),
            "lnw": row(rw_lnw[l]), "lnb": row(rw_lnb[l]),
        }
        if l > 0:
            rp.update(v0=row(rw_v0[l - 1]), v1=rw_v1[l - 1], v2=rw_v2[l - 1])
        o_rwkv, v_first = _rwkv_mixer(z3, v_first if l > 0 else None, rp)

        outs = [o.reshape(t, BRANCH) for o in (o_conv, o_gla, o_diff, o_rwkv)]
        merged = _merge(h, w_gates, outs, w_branch[l].astype(BF16))
        x2 = _proj_norm_res(merged, w_out[l].astype(BF16), x2, row(norm_mix_post[l]), D_MODEL)
        act = _ffn_up(x2, row(norm_ffn_pre[l]), w_gate[l].astype(BF16), w_up[l].astype(BF16))
        x2 = _proj_norm_res(act, w_down[l].astype(BF16), x2, row(norm_ffn_post[l]), D_FF // 4)
    return x2.reshape(bsz, seq, D_MODEL)
```

```python
import functools
import math

import jax
import jax.numpy as jnp
from jax import lax
from jax.experimental import pallas as pl
from jax.experimental.pallas import tpu as pltpu

F32 = jnp.float32
BF16 = jnp.bfloat16

D_MODEL = 2048
CHUNK = 64
N_BRANCH = 4
BRANCH = 512
GLA_HEADS = 4
GLA_DK = 64
GLA_DV = 128
GLA_KEY = GLA_HEADS * GLA_DK
GLA_LOW_RANK = 16
GLA_LOGIT_NORM = 16.0
DIFF_HEADS = 4
DIFF_D = 64
RWKV_HEADS = 8
RWKV_HD = 64
RWKV_DECAY_LORA = 96
RWKV_AAA_LORA = 96
RWKV_MV_LORA = 64
RWKV_GATE_LORA = 256
D_FF = 5632
RMS_EPS = 1e-6
HEAD_EPS = 1e-5
RWKV_GN_EPS = 64e-5
NEG_INF = -1e30
LOG2E = 1.4426950408889634

COL_CONV = 0
COL_GLA = 1536
COL_DIFF = 3072
COL_RWKV = 4608
COL_RW_G = 6144
COL_RW_W = 6400
COL_RW_A = 6528
GLA_WLR_LANE = RWKV_AAA_LORA
N_MIX = 6656
LANE = 128
HEAD_GROUP = 4
GROUP_W = HEAD_GROUP * 64
RWKV_SEQS_PER_STEP = 4

VMEM_LIMIT = 56 * 1024 * 1024
FFN_DOWN_TK = D_FF // 2


def _tile(n, pref):
    t = min(n, pref)
    while n % t:
        t -= 8
    return t


def _cparams(sem):
    return pltpu.CompilerParams(dimension_semantics=sem, vmem_limit_bytes=VMEM_LIMIT)


def _bdot(a, b):
    return jnp.dot(a.astype(BF16), b.astype(BF16), preferred_element_type=F32)


def _bdot_nt(a, b):
    return lax.dot_general(a.astype(BF16), b.astype(BF16), (((1,), (1,)), ((), ())),
                           preferred_element_type=F32)


def _bdot_tn(a, b):
    return lax.dot_general(a.astype(BF16), b.astype(BF16), (((0,), (0,)), ((), ())),
                           preferred_element_type=F32)


def _split_dot(x, ones_bf16):
    hi = x.astype(BF16)
    lo = (x - hi.astype(F32)).astype(BF16)
    return (jnp.dot(hi, ones_bf16, preferred_element_type=F32)
            + jnp.dot(lo, ones_bf16, preferred_element_type=F32))


def _split_dot_left(ones_bf16, x):
    hi = x.astype(BF16)
    lo = (x - hi.astype(F32)).astype(BF16)
    return (jnp.dot(ones_bf16, hi, preferred_element_type=F32)
            + jnp.dot(ones_bf16, lo, preferred_element_type=F32))


def _sigmoid(x):
    return 1.0 / (1.0 + jnp.exp(-x))


def _rms(x, gain, eps):
    return x * lax.rsqrt(jnp.mean(x * x, axis=-1, keepdims=True) + eps) * gain


def _iota2(shape, dim):
    return lax.broadcasted_iota(jnp.int32, shape, dim)


def _tril_incl(n):
    return (_iota2((n, n), 0) >= _iota2((n, n), 1)).astype(BF16)


def _head_ones(n, width):
    return ((_iota2((n, n), 0) // width) == (_iota2((n, n), 1) // width)).astype(BF16)


def _expand(x, col_group):
    rows = HEAD_GROUP * CHUNK
    xt = jnp.concatenate([x] * HEAD_GROUP, axis=0)
    keep = (_iota2((rows, x.shape[1]), 0) // CHUNK) == (_iota2((rows, x.shape[1]), 1) // col_group)
    return jnp.where(keep, xt, 0.0)


def _collapse(xe):
    out = xe[0:CHUNK]
    for h in range(1, HEAD_GROUP):
        out = out + xe[h * CHUNK:(h + 1) * CHUNK]
    return out


def _norm_kernel(x_ref, g_ref, h_ref):
    h_ref[...] = _rms(x_ref[...], g_ref[...], RMS_EPS).astype(BF16)


def _norm(x2, gain):
    t = x2.shape[0]
    tm = _tile(t, 512)
    return pl.pallas_call(
        _norm_kernel,
        out_shape=jax.ShapeDtypeStruct((t, D_MODEL), BF16),
        grid=(t // tm,),
        in_specs=[pl.BlockSpec((tm, D_MODEL), lambda i: (i, 0)),
                  pl.BlockSpec((1, D_MODEL), lambda i: (0, 0))],
        out_specs=pl.BlockSpec((tm, D_MODEL), lambda i: (i, 0)),
        compiler_params=_cparams(("parallel",)),
        name="norm",
    )(x2, gain)


def _inproj_kernel(h_ref, w_ref, z_ref):
    z_ref[...] = jnp.dot(h_ref[...], w_ref[...], preferred_element_type=F32).astype(BF16)


def _inproj(h, w):
    t, n = h.shape[0], w.shape[1]
    tm, tn = _tile(t, 1024), _tile(n, 1664)
    return pl.pallas_call(
        _inproj_kernel,
        out_shape=jax.ShapeDtypeStruct((t, n), BF16),
        grid=(t // tm, n // tn),
        in_specs=[pl.BlockSpec((tm, D_MODEL), lambda i, j: (i, 0)),
                  pl.BlockSpec((D_MODEL, tn), lambda i, j: (0, j))],
        out_specs=pl.BlockSpec((tm, tn), lambda i, j: (i, j)),
        compiler_params=_cparams(("parallel", "arbitrary")),
        name="inproj",
    )(h, w)


def _merge_kernel(h_ref, wg0, wg1, wg2, wg3, o0, o1, o2, o3, p_ref, out_ref):
    h = h_ref[...]
    acc = None
    for n, (wg, o) in enumerate(((wg0, o0), (wg1, o1), (wg2, o2), (wg3, o3))):
        gate = _sigmoid(jnp.dot(h, wg[...], preferred_element_type=F32))
        term = gate * jnp.dot(o[...], p_ref[n], preferred_element_type=F32)
        acc = term if acc is None else acc + term
    out_ref[...] = acc.astype(BF16)


def _merge(h, w_gate, outs, w_branch):
    t = h.shape[0]
    tm, tn = _tile(t, 1024), 512
    nj = D_MODEL // tn
    gate_specs = [pl.BlockSpec((D_MODEL, tn), functools.partial(lambda i, j, n: (0, n * nj + j), n=n))
                  for n in range(N_BRANCH)]
    o_specs = [pl.BlockSpec((tm, BRANCH), lambda i, j: (i, 0)) for _ in range(N_BRANCH)]
    return pl.pallas_call(
        _merge_kernel,
        out_shape=jax.ShapeDtypeStruct((t, D_MODEL), BF16),
        grid=(t // tm, nj),
        in_specs=[pl.BlockSpec((tm, D_MODEL), lambda i, j: (i, 0))] + gate_specs + o_specs
                 + [pl.BlockSpec((N_BRANCH, BRANCH, tn), lambda i, j: (0, 0, j))],
        out_specs=pl.BlockSpec((tm, tn), lambda i, j: (i, j)),
        compiler_params=_cparams(("parallel", "arbitrary")),
        name="merge",
    )(h, w_gate, w_gate, w_gate, w_gate, *outs, w_branch)


def _proj_norm_res_kernel(a_ref, w_ref, x_ref, g_ref, gn_ref, o_ref, *h_ref):
    k = pl.program_id(1)
    part = jnp.dot(a_ref[...], w_ref[...], preferred_element_type=F32)

    @pl.when(k == 0)
    def _():
        o_ref[...] = part

    @pl.when(k > 0)
    def _():
        o_ref[...] += part

    @pl.when(k == pl.num_programs(1) - 1)
    def _():
        xn = x_ref[...] + _rms(o_ref[...], g_ref[...], RMS_EPS)
        o_ref[...] = xn
        if h_ref:
            h_ref[0][...] = _rms(xn, gn_ref[...], RMS_EPS).astype(BF16)


def _proj_norm_res(a, w, x2, gain, next_gain, tk):
    t, kdim = a.shape
    tm = _tile(t, 512)
    emit_h = next_gain is not None
    row_spec = pl.BlockSpec((tm, D_MODEL), lambda i, k: (i, 0))
    vec_spec = pl.BlockSpec((1, D_MODEL), lambda i, k: (0, 0))
    out_shape = [jax.ShapeDtypeStruct((t, D_MODEL), F32)]
    if emit_h:
        out_shape.append(jax.ShapeDtypeStruct((t, D_MODEL), BF16))
    res = pl.pallas_call(
        _proj_norm_res_kernel,
        out_shape=tuple(out_shape),
        grid=(t // tm, kdim // tk),
        in_specs=[pl.BlockSpec((tm, tk), lambda i, k: (i, k)),
                  pl.BlockSpec((tk, D_MODEL), lambda i, k: (k, 0)),
                  row_spec, vec_spec, vec_spec],
        out_specs=tuple([row_spec] * len(out_shape)),
        compiler_params=_cparams(("parallel", "arbitrary")),
        name="proj_norm_res",
    )(a, w, x2, gain, next_gain if emit_h else gain)
    return (res[0], res[1]) if emit_h else (res[0], None)


def _ffn_up_kernel(h_ref, wg_ref, wu_ref, a_ref):
    h = h_ref[...]
    gt = jnp.dot(h, wg_ref[...], preferred_element_type=F32)
    up = jnp.dot(h, wu_ref[...], preferred_element_type=F32)
    a_ref[...] = (gt * _sigmoid(gt) * up).astype(BF16)


def _ffn_up(h, wg, wu):
    t = h.shape[0]
    tm, tn = _tile(t, 1024), 512
    return pl.pallas_call(
        _ffn_up_kernel,
        out_shape=jax.ShapeDtypeStruct((t, D_FF), BF16),
        grid=(t // tm, D_FF // tn),
        in_specs=[pl.BlockSpec((tm, D_MODEL), lambda i, j: (i, 0)),
                  pl.BlockSpec((D_MODEL, tn), lambda i, j: (0, j)),
                  pl.BlockSpec((D_MODEL, tn), lambda i, j: (0, j))],
        out_specs=pl.BlockSpec((tm, tn), lambda i, j: (i, j)),
        compiler_params=_cparams(("parallel", "arbitrary")),
        name="ffn_up",
    )(h, wg, wu)


def _conv_kernel(b_ref, c_ref, u_ref, w_ref, o_ref, carry):
    @pl.when(pl.program_id(1) == 0)
    def _():
        carry[...] = jnp.zeros_like(carry)

    cu = c_ref[0].astype(F32) * u_ref[0].astype(F32)
    ts = cu.shape[0]
    row = _iota2(cu.shape, 0)
    p1 = carry[7:8, :]
    p2 = carry[6:7, :]
    s1 = jnp.where(row == 0, p1, pltpu.roll(cu, 1, axis=0))
    s2 = jnp.where(row == 0, p2, jnp.where(row == 1, p1, pltpu.roll(cu, 2, axis=0)))
    w = w_ref[...]
    y = w[2:3, :] * cu + w[1:2, :] * s1 + w[0:1, :] * s2
    o_ref[0] = (b_ref[0].astype(F32) * y).astype(BF16)
    carry[...] = cu[ts - 8:, :]


def _conv_mixer(z3, conv_w):
    b, s = z3.shape[0], z3.shape[1]
    ts = _tile(s, 512)
    c0 = COL_CONV // BRANCH
    return pl.pallas_call(
        _conv_kernel,
        out_shape=jax.ShapeDtypeStruct((b, s, BRANCH), BF16),
        grid=(b, s // ts),
        in_specs=[pl.BlockSpec((1, ts, BRANCH), lambda i, j: (i, j, c0)),
                  pl.BlockSpec((1, ts, BRANCH), lambda i, j: (i, j, c0 + 1)),
                  pl.BlockSpec((1, ts, BRANCH), lambda i, j: (i, j, c0 + 2)),
                  pl.BlockSpec((3, BRANCH), lambda i, j: (0, 0))],
        out_specs=pl.BlockSpec((1, ts, BRANCH), lambda i, j: (i, j, 0)),
        scratch_shapes=[pltpu.VMEM((8, BRANCH), F32)],
        compiler_params=_cparams(("parallel", "arbitrary")),
        name="conv",
    )(z3, z3, z3, conv_w)


def _gla_kernel(q_ref, k_ref, v_ref, g_ref, wl_ref, wa2_ref, ba_ref, gn_ref, o_ref, state):
    @pl.when(pl.program_id(1) == 0)
    def _():
        state[...] = jnp.zeros_like(state)

    nb = q_ref.shape[0]
    rows = HEAD_GROUP * CHUNK
    causal = (_iota2((rows, rows), 0) % CHUNK) >= (_iota2((rows, rows), 1) % CHUNK)
    tril = _tril_incl(CHUNK)
    q_in, k_in, q_st, k_st, ve, dec = [], [], [], [], [], []
    for b in range(nb):
        q = q_ref[b].astype(F32) * (GLA_DK ** -0.5)
        k = k_ref[b].astype(F32)
        z = _bdot(wl_ref[b], wa2_ref[...]) + ba_ref[...]
        log_a = (jnp.minimum(z, 0.0) - jnp.log(1.0 + jnp.exp(-jnp.abs(z)))) * (1.0 / GLA_LOGIT_NORM)
        cum = _split_dot_left(tril, log_a)
        mid = cum[CHUNK // 2 - 1:CHUNK // 2, :]
        last = cum[CHUNK - 1:CHUNK, :]
        q_in.append(_expand(q * jnp.exp(cum - mid), GLA_DK).astype(BF16))
        k_in.append(_expand(k * jnp.exp(mid - cum), GLA_DK).astype(BF16))
        q_st.append(_expand(q * jnp.exp(cum), GLA_DK).astype(BF16))
        k_st.append(_expand(k * jnp.exp(last - cum), GLA_DK).astype(BF16))
        ve.append(_expand(v_ref[b].astype(F32), GLA_DV).astype(BF16))
        dec.append(jnp.exp(last))
    scores = [jnp.where(causal, _bdot_nt(q_in[b], k_in[b]), 0.0).astype(BF16) for b in range(nb)]
    st = [state[b] for b in range(nb)]
    oe = [_bdot(scores[b], ve[b]) + _bdot_nt(q_st[b], st[b]) for b in range(nb)]
    for b in range(nb):
        state[b] = st[b] * dec[b] + _bdot_tn(ve[b], k_st[b])
    gn = gn_ref[...]
    for b in range(nb):
        o = _collapse(oe[b])
        g = g_ref[b].astype(F32)
        gate = g * _sigmoid(g)
        outs = [_rms(o[:, h * GLA_DV:(h + 1) * GLA_DV], gn, HEAD_EPS) for h in range(GLA_HEADS)]
        o_ref[b] = (jnp.concatenate(outs, axis=1) * gate).astype(BF16)


def _gla_mixer(z3, wa2p, ba, gn):
    b, s = z3.shape[0], z3.shape[1]
    cq = COL_GLA // GLA_KEY
    cv = (COL_GLA + 2 * GLA_KEY) // BRANCH
    cw = COL_RW_A // LANE
    nb = RWKV_SEQS_PER_STEP if b % RWKV_SEQS_PER_STEP == 0 else 1
    return pl.pallas_call(
        _gla_kernel,
        out_shape=jax.ShapeDtypeStruct((b, s, BRANCH), BF16),
        grid=(b // nb, s // CHUNK),
        in_specs=[pl.BlockSpec((nb, CHUNK, GLA_KEY), lambda i, j: (i, j, cq)),
                  pl.BlockSpec((nb, CHUNK, GLA_KEY), lambda i, j: (i, j, cq + 1)),
                  pl.BlockSpec((nb, CHUNK, BRANCH), lambda i, j: (i, j, cv)),
                  pl.BlockSpec((nb, CHUNK, BRANCH), lambda i, j: (i, j, cv + 1)),
                  pl.BlockSpec((nb, CHUNK, LANE), lambda i, j: (i, j, cw)),
                  pl.BlockSpec((LANE, GLA_KEY), lambda i, j: (0, 0)),
                  pl.BlockSpec((1, GLA_KEY), lambda i, j: (0, 0)),
                  pl.BlockSpec((1, GLA_DV), lambda i, j: (0, 0))],
        out_specs=pl.BlockSpec((nb, CHUNK, BRANCH), lambda i, j: (i, j, 0)),
        scratch_shapes=[pltpu.VMEM((nb, GLA_HEADS * GLA_DV, GLA_KEY), F32)],
        compiler_params=_cparams(("parallel", "arbitrary")),
        name="gla",
    )(z3, z3, z3, z3, z3, wa2p, ba, gn)


VT_ROWS = 2 * DIFF_D + 16


def _alibi_slope(h):
    return 2.0 ** (-8.0 * (h + 1) / DIFF_HEADS)


def _diff_kernel(q_ref, k_ref, v_ref, lq1, lk1, lq2, lk2, gn_ref, o_ref, ka_scr, vb_scr, qa_scr, acc_scr,
                 m_scr, s_scr, *, lambda_init, tq):
    qi = pl.program_id(1)
    hw = 2 * DIFF_D
    seq = k_ref.shape[1]
    lane = _iota2((tq, hw), 1)

    @pl.when(qi == 0)
    def _():
        key_row = _iota2((tq, hw), 0).astype(F32)
        for h in range(DIFF_HEADS):
            k_aug = []
            for half in range(2):
                key_bias = (_alibi_slope(h) * LOG2E) * (key_row + float(half * tq))
                hi = key_bias.astype(BF16).astype(F32)
                mid = (key_bias - hi).astype(BF16).astype(F32)
                lo = key_bias - hi - mid
                aug = jnp.where(lane == 0, hi, jnp.where(lane == 1, mid, jnp.where(lane == 2, lo, 0.0)))
                k_aug.append(aug.astype(BF16))
            for c in range(seq // tq):
                rows = slice(c * tq, (c + 1) * tq)
                ka_scr[h, rows, 0:hw] = k_ref[0, rows, h * hw:(h + 1) * hw].astype(BF16)
                ka_scr[h, rows, hw:2 * hw] = k_aug[c % 2]
        sub = _iota2((VT_ROWS - hw, tq), 0)
        tail = jnp.where(sub == 0, 1.0, 0.0).astype(BF16)
        for c in range(seq // tq):
            rows = slice(c * tq, (c + 1) * tq)
            for h in range(DIFF_HEADS):
                vb_scr[h, 0:hw, rows] = v_ref[0, rows, h * hw:(h + 1) * hw].astype(F32).T.astype(BF16)
                vb_scr[h, hw:VT_ROWS, rows] = tail

    lane2 = _iota2((2 * tq, hw), 1)
    ones_aug = (lane2 < 3).astype(F32)
    for h in range(DIFF_HEADS):
        q = q_ref[0, :, h * hw:(h + 1) * hw].astype(F32) * (DIFF_D ** -0.5 * LOG2E)
        q2 = jnp.concatenate([jnp.where(lane < DIFF_D, q, 0.0), jnp.where(lane < DIFF_D, 0.0, q)],
                             axis=0)
        qa_scr[h] = jnp.concatenate([q2, ones_aug], axis=1).astype(BF16)
    acc_scr[...] = jnp.zeros_like(acc_scr)
    m_scr[...] = jnp.full_like(m_scr, NEG_INF)

    tk = 2 * tq
    last_tile = qi // 2

    def tile_scores(j):
        start = pl.multiple_of(j * tk, tk)
        return [lax.dot_general(ka_scr[h, pl.ds(start, tk), :], qa_scr[h], (((1,), (1,)), ((), ())),
                                preferred_element_type=F32) for h in range(DIFF_HEADS)]

    def update(scores, j):
        start = pl.multiple_of(j * tk, tk)
        alphas, probs = [], []
        for h, s in enumerate(scores):
            m_old = m_scr[h]
            m_new = jnp.maximum(m_old, jnp.max(s, axis=0, keepdims=True))
            alphas.append(jnp.exp2(m_old - m_new))
            probs.append(jnp.exp2((s - m_new).astype(BF16)))
            m_scr[h] = m_new - _alibi_slope(h) * LOG2E * tk
        pvs = [jnp.dot(vb_scr[h, :, pl.ds(start, tk)], probs[h], preferred_element_type=F32)
               for h in range(DIFF_HEADS)]
        for h in range(DIFF_HEADS):
            acc_scr[h] = alphas[h] * acc_scr[h] + pvs[h]

    for h, s in enumerate(tile_scores(0)):
        s_scr[h] = s

    def body(j, carry):
        nxt = tile_scores(j + 1)
        update([s_scr[h] for h in range(DIFF_HEADS)], j)
        for h in range(DIFF_HEADS):
            s_scr[h] = nxt[h]
        return carry

    lax.fori_loop(0, last_tile, body, 0)

    k_row = _iota2((tk, 2 * tq), 0)
    col = _iota2((tk, 2 * tq), 1)
    q_row = jnp.where(col >= tq, col - tq, col) + (qi % 2) * tq
    visible = (k_row // CHUNK) <= (q_row // CHUNK)
    d = q_row - k_row
    rel = (d - jnp.abs(d)).astype(F32)
    lam = (jnp.exp(jnp.sum(lq1[...] * lk1[...], axis=-1, keepdims=True))
           - jnp.exp(jnp.sum(lq2[...] * lk2[...], axis=-1, keepdims=True))
           + lambda_init)
    update([s_scr[h] + jnp.where(visible, (_alibi_slope(h) * LOG2E) * rel, NEG_INF)
            for h in range(DIFF_HEADS)], last_tile)
    for h in range(DIFF_HEADS):
        on = acc_scr[h, 0:hw, :] / acc_scr[h, hw:hw + 1, :]
        o = (on[:, :tq] - lam * on[:, tq:]).T
        o_ref[0, :, h * hw:(h + 1) * hw] = (_rms(o, gn_ref[...], HEAD_EPS) * (1.0 - lambda_init)).astype(BF16)


def _diff_mixer(z3, lq1, lk1, lq2, lk2, gn, lambda_init):
    b, s = z3.shape[0], z3.shape[1]
    tq = _tile(s // 2, 256)
    assert s % (2 * tq) == 0 and tq % CHUNK == 0
    hw = 2 * DIFF_D
    cq = COL_DIFF // BRANCH
    nh = DIFF_HEADS
    vec = pl.BlockSpec((1, DIFF_D), lambda i, j: (0, 0))
    return pl.pallas_call(
        functools.partial(_diff_kernel, lambda_init=lambda_init, tq=tq),
        out_shape=jax.ShapeDtypeStruct((b, s, BRANCH), BF16),
        grid=(b, s // tq),
        in_specs=[pl.BlockSpec((1, tq, BRANCH), lambda i, j: (i, j, cq)),
                  pl.BlockSpec((1, s, BRANCH), lambda i, j: (i, 0, cq + 1), pipeline_mode=pl.Buffered(1)),
                  pl.BlockSpec((1, s, BRANCH), lambda i, j: (i, 0, cq + 2), pipeline_mode=pl.Buffered(1)),
                  vec, vec, vec, vec,
                  pl.BlockSpec((1, hw), lambda i, j: (0, 0))],
        out_specs=pl.BlockSpec((1, tq, BRANCH), lambda i, j: (i, j, 0)),
        scratch_shapes=[pltpu.VMEM((nh, s, 2 * hw), BF16), pltpu.VMEM((nh, VT_ROWS, s), BF16),
                        pltpu.VMEM((nh, 2 * tq, 2 * hw), BF16), pltpu.VMEM((nh, VT_ROWS, 2 * tq), F32),
                        pltpu.VMEM((nh, 1, 2 * tq), F32), pltpu.VMEM((nh, 2 * tq, 2 * tq), F32)],
        compiler_params=_cparams(("parallel", "arbitrary")),
        name="diff_attn",
    )(z3, z3, z3, lq1, lk1, lq2, lk2, gn)


def _shift(x, prev_rows):
    row = _iota2(x.shape, 0)
    return jnp.where(row == 0, prev_rows[7:8, :], pltpu.roll(x, 1, axis=0))


def _rwkv_kernel(*refs, has_vres, nb):
    if has_vres:
        (rkv_ref, gl_ref, wb_ref, ab_ref, vf_ref, mu_rkv, mu_g, mu_w, mu_a, w0, w2, a0, a2, g2, kk_s, ka_s,
         rk_s, lnw, lnb, v0, v1, v2, o_ref, p_rkv, p_g, p_w, p_a, state) = refs
    else:
        (rkv_ref, gl_ref, wb_ref, ab_ref, mu_rkv, mu_g, mu_w, mu_a, w0, w2, a0, a2, g2, kk_s, ka_s,
         rk_s, lnw, lnb, o_ref, vf_out, p_rkv, p_g, p_w, p_a, state) = refs

    @pl.when(pl.program_id(1) == 0)
    def _():
        for r in (p_rkv, p_g, p_w, p_a, state):
            r[...] = jnp.zeros_like(r)

    def stack(parts):
        return parts[0] if nb == 1 else jnp.concatenate(parts, axis=0)

    def mixed(x_ref, p_ref, mu_ref):
        parts = []
        for b in range(nb):
            x = x_ref[b].astype(F32)
            parts.append(x + (_shift(x, p_ref[b]) - x) * mu_ref[...])
            p_ref[b] = x[CHUNK - 8:, :]
        return stack(parts)

    rkv = mixed(rkv_ref, p_rkv, mu_rkv)
    g_lr = mixed(gl_ref, p_g, mu_g)
    w_lr = mixed(wb_ref, p_w, mu_w)
    a_lr = mixed(ab_ref, p_a, mu_a)
    r = rkv[:, 0:BRANCH]
    k = rkv[:, BRANCH:2 * BRANCH]
    v = rkv[:, 2 * BRANCH:3 * BRANCH]
    nrow = nb * CHUNK

    y = w0[...] + _bdot(jnp.tanh(w_lr), w2[...])
    softplus = jnp.maximum(-y, 0.0) + jnp.log(1.0 + jnp.exp(-jnp.abs(y)))
    lw = -jnp.exp(-softplus - 0.5)
    a = _sigmoid(a0[...] + _bdot(a_lr, a2[...]))
    gate = _bdot(_sigmoid(g_lr), g2[...])
    if has_vres:
        vf = stack([vf_ref[b] for b in range(nb)])
        v = v + (vf - v) * _sigmoid(v0[...] + _bdot(_bdot(v, v1[...]), v2[...]))
    else:
        for b in range(nb):
            vf_out[b] = v[b * CHUNK:(b + 1) * CHUNK]

    ones = _head_ones(BRANCH, RWKV_HD)
    kk = k * kk_s[...]
    k = k * (1.0 + (a - 1.0) * ka_s[...])
    sums = _split_dot(jnp.concatenate([kk * kk, r * k * rk_s[...]], axis=0), ones)
    kk = kk / jnp.maximum(jnp.sqrt(sums[:nrow]), 1e-12)
    bonus = sums[nrow:] * v

    ri, ci = _iota2((nrow, nrow), 0), _iota2((nrow, nrow), 1)
    tril = ((ri // CHUNK == ci // CHUNK) & (ri >= ci)).astype(BF16)
    cum = _split_dot_left(tril, lw)
    lasts = [cum[(b + 1) * CHUNK - 1:(b + 1) * CHUNK, :] for b in range(nb)]
    to_end = stack([jnp.exp(lasts[b] - cum[b * CHUNK:(b + 1) * CHUNK]) for b in range(nb)])
    decay_all = [jnp.exp(last) for last in lasts]
    a_t = -kk * jnp.exp(cum - lw)
    r_t = r * jnp.exp(cum)
    inv = jnp.exp(-cum)
    kka = kk * a
    b_s = kka * inv
    k_s = k * inv
    b_e = kka * to_end
    k_e = k * to_end

    rows = HEAD_GROUP * CHUNK
    ri = _iota2((rows, rows), 0) % CHUNK
    ci = _iota2((rows, rows), 1) % CHUNK
    strict = ri > ci
    incl = ri >= ci
    eye = (_iota2((rows, rows), 0) == _iota2((rows, rows), 1)).astype(F32)

    ngroup = RWKV_HEADS // HEAD_GROUP
    probs = [(b, gi) for b in range(nb) for gi in range(ngroup)]

    def ex(t):
        return [_expand(t[b * CHUNK:(b + 1) * CHUNK, gi * GROUP_W:(gi + 1) * GROUP_W], RWKV_HD).astype(BF16)
                for b, gi in probs]

    cat0 = lambda x, y_: jnp.concatenate([x, y_], axis=0)
    ae, re_, be, ke, bee, kee, ve = ex(a_t), ex(r_t), ex(b_s), ex(k_s), ex(b_e), ex(k_e), ex(v)
    pm = [_bdot_nt(cat0(ae[i], re_[i]), cat0(be[i], ke[i])) for i in range(len(probs))]
    n_ab = [jnp.where(strict, p[:rows, :rows], 0.0) for p in pm]
    a_ak = [jnp.where(strict, p[:rows, rows:], 0.0).astype(BF16) for p in pm]
    a_rb = [jnp.where(incl, p[rows:, :rows], 0.0).astype(BF16) for p in pm]
    a_rk = [jnp.where(incl, p[rows:, rows:], 0.0).astype(BF16) for p in pm]
    t_inv = [eye + n for n in n_ab]
    n_pow = [n.astype(BF16) for n in n_ab]
    for _ in range(int(math.log2(CHUNK)) - 1):
        n_pow = [_bdot(n, n).astype(BF16) for n in n_pow]
        t_inv = [t + _bdot(t, n) for t, n in zip(t_inv, n_pow)]
    av = [_bdot(cat0(a_ak[i], a_rk[i]), ve[i]) for i in range(len(probs))]
    y2 = [_bdot(t_inv[i], jnp.concatenate([ae[i], av[i][:rows].astype(BF16)], axis=1))
          for i in range(len(probs))]
    st = [state[b, gi] for b, gi in probs]
    x2 = [_bdot_nt(cat0(y2[i][:, :GROUP_W].astype(BF16), re_[i]), st[i]) for i in range(len(probs))]
    ue = [x2[i][:rows] + y2[i][:, GROUP_W:] for i in range(len(probs))]
    oe = [x2[i][rows:] + _bdot(a_rb[i], ue[i]) + av[i][rows:] for i in range(len(probs))]
    for i, (b, gi) in enumerate(probs):
        state[b, gi] = (st[i] * decay_all[b][:, gi * GROUP_W:(gi + 1) * GROUP_W]
                        + _bdot_tn(cat0(ue[i].astype(BF16), ve[i]), cat0(bee[i], kee[i])))
    o = stack([jnp.concatenate([_collapse(oe[b * ngroup + gi]) for gi in range(ngroup)], axis=1)
               for b in range(nb)])

    mean = _split_dot(o, ones) * (1.0 / RWKV_HD)
    cen = o - mean
    var = _split_dot(cen * cen, ones) * (1.0 / RWKV_HD)
    o = cen * lax.rsqrt(var + RWKV_GN_EPS) * lnw[...] + lnb[...]
    o = ((o + bonus) * gate).astype(BF16)
    for b in range(nb):
        o_ref[b] = o[b * CHUNK:(b + 1) * CHUNK]


def _rwkv_mixer(z3, v_first, p):
    b, s = z3.shape[0], z3.shape[1]
    has_vres = v_first is not None
    nb = RWKV_SEQS_PER_STEP if b % RWKV_SEQS_PER_STEP == 0 else 1
    row = lambda w, c: pl.BlockSpec((nb, CHUNK, w), lambda i, j: (i, j, c))
    full = lambda a: pl.BlockSpec(a.shape, lambda i, j: (0,) * a.ndim)
    ins = [z3, z3, z3, z3]
    specs = [row(3 * BRANCH, COL_RWKV // (3 * BRANCH)), row(RWKV_GATE_LORA, COL_RW_G // RWKV_GATE_LORA),
             row(LANE, COL_RW_W // LANE), row(LANE, COL_RW_A // LANE)]
    if has_vres:
        ins.append(v_first)
        specs.append(row(BRANCH, 0))
    names = ["mu_rkv", "mu_g", "mu_w", "mu_a", "w0", "w2", "a0", "a2", "g2", "kk", "ka", "rk", "lnw", "lnb"]
    if has_vres:
        names += ["v0", "v1", "v2"]
    for nme in names:
        ins.append(p[nme])
        specs.append(full(p[nme]))
    o_spec = pl.BlockSpec((nb, CHUNK, BRANCH), lambda i, j: (i, j, 0))
    o_shape = jax.ShapeDtypeStruct((b, s, BRANCH), BF16)
    if has_vres:
        out_shape, out_specs = o_shape, o_spec
    else:
        out_shape = (o_shape, jax.ShapeDtypeStruct((b, s, BRANCH), F32))
        out_specs = (o_spec, pl.BlockSpec((nb, CHUNK, BRANCH), lambda i, j: (i, j, 0)))
    res = pl.pallas_call(
        functools.partial(_rwkv_kernel, has_vres=has_vres, nb=nb),
        out_shape=out_shape,
        grid=(b // nb, s // CHUNK),
        in_specs=specs,
        out_specs=out_specs,
        scratch_shapes=[pltpu.VMEM((nb, 8, 3 * BRANCH), F32), pltpu.VMEM((nb, 8, RWKV_GATE_LORA), F32),
                        pltpu.VMEM((nb, 8, LANE), F32), pltpu.VMEM((nb, 8, LANE), F32),
                        pltpu.VMEM((nb, RWKV_HEADS // HEAD_GROUP, GROUP_W, GROUP_W), F32)],
        compiler_params=_cparams(("parallel", "arbitrary")),
        name="rwkv7",
    )(*ins)
    if has_vres:
        return res, v_first
    return res[0], res[1]


def _pad_rows(w, rows, offset=0):
    return jnp.zeros((rows, w.shape[1]), F32).at[offset:offset + w.shape[0]].set(w)


def _pack_w_in(w):
    off = 0
    conv = w[:, 0:1536]
    gla = w[:, 1536:3072]
    gla_lr = w[:, 3072:3088]
    diff = w[:, 3088:4624]
    rw_rkv = w[:, 4624:6160]
    rw_w = w[:, 6160:6256]
    rw_a = w[:, 6256:6352]
    rw_g = w[:, 6352:6608]
    z = lambda n: jnp.zeros((w.shape[0], n), w.dtype)
    del off
    packed = jnp.concatenate([conv, gla, diff, rw_rkv, rw_g, rw_w, z(32), rw_a, gla_lr, z(16)], axis=1)
    return packed.astype(BF16), w[:, 6608:].astype(BF16)


def _pad_lanes(v, width, offset=0):
    return jnp.zeros((1, width), F32).at[0, offset:offset + v.shape[0]].set(v)


def kernel(x, norm_mix_pre, w_in, conv_w, gla_wa2, gla_ba, gla_norm, diff_lq1, diff_lk1, diff_lq2, diff_lk2,
           diff_norm, rw_mu, rw_w0, rw_w2, rw_a0, rw_a2, rw_g2, rw_kk, rw_ka, rw_rk, rw_lnw, rw_lnb, rw_v0,
           rw_v1, rw_v2, w_branch, w_out, norm_mix_post, norm_ffn_pre, w_gate, w_up, w_down, norm_ffn_post):
    bsz, seq = x.shape[0], x.shape[1]
    depth = w_in.shape[0]
    t = bsz * seq
    x2 = x.reshape(t, D_MODEL)
    row = lambda v: v.reshape(1, -1)
    v_first = None
    h = _norm(x2, row(norm_mix_pre[0]))
    for l in range(depth):
        w_mix, w_gates = _pack_w_in(w_in[l])
        z3 = _inproj(h, w_mix).reshape(bsz, seq, N_MIX)
        lambda_init = 0.8 - 0.6 * math.exp(-0.3 * l)

        o_conv = _conv_mixer(z3, conv_w[l])
        o_gla = _gla_mixer(z3, _pad_rows(gla_wa2[l], LANE, GLA_WLR_LANE), row(gla_ba[l]), row(gla_norm[l]))
        o_diff = _diff_mixer(z3, row(diff_lq1[l]), row(diff_lk1[l]), row(diff_lq2[l]),
                             row(diff_lk2[l]), row(diff_norm[l]), lambda_init)
        mu = rw_mu[l]
        rp = {
            "mu_rkv": row(mu[0:1536]),
            "mu_w": _pad_lanes(mu[1536:1632], LANE),
            "mu_a": _pad_lanes(mu[1632:1728], LANE),
            "mu_g": row(mu[1728:1984]),
            "w0": row(rw_w0[l]), "w2": _pad_rows(rw_w2[l], LANE),
            "a0": row(rw_a0[l]), "a2": _pad_rows(rw_a2[l], LANE),
            "g2": rw_g2[l], "kk": row(rw_kk[l]), "ka": row(rw_ka[l]), "rk": row(rw_rk[l]),
            "lnw": row(rw_lnw[l]), "lnb": row(rw_lnb[l]),
        }
        if l > 0:
            rp.update(v0=row(rw_v0[l - 1]), v1=rw_v1[l - 1], v2=rw_v2[l - 1])
        o_rwkv, v_first = _rwkv_mixer(z3, v_first if l > 0 else None, rp)

        outs = [o.reshape(t, BRANCH) for o in (o_conv, o_gla, o_diff, o_rwkv)]
        merged = _merge(h, w_gates, outs, w_branch[l].astype(BF16))
        x2, h_ffn = _proj_norm_res(merged, w_out[l].astype(BF16), x2, row(norm_mix_post[l]),
                                   row(norm_ffn_pre[l]), D_MODEL)
        act = _ffn_up(h_ffn, w_gate[l].astype(BF16), w_up[l].astype(BF16))
        next_gain = row(norm_mix_pre[l + 1]) if l + 1 < depth else None
        x2, h = _proj_norm_res(act, w_down[l].astype(BF16), x2, row(norm_ffn_post[l]), next_gain, FFN_DOWN_TK)
    return x2.reshape(bsz, seq, D_MODEL)
```

```python
import functools
import math

import jax
import jax.numpy as jnp
from jax import lax
from jax.experimental import pallas as pl
from jax.experimental.pallas import tpu as pltpu

F32 = jnp.float32
BF16 = jnp.bfloat16

D_MODEL = 2048
CHUNK = 64
N_BRANCH = 4
BRANCH = 512
GLA_HEADS = 4
GLA_DK = 64
GLA_DV = 128
GLA_KEY = GLA_HEADS * GLA_DK
GLA_LOW_RANK = 16
GLA_LOGIT_NORM = 16.0
DIFF_HEADS = 4
DIFF_D = 64
RWKV_HEADS = 8
RWKV_HD = 64
RWKV_DECAY_LORA = 96
RWKV_AAA_LORA = 96
RWKV_MV_LORA = 64
RWKV_GATE_LORA = 256
D_FF = 5632
RMS_EPS = 1e-6
HEAD_EPS = 1e-5
RWKV_GN_EPS = 64e-5
NEG_INF = -1e30
LOG2E = 1.4426950408889634

COL_CONV = 0
COL_GLA = 1536
COL_DIFF = 3072
COL_RWKV = 4608
COL_RW_G = 6144
COL_RW_W = 6400
COL_RW_A = 6528
GLA_WLR_LANE = RWKV_AAA_LORA
N_MIX = 6656
LANE = 128
HEAD_GROUP = 4
GROUP_W = HEAD_GROUP * 64
RWKV_SEQS_PER_STEP = 4

VMEM_LIMIT = 56 * 1024 * 1024
PROJ_ROW_SPLIT = 4
FFN_DOWN_TK = D_FF // 2


def _tile(n, pref):
    t = min(n, pref)
    while n % t:
        t -= 8
    return t


def _cparams(sem):
    return pltpu.CompilerParams(dimension_semantics=sem, vmem_limit_bytes=VMEM_LIMIT)


def _bdot(a, b):
    return jnp.dot(a.astype(BF16), b.astype(BF16), preferred_element_type=F32)


def _bdot_nt(a, b):
    return lax.dot_general(a.astype(BF16), b.astype(BF16), (((1,), (1,)), ((), ())),
                           preferred_element_type=F32)


def _bdot_tn(a, b):
    return lax.dot_general(a.astype(BF16), b.astype(BF16), (((0,), (0,)), ((), ())),
                           preferred_element_type=F32)


def _split_dot(x, ones_bf16):
    hi = x.astype(BF16)
    lo = (x - hi.astype(F32)).astype(BF16)
    return (jnp.dot(hi, ones_bf16, preferred_element_type=F32)
            + jnp.dot(lo, ones_bf16, preferred_element_type=F32))


def _split_dot_left(ones_bf16, x):
    hi = x.astype(BF16)
    lo = (x - hi.astype(F32)).astype(BF16)
    return (jnp.dot(ones_bf16, hi, preferred_element_type=F32)
            + jnp.dot(ones_bf16, lo, preferred_element_type=F32))


def _sigmoid(x):
    return 1.0 / (1.0 + jnp.exp(-x))


def _rms(x, gain, eps):
    return x * lax.rsqrt(jnp.mean(x * x, axis=-1, keepdims=True) + eps) * gain


def _iota2(shape, dim):
    return lax.broadcasted_iota(jnp.int32, shape, dim)


def _tril_incl(n):
    return (_iota2((n, n), 0) >= _iota2((n, n), 1)).astype(BF16)


def _head_ones(n, width):
    return ((_iota2((n, n), 0) // width) == (_iota2((n, n), 1) // width)).astype(BF16)


def _expand(x, col_group):
    rows = HEAD_GROUP * CHUNK
    xt = jnp.concatenate([x] * HEAD_GROUP, axis=0)
    keep = (_iota2((rows, x.shape[1]), 0) // CHUNK) == (_iota2((rows, x.shape[1]), 1) // col_group)
    return jnp.where(keep, xt, 0.0)


def _collapse(xe):
    out = xe[0:CHUNK]
    for h in range(1, HEAD_GROUP):
        out = out + xe[h * CHUNK:(h + 1) * CHUNK]
    return out


def _norm_kernel(x_ref, g_ref, h_ref):
    h_ref[...] = _rms(x_ref[...], g_ref[...], RMS_EPS).astype(BF16)


def _norm(x2, gain):
    t = x2.shape[0]
    tm = _tile(t, 512)
    return pl.pallas_call(
        _norm_kernel,
        out_shape=jax.ShapeDtypeStruct((t, D_MODEL), BF16),
        grid=(t // tm,),
        in_specs=[pl.BlockSpec((tm, D_MODEL), lambda i: (i, 0)),
                  pl.BlockSpec((1, D_MODEL), lambda i: (0, 0))],
        out_specs=pl.BlockSpec((tm, D_MODEL), lambda i: (i, 0)),
        compiler_params=_cparams(("parallel",)),
        name="norm",
    )(x2, gain)


def _inproj_kernel(h_ref, w_ref, z_ref):
    z_ref[...] = jnp.dot(h_ref[...], w_ref[...], preferred_element_type=F32).astype(BF16)


def _inproj(h, w):
    t, n = h.shape[0], w.shape[1]
    tm, tn = _tile(t, 1024), _tile(n, 1664)
    return pl.pallas_call(
        _inproj_kernel,
        out_shape=jax.ShapeDtypeStruct((t, n), BF16),
        grid=(t // tm, n // tn),
        in_specs=[pl.BlockSpec((tm, D_MODEL), lambda i, j: (i, 0)),
                  pl.BlockSpec((D_MODEL, tn), lambda i, j: (0, j))],
        out_specs=pl.BlockSpec((tm, tn), lambda i, j: (i, j)),
        compiler_params=_cparams(("parallel", "arbitrary")),
        name="inproj",
    )(h, w)


def _merge_kernel(h_ref, wg0, wg1, wg2, wg3, o0, o1, o2, o3, p_ref, out_ref):
    h = h_ref[...]
    acc = None
    for n, (wg, o) in enumerate(((wg0, o0), (wg1, o1), (wg2, o2), (wg3, o3))):
        gate = _sigmoid(jnp.dot(h, wg[...], preferred_element_type=F32))
        term = gate * jnp.dot(o[...], p_ref[n], preferred_element_type=F32)
        acc = term if acc is None else acc + term
    out_ref[...] = acc.astype(BF16)


def _merge(h, w_gate, outs, w_branch, l):
    t = h.shape[0]
    tm, tn = _tile(t, 1024), 512
    nj = D_MODEL // tn
    gate_specs = [pl.BlockSpec((D_MODEL, tn), functools.partial(lambda i, j, n: (0, n * nj + j), n=n))
                  for n in range(N_BRANCH)]
    o_specs = [pl.BlockSpec((tm, BRANCH), lambda i, j: (i, 0)) for _ in range(N_BRANCH)]
    return pl.pallas_call(
        _merge_kernel,
        out_shape=jax.ShapeDtypeStruct((t, D_MODEL), BF16),
        grid=(t // tm, nj),
        in_specs=[pl.BlockSpec((tm, D_MODEL), lambda i, j: (i, 0))] + gate_specs + o_specs
                 + [pl.BlockSpec((None, N_BRANCH, BRANCH, tn), lambda i, j: (l, 0, 0, j))],
        out_specs=pl.BlockSpec((tm, tn), lambda i, j: (i, j)),
        compiler_params=_cparams(("parallel", "arbitrary")),
        name="merge",
    )(h, w_gate, w_gate, w_gate, w_gate, *outs, w_branch)


def _proj_norm_res_kernel(a_ref, w_ref, x_ref, g_ref, gn_ref, o_ref, *h_ref, nk):
    k = pl.program_id(1)

    def accumulate(first):
        part = jnp.dot(a_ref[...], w_ref[...], preferred_element_type=F32)
        o_ref[...] = part if first else o_ref[...] + part

    def finish():
        sub = o_ref.shape[0] // PROJ_ROW_SPLIT
        for r in range(PROJ_ROW_SPLIT):
            rows = slice(r * sub, (r + 1) * sub)
            y = jnp.dot(a_ref[rows, :], w_ref[...], preferred_element_type=F32)
            if nk > 1:
                y = y + o_ref[rows, :]
            xn = x_ref[rows, :] + _rms(y, g_ref[...], RMS_EPS)
            o_ref[rows, :] = xn
            if h_ref:
                h_ref[0][rows, :] = _rms(xn, gn_ref[...], RMS_EPS).astype(BF16)

    if nk == 1:
        finish()
    else:
        pl.when(k == 0)(functools.partial(accumulate, True))
        if nk > 2:
            pl.when((k > 0) & (k < nk - 1))(functools.partial(accumulate, False))
        pl.when(k == nk - 1)(finish)


def _proj_norm_res(a, w, l, x2, gain, next_gain, tk):
    t, kdim = a.shape
    tm = _tile(t, 512)
    emit_h = next_gain is not None
    row_spec = pl.BlockSpec((tm, D_MODEL), lambda i, k: (i, 0))
    vec_spec = pl.BlockSpec((1, D_MODEL), lambda i, k: (0, 0))
    out_shape = [jax.ShapeDtypeStruct((t, D_MODEL), F32)]
    if emit_h:
        out_shape.append(jax.ShapeDtypeStruct((t, D_MODEL), BF16))
    assert kdim % tk == 0 and tm % (8 * PROJ_ROW_SPLIT) == 0
    res = pl.pallas_call(
        functools.partial(_proj_norm_res_kernel, nk=kdim // tk),
        out_shape=tuple(out_shape),
        grid=(t // tm, kdim // tk),
        in_specs=[pl.BlockSpec((tm, tk), lambda i, k: (i, k)),
                  pl.BlockSpec((None, tk, D_MODEL), lambda i, k: (l, k, 0)),
                  row_spec, vec_spec, vec_spec],
        out_specs=tuple([row_spec] * len(out_shape)),
        compiler_params=_cparams(("parallel", "arbitrary")),
        name="proj_norm_res",
    )(a, w, x2, gain, next_gain if emit_h else gain)
    return (res[0], res[1]) if emit_h else (res[0], None)


def _ffn_up_kernel(h_ref, wg_ref, wu_ref, a_ref):
    h = h_ref[...]
    gt = jnp.dot(h, wg_ref[...], preferred_element_type=F32)
    up = jnp.dot(h, wu_ref[...], preferred_element_type=F32)
    a_ref[...] = (gt * _sigmoid(gt) * up).astype(BF16)


def _ffn_up(h, wg, wu, l):
    t = h.shape[0]
    tm, tn = _tile(t, 1024), 512
    return pl.pallas_call(
        _ffn_up_kernel,
        out_shape=jax.ShapeDtypeStruct((t, D_FF), BF16),
        grid=(t // tm, D_FF // tn),
        in_specs=[pl.BlockSpec((tm, D_MODEL), lambda i, j: (i, 0)),
                  pl.BlockSpec((None, D_MODEL, tn), lambda i, j: (l, 0, j)),
                  pl.BlockSpec((None, D_MODEL, tn), lambda i, j: (l, 0, j))],
        out_specs=pl.BlockSpec((tm, tn), lambda i, j: (i, j)),
        compiler_params=_cparams(("parallel", "arbitrary")),
        name="ffn_up",
    )(h, wg, wu)


def _conv_kernel(b_ref, c_ref, u_ref, w_ref, o_ref, carry):
    @pl.when(pl.program_id(1) == 0)
    def _():
        carry[...] = jnp.zeros_like(carry)

    cu = c_ref[0].astype(F32) * u_ref[0].astype(F32)
    ts = cu.shape[0]
    row = _iota2(cu.shape, 0)
    p1 = carry[7:8, :]
    p2 = carry[6:7, :]
    s1 = jnp.where(row == 0, p1, pltpu.roll(cu, 1, axis=0))
    s2 = jnp.where(row == 0, p2, jnp.where(row == 1, p1, pltpu.roll(cu, 2, axis=0)))
    w = w_ref[...]
    y = w[2:3, :] * cu + w[1:2, :] * s1 + w[0:1, :] * s2
    o_ref[0] = (b_ref[0].astype(F32) * y).astype(BF16)
    carry[...] = cu[ts - 8:, :]


def _conv_mixer(z3, conv_w):
    b, s = z3.shape[0], z3.shape[1]
    ts = _tile(s, 512)
    c0 = COL_CONV // BRANCH
    return pl.pallas_call(
        _conv_kernel,
        out_shape=jax.ShapeDtypeStruct((b, s, BRANCH), BF16),
        grid=(b, s // ts),
        in_specs=[pl.BlockSpec((1, ts, BRANCH), lambda i, j: (i, j, c0)),
                  pl.BlockSpec((1, ts, BRANCH), lambda i, j: (i, j, c0 + 1)),
                  pl.BlockSpec((1, ts, BRANCH), lambda i, j: (i, j, c0 + 2)),
                  pl.BlockSpec((3, BRANCH), lambda i, j: (0, 0))],
        out_specs=pl.BlockSpec((1, ts, BRANCH), lambda i, j: (i, j, 0)),
        scratch_shapes=[pltpu.VMEM((8, BRANCH), F32)],
        compiler_params=_cparams(("parallel", "arbitrary")),
        name="conv",
    )(z3, z3, z3, conv_w)


def _gla_kernel(q_ref, k_ref, v_ref, g_ref, wl_ref, wa2_ref, ba_ref, gn_ref, o_ref, state):
    @pl.when(pl.program_id(1) == 0)
    def _():
        state[...] = jnp.zeros_like(state)

    nb = q_ref.shape[0]
    rows = HEAD_GROUP * CHUNK
    causal = (_iota2((rows, rows), 0) % CHUNK) >= (_iota2((rows, rows), 1) % CHUNK)
    tril = _tril_incl(CHUNK)
    q_in, k_in, q_st, k_st, ve, dec = [], [], [], [], [], []
    for b in range(nb):
        q = q_ref[b].astype(F32) * (GLA_DK ** -0.5)
        k = k_ref[b].astype(F32)
        z = _bdot(wl_ref[b], wa2_ref[...]) + ba_ref[...]
        log_a = (jnp.minimum(z, 0.0) - jnp.log(1.0 + jnp.exp(-jnp.abs(z)))) * (1.0 / GLA_LOGIT_NORM)
        cum = _split_dot_left(tril, log_a)
        mid = cum[CHUNK // 2 - 1:CHUNK // 2, :]
        last = cum[CHUNK - 1:CHUNK, :]
        q_in.append(_expand(q * jnp.exp(cum - mid), GLA_DK).astype(BF16))
        k_in.append(_expand(k * jnp.exp(mid - cum), GLA_DK).astype(BF16))
        q_st.append(_expand(q * jnp.exp(cum), GLA_DK).astype(BF16))
        k_st.append(_expand(k * jnp.exp(last - cum), GLA_DK).astype(BF16))
        ve.append(_expand(v_ref[b].astype(F32), GLA_DV).astype(BF16))
        dec.append(jnp.exp(last))
    scores = [jnp.where(causal, _bdot_nt(q_in[b], k_in[b]), 0.0).astype(BF16) for b in range(nb)]
    st = [state[b] for b in range(nb)]
    oe = [_bdot(scores[b], ve[b]) + _bdot_nt(q_st[b], st[b]) for b in range(nb)]
    for b in range(nb):
        state[b] = st[b] * dec[b] + _bdot_tn(ve[b], k_st[b])
    gn = gn_ref[...]
    for b in range(nb):
        o = _collapse(oe[b])
        g = g_ref[b].astype(F32)
        gate = g * _sigmoid(g)
        outs = [_rms(o[:, h * GLA_DV:(h + 1) * GLA_DV], gn, HEAD_EPS) for h in range(GLA_HEADS)]
        o_ref[b] = (jnp.concatenate(outs, axis=1) * gate).astype(BF16)


def _gla_mixer(z3, wa2p, ba, gn):
    b, s = z3.shape[0], z3.shape[1]
    cq = COL_GLA // GLA_KEY
    cv = (COL_GLA + 2 * GLA_KEY) // BRANCH
    cw = COL_RW_A // LANE
    nb = RWKV_SEQS_PER_STEP if b % RWKV_SEQS_PER_STEP == 0 else 1
    return pl.pallas_call(
        _gla_kernel,
        out_shape=jax.ShapeDtypeStruct((b, s, BRANCH), BF16),
        grid=(b // nb, s // CHUNK),
        in_specs=[pl.BlockSpec((nb, CHUNK, GLA_KEY), lambda i, j: (i, j, cq)),
                  pl.BlockSpec((nb, CHUNK, GLA_KEY), lambda i, j: (i, j, cq + 1)),
                  pl.BlockSpec((nb, CHUNK, BRANCH), lambda i, j: (i, j, cv)),
                  pl.BlockSpec((nb, CHUNK, BRANCH), lambda i, j: (i, j, cv + 1)),
                  pl.BlockSpec((nb, CHUNK, LANE), lambda i, j: (i, j, cw)),
                  pl.BlockSpec((LANE, GLA_KEY), lambda i, j: (0, 0)),
                  pl.BlockSpec((1, GLA_KEY), lambda i, j: (0, 0)),
                  pl.BlockSpec((1, GLA_DV), lambda i, j: (0, 0))],
        out_specs=pl.BlockSpec((nb, CHUNK, BRANCH), lambda i, j: (i, j, 0)),
        scratch_shapes=[pltpu.VMEM((nb, GLA_HEADS * GLA_DV, GLA_KEY), F32)],
        compiler_params=_cparams(("parallel", "arbitrary")),
        name="gla",
    )(z3, z3, z3, z3, z3, wa2p, ba, gn)


VT_ROWS = 2 * DIFF_D + 16


def _alibi_slope(h):
    return 2.0 ** (-8.0 * (h + 1) / DIFF_HEADS)


def _diff_kernel(q_ref, k_ref, v_ref, lq1, lk1, lq2, lk2, gn_ref, o_ref, ka_scr, vb_scr, qa_scr, acc_scr,
                 m_scr, s_scr, rel_scr, *, lambda_init, tq):
    qi = pl.program_id(1)
    hw = 2 * DIFF_D
    seq = k_ref.shape[1]
    lane = _iota2((tq, hw), 1)

    @pl.when(qi == 0)
    def _():
        key_row = _iota2((tq, hw), 0).astype(F32)
        for h in range(DIFF_HEADS):
            k_aug = []
            for half in range(2):
                key_bias = (_alibi_slope(h) * LOG2E) * (key_row + float(half * tq))
                hi = key_bias.astype(BF16).astype(F32)
                mid = (key_bias - hi).astype(BF16).astype(F32)
                lo = key_bias - hi - mid
                aug = jnp.where(lane == 0, hi, jnp.where(lane == 1, mid, jnp.where(lane == 2, lo, 0.0)))
                k_aug.append(aug.astype(BF16))
            for c in range(seq // tq):
                rows = slice(c * tq, (c + 1) * tq)
                ka_scr[h, rows, 0:hw] = k_ref[0, rows, h * hw:(h + 1) * hw].astype(BF16)
                ka_scr[h, rows, hw:2 * hw] = k_aug[c % 2]
        k_row = _iota2((2 * tq, 2 * tq), 0)
        col = _iota2((2 * tq, 2 * tq), 1)
        for par in range(2):
            q_row = jnp.where(col >= tq, col - tq, col) + par * tq
            d = q_row - k_row
            rel_scr[par] = jnp.where((k_row // CHUNK) <= (q_row // CHUNK), (d - jnp.abs(d)).astype(F32), NEG_INF)
        sub = _iota2((VT_ROWS - hw, tq), 0)
        tail = jnp.where(sub == 0, 1.0, 0.0).astype(BF16)
        for c in range(seq // tq):
            rows = slice(c * tq, (c + 1) * tq)
            for h in range(DIFF_HEADS):
                vb_scr[h, 0:hw, rows] = v_ref[0, rows, h * hw:(h + 1) * hw].astype(F32).T.astype(BF16)
                vb_scr[h, hw:VT_ROWS, rows] = tail

    lane2 = _iota2((2 * tq, hw), 1)
    ones_aug = (lane2 < 3).astype(F32)
    for h in range(DIFF_HEADS):
        q = q_ref[0, :, h * hw:(h + 1) * hw].astype(F32) * (DIFF_D ** -0.5 * LOG2E)
        q2 = jnp.concatenate([jnp.where(lane < DIFF_D, q, 0.0), jnp.where(lane < DIFF_D, 0.0, q)],
                             axis=0)
        qa_scr[h] = jnp.concatenate([q2, ones_aug], axis=1).astype(BF16)
    acc_scr[...] = jnp.zeros_like(acc_scr)
    m_scr[...] = jnp.full_like(m_scr, NEG_INF)

    tk = 2 * tq
    last_tile = qi // 2

    def tile_scores(j):
        start = pl.multiple_of(j * tk, tk)
        return [lax.dot_general(ka_scr[h, pl.ds(start, tk), :], qa_scr[h], (((1,), (1,)), ((), ())),
                                preferred_element_type=F32) for h in range(DIFF_HEADS)]

    def update(scores, j):
        start = pl.multiple_of(j * tk, tk)
        alphas, probs = [], []
        for h, s in enumerate(scores):
            m_old = m_scr[h]
            m_new = jnp.maximum(m_old, jnp.max(s, axis=0, keepdims=True))
            alphas.append(jnp.exp2(m_old - m_new))
            probs.append(jnp.exp2((s - m_new).astype(BF16)))
            m_scr[h] = m_new - _alibi_slope(h) * LOG2E * tk
        pvs = [jnp.dot(vb_scr[h, :, pl.ds(start, tk)], probs[h], preferred_element_type=F32)
               for h in range(DIFF_HEADS)]
        for h in range(DIFF_HEADS):
            acc_scr[h] = alphas[h] * acc_scr[h] + pvs[h]

    for h, s in enumerate(tile_scores(0)):
        s_scr[h] = s

    def body(j, carry):
        nxt = tile_scores(j + 1)
        update([s_scr[h] for h in range(DIFF_HEADS)], j)
        for h in range(DIFF_HEADS):
            s_scr[h] = nxt[h]
        return carry

    lax.fori_loop(0, last_tile, body, 0)

    rel = rel_scr[qi % 2]
    lam = (jnp.exp(jnp.sum(lq1[...] * lk1[...], axis=-1, keepdims=True))
           - jnp.exp(jnp.sum(lq2[...] * lk2[...], axis=-1, keepdims=True))
           + lambda_init)
    update([s_scr[h] + (_alibi_slope(h) * LOG2E) * rel for h in range(DIFF_HEADS)], last_tile)
    for h in range(DIFF_HEADS):
        on = acc_scr[h, 0:hw, :] / acc_scr[h, hw:hw + 1, :]
        o = (on[:, :tq] - lam * on[:, tq:]).T
        o_ref[0, :, h * hw:(h + 1) * hw] = (_rms(o, gn_ref[...], HEAD_EPS) * (1.0 - lambda_init)).astype(BF16)


def _diff_mixer(z3, lq1, lk1, lq2, lk2, gn, lambda_init):
    b, s = z3.shape[0], z3.shape[1]
    tq = _tile(s // 2, 256)
    assert s % (2 * tq) == 0 and tq % CHUNK == 0
    hw = 2 * DIFF_D
    cq = COL_DIFF // BRANCH
    nh = DIFF_HEADS
    vec = pl.BlockSpec((1, DIFF_D), lambda i, j: (0, 0))
    return pl.pallas_call(
        functools.partial(_diff_kernel, lambda_init=lambda_init, tq=tq),
        out_shape=jax.ShapeDtypeStruct((b, s, BRANCH), BF16),
        grid=(b, s // tq),
        in_specs=[pl.BlockSpec((1, tq, BRANCH), lambda i, j: (i, j, cq)),
                  pl.BlockSpec((1, s, BRANCH), lambda i, j: (i, 0, cq + 1), pipeline_mode=pl.Buffered(1)),
                  pl.BlockSpec((1, s, BRANCH), lambda i, j: (i, 0, cq + 2), pipeline_mode=pl.Buffered(1)),
                  vec, vec, vec, vec,
                  pl.BlockSpec((1, hw), lambda i, j: (0, 0))],
        out_specs=pl.BlockSpec((1, tq, BRANCH), lambda i, j: (i, j, 0)),
        scratch_shapes=[pltpu.VMEM((nh, s, 2 * hw), BF16), pltpu.VMEM((nh, VT_ROWS, s), BF16),
                        pltpu.VMEM((nh, 2 * tq, 2 * hw), BF16), pltpu.VMEM((nh, VT_ROWS, 2 * tq), F32),
                        pltpu.VMEM((nh, 1, 2 * tq), F32), pltpu.VMEM((nh, 2 * tq, 2 * tq), F32),
                        pltpu.VMEM((2, 2 * tq, 2 * tq), F32)],
        compiler_params=_cparams(("parallel", "arbitrary")),
        name="diff_attn",
    )(z3, z3, z3, lq1, lk1, lq2, lk2, gn)


def _shift(x, prev_rows):
    row = _iota2(x.shape, 0)
    return jnp.where(row == 0, prev_rows[7:8, :], pltpu.roll(x, 1, axis=0))


def _rwkv_kernel(*refs, has_vres, nb):
    if has_vres:
        (rkv_ref, gl_ref, wb_ref, ab_ref, vf_ref, mu_rkv, mu_g, mu_w, mu_a, w0, w2, a0, a2, g2, kk_s, ka_s,
         rk_s, lnw, lnb, v0, v1, v2, o_ref, p_rkv, p_g, p_w, p_a, state) = refs
    else:
        (rkv_ref, gl_ref, wb_ref, ab_ref, mu_rkv, mu_g, mu_w, mu_a, w0, w2, a0, a2, g2, kk_s, ka_s,
         rk_s, lnw, lnb, o_ref, vf_out, p_rkv, p_g, p_w, p_a, state) = refs

    @pl.when(pl.program_id(1) == 0)
    def _():
        for r in (p_rkv, p_g, p_w, p_a, state):
            r[...] = jnp.zeros_like(r)

    def stack(parts):
        return parts[0] if nb == 1 else jnp.concatenate(parts, axis=0)

    def mixed(x_ref, p_ref, mu_ref):
        parts = []
        for b in range(nb):
            x = x_ref[b].astype(F32)
            parts.append(x + (_shift(x, p_ref[b]) - x) * mu_ref[...])
            p_ref[b] = x[CHUNK - 8:, :]
        return stack(parts)

    rkv = mixed(rkv_ref, p_rkv, mu_rkv)
    g_lr = mixed(gl_ref, p_g, mu_g)
    w_lr = mixed(wb_ref, p_w, mu_w)
    a_lr = mixed(ab_ref, p_a, mu_a)
    r = rkv[:, 0:BRANCH]
    k = rkv[:, BRANCH:2 * BRANCH]
    v = rkv[:, 2 * BRANCH:3 * BRANCH]
    nrow = nb * CHUNK

    y = w0[...] + _bdot(jnp.tanh(w_lr), w2[...])
    softplus = jnp.maximum(-y, 0.0) + jnp.log(1.0 + jnp.exp(-jnp.abs(y)))
    lw = -jnp.exp(-softplus - 0.5)
    a = _sigmoid(a0[...] + _bdot(a_lr, a2[...]))
    gate = _bdot(_sigmoid(g_lr), g2[...])
    if has_vres:
        vf = stack([vf_ref[b] for b in range(nb)])
        v = v + (vf - v) * _sigmoid(v0[...] + _bdot(_bdot(v, v1[...]), v2[...]))
    else:
        for b in range(nb):
            vf_out[b] = v[b * CHUNK:(b + 1) * CHUNK]

    ones = _head_ones(BRANCH, RWKV_HD)
    kk = k * kk_s[...]
    k = k * (1.0 + (a - 1.0) * ka_s[...])
    sums = _bdot(jnp.concatenate([kk * kk, r * k * rk_s[...]], axis=0), ones)
    kk = kk / jnp.maximum(jnp.sqrt(sums[:nrow]), 1e-12)
    bonus = sums[nrow:] * v

    ri, ci = _iota2((nrow, nrow), 0), _iota2((nrow, nrow), 1)
    tril = ((ri // CHUNK == ci // CHUNK) & (ri >= ci)).astype(BF16)
    cum = _split_dot_left(tril, lw)
    lasts = [cum[(b + 1) * CHUNK - 1:(b + 1) * CHUNK, :] for b in range(nb)]
    to_end = stack([jnp.exp(lasts[b] - cum[b * CHUNK:(b + 1) * CHUNK]) for b in range(nb)])
    decay_all = [jnp.exp(last) for last in lasts]
    a_t = -kk * jnp.exp(cum - lw)
    r_t = r * jnp.exp(cum)
    inv = jnp.exp(-cum)
    kka = kk * a
    b_s = kka * inv
    k_s = k * inv
    b_e = kka * to_end
    k_e = k * to_end

    rows = HEAD_GROUP * CHUNK
    ri = _iota2((rows, rows), 0) % CHUNK
    ci = _iota2((rows, rows), 1) % CHUNK
    strict = ri > ci
    incl = ri >= ci
    eye = (_iota2((rows, rows), 0) == _iota2((rows, rows), 1)).astype(F32)

    ngroup = RWKV_HEADS // HEAD_GROUP
    probs = [(b, gi) for b in range(nb) for gi in range(ngroup)]

    def ex(t):
        return [_expand(t[b * CHUNK:(b + 1) * CHUNK, gi * GROUP_W:(gi + 1) * GROUP_W], RWKV_HD).astype(BF16)
                for b, gi in probs]

    cat0 = lambda x, y_: jnp.concatenate([x, y_], axis=0)
    ae, re_, be, ke, bee, kee, ve = ex(a_t), ex(r_t), ex(b_s), ex(k_s), ex(b_e), ex(k_e), ex(v)
    pm = [_bdot_nt(cat0(ae[i], re_[i]), cat0(be[i], ke[i])) for i in range(len(probs))]
    n_ab = [jnp.where(strict, p[:rows, :rows], 0.0) for p in pm]
    a_ak = [jnp.where(strict, p[:rows, rows:], 0.0).astype(BF16) for p in pm]
    a_rb = [jnp.where(incl, p[rows:, :rows], 0.0).astype(BF16) for p in pm]
    a_rk = [jnp.where(incl, p[rows:, rows:], 0.0).astype(BF16) for p in pm]
    t_inv = [eye + n for n in n_ab]
    n_pow = [n.astype(BF16) for n in n_ab]
    for _ in range(int(math.log2(CHUNK)) - 1):
        n_pow = [_bdot(n, n).astype(BF16) for n in n_pow]
        t_inv = [t + _bdot(t, n) for t, n in zip(t_inv, n_pow)]
    av = [_bdot(cat0(a_ak[i], a_rk[i]), ve[i]) for i in range(len(probs))]
    y2 = [_bdot(t_inv[i], jnp.concatenate([ae[i], av[i][:rows].astype(BF16)], axis=1))
          for i in range(len(probs))]
    st = [state[b, gi] for b, gi in probs]
    x2 = [_bdot_nt(cat0(y2[i][:, :GROUP_W].astype(BF16), re_[i]), st[i]) for i in range(len(probs))]
    ue = [x2[i][:rows] + y2[i][:, GROUP_W:] for i in range(len(probs))]
    oe = [x2[i][rows:] + _bdot(a_rb[i], ue[i]) + av[i][rows:] for i in range(len(probs))]
    for i, (b, gi) in enumerate(probs):
        state[b, gi] = (st[i] * decay_all[b][:, gi * GROUP_W:(gi + 1) * GROUP_W]
                        + _bdot_tn(cat0(ue[i].astype(BF16), ve[i]), cat0(bee[i], kee[i])))
    o = stack([jnp.concatenate([_collapse(oe[b * ngroup + gi]) for gi in range(ngroup)], axis=1)
               for b in range(nb)])

    mean = _bdot(o, ones) * (1.0 / RWKV_HD)
    cen = o - mean
    var = _bdot(cen * cen, ones) * (1.0 / RWKV_HD)
    o = cen * lax.rsqrt(var + RWKV_GN_EPS) * lnw[...] + lnb[...]
    o = ((o + bonus) * gate).astype(BF16)
    for b in range(nb):
        o_ref[b] = o[b * CHUNK:(b + 1) * CHUNK]


def _rwkv_mixer(z3, v_first, p):
    b, s = z3.shape[0], z3.shape[1]
    has_vres = v_first is not None
    nb = RWKV_SEQS_PER_STEP if b % RWKV_SEQS_PER_STEP == 0 else 1
    row = lambda w, c: pl.BlockSpec((nb, CHUNK, w), lambda i, j: (i, j, c))
    full = lambda a: pl.BlockSpec(a.shape, lambda i, j: (0,) * a.ndim)
    ins = [z3, z3, z3, z3]
    specs = [row(3 * BRANCH, COL_RWKV // (3 * BRANCH)), row(RWKV_GATE_LORA, COL_RW_G // RWKV_GATE_LORA),
             row(LANE, COL_RW_W // LANE), row(LANE, COL_RW_A // LANE)]
    if has_vres:
        ins.append(v_first)
        specs.append(row(BRANCH, 0))
    names = ["mu_rkv", "mu_g", "mu_w", "mu_a", "w0", "w2", "a0", "a2", "g2", "kk", "ka", "rk", "lnw", "lnb"]
    if has_vres:
        names += ["v0", "v1", "v2"]
    for nme in names:
        ins.append(p[nme])
        specs.append(full(p[nme]))
    o_spec = pl.BlockSpec((nb, CHUNK, BRANCH), lambda i, j: (i, j, 0))
    o_shape = jax.ShapeDtypeStruct((b, s, BRANCH), BF16)
    if has_vres:
        out_shape, out_specs = o_shape, o_spec
    else:
        out_shape = (o_shape, jax.ShapeDtypeStruct((b, s, BRANCH), F32))
        out_specs = (o_spec, pl.BlockSpec((nb, CHUNK, BRANCH), lambda i, j: (i, j, 0)))
    res = pl.pallas_call(
        functools.partial(_rwkv_kernel, has_vres=has_vres, nb=nb),
        out_shape=out_shape,
        grid=(b // nb, s // CHUNK),
        in_specs=specs,
        out_specs=out_specs,
        scratch_shapes=[pltpu.VMEM((nb, 8, 3 * BRANCH), F32), pltpu.VMEM((nb, 8, RWKV_GATE_LORA), F32),
                        pltpu.VMEM((nb, 8, LANE), F32), pltpu.VMEM((nb, 8, LANE), F32),
                        pltpu.VMEM((nb, RWKV_HEADS // HEAD_GROUP, GROUP_W, GROUP_W), F32)],
        compiler_params=_cparams(("parallel", "arbitrary")),
        name="rwkv7",
    )(*ins)
    if has_vres:
        return res, v_first
    return res[0], res[1]


def _pad_rows(w, rows, offset=0):
    return jnp.zeros((rows, w.shape[1]), F32).at[offset:offset + w.shape[0]].set(w)


def _pack_w_in(w):
    conv = w[:, 0:1536]
    gla = w[:, 1536:3072]
    gla_lr = w[:, 3072:3088]
    diff = w[:, 3088:4624]
    rw_rkv = w[:, 4624:6160]
    rw_w = w[:, 6160:6256]
    rw_a = w[:, 6256:6352]
    rw_g = w[:, 6352:6608]
    z = lambda n: jnp.zeros((w.shape[0], n), w.dtype)
    packed = jnp.concatenate([conv, gla, diff, rw_rkv, rw_g, rw_w, z(32), rw_a, gla_lr, z(16)], axis=1)
    return packed, w[:, 6608:]


def _pad_lanes(v, width, offset=0):
    return jnp.zeros((1, width), F32).at[0, offset:offset + v.shape[0]].set(v)


def kernel(x, norm_mix_pre, w_in, conv_w, gla_wa2, gla_ba, gla_norm, diff_lq1, diff_lk1, diff_lq2, diff_lk2,
           diff_norm, rw_mu, rw_w0, rw_w2, rw_a0, rw_a2, rw_g2, rw_kk, rw_ka, rw_rk, rw_lnw, rw_lnb, rw_v0,
           rw_v1, rw_v2, w_branch, w_out, norm_mix_post, norm_ffn_pre, w_gate, w_up, w_down, norm_ffn_post):
    bsz, seq = x.shape[0], x.shape[1]
    depth = w_in.shape[0]
    w_in_b, w_branch_b, w_out_b, w_gate_b, w_up_b, w_down_b = (
        w.astype(BF16) for w in (w_in, w_branch, w_out, w_gate, w_up, w_down))
    t = bsz * seq
    x2 = x.reshape(t, D_MODEL)
    row = lambda v: v.reshape(1, -1)
    v_first = None
    h = _norm(x2, row(norm_mix_pre[0]))
    for l in range(depth):
        w_mix, w_gates = _pack_w_in(w_in_b[l])
        z3 = _inproj(h, w_mix).reshape(bsz, seq, N_MIX)
        lambda_init = 0.8 - 0.6 * math.exp(-0.3 * l)

        o_conv = _conv_mixer(z3, conv_w[l])
        o_gla = _gla_mixer(z3, _pad_rows(gla_wa2[l], LANE, GLA_WLR_LANE), row(gla_ba[l]), row(gla_norm[l]))
        o_diff = _diff_mixer(z3, row(diff_lq1[l]), row(diff_lk1[l]), row(diff_lq2[l]),
                             row(diff_lk2[l]), row(diff_norm[l]), lambda_init)
        mu = rw_mu[l]
        rp = {
            "mu_rkv": row(mu[0:1536]),
            "mu_w": _pad_lanes(mu[1536:1632], LANE),
            "mu_a": _pad_lanes(mu[1632:1728], LANE),
            "mu_g": row(mu[1728:1984]),
            "w0": row(rw_w0[l]), "w2": _pad_rows(rw_w2[l], LANE),
            "a0": row(rw_a0[l]), "a2": _pad_rows(rw_a2[l], LANE),
            "g2": rw_g2[l], "kk": row(rw_kk[l]), "ka": row(rw_ka[l]), "rk": row(rw_rk[l]),
            "lnw": row(rw_lnw[l]), "lnb": row(rw_lnb[l]),
        }
        if l > 0:
            rp.update(v0=row(rw_v0[l - 1]), v1=rw_v1[l - 1], v2=rw_v2[l - 1])
        o_rwkv, v_first = _rwkv_mixer(z3, v_first if l > 0 else None, rp)

        outs = [o.reshape(t, BRANCH) for o in (o_conv, o_gla, o_diff, o_rwkv)]
        merged = _merge(h, w_gates, outs, w_branch_b, l)
        x2, h_ffn = _proj_norm_res(merged, w_out_b, l, x2, row(norm_mix_post[l]),
                                   row(norm_ffn_pre[l]), D_MODEL)
        act = _ffn_up(h_ffn, w_gate_b, w_up_b, l)
        next_gain = row(norm_mix_pre[l + 1]) if l + 1 < depth else None
        x2, h = _proj_norm_res(act, w_down_b, l, x2, row(norm_ffn_post[l]), next_gain, FFN_DOWN_TK)
    return x2.reshape(bsz, seq, D_MODEL)
```

```python
import functools
import math

import jax
import jax.numpy as jnp
from jax import lax
from jax.experimental import pallas as pl
from jax.experimental.pallas import tpu as pltpu

F32 = jnp.float32
BF16 = jnp.bfloat16

D_MODEL = 2048
CHUNK = 64
N_BRANCH = 4
BRANCH = 512
GLA_HEADS = 4
GLA_DK = 64
GLA_DV = 128
GLA_KEY = GLA_HEADS * GLA_DK
GLA_LOW_RANK = 16
GLA_LOGIT_NORM = 16.0
DIFF_HEADS = 4
DIFF_D = 64
RWKV_HEADS = 8
RWKV_HD = 64
RWKV_DECAY_LORA = 96
RWKV_AAA_LORA = 96
RWKV_MV_LORA = 64
RWKV_GATE_LORA = 256
D_FF = 5632
RMS_EPS = 1e-6
HEAD_EPS = 1e-5
RWKV_GN_EPS = 64e-5
NEG_INF = -1e30
LOG2E = 1.4426950408889634

COL_CONV = 0
COL_GLA = 1536
COL_DIFF = 3072
COL_RWKV = 4608
COL_RW_G = 6144
COL_RW_W = 6400
COL_RW_A = 6528
GLA_WLR_LANE = RWKV_AAA_LORA
N_MIX = 6656
LANE = 128
HEAD_GROUP = 4
GROUP_W = HEAD_GROUP * 64
RWKV_SEQS_PER_STEP = 4

VMEM_LIMIT = 56 * 1024 * 1024
PROJ_ROW_SPLIT = 4
FFN_DOWN_TK = D_FF // 2


def _tile(n, pref):
    t = min(n, pref)
    while n % t:
        t -= 8
    return t


def _cparams(sem):
    return pltpu.CompilerParams(dimension_semantics=sem, vmem_limit_bytes=VMEM_LIMIT)


def _bdot(a, b):
    return jnp.dot(a.astype(BF16), b.astype(BF16), preferred_element_type=F32)


def _bdot_nt(a, b):
    return lax.dot_general(a.astype(BF16), b.astype(BF16), (((1,), (1,)), ((), ())),
                           preferred_element_type=F32)


def _bdot_tn(a, b):
    return lax.dot_general(a.astype(BF16), b.astype(BF16), (((0,), (0,)), ((), ())),
                           preferred_element_type=F32)


def _split_dot(x, ones_bf16):
    hi = x.astype(BF16)
    lo = (x - hi.astype(F32)).astype(BF16)
    return (jnp.dot(hi, ones_bf16, preferred_element_type=F32)
            + jnp.dot(lo, ones_bf16, preferred_element_type=F32))


def _split_dot_left(ones_bf16, x):
    hi = x.astype(BF16)
    lo = (x - hi.astype(F32)).astype(BF16)
    return (jnp.dot(ones_bf16, hi, preferred_element_type=F32)
            + jnp.dot(ones_bf16, lo, preferred_element_type=F32))


def _sigmoid(x):
    return 1.0 / (1.0 + jnp.exp(-x))


def _rms(x, gain, eps):
    return x * lax.rsqrt(jnp.mean(x * x, axis=-1, keepdims=True) + eps) * gain


def _iota2(shape, dim):
    return lax.broadcasted_iota(jnp.int32, shape, dim)


def _tril_incl(n):
    return (_iota2((n, n), 0) >= _iota2((n, n), 1)).astype(BF16)


def _head_ones(n, width):
    return ((_iota2((n, n), 0) // width) == (_iota2((n, n), 1) // width)).astype(BF16)


def _expand(x, col_group):
    rows = HEAD_GROUP * CHUNK
    xt = jnp.concatenate([x] * HEAD_GROUP, axis=0)
    keep = (_iota2((rows, x.shape[1]), 0) // CHUNK) == (_iota2((rows, x.shape[1]), 1) // col_group)
    return jnp.where(keep, xt, 0.0)


def _collapse(xe):
    out = xe[0:CHUNK]
    for h in range(1, HEAD_GROUP):
        out = out + xe[h * CHUNK:(h + 1) * CHUNK]
    return out


def _norm_kernel(x_ref, g_ref, h_ref):
    h_ref[...] = _rms(x_ref[...], g_ref[...], RMS_EPS).astype(BF16)


def _norm(x2, gain):
    t = x2.shape[0]
    tm = _tile(t, 512)
    return pl.pallas_call(
        _norm_kernel,
        out_shape=jax.ShapeDtypeStruct((t, D_MODEL), BF16),
        grid=(t // tm,),
        in_specs=[pl.BlockSpec((tm, D_MODEL), lambda i: (i, 0)),
                  pl.BlockSpec((1, D_MODEL), lambda i: (0, 0))],
        out_specs=pl.BlockSpec((tm, D_MODEL), lambda i: (i, 0)),
        compiler_params=_cparams(("parallel",)),
        name="norm",
    )(x2, gain)


def _inproj_kernel(h_ref, w_ref, z_ref):
    z_ref[...] = jnp.dot(h_ref[...], w_ref[...], preferred_element_type=F32).astype(BF16)


def _inproj(h, w, l):
    t, n = h.shape[0], w.shape[2]
    tm, tn = _tile(t, 1024), _tile(n, 1664)
    return pl.pallas_call(
        _inproj_kernel,
        out_shape=jax.ShapeDtypeStruct((t, n), BF16),
        grid=(t // tm, n // tn),
        in_specs=[pl.BlockSpec((tm, D_MODEL), lambda i, j: (i, 0)),
                  pl.BlockSpec((None, D_MODEL, tn), lambda i, j: (l, 0, j))],
        out_specs=pl.BlockSpec((tm, tn), lambda i, j: (i, j)),
        compiler_params=_cparams(("parallel", "arbitrary")),
        name="inproj",
    )(h, w)


def _merge_kernel(h_ref, wg0, wg1, wg2, wg3, o0, o1, o2, o3, p_ref, out_ref):
    h = h_ref[...]
    acc = None
    for n, (wg, o) in enumerate(((wg0, o0), (wg1, o1), (wg2, o2), (wg3, o3))):
        gate = _sigmoid(jnp.dot(h, wg[...], preferred_element_type=F32))
        term = gate * jnp.dot(o[...], p_ref[n], preferred_element_type=F32)
        acc = term if acc is None else acc + term
    out_ref[...] = acc.astype(BF16)


def _merge(h, w_gate, outs, w_branch, l):
    t = h.shape[0]
    tm, tn = _tile(t, 1024), 512
    nj = D_MODEL // tn
    gate_specs = [pl.BlockSpec((None, D_MODEL, tn), functools.partial(lambda i, j, n: (l, 0, n * nj + j), n=n))
                  for n in range(N_BRANCH)]
    o_specs = [pl.BlockSpec((tm, BRANCH), lambda i, j: (i, 0)) for _ in range(N_BRANCH)]
    return pl.pallas_call(
        _merge_kernel,
        out_shape=jax.ShapeDtypeStruct((t, D_MODEL), BF16),
        grid=(t // tm, nj),
        in_specs=[pl.BlockSpec((tm, D_MODEL), lambda i, j: (i, 0))] + gate_specs + o_specs
                 + [pl.BlockSpec((None, N_BRANCH, BRANCH, tn), lambda i, j: (l, 0, 0, j))],
        out_specs=pl.BlockSpec((tm, tn), lambda i, j: (i, j)),
        compiler_params=_cparams(("parallel", "arbitrary")),
        name="merge",
    )(h, w_gate, w_gate, w_gate, w_gate, *outs, w_branch)


def _proj_norm_res_kernel(a_ref, w_ref, x_ref, g_ref, gn_ref, o_ref, *h_ref, nk):
    k = pl.program_id(1)

    def accumulate(first):
        part = jnp.dot(a_ref[...], w_ref[...], preferred_element_type=F32)
        o_ref[...] = part if first else o_ref[...] + part

    def finish():
        sub = o_ref.shape[0] // PROJ_ROW_SPLIT
        for r in range(PROJ_ROW_SPLIT):
            rows = slice(r * sub, (r + 1) * sub)
            y = jnp.dot(a_ref[rows, :], w_ref[...], preferred_element_type=F32)
            if nk > 1:
                y = y + o_ref[rows, :]
            xn = x_ref[rows, :] + _rms(y, g_ref[...], RMS_EPS)
            o_ref[rows, :] = xn
            if h_ref:
                h_ref[0][rows, :] = _rms(xn, gn_ref[...], RMS_EPS).astype(BF16)

    if nk == 1:
        finish()
    else:
        pl.when(k == 0)(functools.partial(accumulate, True))
        if nk > 2:
            pl.when((k > 0) & (k < nk - 1))(functools.partial(accumulate, False))
        pl.when(k == nk - 1)(finish)


def _proj_norm_res(a, w, l, x2, gain, next_gain, tk):
    t, kdim = a.shape
    tm = _tile(t, 512)
    emit_h = next_gain is not None
    row_spec = pl.BlockSpec((tm, D_MODEL), lambda i, k: (i, 0))
    vec_spec = pl.BlockSpec((1, D_MODEL), lambda i, k: (0, 0))
    out_shape = [jax.ShapeDtypeStruct((t, D_MODEL), F32)]
    if emit_h:
        out_shape.append(jax.ShapeDtypeStruct((t, D_MODEL), BF16))
    assert kdim % tk == 0 and tm % (8 * PROJ_ROW_SPLIT) == 0
    res = pl.pallas_call(
        functools.partial(_proj_norm_res_kernel, nk=kdim // tk),
        out_shape=tuple(out_shape),
        grid=(t // tm, kdim // tk),
        in_specs=[pl.BlockSpec((tm, tk), lambda i, k: (i, k)),
                  pl.BlockSpec((None, tk, D_MODEL), lambda i, k: (l, k, 0)),
                  row_spec, vec_spec, vec_spec],
        out_specs=tuple([row_spec] * len(out_shape)),
        compiler_params=_cparams(("parallel", "arbitrary")),
        name="proj_norm_res",
    )(a, w, x2, gain, next_gain if emit_h else gain)
    return (res[0], res[1]) if emit_h else (res[0], None)


def _ffn_up_kernel(h_ref, wg_ref, wu_ref, a_ref, wg_b, wu_b):
    @pl.when(pl.program_id(1) == 0)
    def _():
        wg_b[...] = wg_ref[...].astype(BF16)
        wu_b[...] = wu_ref[...].astype(BF16)

    h = h_ref[...]
    gt = jnp.dot(h, wg_b[...], preferred_element_type=F32)
    up = jnp.dot(h, wu_b[...], preferred_element_type=F32)
    a_ref[...] = (gt * _sigmoid(gt) * up).astype(BF16)


def _ffn_up(h, wg, wu, l):
    t = h.shape[0]
    tm, tn = _tile(t, 1024), 512
    return pl.pallas_call(
        _ffn_up_kernel,
        out_shape=jax.ShapeDtypeStruct((t, D_FF), BF16),
        grid=(D_FF // tn, t // tm),
        in_specs=[pl.BlockSpec((tm, D_MODEL), lambda j, i: (i, 0)),
                  pl.BlockSpec((None, D_MODEL, tn), lambda j, i: (l, 0, j)),
                  pl.BlockSpec((None, D_MODEL, tn), lambda j, i: (l, 0, j))],
        out_specs=pl.BlockSpec((tm, tn), lambda j, i: (i, j)),
        scratch_shapes=[pltpu.VMEM((D_MODEL, tn), BF16), pltpu.VMEM((D_MODEL, tn), BF16)],
        compiler_params=_cparams(("parallel", "arbitrary")),
        name="ffn_up",
    )(h, wg, wu)


def _conv_kernel(b_ref, c_ref, u_ref, w_ref, o_ref, carry):
    @pl.when(pl.program_id(1) == 0)
    def _():
        carry[...] = jnp.zeros_like(carry)

    cu = c_ref[0].astype(F32) * u_ref[0].astype(F32)
    ts = cu.shape[0]
    row = _iota2(cu.shape, 0)
    p1 = carry[7:8, :]
    p2 = carry[6:7, :]
    s1 = jnp.where(row == 0, p1, pltpu.roll(cu, 1, axis=0))
    s2 = jnp.where(row == 0, p2, jnp.where(row == 1, p1, pltpu.roll(cu, 2, axis=0)))
    w = w_ref[...]
    y = w[2:3, :] * cu + w[1:2, :] * s1 + w[0:1, :] * s2
    o_ref[0] = (b_ref[0].astype(F32) * y).astype(BF16)
    carry[...] = cu[ts - 8:, :]


def _conv_mixer(z3, conv_w):
    b, s = z3.shape[0], z3.shape[1]
    ts = _tile(s, 512)
    c0 = COL_CONV // BRANCH
    return pl.pallas_call(
        _conv_kernel,
        out_shape=jax.ShapeDtypeStruct((b, s, BRANCH), BF16),
        grid=(b, s // ts),
        in_specs=[pl.BlockSpec((1, ts, BRANCH), lambda i, j: (i, j, c0)),
                  pl.BlockSpec((1, ts, BRANCH), lambda i, j: (i, j, c0 + 1)),
                  pl.BlockSpec((1, ts, BRANCH), lambda i, j: (i, j, c0 + 2)),
                  pl.BlockSpec((3, BRANCH), lambda i, j: (0, 0))],
        out_specs=pl.BlockSpec((1, ts, BRANCH), lambda i, j: (i, j, 0)),
        scratch_shapes=[pltpu.VMEM((8, BRANCH), F32)],
        compiler_params=_cparams(("parallel", "arbitrary")),
        name="conv",
    )(z3, z3, z3, conv_w)


def _gla_kernel(q_ref, k_ref, v_ref, g_ref, wl_ref, wa2_ref, ba_ref, gn_ref, o_ref, state):
    @pl.when(pl.program_id(1) == 0)
    def _():
        state[...] = jnp.zeros_like(state)

    nb = q_ref.shape[0]
    rows = HEAD_GROUP * CHUNK
    causal = (_iota2((rows, rows), 0) % CHUNK) >= (_iota2((rows, rows), 1) % CHUNK)
    tril = _tril_incl(CHUNK)
    q_in, k_in, q_st, k_st, ve, dec = [], [], [], [], [], []
    for b in range(nb):
        q = q_ref[b].astype(F32) * (GLA_DK ** -0.5)
        k = k_ref[b].astype(F32)
        z = _bdot(wl_ref[b], wa2_ref[...]) + ba_ref[...]
        log_a = (jnp.minimum(z, 0.0) - jnp.log(1.0 + jnp.exp(-jnp.abs(z)))) * (1.0 / GLA_LOGIT_NORM)
        cum = _split_dot_left(tril, log_a)
        mid = cum[CHUNK // 2 - 1:CHUNK // 2, :]
        last = cum[CHUNK - 1:CHUNK, :]
        q_in.append(_expand(q * jnp.exp(cum - mid), GLA_DK).astype(BF16))
        k_in.append(_expand(k * jnp.exp(mid - cum), GLA_DK).astype(BF16))
        q_st.append(_expand(q * jnp.exp(cum), GLA_DK).astype(BF16))
        k_st.append(_expand(k * jnp.exp(last - cum), GLA_DK).astype(BF16))
        ve.append(_expand(v_ref[b].astype(F32), GLA_DV).astype(BF16))
        dec.append(jnp.exp(last))
    scores = [jnp.where(causal, _bdot_nt(q_in[b], k_in[b]), 0.0).astype(BF16) for b in range(nb)]
    st = [state[b] for b in range(nb)]
    oe = [_bdot(scores[b], ve[b]) + _bdot_nt(q_st[b], st[b]) for b in range(nb)]
    for b in range(nb):
        state[b] = st[b] * dec[b] + _bdot_tn(ve[b], k_st[b])
    gn = gn_ref[...]
    for b in range(nb):
        o = _collapse(oe[b])
        g = g_ref[b].astype(F32)
        gate = g * _sigmoid(g)
        outs = [_rms(o[:, h * GLA_DV:(h + 1) * GLA_DV], gn, HEAD_EPS) for h in range(GLA_HEADS)]
        o_ref[b] = (jnp.concatenate(outs, axis=1) * gate).astype(BF16)


def _gla_mixer(z3, wa2p, ba, gn):
    b, s = z3.shape[0], z3.shape[1]
    cq = COL_GLA // GLA_KEY
    cv = (COL_GLA + 2 * GLA_KEY) // BRANCH
    cw = COL_RW_A // LANE
    nb = RWKV_SEQS_PER_STEP if b % RWKV_SEQS_PER_STEP == 0 else 1
    return pl.pallas_call(
        _gla_kernel,
        out_shape=jax.ShapeDtypeStruct((b, s, BRANCH), BF16),
        grid=(b // nb, s // CHUNK),
        in_specs=[pl.BlockSpec((nb, CHUNK, GLA_KEY), lambda i, j: (i, j, cq)),
                  pl.BlockSpec((nb, CHUNK, GLA_KEY), lambda i, j: (i, j, cq + 1)),
                  pl.BlockSpec((nb, CHUNK, BRANCH), lambda i, j: (i, j, cv)),
                  pl.BlockSpec((nb, CHUNK, BRANCH), lambda i, j: (i, j, cv + 1)),
                  pl.BlockSpec((nb, CHUNK, LANE), lambda i, j: (i, j, cw)),
                  pl.BlockSpec((LANE, GLA_KEY), lambda i, j: (0, 0)),
                  pl.BlockSpec((1, GLA_KEY), lambda i, j: (0, 0)),
                  pl.BlockSpec((1, GLA_DV), lambda i, j: (0, 0))],
        out_specs=pl.BlockSpec((nb, CHUNK, BRANCH), lambda i, j: (i, j, 0)),
        scratch_shapes=[pltpu.VMEM((nb, GLA_HEADS * GLA_DV, GLA_KEY), F32)],
        compiler_params=_cparams(("parallel", "arbitrary")),
        name="gla",
    )(z3, z3, z3, z3, z3, wa2p, ba, gn)


VT_ROWS = 2 * DIFF_D + 16


def _alibi_slope(h):
    return 2.0 ** (-8.0 * (h + 1) / DIFF_HEADS)


def _diff_kernel(q_ref, k_ref, v_ref, lq1, lk1, lq2, lk2, gn_ref, o_ref, ka_scr, vb_scr, qa_scr, acc_scr,
                 m_scr, s_scr, rel_scr, *, lambda_init, tq):
    qi = pl.program_id(1)
    hw = 2 * DIFF_D
    seq = k_ref.shape[1]
    lane = _iota2((tq, hw), 1)

    @pl.when(qi == 0)
    def _():
        key_row = _iota2((tq, hw), 0).astype(F32)
        for h in range(DIFF_HEADS):
            k_aug = []
            for half in range(2):
                key_bias = (_alibi_slope(h) * LOG2E) * (key_row + float(half * tq))
                hi = key_bias.astype(BF16).astype(F32)
                mid = (key_bias - hi).astype(BF16).astype(F32)
                lo = key_bias - hi - mid
                aug = jnp.where(lane == 0, hi, jnp.where(lane == 1, mid, jnp.where(lane == 2, lo, 0.0)))
                k_aug.append(aug.astype(BF16))
            for c in range(seq // tq):
                rows = slice(c * tq, (c + 1) * tq)
                ka_scr[h, rows, 0:hw] = k_ref[0, rows, h * hw:(h + 1) * hw].astype(BF16)
                ka_scr[h, rows, hw:2 * hw] = k_aug[c % 2]
        k_row = _iota2((2 * tq, 2 * tq), 0)
        col = _iota2((2 * tq, 2 * tq), 1)
        for par in range(2):
            q_row = jnp.where(col >= tq, col - tq, col) + par * tq
            d = q_row - k_row
            rel_scr[par] = jnp.where((k_row // CHUNK) <= (q_row // CHUNK), (d - jnp.abs(d)).astype(F32), NEG_INF)
        sub = _iota2((VT_ROWS - hw, tq), 0)
        tail = jnp.where(sub == 0, 1.0, 0.0).astype(BF16)
        for c in range(seq // tq):
            rows = slice(c * tq, (c + 1) * tq)
            for h in range(DIFF_HEADS):
                vb_scr[h, 0:hw, rows] = v_ref[0, rows, h * hw:(h + 1) * hw].astype(F32).T.astype(BF16)
                vb_scr[h, hw:VT_ROWS, rows] = tail

    lane2 = _iota2((2 * tq, hw), 1)
    ones_aug = (lane2 < 3).astype(F32)
    for h in range(DIFF_HEADS):
        q = q_ref[0, :, h * hw:(h + 1) * hw].astype(F32) * (DIFF_D ** -0.5 * LOG2E)
        q2 = jnp.concatenate([jnp.where(lane < DIFF_D, q, 0.0), jnp.where(lane < DIFF_D, 0.0, q)],
                             axis=0)
        qa_scr[h] = jnp.concatenate([q2, ones_aug], axis=1).astype(BF16)
    acc_scr[...] = jnp.zeros_like(acc_scr)
    m_scr[...] = jnp.full_like(m_scr, NEG_INF)

    tk = 2 * tq
    last_tile = qi // 2

    def tile_scores(j):
        start = pl.multiple_of(j * tk, tk)
        return [lax.dot_general(ka_scr[h, pl.ds(start, tk), :], qa_scr[h], (((1,), (1,)), ((), ())),
                                preferred_element_type=F32) for h in range(DIFF_HEADS)]

    def update(scores, j):
        start = pl.multiple_of(j * tk, tk)
        alphas, probs = [], []
        for h, s in enumerate(scores):
            m_old = m_scr[h]
            m_new = jnp.maximum(m_old, jnp.max(s, axis=0, keepdims=True))
            alphas.append(jnp.exp2(m_old - m_new))
            probs.append(jnp.exp2((s - m_new).astype(BF16)))
            m_scr[h] = m_new - _alibi_slope(h) * LOG2E * tk
        pvs = [jnp.dot(vb_scr[h, :, pl.ds(start, tk)], probs[h], preferred_element_type=F32)
               for h in range(DIFF_HEADS)]
        for h in range(DIFF_HEADS):
            acc_scr[h] = alphas[h] * acc_scr[h] + pvs[h]

    for h, s in enumerate(tile_scores(0)):
        s_scr[h] = s

    def body(j, carry):
        nxt = tile_scores(j + 1)
        update([s_scr[h] for h in range(DIFF_HEADS)], j)
        for h in range(DIFF_HEADS):
            s_scr[h] = nxt[h]
        return carry

    lax.fori_loop(0, last_tile, body, 0)

    rel = rel_scr[qi % 2]
    lam = (jnp.exp(jnp.sum(lq1[...] * lk1[...], axis=-1, keepdims=True))
           - jnp.exp(jnp.sum(lq2[...] * lk2[...], axis=-1, keepdims=True))
           + lambda_init)
    update([s_scr[h] + (_alibi_slope(h) * LOG2E) * rel for h in range(DIFF_HEADS)], last_tile)
    for h in range(DIFF_HEADS):
        on = acc_scr[h, 0:hw, :] / acc_scr[h, hw:hw + 1, :]
        o = (on[:, :tq] - lam * on[:, tq:]).T
        o_ref[0, :, h * hw:(h + 1) * hw] = (_rms(o, gn_ref[...], HEAD_EPS) * (1.0 - lambda_init)).astype(BF16)


def _diff_mixer(z3, lq1, lk1, lq2, lk2, gn, lambda_init):
    b, s = z3.shape[0], z3.shape[1]
    tq = _tile(s // 2, 256)
    assert s % (2 * tq) == 0 and tq % CHUNK == 0
    hw = 2 * DIFF_D
    cq = COL_DIFF // BRANCH
    nh = DIFF_HEADS
    vec = pl.BlockSpec((1, DIFF_D), lambda i, j: (0, 0))
    return pl.pallas_call(
        functools.partial(_diff_kernel, lambda_init=lambda_init, tq=tq),
        out_shape=jax.ShapeDtypeStruct((b, s, BRANCH), BF16),
        grid=(b, s // tq),
        in_specs=[pl.BlockSpec((1, tq, BRANCH), lambda i, j: (i, j, cq)),
                  pl.BlockSpec((1, s, BRANCH), lambda i, j: (i, 0, cq + 1), pipeline_mode=pl.Buffered(1)),
                  pl.BlockSpec((1, s, BRANCH), lambda i, j: (i, 0, cq + 2), pipeline_mode=pl.Buffered(1)),
                  vec, vec, vec, vec,
                  pl.BlockSpec((1, hw), lambda i, j: (0, 0))],
        out_specs=pl.BlockSpec((1, tq, BRANCH), lambda i, j: (i, j, 0)),
        scratch_shapes=[pltpu.VMEM((nh, s, 2 * hw), BF16), pltpu.VMEM((nh, VT_ROWS, s), BF16),
                        pltpu.VMEM((nh, 2 * tq, 2 * hw), BF16), pltpu.VMEM((nh, VT_ROWS, 2 * tq), F32),
                        pltpu.VMEM((nh, 1, 2 * tq), F32), pltpu.VMEM((nh, 2 * tq, 2 * tq), F32),
                        pltpu.VMEM((2, 2 * tq, 2 * tq), F32)],
        compiler_params=_cparams(("parallel", "arbitrary")),
        name="diff_attn",
    )(z3, z3, z3, lq1, lk1, lq2, lk2, gn)


def _shift(x, prev_rows):
    row = _iota2(x.shape, 0)
    return jnp.where(row == 0, prev_rows[7:8, :], pltpu.roll(x, 1, axis=0))


def _rwkv_kernel(*refs, has_vres, nb):
    if has_vres:
        (rkv_ref, gl_ref, wb_ref, ab_ref, vf_ref, mu_rkv, mu_g, mu_w, mu_a, w0, w2, a0, a2, g2, kk_s, ka_s,
         rk_s, lnw, lnb, v0, v1, v2, o_ref, p_rkv, p_g, p_w, p_a, state) = refs
    else:
        (rkv_ref, gl_ref, wb_ref, ab_ref, mu_rkv, mu_g, mu_w, mu_a, w0, w2, a0, a2, g2, kk_s, ka_s,
         rk_s, lnw, lnb, o_ref, vf_out, p_rkv, p_g, p_w, p_a, state) = refs

    @pl.when(pl.program_id(1) == 0)
    def _():
        for r in (p_rkv, p_g, p_w, p_a, state):
            r[...] = jnp.zeros_like(r)

    def stack(parts):
        return parts[0] if nb == 1 else jnp.concatenate(parts, axis=0)

    def mixed(x_ref, p_ref, mu_ref):
        parts = []
        for b in range(nb):
            x = x_ref[b].astype(F32)
            parts.append(x + (_shift(x, p_ref[b]) - x) * mu_ref[...])
            p_ref[b] = x[CHUNK - 8:, :]
        return stack(parts)

    rkv = mixed(rkv_ref, p_rkv, mu_rkv)
    g_lr = mixed(gl_ref, p_g, mu_g)
    w_lr = mixed(wb_ref, p_w, mu_w)
    a_lr = mixed(ab_ref, p_a, mu_a)
    r = rkv[:, 0:BRANCH]
    k = rkv[:, BRANCH:2 * BRANCH]
    v = rkv[:, 2 * BRANCH:3 * BRANCH]
    nrow = nb * CHUNK

    y = w0[...] + _bdot(jnp.tanh(w_lr), w2[...])
    softplus = jnp.maximum(-y, 0.0) + jnp.log(1.0 + jnp.exp(-jnp.abs(y)))
    lw = -jnp.exp(-softplus - 0.5)
    a = _sigmoid(a0[...] + _bdot(a_lr, a2[...]))
    gate = _bdot(_sigmoid(g_lr), g2[...])
    if has_vres:
        vf = stack([vf_ref[b] for b in range(nb)])
        v = v + (vf - v) * _sigmoid(v0[...] + _bdot(_bdot(v, v1[...]), v2[...]))
    else:
        for b in range(nb):
            vf_out[b] = v[b * CHUNK:(b + 1) * CHUNK]

    ones = _head_ones(BRANCH, RWKV_HD)
    kk = k * kk_s[...]
    k = k * (1.0 + (a - 1.0) * ka_s[...])
    sums = _bdot(jnp.concatenate([kk * kk, r * k * rk_s[...]], axis=0), ones)
    kk = kk / jnp.maximum(jnp.sqrt(sums[:nrow]), 1e-12)
    bonus = sums[nrow:] * v

    ri, ci = _iota2((nrow, nrow), 0), _iota2((nrow, nrow), 1)
    tril = ((ri // CHUNK == ci // CHUNK) & (ri >= ci)).astype(BF16)
    cum = _split_dot_left(tril, lw)
    lasts = [cum[(b + 1) * CHUNK - 1:(b + 1) * CHUNK, :] for b in range(nb)]
    to_end = stack([jnp.exp(lasts[b] - cum[b * CHUNK:(b + 1) * CHUNK]) for b in range(nb)])
    decay_all = [jnp.exp(last) for last in lasts]
    a_t = -kk * jnp.exp(cum - lw)
    r_t = r * jnp.exp(cum)
    inv = jnp.exp(-cum)
    kka = kk * a
    b_s = kka * inv
    k_s = k * inv
    b_e = kka * to_end
    k_e = k * to_end

    rows = HEAD_GROUP * CHUNK
    ri = _iota2((rows, rows), 0) % CHUNK
    ci = _iota2((rows, rows), 1) % CHUNK
    strict = ri > ci
    incl = ri >= ci
    eye = (_iota2((rows, rows), 0) == _iota2((rows, rows), 1)).astype(F32)

    ngroup = RWKV_HEADS // HEAD_GROUP
    probs = [(b, gi) for b in range(nb) for gi in range(ngroup)]

    def ex(t):
        return [_expand(t[b * CHUNK:(b + 1) * CHUNK, gi * GROUP_W:(gi + 1) * GROUP_W], RWKV_HD).astype(BF16)
                for b, gi in probs]

    cat0 = lambda x, y_: jnp.concatenate([x, y_], axis=0)
    ae, re_, be, ke, bee, kee, ve = ex(a_t), ex(r_t), ex(b_s), ex(k_s), ex(b_e), ex(k_e), ex(v)
    pm = [_bdot_nt(cat0(ae[i], re_[i]), cat0(be[i], ke[i])) for i in range(len(probs))]
    n_ab = [jnp.where(strict, p[:rows, :rows], 0.0) for p in pm]
    a_ak = [jnp.where(strict, p[:rows, rows:], 0.0).astype(BF16) for p in pm]
    a_rb = [jnp.where(incl, p[rows:, :rows], 0.0).astype(BF16) for p in pm]
    a_rk = [jnp.where(incl, p[rows:, rows:], 0.0).astype(BF16) for p in pm]
    t_inv = [eye + n for n in n_ab]
    n_pow = [n.astype(BF16) for n in n_ab]
    for _ in range(int(math.log2(CHUNK)) - 1):
        n_pow = [_bdot(n, n).astype(BF16) for n in n_pow]
        t_inv = [t + _bdot(t, n) for t, n in zip(t_inv, n_pow)]
    av = [_bdot(cat0(a_ak[i], a_rk[i]), ve[i]) for i in range(len(probs))]
    y2 = [_bdot(t_inv[i], jnp.concatenate([ae[i], av[i][:rows].astype(BF16)], axis=1))
          for i in range(len(probs))]
    st = [state[b, gi] for b, gi in probs]
    x2 = [_bdot_nt(cat0(y2[i][:, :GROUP_W].astype(BF16), re_[i]), st[i]) for i in range(len(probs))]
    ue = [x2[i][:rows] + y2[i][:, GROUP_W:] for i in range(len(probs))]
    oe = [x2[i][rows:] + _bdot(a_rb[i], ue[i]) + av[i][rows:] for i in range(len(probs))]
    for i, (b, gi) in enumerate(probs):
        state[b, gi] = (st[i] * decay_all[b][:, gi * GROUP_W:(gi + 1) * GROUP_W]
                        + _bdot_tn(cat0(ue[i].astype(BF16), ve[i]), cat0(bee[i], kee[i])))
    o = stack([jnp.concatenate([_collapse(oe[b * ngroup + gi]) for gi in range(ngroup)], axis=1)
               for b in range(nb)])

    mean = _bdot(o, ones) * (1.0 / RWKV_HD)
    cen = o - mean
    var = _bdot(cen * cen, ones) * (1.0 / RWKV_HD)
    o = cen * lax.rsqrt(var + RWKV_GN_EPS) * lnw[...] + lnb[...]
    o = ((o + bonus) * gate).astype(BF16)
    for b in range(nb):
        o_ref[b] = o[b * CHUNK:(b + 1) * CHUNK]


def _rwkv_mixer(z3, v_first, p):
    b, s = z3.shape[0], z3.shape[1]
    has_vres = v_first is not None
    nb = RWKV_SEQS_PER_STEP if b % RWKV_SEQS_PER_STEP == 0 else 1
    row = lambda w, c: pl.BlockSpec((nb, CHUNK, w), lambda i, j: (i, j, c))
    full = lambda a: pl.BlockSpec(a.shape, lambda i, j: (0,) * a.ndim)
    ins = [z3, z3, z3, z3]
    specs = [row(3 * BRANCH, COL_RWKV // (3 * BRANCH)), row(RWKV_GATE_LORA, COL_RW_G // RWKV_GATE_LORA),
             row(LANE, COL_RW_W // LANE), row(LANE, COL_RW_A // LANE)]
    if has_vres:
        ins.append(v_first)
        specs.append(row(BRANCH, 0))
    names = ["mu_rkv", "mu_g", "mu_w", "mu_a", "w0", "w2", "a0", "a2", "g2", "kk", "ka", "rk", "lnw", "lnb"]
    if has_vres:
        names += ["v0", "v1", "v2"]
    for nme in names:
        ins.append(p[nme])
        specs.append(full(p[nme]))
    o_spec = pl.BlockSpec((nb, CHUNK, BRANCH), lambda i, j: (i, j, 0))
    o_shape = jax.ShapeDtypeStruct((b, s, BRANCH), BF16)
    if has_vres:
        out_shape, out_specs = o_shape, o_spec
    else:
        out_shape = (o_shape, jax.ShapeDtypeStruct((b, s, BRANCH), F32))
        out_specs = (o_spec, pl.BlockSpec((nb, CHUNK, BRANCH), lambda i, j: (i, j, 0)))
    res = pl.pallas_call(
        functools.partial(_rwkv_kernel, has_vres=has_vres, nb=nb),
        out_shape=out_shape,
        grid=(b // nb, s // CHUNK),
        in_specs=specs,
        out_specs=out_specs,
        scratch_shapes=[pltpu.VMEM((nb, 8, 3 * BRANCH), F32), pltpu.VMEM((nb, 8, RWKV_GATE_LORA), F32),
                        pltpu.VMEM((nb, 8, LANE), F32), pltpu.VMEM((nb, 8, LANE), F32),
                        pltpu.VMEM((nb, RWKV_HEADS // HEAD_GROUP, GROUP_W, GROUP_W), F32)],
        compiler_params=_cparams(("parallel", "arbitrary")),
        name="rwkv7",
    )(*ins)
    if has_vres:
        return res, v_first
    return res[0], res[1]


def _pad_rows(w, rows, offset=0):
    return jnp.zeros((rows, w.shape[1]), F32).at[offset:offset + w.shape[0]].set(w)


N_IN = 14800
COL_GATES_SRC = 6608
MIX_PIECES = ((COL_CONV, 0, 3072), (COL_DIFF, 3088, 1536), (COL_RWKV, 4624, 1536), (COL_RW_G, 6352, 256),
              (COL_RW_W, 6160, RWKV_DECAY_LORA), (COL_RW_A, 6256, RWKV_AAA_LORA),
              (COL_RW_A + GLA_WLR_LANE, 3072, GLA_LOW_RANK))
MIX_ZERO = ((COL_RW_W + RWKV_DECAY_LORA, LANE - RWKV_DECAY_LORA),
            (COL_RW_A + GLA_WLR_LANE + GLA_LOW_RANK, LANE - GLA_WLR_LANE - GLA_LOW_RANK))


def _pack_w_in_kernel(w_ref, mix_ref, gate_ref):
    for dst, src, n in MIX_PIECES:
        mix_ref[:, dst:dst + n] = w_ref[:, src:src + n].astype(BF16)
    for dst, n in MIX_ZERO:
        mix_ref[:, dst:dst + n] = jnp.zeros((mix_ref.shape[0], n), BF16)
    gate_ref[...] = w_ref[:, COL_GATES_SRC:N_IN].astype(BF16)


def _pack_w_in(w_in):
    depth = w_in.shape[0]
    tr = 128
    n_gate = N_IN - COL_GATES_SRC
    return pl.pallas_call(
        _pack_w_in_kernel,
        out_shape=(jax.ShapeDtypeStruct((depth, D_MODEL, N_MIX), BF16),
                   jax.ShapeDtypeStruct((depth, D_MODEL, n_gate), BF16)),
        grid=(depth, D_MODEL // tr),
        in_specs=[pl.BlockSpec((None, tr, N_IN), lambda l, i: (l, i, 0))],
        out_specs=(pl.BlockSpec((None, tr, N_MIX), lambda l, i: (l, i, 0)),
                   pl.BlockSpec((None, tr, n_gate), lambda l, i: (l, i, 0))),
        compiler_params=_cparams(("parallel", "parallel")),
        name="pack_w_in",
    )(w_in)


def _pad_lanes(v, width, offset=0):
    return jnp.zeros((1, width), F32).at[0, offset:offset + v.shape[0]].set(v)


def kernel(x, norm_mix_pre, w_in, conv_w, gla_wa2, gla_ba, gla_norm, diff_lq1, diff_lk1, diff_lq2, diff_lk2,
           diff_norm, rw_mu, rw_w0, rw_w2, rw_a0, rw_a2, rw_g2, rw_kk, rw_ka, rw_rk, rw_lnw, rw_lnb, rw_v0,
           rw_v1, rw_v2, w_branch, w_out, norm_mix_post, norm_ffn_pre, w_gate, w_up, w_down, norm_ffn_post):
    bsz, seq = x.shape[0], x.shape[1]
    depth = w_in.shape[0]
    w_branch_b, w_out_b, w_down_b = (w.astype(BF16) for w in (w_branch, w_out, w_down))
    w_mix, w_gates = _pack_w_in(w_in)
    t = bsz * seq
    x2 = x.reshape(t, D_MODEL)
    row = lambda v: v.reshape(1, -1)
    v_first = None
    h = _norm(x2, row(norm_mix_pre[0]))
    for l in range(depth):
        z3 = _inproj(h, w_mix, l).reshape(bsz, seq, N_MIX)
        lambda_init = 0.8 - 0.6 * math.exp(-0.3 * l)

        o_conv = _conv_mixer(z3, conv_w[l])
        o_gla = _gla_mixer(z3, _pad_rows(gla_wa2[l], LANE, GLA_WLR_LANE), row(gla_ba[l]), row(gla_norm[l]))
        o_diff = _diff_mixer(z3, row(diff_lq1[l]), row(diff_lk1[l]), row(diff_lq2[l]),
                             row(diff_lk2[l]), row(diff_norm[l]), lambda_init)
        mu = rw_mu[l]
        rp = {
            "mu_rkv": row(mu[0:1536]),
            "mu_w": _pad_lanes(mu[1536:1632], LANE),
            "mu_a": _pad_lanes(mu[1632:1728], LANE),
            "mu_g": row(mu[1728:1984]),
            "w0": row(rw_w0[l]), "w2": _pad_rows(rw_w2[l], LANE),
            "a0": row(rw_a0[l]), "a2": _pad_rows(rw_a2[l], LANE),
            "g2": rw_g2[l], "kk": row(rw_kk[l]), "ka": row(rw_ka[l]), "rk": row(rw_rk[l]),
            "lnw": row(rw_lnw[l]), "lnb": row(rw_lnb[l]),
        }
        if l > 0:
            rp.update(v0=row(rw_v0[l - 1]), v1=rw_v1[l - 1], v2=rw_v2[l - 1])
        o_rwkv, v_first = _rwkv_mixer(z3, v_first if l > 0 else None, rp)

        outs = [o.reshape(t, BRANCH) for o in (o_conv, o_gla, o_diff, o_rwkv)]
        merged = _merge(h, w_gates, outs, w_branch_b, l)
        x2, h_ffn = _proj_norm_res(merged, w_out_b, l, x2, row(norm_mix_post[l]),
                                   row(norm_ffn_pre[l]), D_MODEL)
        act = _ffn_up(h_ffn, w_gate, w_up, l)
        next_gain = row(norm_mix_pre[l + 1]) if l + 1 < depth else None
        x2, h = _proj_norm_res(act, w_down_b, l, x2, row(norm_ffn_post[l]), next_gain, FFN_DOWN_TK)
    return x2.reshape(bsz, seq, D_MODEL)
```

```python
import functools
import math

import jax
import jax.numpy as jnp
from jax import lax
from jax.experimental import pallas as pl
from jax.experimental.pallas import tpu as pltpu

F32 = jnp.float32
BF16 = jnp.bfloat16

D_MODEL = 2048
CHUNK = 64
N_BRANCH = 4
BRANCH = 512
GLA_HEADS = 4
GLA_DK = 64
GLA_DV = 128
GLA_KEY = GLA_HEADS * GLA_DK
GLA_LOW_RANK = 16
GLA_LOGIT_NORM = 16.0
DIFF_HEADS = 4
DIFF_D = 64
RWKV_HEADS = 8
RWKV_HD = 64
RWKV_DECAY_LORA = 96
RWKV_AAA_LORA = 96
RWKV_MV_LORA = 64
RWKV_GATE_LORA = 256
D_FF = 5632
RMS_EPS = 1e-6
HEAD_EPS = 1e-5
RWKV_GN_EPS = 64e-5
NEG_INF = -1e30
LOG2E = 1.4426950408889634

COL_CONV = 0
COL_GLA = 1536
COL_DIFF = 3072
COL_RWKV = 4608
COL_RW_G = 6144
COL_RW_W = 6400
COL_RW_A = 6528
GLA_WLR_LANE = RWKV_AAA_LORA
N_MIX = 6656
LANE = 128
HEAD_GROUP = 4
GROUP_W = HEAD_GROUP * 64
RWKV_SEQS_PER_STEP = 4

VMEM_LIMIT = 56 * 1024 * 1024
PROJ_ROW_SPLIT = 4
FFN_DOWN_TK = D_FF // 2


def _tile(n, pref):
    t = min(n, pref)
    while n % t:
        t -= 8
    return t


def _cparams(sem):
    return pltpu.CompilerParams(dimension_semantics=sem, vmem_limit_bytes=VMEM_LIMIT)


def _bdot(a, b):
    return jnp.dot(a.astype(BF16), b.astype(BF16), preferred_element_type=F32)


def _bdot_nt(a, b):
    return lax.dot_general(a.astype(BF16), b.astype(BF16), (((1,), (1,)), ((), ())),
                           preferred_element_type=F32)


def _bdot_tn(a, b):
    return lax.dot_general(a.astype(BF16), b.astype(BF16), (((0,), (0,)), ((), ())),
                           preferred_element_type=F32)


def _split_dot(x, ones_bf16):
    hi = x.astype(BF16)
    lo = (x - hi.astype(F32)).astype(BF16)
    return (jnp.dot(hi, ones_bf16, preferred_element_type=F32)
            + jnp.dot(lo, ones_bf16, preferred_element_type=F32))


def _split_dot_left(ones_bf16, x):
    hi = x.astype(BF16)
    lo = (x - hi.astype(F32)).astype(BF16)
    return (jnp.dot(ones_bf16, hi, preferred_element_type=F32)
            + jnp.dot(ones_bf16, lo, preferred_element_type=F32))


def _sigmoid(x):
    return 1.0 / (1.0 + jnp.exp(-x))


def _rms(x, gain, eps):
    return x * lax.rsqrt(jnp.mean(x * x, axis=-1, keepdims=True) + eps) * gain


def _iota2(shape, dim):
    return lax.broadcasted_iota(jnp.int32, shape, dim)


def _tril_incl(n):
    return (_iota2((n, n), 0) >= _iota2((n, n), 1)).astype(BF16)


def _head_ones(n, width):
    return ((_iota2((n, n), 0) // width) == (_iota2((n, n), 1) // width)).astype(BF16)


def _expand(x, col_group):
    rows = HEAD_GROUP * CHUNK
    xt = jnp.concatenate([x] * HEAD_GROUP, axis=0)
    keep = (_iota2((rows, x.shape[1]), 0) // CHUNK) == (_iota2((rows, x.shape[1]), 1) // col_group)
    return jnp.where(keep, xt, 0.0)


def _collapse(xe):
    out = xe[0:CHUNK]
    for h in range(1, HEAD_GROUP):
        out = out + xe[h * CHUNK:(h + 1) * CHUNK]
    return out


def _norm_kernel(x_ref, g_ref, h_ref):
    h_ref[...] = _rms(x_ref[...], g_ref[...], RMS_EPS).astype(BF16)


def _norm(x2, gain):
    t = x2.shape[0]
    tm = _tile(t, 512)
    return pl.pallas_call(
        _norm_kernel,
        out_shape=jax.ShapeDtypeStruct((t, D_MODEL), BF16),
        grid=(t // tm,),
        in_specs=[pl.BlockSpec((tm, D_MODEL), lambda i: (i, 0)),
                  pl.BlockSpec((1, D_MODEL), lambda i: (0, 0))],
        out_specs=pl.BlockSpec((tm, D_MODEL), lambda i: (i, 0)),
        compiler_params=_cparams(("parallel",)),
        name="norm",
    )(x2, gain)


def _inproj_kernel(h_ref, w_ref, z_ref):
    z_ref[...] = jnp.dot(h_ref[...], w_ref[...], preferred_element_type=F32).astype(BF16)


def _inproj(h, w, l):
    t, n = h.shape[0], w.shape[2]
    tm, tn = _tile(t, 1024), _tile(n, 1664)
    return pl.pallas_call(
        _inproj_kernel,
        out_shape=jax.ShapeDtypeStruct((t, n), BF16),
        grid=(t // tm, n // tn),
        in_specs=[pl.BlockSpec((tm, D_MODEL), lambda i, j: (i, 0)),
                  pl.BlockSpec((None, D_MODEL, tn), lambda i, j: (l, 0, j))],
        out_specs=pl.BlockSpec((tm, tn), lambda i, j: (i, j)),
        compiler_params=_cparams(("parallel", "arbitrary")),
        name="inproj",
    )(h, w)


def _merge_kernel(h_ref, wg0, wg1, wg2, wg3, o0, o1, o2, o3, p_ref, out_ref):
    h = h_ref[...]
    acc = None
    for n, (wg, o) in enumerate(((wg0, o0), (wg1, o1), (wg2, o2), (wg3, o3))):
        gate = _sigmoid(jnp.dot(h, wg[...], preferred_element_type=F32))
        term = gate * jnp.dot(o[...], p_ref[n], preferred_element_type=F32)
        acc = term if acc is None else acc + term
    out_ref[...] = acc.astype(BF16)


def _merge(h, w_gate, outs, w_branch, l):
    t = h.shape[0]
    tm, tn = _tile(t, 1024), 512
    nj = D_MODEL // tn
    gate_specs = [pl.BlockSpec((None, D_MODEL, tn), functools.partial(lambda i, j, n: (l, 0, n * nj + j), n=n))
                  for n in range(N_BRANCH)]
    o_specs = [pl.BlockSpec((tm, BRANCH), lambda i, j: (i, 0)) for _ in range(N_BRANCH)]
    return pl.pallas_call(
        _merge_kernel,
        out_shape=jax.ShapeDtypeStruct((t, D_MODEL), BF16),
        grid=(t // tm, nj),
        in_specs=[pl.BlockSpec((tm, D_MODEL), lambda i, j: (i, 0))] + gate_specs + o_specs
                 + [pl.BlockSpec((None, N_BRANCH, BRANCH, tn), lambda i, j: (l, 0, 0, j))],
        out_specs=pl.BlockSpec((tm, tn), lambda i, j: (i, j)),
        compiler_params=_cparams(("parallel", "arbitrary")),
        name="merge",
    )(h, w_gate, w_gate, w_gate, w_gate, *outs, w_branch)


def _proj_norm_res_kernel(a_ref, w_ref, x_ref, g_ref, gn_ref, o_ref, *h_ref, nk):
    k = pl.program_id(1)

    def accumulate(first):
        part = jnp.dot(a_ref[...], w_ref[...], preferred_element_type=F32)
        o_ref[...] = part if first else o_ref[...] + part

    def finish():
        sub = o_ref.shape[0] // PROJ_ROW_SPLIT
        for r in range(PROJ_ROW_SPLIT):
            rows = slice(r * sub, (r + 1) * sub)
            y = jnp.dot(a_ref[rows, :], w_ref[...], preferred_element_type=F32)
            if nk > 1:
                y = y + o_ref[rows, :]
            xn = x_ref[rows, :] + _rms(y, g_ref[...], RMS_EPS)
            o_ref[rows, :] = xn
            if h_ref:
                h_ref[0][rows, :] = _rms(xn, gn_ref[...], RMS_EPS).astype(BF16)

    if nk == 1:
        finish()
    else:
        pl.when(k == 0)(functools.partial(accumulate, True))
        if nk > 2:
            pl.when((k > 0) & (k < nk - 1))(functools.partial(accumulate, False))
        pl.when(k == nk - 1)(finish)


def _proj_norm_res(a, w, l, x2, gain, next_gain, tk):
    t, kdim = a.shape
    tm = _tile(t, 512)
    emit_h = next_gain is not None
    row_spec = pl.BlockSpec((tm, D_MODEL), lambda i, k: (i, 0))
    vec_spec = pl.BlockSpec((1, D_MODEL), lambda i, k: (0, 0))
    out_shape = [jax.ShapeDtypeStruct((t, D_MODEL), F32)]
    if emit_h:
        out_shape.append(jax.ShapeDtypeStruct((t, D_MODEL), BF16))
    assert kdim % tk == 0 and tm % (8 * PROJ_ROW_SPLIT) == 0
    res = pl.pallas_call(
        functools.partial(_proj_norm_res_kernel, nk=kdim // tk),
        out_shape=tuple(out_shape),
        grid=(t // tm, kdim // tk),
        in_specs=[pl.BlockSpec((tm, tk), lambda i, k: (i, k)),
                  pl.BlockSpec((None, tk, D_MODEL), lambda i, k: (l, k, 0)),
                  row_spec, vec_spec, vec_spec],
        out_specs=tuple([row_spec] * len(out_shape)),
        compiler_params=_cparams(("parallel", "arbitrary")),
        name="proj_norm_res",
    )(a, w, x2, gain, next_gain if emit_h else gain)
    return (res[0], res[1]) if emit_h else (res[0], None)


def _ffn_up_kernel(h_ref, wg_ref, wu_ref, a_ref, wg_b, wu_b):
    @pl.when(pl.program_id(1) == 0)
    def _():
        wg_b[...] = wg_ref[...].astype(BF16)
        wu_b[...] = wu_ref[...].astype(BF16)

    h = h_ref[...]
    gt = jnp.dot(h, wg_b[...], preferred_element_type=F32)
    up = jnp.dot(h, wu_b[...], preferred_element_type=F32)
    a_ref[...] = (gt * _sigmoid(gt) * up).astype(BF16)


def _ffn_up(h, wg, wu, l):
    t = h.shape[0]
    tm, tn = _tile(t, 1024), 512
    return pl.pallas_call(
        _ffn_up_kernel,
        out_shape=jax.ShapeDtypeStruct((t, D_FF), BF16),
        grid=(D_FF // tn, t // tm),
        in_specs=[pl.BlockSpec((tm, D_MODEL), lambda j, i: (i, 0)),
                  pl.BlockSpec((None, D_MODEL, tn), lambda j, i: (l, 0, j)),
                  pl.BlockSpec((None, D_MODEL, tn), lambda j, i: (l, 0, j))],
        out_specs=pl.BlockSpec((tm, tn), lambda j, i: (i, j)),
        scratch_shapes=[pltpu.VMEM((D_MODEL, tn), BF16), pltpu.VMEM((D_MODEL, tn), BF16)],
        compiler_params=_cparams(("parallel", "arbitrary")),
        name="ffn_up",
    )(h, wg, wu)


def _conv_kernel(b_ref, c_ref, u_ref, w_ref, o_ref, carry):
    @pl.when(pl.program_id(1) == 0)
    def _():
        carry[...] = jnp.zeros_like(carry)

    cu = c_ref[0].astype(F32) * u_ref[0].astype(F32)
    ts = cu.shape[0]
    row = _iota2(cu.shape, 0)
    p1 = carry[7:8, :]
    p2 = carry[6:7, :]
    s1 = jnp.where(row == 0, p1, pltpu.roll(cu, 1, axis=0))
    s2 = jnp.where(row == 0, p2, jnp.where(row == 1, p1, pltpu.roll(cu, 2, axis=0)))
    w = w_ref[...]
    y = w[2:3, :] * cu + w[1:2, :] * s1 + w[0:1, :] * s2
    o_ref[0] = (b_ref[0].astype(F32) * y).astype(BF16)
    carry[...] = cu[ts - 8:, :]


def _conv_mixer(z3, conv_w):
    b, s = z3.shape[0], z3.shape[1]
    ts = _tile(s, 512)
    c0 = COL_CONV // BRANCH
    return pl.pallas_call(
        _conv_kernel,
        out_shape=jax.ShapeDtypeStruct((b, s, BRANCH), BF16),
        grid=(b, s // ts),
        in_specs=[pl.BlockSpec((1, ts, BRANCH), lambda i, j: (i, j, c0)),
                  pl.BlockSpec((1, ts, BRANCH), lambda i, j: (i, j, c0 + 1)),
                  pl.BlockSpec((1, ts, BRANCH), lambda i, j: (i, j, c0 + 2)),
                  pl.BlockSpec((3, BRANCH), lambda i, j: (0, 0))],
        out_specs=pl.BlockSpec((1, ts, BRANCH), lambda i, j: (i, j, 0)),
        scratch_shapes=[pltpu.VMEM((8, BRANCH), F32)],
        compiler_params=_cparams(("parallel", "arbitrary")),
        name="conv",
    )(z3, z3, z3, conv_w)


def _gla_kernel(q_ref, k_ref, v_ref, g_ref, wl_ref, wa2_ref, ba_ref, gn_ref, o_ref, state):
    @pl.when(pl.program_id(1) == 0)
    def _():
        state[...] = jnp.zeros_like(state)

    nb = q_ref.shape[0]
    causal = _iota2((CHUNK, GLA_KEY), 0) >= (_iota2((CHUNK, GLA_KEY), 1) % CHUNK)
    same_head = ((_iota2((GLA_HEADS * GLA_DV, GLA_KEY), 0) // GLA_DV)
                 == (_iota2((GLA_HEADS * GLA_DV, GLA_KEY), 1) // GLA_DK))
    tril = _tril_incl(CHUNK)
    q_in, k_in, q_st, k_st, v, ve, dec = [], [], [], [], [], [], []
    for b in range(nb):
        q = q_ref[b].astype(F32) * (GLA_DK ** -0.5)
        k = k_ref[b].astype(F32)
        z = _bdot(wl_ref[b], wa2_ref[...]) + ba_ref[...]
        log_a = (jnp.minimum(z, 0.0) - jnp.log(1.0 + jnp.exp(-jnp.abs(z)))) * (1.0 / GLA_LOGIT_NORM)
        cum = _split_dot_left(tril, log_a)
        mid = cum[CHUNK // 2 - 1:CHUNK // 2, :]
        last = cum[CHUNK - 1:CHUNK, :]
        q_in.append(q * jnp.exp(cum - mid))
        k_in.append(_expand(k * jnp.exp(mid - cum), GLA_DK).astype(BF16))
        q_st.append(q * jnp.exp(cum))
        k_st.append(k * jnp.exp(last - cum))
        v.append(v_ref[b])
        ve.append(_expand(v[b].astype(F32), GLA_DV).astype(BF16))
        dec.append(jnp.exp(last))
    scores = [jnp.where(causal, _bdot_nt(q_in[b], k_in[b]), 0.0) for b in range(nb)]
    st = [state[b] for b in range(nb)]
    o_l = [_bdot(scores[b], ve[b]) + _bdot_nt(q_st[b], st[b]) for b in range(nb)]
    for b in range(nb):
        state[b] = st[b] * dec[b] + jnp.where(same_head, _bdot_tn(v[b], k_st[b]), 0.0)
    gn = gn_ref[...]
    for b in range(nb):
        o = o_l[b]
        g = g_ref[b].astype(F32)
        gate = g * _sigmoid(g)
        outs = [_rms(o[:, h * GLA_DV:(h + 1) * GLA_DV], gn, HEAD_EPS) for h in range(GLA_HEADS)]
        o_ref[b] = (jnp.concatenate(outs, axis=1) * gate).astype(BF16)


def _gla_mixer(z3, wa2p, ba, gn):
    b, s = z3.shape[0], z3.shape[1]
    cq = COL_GLA // GLA_KEY
    cv = (COL_GLA + 2 * GLA_KEY) // BRANCH
    cw = COL_RW_A // LANE
    nb = RWKV_SEQS_PER_STEP if b % RWKV_SEQS_PER_STEP == 0 else 1
    return pl.pallas_call(
        _gla_kernel,
        out_shape=jax.ShapeDtypeStruct((b, s, BRANCH), BF16),
        grid=(b // nb, s // CHUNK),
        in_specs=[pl.BlockSpec((nb, CHUNK, GLA_KEY), lambda i, j: (i, j, cq)),
                  pl.BlockSpec((nb, CHUNK, GLA_KEY), lambda i, j: (i, j, cq + 1)),
                  pl.BlockSpec((nb, CHUNK, BRANCH), lambda i, j: (i, j, cv)),
                  pl.BlockSpec((nb, CHUNK, BRANCH), lambda i, j: (i, j, cv + 1)),
                  pl.BlockSpec((nb, CHUNK, LANE), lambda i, j: (i, j, cw)),
                  pl.BlockSpec((LANE, GLA_KEY), lambda i, j: (0, 0)),
                  pl.BlockSpec((1, GLA_KEY), lambda i, j: (0, 0)),
                  pl.BlockSpec((1, GLA_DV), lambda i, j: (0, 0))],
        out_specs=pl.BlockSpec((nb, CHUNK, BRANCH), lambda i, j: (i, j, 0)),
        scratch_shapes=[pltpu.VMEM((nb, GLA_HEADS * GLA_DV, GLA_KEY), F32)],
        compiler_params=_cparams(("parallel", "arbitrary")),
        name="gla",
    )(z3, z3, z3, z3, z3, wa2p, ba, gn)


VT_ROWS = 2 * DIFF_D + 16


def _alibi_slope(h):
    return 2.0 ** (-8.0 * (h + 1) / DIFF_HEADS)


def _diff_kernel(q_ref, k_ref, v_ref, lq1, lk1, lq2, lk2, gn_ref, o_ref, ka_scr, vb_scr, qa_scr, acc_scr,
                 m_scr, s_scr, rel_scr, *, lambda_init, tq):
    qi = pl.program_id(1)
    hw = 2 * DIFF_D
    seq = k_ref.shape[1]
    lane = _iota2((tq, hw), 1)

    @pl.when(qi == 0)
    def _():
        key_row = _iota2((tq, hw), 0).astype(F32)
        for h in range(DIFF_HEADS):
            k_aug = []
            for half in range(2):
                key_bias = (_alibi_slope(h) * LOG2E) * (key_row + float(half * tq))
                hi = key_bias.astype(BF16).astype(F32)
                mid = (key_bias - hi).astype(BF16).astype(F32)
                lo = key_bias - hi - mid
                aug = jnp.where(lane == 0, hi, jnp.where(lane == 1, mid, jnp.where(lane == 2, lo, 0.0)))
                k_aug.append(aug.astype(BF16))
            for c in range(seq // tq):
                rows = slice(c * tq, (c + 1) * tq)
                ka_scr[h, rows, 0:hw] = k_ref[0, rows, h * hw:(h + 1) * hw].astype(BF16)
                ka_scr[h, rows, hw:2 * hw] = k_aug[c % 2]
        k_row = _iota2((2 * tq, 2 * tq), 0)
        col = _iota2((2 * tq, 2 * tq), 1)
        for par in range(2):
            q_row = jnp.where(col >= tq, col - tq, col) + par * tq
            d = q_row - k_row
            rel_scr[par] = jnp.where((k_row // CHUNK) <= (q_row // CHUNK), (d - jnp.abs(d)).astype(F32), NEG_INF)
        sub = _iota2((VT_ROWS - hw, tq), 0)
        tail = jnp.where(sub == 0, 1.0, 0.0).astype(BF16)
        for c in range(seq // tq):
            rows = slice(c * tq, (c + 1) * tq)
            for h in range(DIFF_HEADS):
                vb_scr[h, 0:hw, rows] = v_ref[0, rows, h * hw:(h + 1) * hw].astype(F32).T.astype(BF16)
                vb_scr[h, hw:VT_ROWS, rows] = tail

    lane2 = _iota2((2 * tq, hw), 1)
    ones_aug = (lane2 < 3).astype(F32)
    for h in range(DIFF_HEADS):
        q = q_ref[0, :, h * hw:(h + 1) * hw].astype(F32) * (DIFF_D ** -0.5 * LOG2E)
        q2 = jnp.concatenate([jnp.where(lane < DIFF_D, q, 0.0), jnp.where(lane < DIFF_D, 0.0, q)],
                             axis=0)
        qa_scr[h] = jnp.concatenate([q2, ones_aug], axis=1).astype(BF16)
    acc_scr[...] = jnp.zeros_like(acc_scr)
    m_scr[...] = jnp.full_like(m_scr, NEG_INF)

    tk = 2 * tq
    last_tile = qi // 2

    def tile_scores(j):
        start = pl.multiple_of(j * tk, tk)
        return [lax.dot_general(ka_scr[h, pl.ds(start, tk), :], qa_scr[h], (((1,), (1,)), ((), ())),
                                preferred_element_type=F32) for h in range(DIFF_HEADS)]

    def update(scores, j):
        start = pl.multiple_of(j * tk, tk)
        alphas, probs = [], []
        for h, s in enumerate(scores):
            m_old = m_scr[h]
            m_new = jnp.maximum(m_old, jnp.max(s, axis=0, keepdims=True))
            alphas.append(jnp.exp2(m_old - m_new))
            probs.append(jnp.exp2((s - m_new).astype(BF16)))
            m_scr[h] = m_new - _alibi_slope(h) * LOG2E * tk
        pvs = [jnp.dot(vb_scr[h, :, pl.ds(start, tk)], probs[h], preferred_element_type=F32)
               for h in range(DIFF_HEADS)]
        for h in range(DIFF_HEADS):
            acc_scr[h] = alphas[h] * acc_scr[h] + pvs[h]

    for h, s in enumerate(tile_scores(0)):
        s_scr[h] = s

    def body(j, carry):
        nxt = tile_scores(j + 1)
        update([s_scr[h] for h in range(DIFF_HEADS)], j)
        for h in range(DIFF_HEADS):
            s_scr[h] = nxt[h]
        return carry

    lax.fori_loop(0, last_tile, body, 0)

    rel = rel_scr[qi % 2]
    lam = (jnp.exp(jnp.sum(lq1[...] * lk1[...], axis=-1, keepdims=True))
           - jnp.exp(jnp.sum(lq2[...] * lk2[...], axis=-1, keepdims=True))
           + lambda_init)
    update([s_scr[h] + (_alibi_slope(h) * LOG2E) * rel for h in range(DIFF_HEADS)], last_tile)
    for h in range(DIFF_HEADS):
        on = acc_scr[h, 0:hw, :] / acc_scr[h, hw:hw + 1, :]
        o = (on[:, :tq] - lam * on[:, tq:]).T
        o_ref[0, :, h * hw:(h + 1) * hw] = (_rms(o, gn_ref[...], HEAD_EPS) * (1.0 - lambda_init)).astype(BF16)


def _diff_mixer(z3, lq1, lk1, lq2, lk2, gn, lambda_init):
    b, s = z3.shape[0], z3.shape[1]
    tq = _tile(s // 2, 256)
    assert s % (2 * tq) == 0 and tq % CHUNK == 0
    hw = 2 * DIFF_D
    cq = COL_DIFF // BRANCH
    nh = DIFF_HEADS
    vec = pl.BlockSpec((1, DIFF_D), lambda i, j: (0, 0))
    return pl.pallas_call(
        functools.partial(_diff_kernel, lambda_init=lambda_init, tq=tq),
        out_shape=jax.ShapeDtypeStruct((b, s, BRANCH), BF16),
        grid=(b, s // tq),
        in_specs=[pl.BlockSpec((1, tq, BRANCH), lambda i, j: (i, j, cq)),
                  pl.BlockSpec((1, s, BRANCH), lambda i, j: (i, 0, cq + 1), pipeline_mode=pl.Buffered(1)),
                  pl.BlockSpec((1, s, BRANCH), lambda i, j: (i, 0, cq + 2), pipeline_mode=pl.Buffered(1)),
                  vec, vec, vec, vec,
                  pl.BlockSpec((1, hw), lambda i, j: (0, 0))],
        out_specs=pl.BlockSpec((1, tq, BRANCH), lambda i, j: (i, j, 0)),
        scratch_shapes=[pltpu.VMEM((nh, s, 2 * hw), BF16), pltpu.VMEM((nh, VT_ROWS, s), BF16),
                        pltpu.VMEM((nh, 2 * tq, 2 * hw), BF16), pltpu.VMEM((nh, VT_ROWS, 2 * tq), F32),
                        pltpu.VMEM((nh, 1, 2 * tq), F32), pltpu.VMEM((nh, 2 * tq, 2 * tq), F32),
                        pltpu.VMEM((2, 2 * tq, 2 * tq), F32)],
        compiler_params=_cparams(("parallel", "arbitrary")),
        name="diff_attn",
    )(z3, z3, z3, lq1, lk1, lq2, lk2, gn)


def _shift(x, prev_rows):
    row = _iota2(x.shape, 0)
    return jnp.where(row == 0, prev_rows[7:8, :], pltpu.roll(x, 1, axis=0))


def _rwkv_kernel(*refs, has_vres, nb):
    if has_vres:
        (rkv_ref, gl_ref, wb_ref, ab_ref, vf_ref, mu_rkv, mu_g, mu_w, mu_a, w0, w2, a0, a2, g2, kk_s, ka_s,
         rk_s, lnw, lnb, v0, v1, v2, o_ref, p_rkv, p_g, p_w, p_a, state) = refs
    else:
        (rkv_ref, gl_ref, wb_ref, ab_ref, mu_rkv, mu_g, mu_w, mu_a, w0, w2, a0, a2, g2, kk_s, ka_s,
         rk_s, lnw, lnb, o_ref, vf_out, p_rkv, p_g, p_w, p_a, state) = refs

    @pl.when(pl.program_id(1) == 0)
    def _():
        for r in (p_rkv, p_g, p_w, p_a, state):
            r[...] = jnp.zeros_like(r)

    def stack(parts):
        return parts[0] if nb == 1 else jnp.concatenate(parts, axis=0)

    def mixed(x_ref, p_ref, mu_ref):
        parts = []
        for b in range(nb):
            x = x_ref[b].astype(F32)
            parts.append(x + (_shift(x, p_ref[b]) - x) * mu_ref[...])
            p_ref[b] = x[CHUNK - 8:, :]
        return stack(parts)

    rkv = mixed(rkv_ref, p_rkv, mu_rkv)
    g_lr = mixed(gl_ref, p_g, mu_g)
    w_lr = mixed(wb_ref, p_w, mu_w)
    a_lr = mixed(ab_ref, p_a, mu_a)
    r = rkv[:, 0:BRANCH]
    k = rkv[:, BRANCH:2 * BRANCH]
    v = rkv[:, 2 * BRANCH:3 * BRANCH]
    nrow = nb * CHUNK

    y = w0[...] + _bdot(jnp.tanh(w_lr), w2[...])
    softplus = jnp.maximum(-y, 0.0) + jnp.log(1.0 + jnp.exp(-jnp.abs(y)))
    lw = -jnp.exp(-softplus - 0.5)
    a = _sigmoid(a0[...] + _bdot(a_lr, a2[...]))
    gate = _bdot(_sigmoid(g_lr), g2[...])
    if has_vres:
        vf = stack([vf_ref[b] for b in range(nb)])
        v = v + (vf - v) * _sigmoid(v0[...] + _bdot(_bdot(v, v1[...]), v2[...]))
    else:
        for b in range(nb):
            vf_out[b] = v[b * CHUNK:(b + 1) * CHUNK]

    ones = _head_ones(BRANCH, RWKV_HD)
    kk = k * kk_s[...]
    k = k * (1.0 + (a - 1.0) * ka_s[...])
    sums = _bdot(jnp.concatenate([kk * kk, r * k * rk_s[...]], axis=0), ones)
    kk = kk / jnp.maximum(jnp.sqrt(sums[:nrow]), 1e-12)
    bonus = sums[nrow:] * v

    ri, ci = _iota2((nrow, nrow), 0), _iota2((nrow, nrow), 1)
    tril = ((ri // CHUNK == ci // CHUNK) & (ri >= ci)).astype(BF16)
    cum = _split_dot_left(tril, lw)
    lasts = [cum[(b + 1) * CHUNK - 1:(b + 1) * CHUNK, :] for b in range(nb)]
    to_end = stack([jnp.exp(lasts[b] - cum[b * CHUNK:(b + 1) * CHUNK]) for b in range(nb)])
    decay_all = [jnp.exp(last) for last in lasts]
    a_t = -kk * jnp.exp(cum - lw)
    r_t = r * jnp.exp(cum)
    inv = jnp.exp(-cum)
    kka = kk * a
    b_s = kka * inv
    k_s = k * inv
    b_e = kka * to_end
    k_e = k * to_end

    rows = HEAD_GROUP * CHUNK
    blk = ((_iota2((rows, GROUP_W), 0) // CHUNK) == (_iota2((rows, GROUP_W), 1) // RWKV_HD))
    blk_bf = blk.astype(F32).astype(BF16)
    t_idx = _iota2((CHUNK, GROUP_W), 0)
    s_idx = _iota2((CHUNK, GROUP_W), 1) % CHUNK
    strict = t_idx > s_idx
    incl = t_idx >= s_idx
    eye = (t_idx == s_idx).astype(F32)

    def blocks(x):
        return jnp.concatenate([x.astype(BF16)] * HEAD_GROUP, axis=0) * blk_bf

    ngroup = RWKV_HEADS // HEAD_GROUP
    probs = [(b, gi) for b in range(nb) for gi in range(ngroup)]
    nprob = len(probs)

    def part(t):
        return [t[b * CHUNK:(b + 1) * CHUNK, gi * GROUP_W:(gi + 1) * GROUP_W] for b, gi in probs]

    cat0 = lambda x, y_: jnp.concatenate([x, y_], axis=0)
    a_l, r_l, b_l, k_l, be_l, ke_l, v_l = (part(t) for t in (a_t, r_t, b_s, k_s, b_e, k_e, v))
    pm = [_bdot_nt(cat0(a_l[i], r_l[i]), cat0(blocks(b_l[i]), blocks(k_l[i]))) for i in range(nprob)]
    n_ab = [jnp.where(strict, p[:CHUNK, :GROUP_W], 0.0) for p in pm]
    a_ak = [jnp.where(strict, p[:CHUNK, GROUP_W:], 0.0) for p in pm]
    a_rb = [jnp.where(incl, p[CHUNK:, :GROUP_W], 0.0) for p in pm]
    a_rk = [jnp.where(incl, p[CHUNK:, GROUP_W:], 0.0) for p in pm]
    t_inv = [eye + n for n in n_ab]
    n_pow = n_ab
    n_blk = [blocks(n) for n in n_pow]
    for _ in range(int(math.log2(CHUNK)) - 1):
        n_pow = [_bdot(n_pow[i], n_blk[i]) for i in range(nprob)]
        n_blk = [blocks(n) for n in n_pow]
        t_inv = [t_inv[i] + _bdot(t_inv[i], n_blk[i]) for i in range(nprob)]
    av = [_bdot(cat0(a_ak[i], a_rk[i]), blocks(v_l[i])) for i in range(nprob)]
    y2 = [_bdot(t_inv[i], jnp.concatenate([blocks(a_l[i]), blocks(av[i][:CHUNK])], axis=1))
          for i in range(nprob)]
    st = [state[b, gi] for b, gi in probs]
    x2 = [_bdot_nt(cat0(y2[i][:, :GROUP_W], r_l[i]), st[i]) for i in range(nprob)]
    u_l = [x2[i][:CHUNK] + y2[i][:, GROUP_W:] for i in range(nprob)]
    o_l = [x2[i][CHUNK:] + _bdot(a_rb[i], blocks(u_l[i])) + av[i][CHUNK:] for i in range(nprob)]
    for i, (b, gi) in enumerate(probs):
        outer = _bdot_tn(cat0(u_l[i], v_l[i]), cat0(be_l[i], ke_l[i]))
        state[b, gi] = (st[i] * decay_all[b][:, gi * GROUP_W:(gi + 1) * GROUP_W]
                        + jnp.where(blk, outer, 0.0))
    o = stack([jnp.concatenate([o_l[b * ngroup + gi] for gi in range(ngroup)], axis=1)
               for b in range(nb)])

    mean = _bdot(o, ones) * (1.0 / RWKV_HD)
    cen = o - mean
    var = _bdot(cen * cen, ones) * (1.0 / RWKV_HD)
    o = cen * lax.rsqrt(var + RWKV_GN_EPS) * lnw[...] + lnb[...]
    o = ((o + bonus) * gate).astype(BF16)
    for b in range(nb):
        o_ref[b] = o[b * CHUNK:(b + 1) * CHUNK]


def _rwkv_mixer(z3, v_first, p):
    b, s = z3.shape[0], z3.shape[1]
    has_vres = v_first is not None
    nb = RWKV_SEQS_PER_STEP if b % RWKV_SEQS_PER_STEP == 0 else 1
    row = lambda w, c: pl.BlockSpec((nb, CHUNK, w), lambda i, j: (i, j, c))
    full = lambda a: pl.BlockSpec(a.shape, lambda i, j: (0,) * a.ndim)
    ins = [z3, z3, z3, z3]
    specs = [row(3 * BRANCH, COL_RWKV // (3 * BRANCH)), row(RWKV_GATE_LORA, COL_RW_G // RWKV_GATE_LORA),
             row(LANE, COL_RW_W // LANE), row(LANE, COL_RW_A // LANE)]
    if has_vres:
        ins.append(v_first)
        specs.append(row(BRANCH, 0))
    names = ["mu_rkv", "mu_g", "mu_w", "mu_a", "w0", "w2", "a0", "a2", "g2", "kk", "ka", "rk", "lnw", "lnb"]
    if has_vres:
        names += ["v0", "v1", "v2"]
    for nme in names:
        ins.append(p[nme])
        specs.append(full(p[nme]))
    o_spec = pl.BlockSpec((nb, CHUNK, BRANCH), lambda i, j: (i, j, 0))
    o_shape = jax.ShapeDtypeStruct((b, s, BRANCH), BF16)
    if has_vres:
        out_shape, out_specs = o_shape, o_spec
    else:
        out_shape = (o_shape, jax.ShapeDtypeStruct((b, s, BRANCH), F32))
        out_specs = (o_spec, pl.BlockSpec((nb, CHUNK, BRANCH), lambda i, j: (i, j, 0)))
    res = pl.pallas_call(
        functools.partial(_rwkv_kernel, has_vres=has_vres, nb=nb),
        out_shape=out_shape,
        grid=(b // nb, s // CHUNK),
        in_specs=specs,
        out_specs=out_specs,
        scratch_shapes=[pltpu.VMEM((nb, 8, 3 * BRANCH), F32), pltpu.VMEM((nb, 8, RWKV_GATE_LORA), F32),
                        pltpu.VMEM((nb, 8, LANE), F32), pltpu.VMEM((nb, 8, LANE), F32),
                        pltpu.VMEM((nb, RWKV_HEADS // HEAD_GROUP, GROUP_W, GROUP_W), F32)],
        compiler_params=_cparams(("parallel", "arbitrary")),
        name="rwkv7",
    )(*ins)
    if has_vres:
        return res, v_first
    return res[0], res[1]


def _pad_rows(w, rows, offset=0):
    return jnp.zeros((rows, w.shape[1]), F32).at[offset:offset + w.shape[0]].set(w)


N_IN = 14800
COL_GATES_SRC = 6608
MIX_PIECES = ((COL_CONV, 0, 3072), (COL_DIFF, 3088, 1536), (COL_RWKV, 4624, 1536), (COL_RW_G, 6352, 256),
              (COL_RW_W, 6160, RWKV_DECAY_LORA), (COL_RW_A, 6256, RWKV_AAA_LORA),
              (COL_RW_A + GLA_WLR_LANE, 3072, GLA_LOW_RANK))
MIX_ZERO = ((COL_RW_W + RWKV_DECAY_LORA, LANE - RWKV_DECAY_LORA),
            (COL_RW_A + GLA_WLR_LANE + GLA_LOW_RANK, LANE - GLA_WLR_LANE - GLA_LOW_RANK))


def _pack_w_in_kernel(w_ref, mix_ref, gate_ref):
    for dst, src, n in MIX_PIECES:
        mix_ref[:, dst:dst + n] = w_ref[:, src:src + n].astype(BF16)
    for dst, n in MIX_ZERO:
        mix_ref[:, dst:dst + n] = jnp.zeros((mix_ref.shape[0], n), BF16)
    gate_ref[...] = w_ref[:, COL_GATES_SRC:N_IN].astype(BF16)


def _pack_w_in(w_in):
    depth = w_in.shape[0]
    tr = 128
    n_gate = N_IN - COL_GATES_SRC
    return pl.pallas_call(
        _pack_w_in_kernel,
        out_shape=(jax.ShapeDtypeStruct((depth, D_MODEL, N_MIX), BF16),
                   jax.ShapeDtypeStruct((depth, D_MODEL, n_gate), BF16)),
        grid=(depth, D_MODEL // tr),
        in_specs=[pl.BlockSpec((None, tr, N_IN), lambda l, i: (l, i, 0))],
        out_specs=(pl.BlockSpec((None, tr, N_MIX), lambda l, i: (l, i, 0)),
                   pl.BlockSpec((None, tr, n_gate), lambda l, i: (l, i, 0))),
        compiler_params=_cparams(("parallel", "parallel")),
        name="pack_w_in",
    )(w_in)


def _pad_lanes(v, width, offset=0):
    return jnp.zeros((1, width), F32).at[0, offset:offset + v.shape[0]].set(v)


def kernel(x, norm_mix_pre, w_in, conv_w, gla_wa2, gla_ba, gla_norm, diff_lq1, diff_lk1, diff_lq2, diff_lk2,
           diff_norm, rw_mu, rw_w0, rw_w2, rw_a0, rw_a2, rw_g2, rw_kk, rw_ka, rw_rk, rw_lnw, rw_lnb, rw_v0,
           rw_v1, rw_v2, w_branch, w_out, norm_mix_post, norm_ffn_pre, w_gate, w_up, w_down, norm_ffn_post):
    bsz, seq = x.shape[0], x.shape[1]
    depth = w_in.shape[0]
    w_branch_b, w_out_b, w_down_b = (w.astype(BF16) for w in (w_branch, w_out, w_down))
    w_mix, w_gates = _pack_w_in(w_in)
    t = bsz * seq
    x2 = x.reshape(t, D_MODEL)
    row = lambda v: v.reshape(1, -1)
    v_first = None
    h = _norm(x2, row(norm_mix_pre[0]))
    for l in range(depth):
        z3 = _inproj(h, w_mix, l).reshape(bsz, seq, N_MIX)
        lambda_init = 0.8 - 0.6 * math.exp(-0.3 * l)

        o_conv = _conv_mixer(z3, conv_w[l])
        o_gla = _gla_mixer(z3, _pad_rows(gla_wa2[l], LANE, GLA_WLR_LANE), row(gla_ba[l]), row(gla_norm[l]))
        o_diff = _diff_mixer(z3, row(diff_lq1[l]), row(diff_lk1[l]), row(diff_lq2[l]),
                             row(diff_lk2[l]), row(diff_norm[l]), lambda_init)
        mu = rw_mu[l]
        rp = {
            "mu_rkv": row(mu[0:1536]),
            "mu_w": _pad_lanes(mu[1536:1632], LANE),
            "mu_a": _pad_lanes(mu[1632:1728], LANE),
            "mu_g": row(mu[1728:1984]),
            "w0": row(rw_w0[l]), "w2": _pad_rows(rw_w2[l], LANE),
            "a0": row(rw_a0[l]), "a2": _pad_rows(rw_a2[l], LANE),
            "g2": rw_g2[l], "kk": row(rw_kk[l]), "ka": row(rw_ka[l]), "rk": row(rw_rk[l]),
            "lnw": row(rw_lnw[l]), "lnb": row(rw_lnb[l]),
        }
        if l > 0:
            rp.update(v0=row(rw_v0[l - 1]), v1=rw_v1[l - 1], v2=rw_v2[l - 1])
        o_rwkv, v_first = _rwkv_mixer(z3, v_first if l > 0 else None, rp)

        outs = [o.reshape(t, BRANCH) for o in (o_conv, o_gla, o_diff, o_rwkv)]
        merged = _merge(h, w_gates, outs, w_branch_b, l)
        x2, h_ffn = _proj_norm_res(merged, w_out_b, l, x2, row(norm_mix_post[l]),
                                   row(norm_ffn_pre[l]), D_MODEL)
        act = _ffn_up(h_ffn, w_gate, w_up, l)
        next_gain = row(norm_mix_pre[l + 1]) if l + 1 < depth else None
        x2, h = _proj_norm_res(act, w_down_b, l, x2, row(norm_ffn_post[l]), next_gain, FFN_DOWN_TK)
    return x2.reshape(bsz, seq, D_MODEL)
```

```python
import functools
import math

import jax
import jax.numpy as jnp
from jax import lax
from jax.experimental import pallas as pl
from jax.experimental.pallas import tpu as pltpu

F32 = jnp.float32
BF16 = jnp.bfloat16

D_MODEL = 2048
CHUNK = 64
N_BRANCH = 4
BRANCH = 512
GLA_HEADS = 4
GLA_DK = 64
GLA_DV = 128
GLA_KEY = GLA_HEADS * GLA_DK
GLA_LOW_RANK = 16
GLA_LOGIT_NORM = 16.0
DIFF_HEADS = 4
DIFF_D = 64
RWKV_HEADS = 8
RWKV_HD = 64
RWKV_DECAY_LORA = 96
RWKV_AAA_LORA = 96
RWKV_MV_LORA = 64
RWKV_GATE_LORA = 256
D_FF = 5632
RMS_EPS = 1e-6
HEAD_EPS = 1e-5
RWKV_GN_EPS = 64e-5
NEG_INF = -1e30
LOG2E = 1.4426950408889634

COL_CONV = 0
COL_GLA = 1536
COL_DIFF = 3072
COL_RWKV = 4608
COL_RW_G = 6144
COL_RW_W = 6400
COL_RW_A = 6528
GLA_WLR_LANE = RWKV_AAA_LORA
N_MIX = 6656
LANE = 128
HEAD_GROUP = 4
GROUP_W = HEAD_GROUP * 64
RWKV_SEQS_PER_STEP = 4

VMEM_LIMIT = 56 * 1024 * 1024
PROJ_ROW_SPLIT = 4
FFN_DOWN_TK = D_FF // 2


def _tile(n, pref):
    t = min(n, pref)
    while n % t:
        t -= 8
    return t


def _cparams(sem):
    return pltpu.CompilerParams(dimension_semantics=sem, vmem_limit_bytes=VMEM_LIMIT)


def _bdot(a, b):
    return jnp.dot(a.astype(BF16), b.astype(BF16), preferred_element_type=F32)


def _bdot_nt(a, b):
    return lax.dot_general(a.astype(BF16), b.astype(BF16), (((1,), (1,)), ((), ())),
                           preferred_element_type=F32)


def _bdot_tn(a, b):
    return lax.dot_general(a.astype(BF16), b.astype(BF16), (((0,), (0,)), ((), ())),
                           preferred_element_type=F32)


def _split_dot_left(ones_bf16, x):
    hi = x.astype(BF16)
    lo = (x - hi.astype(F32)).astype(BF16)
    return (jnp.dot(ones_bf16, hi, preferred_element_type=F32)
            + jnp.dot(ones_bf16, lo, preferred_element_type=F32))


def _sigmoid(x):
    return 0.5 * jnp.tanh(0.5 * x) + 0.5


def _rms(x, gain, eps):
    return x * lax.rsqrt(jnp.mean(x * x, axis=-1, keepdims=True) + eps) * gain


def _iota2(shape, dim):
    return lax.broadcasted_iota(jnp.int32, shape, dim)


def _tril_incl(n):
    return (_iota2((n, n), 0) >= _iota2((n, n), 1)).astype(BF16)


def _head_ones(n, width):
    return ((_iota2((n, n), 0) // width) == (_iota2((n, n), 1) // width)).astype(BF16)


def _expand(x, col_group):
    rows = HEAD_GROUP * CHUNK
    xt = jnp.concatenate([x] * HEAD_GROUP, axis=0)
    keep = (_iota2((rows, x.shape[1]), 0) // CHUNK) == (_iota2((rows, x.shape[1]), 1) // col_group)
    return jnp.where(keep, xt, 0.0)


def _norm_kernel(x_ref, g_ref, h_ref):
    h_ref[...] = _rms(x_ref[...], g_ref[...], RMS_EPS).astype(BF16)


def _norm(x2, gain):
    t = x2.shape[0]
    tm = _tile(t, 512)
    return pl.pallas_call(
        _norm_kernel,
        out_shape=jax.ShapeDtypeStruct((t, D_MODEL), BF16),
        grid=(t // tm,),
        in_specs=[pl.BlockSpec((tm, D_MODEL), lambda i: (i, 0)),
                  pl.BlockSpec((1, D_MODEL), lambda i: (0, 0))],
        out_specs=pl.BlockSpec((tm, D_MODEL), lambda i: (i, 0)),
        compiler_params=_cparams(("parallel",)),
        name="norm",
    )(x2, gain)


def _inproj_kernel(h_ref, w_ref, z_ref):
    z_ref[...] = jnp.dot(h_ref[...], w_ref[...], preferred_element_type=F32).astype(BF16)


def _inproj(h, w, l):
    t, n = h.shape[0], w.shape[2]
    tm, tn = _tile(t, 1024), _tile(n, 1664)
    return pl.pallas_call(
        _inproj_kernel,
        out_shape=jax.ShapeDtypeStruct((t, n), BF16),
        grid=(t // tm, n // tn),
        in_specs=[pl.BlockSpec((tm, D_MODEL), lambda i, j: (i, 0)),
                  pl.BlockSpec((None, D_MODEL, tn), lambda i, j: (l, 0, j))],
        out_specs=pl.BlockSpec((tm, tn), lambda i, j: (i, j)),
        compiler_params=_cparams(("parallel", "arbitrary")),
        name="inproj",
    )(h, w)


def _merge_kernel(h_ref, wg0, wg1, wg2, wg3, o0, o1, o2, o3, p_ref, out_ref):
    h = h_ref[...]
    acc = None
    for n, (wg, o) in enumerate(((wg0, o0), (wg1, o1), (wg2, o2), (wg3, o3))):
        gate = _sigmoid(jnp.dot(h, wg[...], preferred_element_type=F32))
        term = gate * jnp.dot(o[...], p_ref[n], preferred_element_type=F32)
        acc = term if acc is None else acc + term
    out_ref[...] = acc.astype(BF16)


def _merge(h, w_gate, outs, w_branch, l):
    t = h.shape[0]
    tm, tn = _tile(t, 1024), 512
    nj = D_MODEL // tn
    gate_specs = [pl.BlockSpec((None, D_MODEL, tn), functools.partial(lambda i, j, n: (l, 0, n * nj + j), n=n))
                  for n in range(N_BRANCH)]
    o_specs = [pl.BlockSpec((tm, BRANCH), lambda i, j: (i, 0)) for _ in range(N_BRANCH)]
    return pl.pallas_call(
        _merge_kernel,
        out_shape=jax.ShapeDtypeStruct((t, D_MODEL), BF16),
        grid=(t // tm, nj),
        in_specs=[pl.BlockSpec((tm, D_MODEL), lambda i, j: (i, 0))] + gate_specs + o_specs
                 + [pl.BlockSpec((None, N_BRANCH, BRANCH, tn), lambda i, j: (l, 0, 0, j))],
        out_specs=pl.BlockSpec((tm, tn), lambda i, j: (i, j)),
        compiler_params=_cparams(("parallel", "arbitrary")),
        name="merge",
    )(h, w_gate, w_gate, w_gate, w_gate, *outs, w_branch)


def _proj_norm_res_kernel(a_ref, w_ref, x_ref, g_ref, gn_ref, o_ref, *h_ref, nk):
    k = pl.program_id(1)

    def accumulate(first):
        part = jnp.dot(a_ref[...], w_ref[...], preferred_element_type=F32)
        o_ref[...] = part if first else o_ref[...] + part

    def finish():
        sub = o_ref.shape[0] // PROJ_ROW_SPLIT
        for r in range(PROJ_ROW_SPLIT):
            rows = slice(r * sub, (r + 1) * sub)
            y = jnp.dot(a_ref[rows, :], w_ref[...], preferred_element_type=F32)
            if nk > 1:
                y = y + o_ref[rows, :]
            xn = x_ref[rows, :] + _rms(y, g_ref[...], RMS_EPS)
            o_ref[rows, :] = xn
            if h_ref:
                h_ref[0][rows, :] = _rms(xn, gn_ref[...], RMS_EPS).astype(BF16)

    if nk == 1:
        finish()
    else:
        pl.when(k == 0)(functools.partial(accumulate, True))
        if nk > 2:
            pl.when((k > 0) & (k < nk - 1))(functools.partial(accumulate, False))
        pl.when(k == nk - 1)(finish)


def _proj_norm_res(a, w, l, x2, gain, next_gain, tk):
    t, kdim = a.shape
    tm = _tile(t, 512)
    emit_h = next_gain is not None
    row_spec = pl.BlockSpec((tm, D_MODEL), lambda i, k: (i, 0))
    vec_spec = pl.BlockSpec((1, D_MODEL), lambda i, k: (0, 0))
    out_shape = [jax.ShapeDtypeStruct((t, D_MODEL), F32)]
    if emit_h:
        out_shape.append(jax.ShapeDtypeStruct((t, D_MODEL), BF16))
    assert kdim % tk == 0 and tm % (8 * PROJ_ROW_SPLIT) == 0
    res = pl.pallas_call(
        functools.partial(_proj_norm_res_kernel, nk=kdim // tk),
        out_shape=tuple(out_shape),
        grid=(t // tm, kdim // tk),
        in_specs=[pl.BlockSpec((tm, tk), lambda i, k: (i, k)),
                  pl.BlockSpec((None, tk, D_MODEL), lambda i, k: (l, k, 0)),
                  row_spec, vec_spec, vec_spec],
        out_specs=tuple([row_spec] * len(out_shape)),
        compiler_params=_cparams(("parallel", "arbitrary")),
        name="proj_norm_res",
    )(a, w, x2, gain, next_gain if emit_h else gain)
    return (res[0], res[1]) if emit_h else (res[0], None)


def _ffn_up_kernel(h_ref, wg_ref, wu_ref, a_ref, wg_b, wu_b):
    @pl.when(pl.program_id(1) == 0)
    def _():
        wg_b[...] = wg_ref[...].astype(BF16)
        wu_b[...] = wu_ref[...].astype(BF16)

    h = h_ref[...]
    gt = jnp.dot(h, wg_b[...], preferred_element_type=F32)
    up = jnp.dot(h, wu_b[...], preferred_element_type=F32)
    a_ref[...] = (gt * _sigmoid(gt) * up).astype(BF16)


def _ffn_up(h, wg, wu, l):
    t = h.shape[0]
    tm, tn = _tile(t, 2048), 512
    return pl.pallas_call(
        _ffn_up_kernel,
        out_shape=jax.ShapeDtypeStruct((t, D_FF), BF16),
        grid=(D_FF // tn, t // tm),
        in_specs=[pl.BlockSpec((tm, D_MODEL), lambda j, i: (i, 0)),
                  pl.BlockSpec((None, D_MODEL, tn), lambda j, i: (l, 0, j)),
                  pl.BlockSpec((None, D_MODEL, tn), lambda j, i: (l, 0, j))],
        out_specs=pl.BlockSpec((tm, tn), lambda j, i: (i, j)),
        scratch_shapes=[pltpu.VMEM((D_MODEL, tn), BF16), pltpu.VMEM((D_MODEL, tn), BF16)],
        compiler_params=_cparams(("parallel", "arbitrary")),
        name="ffn_up",
    )(h, wg, wu)


def _conv_kernel(b_ref, c_ref, u_ref, w_ref, o_ref, carry):
    @pl.when(pl.program_id(1) == 0)
    def _():
        carry[...] = jnp.zeros_like(carry)

    cu = c_ref[0].astype(F32) * u_ref[0].astype(F32)
    ts = cu.shape[0]
    row = _iota2(cu.shape, 0)
    p1 = carry[7:8, :]
    p2 = carry[6:7, :]
    s1 = jnp.where(row == 0, p1, pltpu.roll(cu, 1, axis=0))
    s2 = jnp.where(row == 0, p2, jnp.where(row == 1, p1, pltpu.roll(cu, 2, axis=0)))
    w = w_ref[...]
    y = w[2:3, :] * cu + w[1:2, :] * s1 + w[0:1, :] * s2
    o_ref[0] = (b_ref[0].astype(F32) * y).astype(BF16)
    carry[...] = cu[ts - 8:, :]


def _conv_mixer(z3, conv_w):
    b, s = z3.shape[0], z3.shape[1]
    ts = _tile(s, 512)
    c0 = COL_CONV // BRANCH
    return pl.pallas_call(
        _conv_kernel,
        out_shape=jax.ShapeDtypeStruct((b, s, BRANCH), BF16),
        grid=(b, s // ts),
        in_specs=[pl.BlockSpec((1, ts, BRANCH), lambda i, j: (i, j, c0)),
                  pl.BlockSpec((1, ts, BRANCH), lambda i, j: (i, j, c0 + 1)),
                  pl.BlockSpec((1, ts, BRANCH), lambda i, j: (i, j, c0 + 2)),
                  pl.BlockSpec((3, BRANCH), lambda i, j: (0, 0))],
        out_specs=pl.BlockSpec((1, ts, BRANCH), lambda i, j: (i, j, 0)),
        scratch_shapes=[pltpu.VMEM((8, BRANCH), F32)],
        compiler_params=_cparams(("parallel", "arbitrary")),
        name="conv",
    )(z3, z3, z3, conv_w)


def _gla_kernel(q_ref, k_ref, v_ref, g_ref, wl_ref, wa2_ref, ba_ref, gn_ref, o_ref, state):
    @pl.when(pl.program_id(1) == 0)
    def _():
        state[...] = jnp.zeros_like(state)

    nb = q_ref.shape[0]
    causal = _iota2((CHUNK, GLA_KEY), 0) >= (_iota2((CHUNK, GLA_KEY), 1) % CHUNK)
    same_head = ((_iota2((GLA_HEADS * GLA_DV, GLA_KEY), 0) // GLA_DV)
                 == (_iota2((GLA_HEADS * GLA_DV, GLA_KEY), 1) // GLA_DK))
    tril = _tril_incl(CHUNK)
    q_in, k_in, q_st, k_st, v, ve, dec = [], [], [], [], [], [], []
    for b in range(nb):
        q = q_ref[b].astype(F32) * (GLA_DK ** -0.5)
        k = k_ref[b].astype(F32)
        z = _bdot(wl_ref[b], wa2_ref[...]) + ba_ref[...]
        log_a = (jnp.minimum(z, 0.0) - jnp.log(1.0 + jnp.exp(-jnp.abs(z)))) * (1.0 / GLA_LOGIT_NORM)
        cum = _split_dot_left(tril, log_a)
        mid = cum[CHUNK // 2 - 1:CHUNK // 2, :]
        last = cum[CHUNK - 1:CHUNK, :]
        q_in.append(q * jnp.exp(cum - mid))
        k_in.append(_expand(k * jnp.exp(mid - cum), GLA_DK).astype(BF16))
        q_st.append(q * jnp.exp(cum))
        k_st.append(k * jnp.exp(last - cum))
        v.append(v_ref[b])
        ve.append(_expand(v[b].astype(F32), GLA_DV).astype(BF16))
        dec.append(jnp.exp(last))
    scores = [jnp.where(causal, _bdot_nt(q_in[b], k_in[b]), 0.0) for b in range(nb)]
    st = [state[b] for b in range(nb)]
    o_l = [_bdot(scores[b], ve[b]) + _bdot_nt(q_st[b], st[b]) for b in range(nb)]
    for b in range(nb):
        state[b] = st[b] * dec[b] + jnp.where(same_head, _bdot_tn(v[b], k_st[b]), 0.0)
    gn = gn_ref[...]
    for b in range(nb):
        o = o_l[b]
        g = g_ref[b].astype(F32)
        gate = g * _sigmoid(g)
        outs = [_rms(o[:, h * GLA_DV:(h + 1) * GLA_DV], gn, HEAD_EPS) for h in range(GLA_HEADS)]
        o_ref[b] = (jnp.concatenate(outs, axis=1) * gate).astype(BF16)


def _gla_mixer(z3, wa2p, ba, gn):
    b, s = z3.shape[0], z3.shape[1]
    cq = COL_GLA // GLA_KEY
    cv = (COL_GLA + 2 * GLA_KEY) // BRANCH
    cw = COL_RW_A // LANE
    nb = RWKV_SEQS_PER_STEP if b % RWKV_SEQS_PER_STEP == 0 else 1
    return pl.pallas_call(
        _gla_kernel,
        out_shape=jax.ShapeDtypeStruct((b, s, BRANCH), BF16),
        grid=(b // nb, s // CHUNK),
        in_specs=[pl.BlockSpec((nb, CHUNK, GLA_KEY), lambda i, j: (i, j, cq)),
                  pl.BlockSpec((nb, CHUNK, GLA_KEY), lambda i, j: (i, j, cq + 1)),
                  pl.BlockSpec((nb, CHUNK, BRANCH), lambda i, j: (i, j, cv)),
                  pl.BlockSpec((nb, CHUNK, BRANCH), lambda i, j: (i, j, cv + 1)),
                  pl.BlockSpec((nb, CHUNK, LANE), lambda i, j: (i, j, cw)),
                  pl.BlockSpec((LANE, GLA_KEY), lambda i, j: (0, 0)),
                  pl.BlockSpec((1, GLA_KEY), lambda i, j: (0, 0)),
                  pl.BlockSpec((1, GLA_DV), lambda i, j: (0, 0))],
        out_specs=pl.BlockSpec((nb, CHUNK, BRANCH), lambda i, j: (i, j, 0)),
        scratch_shapes=[pltpu.VMEM((nb, GLA_HEADS * GLA_DV, GLA_KEY), F32)],
        compiler_params=_cparams(("parallel", "arbitrary")),
        name="gla",
    )(z3, z3, z3, z3, z3, wa2p, ba, gn)


VT_ROWS = 2 * DIFF_D + 16


def _alibi_slope(h):
    return 2.0 ** (-8.0 * (h + 1) / DIFF_HEADS)


def _diff_kernel(q_ref, k_ref, v_ref, lq1, lk1, lq2, lk2, gn_ref, o_ref, ka_scr, vb_scr, qa_scr, acc_scr,
                 m_scr, s_scr, rel_scr, *, lambda_init, tq):
    qi = pl.program_id(1)
    hw = 2 * DIFF_D
    seq = k_ref.shape[1]
    lane = _iota2((tq, hw), 1)

    @pl.when(qi == 0)
    def _():
        key_row = _iota2((tq, hw), 0).astype(F32)
        for h in range(DIFF_HEADS):
            k_aug = []
            for half in range(2):
                key_bias = (_alibi_slope(h) * LOG2E) * (key_row + float(half * tq))
                hi = key_bias.astype(BF16).astype(F32)
                mid = (key_bias - hi).astype(BF16).astype(F32)
                lo = key_bias - hi - mid
                aug = jnp.where(lane == 0, hi, jnp.where(lane == 1, mid, jnp.where(lane == 2, lo, 0.0)))
                k_aug.append(aug.astype(BF16))
            for c in range(seq // tq):
                rows = slice(c * tq, (c + 1) * tq)
                ka_scr[h, rows, 0:hw] = k_ref[0, rows, h * hw:(h + 1) * hw].astype(BF16)
                ka_scr[h, rows, hw:2 * hw] = k_aug[c % 2]
        k_row = _iota2((2 * tq, 2 * tq), 0)
        col = _iota2((2 * tq, 2 * tq), 1)
        for par in range(2):
            q_row = jnp.where(col >= tq, col - tq, col) + par * tq
            d = q_row - k_row
            rel_scr[par] = jnp.where((k_row // CHUNK) <= (q_row // CHUNK), (d - jnp.abs(d)).astype(F32), NEG_INF)
        sub = _iota2((VT_ROWS - hw, tq), 0)
        tail = jnp.where(sub == 0, 1.0, 0.0).astype(BF16)
        for c in range(seq // tq):
            rows = slice(c * tq, (c + 1) * tq)
            for h in range(DIFF_HEADS):
                vb_scr[h, 0:hw, rows] = v_ref[0, rows, h * hw:(h + 1) * hw].astype(F32).T.astype(BF16)
                vb_scr[h, hw:VT_ROWS, rows] = tail

    lane2 = _iota2((2 * tq, hw), 1)
    ones_aug = (lane2 < 3).astype(F32)
    for h in range(DIFF_HEADS):
        q = q_ref[0, :, h * hw:(h + 1) * hw].astype(F32) * (DIFF_D ** -0.5 * LOG2E)
        q2 = jnp.concatenate([jnp.where(lane < DIFF_D, q, 0.0), jnp.where(lane < DIFF_D, 0.0, q)],
                             axis=0)
        qa_scr[h] = jnp.concatenate([q2, ones_aug], axis=1).astype(BF16)
    acc_scr[...] = jnp.zeros_like(acc_scr)
    m_scr[...] = jnp.full_like(m_scr, NEG_INF)

    tk = 2 * tq
    last_tile = qi // 2

    def tile_scores(j):
        start = pl.multiple_of(j * tk, tk)
        return [lax.dot_general(ka_scr[h, pl.ds(start, tk), :], qa_scr[h], (((1,), (1,)), ((), ())),
                                preferred_element_type=F32) for h in range(DIFF_HEADS)]

    def update(scores, j):
        start = pl.multiple_of(j * tk, tk)
        alphas, probs = [], []
        for h, s in enumerate(scores):
            m_old = m_scr[h]
            m_new = jnp.maximum(m_old, jnp.max(s, axis=0, keepdims=True))
            alphas.append(jnp.exp2(m_old - m_new))
            probs.append(jnp.exp2((s - m_new).astype(BF16)))
            m_scr[h] = m_new - _alibi_slope(h) * LOG2E * tk
        pvs = [jnp.dot(vb_scr[h, :, pl.ds(start, tk)], probs[h], preferred_element_type=F32)
               for h in range(DIFF_HEADS)]
        for h in range(DIFF_HEADS):
            acc_scr[h] = alphas[h] * acc_scr[h] + pvs[h]

    for h, s in enumerate(tile_scores(0)):
        s_scr[h] = s

    def body(j, carry):
        nxt = tile_scores(j + 1)
        update([s_scr[h] for h in range(DIFF_HEADS)], j)
        for h in range(DIFF_HEADS):
            s_scr[h] = nxt[h]
        return carry

    lax.fori_loop(0, last_tile, body, 0)

    rel = rel_scr[qi % 2]
    lam = (jnp.exp(jnp.sum(lq1[...] * lk1[...], axis=-1, keepdims=True))
           - jnp.exp(jnp.sum(lq2[...] * lk2[...], axis=-1, keepdims=True))
           + lambda_init)
    update([s_scr[h] + (_alibi_slope(h) * LOG2E) * rel for h in range(DIFF_HEADS)], last_tile)
    for h in range(DIFF_HEADS):
        on = acc_scr[h, 0:hw, :] / acc_scr[h, hw:hw + 1, :]
        o = (on[:, :tq] - lam * on[:, tq:]).T
        o_ref[0, :, h * hw:(h + 1) * hw] = (_rms(o, gn_ref[...], HEAD_EPS) * (1.0 - lambda_init)).astype(BF16)


def _diff_mixer(z3, lq1, lk1, lq2, lk2, gn, lambda_init):
    b, s = z3.shape[0], z3.shape[1]
    tq = _tile(s // 2, 256)
    assert s % (2 * tq) == 0 and tq % CHUNK == 0
    hw = 2 * DIFF_D
    cq = COL_DIFF // BRANCH
    nh = DIFF_HEADS
    vec = pl.BlockSpec((1, DIFF_D), lambda i, j: (0, 0))
    return pl.pallas_call(
        functools.partial(_diff_kernel, lambda_init=lambda_init, tq=tq),
        out_shape=jax.ShapeDtypeStruct((b, s, BRANCH), BF16),
        grid=(b, s // tq),
        in_specs=[pl.BlockSpec((1, tq, BRANCH), lambda i, j: (i, j, cq)),
                  pl.BlockSpec((1, s, BRANCH), lambda i, j: (i, 0, cq + 1), pipeline_mode=pl.Buffered(1)),
                  pl.BlockSpec((1, s, BRANCH), lambda i, j: (i, 0, cq + 2), pipeline_mode=pl.Buffered(1)),
                  vec, vec, vec, vec,
                  pl.BlockSpec((1, hw), lambda i, j: (0, 0))],
        out_specs=pl.BlockSpec((1, tq, BRANCH), lambda i, j: (i, j, 0)),
        scratch_shapes=[pltpu.VMEM((nh, s, 2 * hw), BF16), pltpu.VMEM((nh, VT_ROWS, s), BF16),
                        pltpu.VMEM((nh, 2 * tq, 2 * hw), BF16), pltpu.VMEM((nh, VT_ROWS, 2 * tq), F32),
                        pltpu.VMEM((nh, 1, 2 * tq), F32), pltpu.VMEM((nh, 2 * tq, 2 * tq), F32),
                        pltpu.VMEM((2, 2 * tq, 2 * tq), F32)],
        compiler_params=_cparams(("parallel", "arbitrary")),
        name="diff_attn",
    )(z3, z3, z3, lq1, lk1, lq2, lk2, gn)


def _shift(x, prev_rows):
    row = _iota2(x.shape, 0)
    return jnp.where(row == 0, prev_rows[7:8, :], pltpu.roll(x, 1, axis=0))


def _rwkv_kernel(*refs, has_vres, nb):
    if has_vres:
        (rkv_ref, gl_ref, wb_ref, ab_ref, vf_ref, mu_rkv, mu_g, mu_w, mu_a, w0, w2, a0, a2, g2, kk_s, ka_s,
         rk_s, lnw, lnb, v0, v1, v2, o_ref, p_rkv, p_g, p_w, p_a, state) = refs
    else:
        (rkv_ref, gl_ref, wb_ref, ab_ref, mu_rkv, mu_g, mu_w, mu_a, w0, w2, a0, a2, g2, kk_s, ka_s,
         rk_s, lnw, lnb, o_ref, vf_out, p_rkv, p_g, p_w, p_a, state) = refs

    @pl.when(pl.program_id(1) == 0)
    def _():
        for r in (p_rkv, p_g, p_w, p_a, state):
            r[...] = jnp.zeros_like(r)

    def stack(parts):
        return parts[0] if nb == 1 else jnp.concatenate(parts, axis=0)

    def mixed(x_ref, p_ref, mu_ref):
        parts = []
        for b in range(nb):
            x = x_ref[b].astype(F32)
            parts.append(x + (_shift(x, p_ref[b]) - x) * mu_ref[...])
            p_ref[b] = x[CHUNK - 8:, :]
        return stack(parts)

    rkv = mixed(rkv_ref, p_rkv, mu_rkv)
    g_lr = mixed(gl_ref, p_g, mu_g)
    w_lr = mixed(wb_ref, p_w, mu_w)
    a_lr = mixed(ab_ref, p_a, mu_a)
    r = rkv[:, 0:BRANCH]
    k = rkv[:, BRANCH:2 * BRANCH]
    v = rkv[:, 2 * BRANCH:3 * BRANCH]
    nrow = nb * CHUNK

    y = w0[...] + _bdot(jnp.tanh(w_lr), w2[...])
    lw = -math.exp(-0.5) * _sigmoid(y)
    a = _sigmoid(a0[...] + _bdot(a_lr, a2[...]))
    gate = _bdot(_sigmoid(g_lr), g2[...])
    if has_vres:
        vf = stack([vf_ref[b] for b in range(nb)])
        v = v + (vf - v) * _sigmoid(v0[...] + _bdot(_bdot(v, v1[...]), v2[...]))
    else:
        for b in range(nb):
            vf_out[b] = v[b * CHUNK:(b + 1) * CHUNK]

    ones = _head_ones(BRANCH, RWKV_HD)
    kk = k * kk_s[...]
    k = k * (1.0 + (a - 1.0) * ka_s[...])
    sums = _bdot(jnp.concatenate([kk * kk, r * k * rk_s[...]], axis=0), ones)
    kk = kk * lax.rsqrt(jnp.maximum(sums[:nrow], 1e-24))
    bonus = sums[nrow:] * v

    ri, ci = _iota2((nrow, nrow), 0), _iota2((nrow, nrow), 1)
    tril = ((ri // CHUNK == ci // CHUNK) & (ri >= ci)).astype(BF16)
    cum = _split_dot_left(tril, lw)
    lasts = [cum[(b + 1) * CHUNK - 1:(b + 1) * CHUNK, :] for b in range(nb)]
    to_end = stack([jnp.exp(lasts[b] - cum[b * CHUNK:(b + 1) * CHUNK]) for b in range(nb)])
    decay_all = [jnp.exp(last) for last in lasts]
    a_t = -kk * jnp.exp(cum - lw)
    r_t = r * jnp.exp(cum)
    inv = jnp.exp(-cum)
    kka = kk * a
    b_s = kka * inv
    k_s = k * inv
    b_e = kka * to_end
    k_e = k * to_end

    rows = HEAD_GROUP * CHUNK
    blk = ((_iota2((rows, GROUP_W), 0) // CHUNK) == (_iota2((rows, GROUP_W), 1) // RWKV_HD))
    blk_bf = blk.astype(F32).astype(BF16)
    t_idx = _iota2((CHUNK, GROUP_W), 0)
    s_idx = _iota2((CHUNK, GROUP_W), 1) % CHUNK
    strict = t_idx > s_idx
    incl = t_idx >= s_idx
    eye = (t_idx == s_idx).astype(F32)

    def blocks(x):
        return jnp.concatenate([x.astype(BF16)] * HEAD_GROUP, axis=0) * blk_bf

    ngroup = RWKV_HEADS // HEAD_GROUP
    probs = [(b, gi) for b in range(nb) for gi in range(ngroup)]
    nprob = len(probs)

    def part(t):
        return [t[b * CHUNK:(b + 1) * CHUNK, gi * GROUP_W:(gi + 1) * GROUP_W] for b, gi in probs]

    cat0 = lambda x, y_: jnp.concatenate([x, y_], axis=0)
    a_l, r_l, b_l, k_l, be_l, ke_l, v_l = (part(t) for t in (a_t, r_t, b_s, k_s, b_e, k_e, v))
    pm = [_bdot_nt(cat0(a_l[i], r_l[i]), cat0(blocks(b_l[i]), blocks(k_l[i]))) for i in range(nprob)]
    n_ab = [jnp.where(strict, p[:CHUNK, :GROUP_W], 0.0) for p in pm]
    a_ak = [jnp.where(strict, p[:CHUNK, GROUP_W:], 0.0) for p in pm]
    a_rb = [jnp.where(incl, p[CHUNK:, :GROUP_W], 0.0) for p in pm]
    a_rk = [jnp.where(incl, p[CHUNK:, GROUP_W:], 0.0) for p in pm]
    t_inv = [eye + n for n in n_ab]
    n_pow = n_ab
    n_blk = [blocks(n) for n in n_pow]
    for _ in range(int(math.log2(CHUNK)) - 1):
        n_pow = [_bdot(n_pow[i], n_blk[i]) for i in range(nprob)]
        n_blk = [blocks(n) for n in n_pow]
        t_inv = [t_inv[i] + _bdot(t_inv[i], n_blk[i]) for i in range(nprob)]
    av = [_bdot(cat0(a_ak[i], a_rk[i]), blocks(v_l[i])) for i in range(nprob)]
    y2 = [_bdot(t_inv[i], jnp.concatenate([blocks(a_l[i]), blocks(av[i][:CHUNK])], axis=1))
          for i in range(nprob)]
    st = [state[b, gi] for b, gi in probs]
    x2 = [_bdot_nt(cat0(y2[i][:, :GROUP_W], r_l[i]), st[i]) for i in range(nprob)]
    u_l = [x2[i][:CHUNK] + y2[i][:, GROUP_W:] for i in range(nprob)]
    o_l = [x2[i][CHUNK:] + _bdot(a_rb[i], blocks(u_l[i])) + av[i][CHUNK:] for i in range(nprob)]
    for i, (b, gi) in enumerate(probs):
        outer = _bdot_tn(cat0(u_l[i], v_l[i]), cat0(be_l[i], ke_l[i]))
        state[b, gi] = (st[i] * decay_all[b][:, gi * GROUP_W:(gi + 1) * GROUP_W]
                        + jnp.where(blk, outer, 0.0))
    o = stack([jnp.concatenate([o_l[b * ngroup + gi] for gi in range(ngroup)], axis=1)
               for b in range(nb)])

    mean = _bdot(o, ones) * (1.0 / RWKV_HD)
    cen = o - mean
    var = _bdot(cen * cen, ones) * (1.0 / RWKV_HD)
    o = cen * lax.rsqrt(var + RWKV_GN_EPS) * lnw[...] + lnb[...]
    o = ((o + bonus) * gate).astype(BF16)
    for b in range(nb):
        o_ref[b] = o[b * CHUNK:(b + 1) * CHUNK]


def _rwkv_mixer(z3, v_first, p):
    b, s = z3.shape[0], z3.shape[1]
    has_vres = v_first is not None
    nb = RWKV_SEQS_PER_STEP if b % RWKV_SEQS_PER_STEP == 0 else 1
    row = lambda w, c: pl.BlockSpec((nb, CHUNK, w), lambda i, j: (i, j, c))
    full = lambda a: pl.BlockSpec(a.shape, lambda i, j: (0,) * a.ndim)
    ins = [z3, z3, z3, z3]
    specs = [row(3 * BRANCH, COL_RWKV // (3 * BRANCH)), row(RWKV_GATE_LORA, COL_RW_G // RWKV_GATE_LORA),
             row(LANE, COL_RW_W // LANE), row(LANE, COL_RW_A // LANE)]
    if has_vres:
        ins.append(v_first)
        specs.append(row(BRANCH, 0))
    names = ["mu_rkv", "mu_g", "mu_w", "mu_a", "w0", "w2", "a0", "a2", "g2", "kk", "ka", "rk", "lnw", "lnb"]
    if has_vres:
        names += ["v0", "v1", "v2"]
    for nme in names:
        ins.append(p[nme])
        specs.append(full(p[nme]))
    o_spec = pl.BlockSpec((nb, CHUNK, BRANCH), lambda i, j: (i, j, 0))
    o_shape = jax.ShapeDtypeStruct((b, s, BRANCH), BF16)
    if has_vres:
        out_shape, out_specs = o_shape, o_spec
    else:
        out_shape = (o_shape, jax.ShapeDtypeStruct((b, s, BRANCH), F32))
        out_specs = (o_spec, pl.BlockSpec((nb, CHUNK, BRANCH), lambda i, j: (i, j, 0)))
    res = pl.pallas_call(
        functools.partial(_rwkv_kernel, has_vres=has_vres, nb=nb),
        out_shape=out_shape,
        grid=(b // nb, s // CHUNK),
        in_specs=specs,
        out_specs=out_specs,
        scratch_shapes=[pltpu.VMEM((nb, 8, 3 * BRANCH), F32), pltpu.VMEM((nb, 8, RWKV_GATE_LORA), F32),
                        pltpu.VMEM((nb, 8, LANE), F32), pltpu.VMEM((nb, 8, LANE), F32),
                        pltpu.VMEM((nb, RWKV_HEADS // HEAD_GROUP, GROUP_W, GROUP_W), F32)],
        compiler_params=_cparams(("parallel", "arbitrary")),
        name="rwkv7",
    )(*ins)
    if has_vres:
        return res, v_first
    return res[0], res[1]


def _pad_rows(w, rows, offset=0):
    return jnp.zeros((rows, w.shape[1]), F32).at[offset:offset + w.shape[0]].set(w)


N_IN = 14800
COL_GATES_SRC = 6608
MIX_PIECES = ((COL_CONV, 0, 3072), (COL_DIFF, 3088, 1536), (COL_RWKV, 4624, 1536), (COL_RW_G, 6352, 256),
              (COL_RW_W, 6160, RWKV_DECAY_LORA), (COL_RW_A, 6256, RWKV_AAA_LORA),
              (COL_RW_A + GLA_WLR_LANE, 3072, GLA_LOW_RANK))
MIX_ZERO = ((COL_RW_W + RWKV_DECAY_LORA, LANE - RWKV_DECAY_LORA),
            (COL_RW_A + GLA_WLR_LANE + GLA_LOW_RANK, LANE - GLA_WLR_LANE - GLA_LOW_RANK))


def _pack_w_in_kernel(w_ref, mix_ref, gate_ref):
    for dst, src, n in MIX_PIECES:
        mix_ref[:, dst:dst + n] = w_ref[:, src:src + n].astype(BF16)
    for dst, n in MIX_ZERO:
        mix_ref[:, dst:dst + n] = jnp.zeros((mix_ref.shape[0], n), BF16)
    gate_ref[...] = w_ref[:, COL_GATES_SRC:N_IN].astype(BF16)


def _pack_w_in(w_in):
    depth = w_in.shape[0]
    tr = 128
    n_gate = N_IN - COL_GATES_SRC
    return pl.pallas_call(
        _pack_w_in_kernel,
        out_shape=(jax.ShapeDtypeStruct((depth, D_MODEL, N_MIX), BF16),
                   jax.ShapeDtypeStruct((depth, D_MODEL, n_gate), BF16)),
        grid=(depth, D_MODEL // tr),
        in_specs=[pl.BlockSpec((None, tr, N_IN), lambda l, i: (l, i, 0))],
        out_specs=(pl.BlockSpec((None, tr, N_MIX), lambda l, i: (l, i, 0)),
                   pl.BlockSpec((None, tr, n_gate), lambda l, i: (l, i, 0))),
        compiler_params=_cparams(("parallel", "parallel")),
        name="pack_w_in",
    )(w_in)


def _pad_lanes(v, width, offset=0):
    return jnp.zeros((1, width), F32).at[0, offset:offset + v.shape[0]].set(v)


def kernel(x, norm_mix_pre, w_in, conv_w, gla_wa2, gla_ba, gla_norm, diff_lq1, diff_lk1, diff_lq2, diff_lk2,
           diff_norm, rw_mu, rw_w0, rw_w2, rw_a0, rw_a2, rw_g2, rw_kk, rw_ka, rw_rk, rw_lnw, rw_lnb, rw_v0,
           rw_v1, rw_v2, w_branch, w_out, norm_mix_post, norm_ffn_pre, w_gate, w_up, w_down, norm_ffn_post):
    bsz, seq = x.shape[0], x.shape[1]
    depth = w_in.shape[0]
    w_branch_b, w_out_b, w_down_b = (w.astype(BF16) for w in (w_branch, w_out, w_down))
    w_mix, w_gates = _pack_w_in(w_in)
    t = bsz * seq
    x2 = x.reshape(t, D_MODEL)
    row = lambda v: v.reshape(1, -1)
    v_first = None
    h = _norm(x2, row(norm_mix_pre[0]))
    for l in range(depth):
        z3 = _inproj(h, w_mix, l).reshape(bsz, seq, N_MIX)
        lambda_init = 0.8 - 0.6 * math.exp(-0.3 * l)

        o_conv = _conv_mixer(z3, conv_w[l])
        o_gla = _gla_mixer(z3, _pad_rows(gla_wa2[l], LANE, GLA_WLR_LANE), row(gla_ba[l]), row(gla_norm[l]))
        o_diff = _diff_mixer(z3, row(diff_lq1[l]), row(diff_lk1[l]), row(diff_lq2[l]),
                             row(diff_lk2[l]), row(diff_norm[l]), lambda_init)
        mu = rw_mu[l]
        rp = {
            "mu_rkv": row(mu[0:1536]),
            "mu_w": _pad_lanes(mu[1536:1632], LANE),
            "mu_a": _pad_lanes(mu[1632:1728], LANE),
            "mu_g": row(mu[1728:1984]),
            "w0": row(rw_w0[l]), "w2": _pad_rows(rw_w2[l], LANE),
            "a0": row(rw_a0[l]), "a2": _pad_rows(rw_a2[l], LANE),
            "g2": rw_g2[l], "kk": row(rw_kk[l]), "ka": row(rw_ka[l]), "rk": row(rw_rk[l]),
            "lnw": row(rw_lnw[l]), "lnb": row(rw_lnb[l]),
        }
        if l > 0:
            rp.update(v0=row(rw_v0[l - 1]), v1=rw_v1[l - 1], v2=rw_v2[l - 1])
        o_rwkv, v_first = _rwkv_mixer(z3, v_first if l > 0 else None, rp)

        outs = [o.reshape(t, BRANCH) for o in (o_conv, o_gla, o_diff, o_rwkv)]
        merged = _merge(h, w_gates, outs, w_branch_b, l)
        x2, h_ffn = _proj_norm_res(merged, w_out_b, l, x2, row(norm_mix_post[l]),
                                   row(norm_ffn_pre[l]), D_MODEL)
        act = _ffn_up(h_ffn, w_gate, w_up, l)
        next_gain = row(norm_mix_pre[l + 1]) if l + 1 < depth else None
        x2, h = _proj_norm_res(act, w_down_b, l, x2, row(norm_ffn_post[l]), next_gain, FFN_DOWN_TK)
    return x2.reshape(bsz, seq, D_MODEL)
```

```python
import functools
import math

import jax
import jax.numpy as jnp
from jax import lax
from jax.experimental import pallas as pl
from jax.experimental.pallas import tpu as pltpu

F32 = jnp.float32
BF16 = jnp.bfloat16

D_MODEL = 2048
CHUNK = 64
N_BRANCH = 4
BRANCH = 512
GLA_HEADS = 4
GLA_DK = 64
GLA_DV = 128
GLA_KEY = GLA_HEADS * GLA_DK
GLA_LOW_RANK = 16
GLA_LOGIT_NORM = 16.0
DIFF_HEADS = 4
DIFF_D = 64
RWKV_HEADS = 8
RWKV_HD = 64
RWKV_DECAY_LORA = 96
RWKV_AAA_LORA = 96
RWKV_MV_LORA = 64
RWKV_GATE_LORA = 256
D_FF = 5632
RMS_EPS = 1e-6
HEAD_EPS = 1e-5
RWKV_GN_EPS = 64e-5
NEG_INF = -1e30
LOG2E = 1.4426950408889634

COL_CONV = 0
COL_GLA = 1536
COL_DIFF = 3072
COL_RWKV = 4608
COL_RW_G = 6144
COL_RW_W = 6400
COL_RW_A = 6528
GLA_WLR_LANE = RWKV_AAA_LORA
N_MIX = 6656
LANE = 128
HEAD_GROUP = 4
GROUP_W = HEAD_GROUP * 64
RWKV_SEQS_PER_STEP = 4

VMEM_LIMIT = 56 * 1024 * 1024
PROJ_ROW_SPLIT = 4
FFN_DOWN_TK = D_FF // 2


def _tile(n, pref):
    t = min(n, pref)
    while n % t:
        t -= 8
    return t


def _cparams(sem):
    return pltpu.CompilerParams(dimension_semantics=sem, vmem_limit_bytes=VMEM_LIMIT)


def _bdot(a, b):
    return jnp.dot(a.astype(BF16), b.astype(BF16), preferred_element_type=F32)


def _bdot_nt(a, b):
    return lax.dot_general(a.astype(BF16), b.astype(BF16), (((1,), (1,)), ((), ())),
                           preferred_element_type=F32)


def _bdot_tn(a, b):
    return lax.dot_general(a.astype(BF16), b.astype(BF16), (((0,), (0,)), ((), ())),
                           preferred_element_type=F32)


def _split_dot_left(ones_bf16, x):
    hi = x.astype(BF16)
    lo = (x - hi.astype(F32)).astype(BF16)
    return (jnp.dot(ones_bf16, hi, preferred_element_type=F32)
            + jnp.dot(ones_bf16, lo, preferred_element_type=F32))


def _sigmoid(x):
    return 0.5 * jnp.tanh(0.5 * x) + 0.5


def _rms(x, gain, eps):
    return x * lax.rsqrt(jnp.mean(x * x, axis=-1, keepdims=True) + eps) * gain


def _iota2(shape, dim):
    return lax.broadcasted_iota(jnp.int32, shape, dim)


def _tril_incl(n):
    return (_iota2((n, n), 0) >= _iota2((n, n), 1)).astype(BF16)


def _head_ones(n, width):
    return ((_iota2((n, n), 0) // width) == (_iota2((n, n), 1) // width)).astype(BF16)


def _expand(x, col_group):
    rows = HEAD_GROUP * CHUNK
    xt = jnp.concatenate([x] * HEAD_GROUP, axis=0)
    keep = (_iota2((rows, x.shape[1]), 0) // CHUNK) == (_iota2((rows, x.shape[1]), 1) // col_group)
    return jnp.where(keep, xt, 0.0)


def _norm_kernel(x_ref, g_ref, h_ref):
    h_ref[...] = _rms(x_ref[...], g_ref[...], RMS_EPS).astype(BF16)


def _norm(x2, gain):
    t = x2.shape[0]
    tm = _tile(t, 512)
    return pl.pallas_call(
        _norm_kernel,
        out_shape=jax.ShapeDtypeStruct((t, D_MODEL), BF16),
        grid=(t // tm,),
        in_specs=[pl.BlockSpec((tm, D_MODEL), lambda i: (i, 0)),
                  pl.BlockSpec((1, D_MODEL), lambda i: (0, 0))],
        out_specs=pl.BlockSpec((tm, D_MODEL), lambda i: (i, 0)),
        compiler_params=_cparams(("parallel",)),
        name="norm",
    )(x2, gain)


def _inproj_kernel(h_ref, w_ref, z_ref):
    z_ref[...] = jnp.dot(h_ref[...], w_ref[...], preferred_element_type=F32).astype(BF16)


def _inproj(h, w, l):
    t, n = h.shape[0], w.shape[2]
    tm, tn = _tile(t, 1024), _tile(n, 1664)
    return pl.pallas_call(
        _inproj_kernel,
        out_shape=jax.ShapeDtypeStruct((t, n), BF16),
        grid=(t // tm, n // tn),
        in_specs=[pl.BlockSpec((tm, D_MODEL), lambda i, j: (i, 0)),
                  pl.BlockSpec((None, D_MODEL, tn), lambda i, j: (l, 0, j))],
        out_specs=pl.BlockSpec((tm, tn), lambda i, j: (i, j)),
        compiler_params=_cparams(("parallel", "arbitrary")),
        name="inproj",
    )(h, w)


def _merge_kernel(h_ref, wg0, wg1, wg2, wg3, o0, o1, o2, o3, p_ref, out_ref):
    h = h_ref[...]
    acc = None
    for n, (wg, o) in enumerate(((wg0, o0), (wg1, o1), (wg2, o2), (wg3, o3))):
        gate = _sigmoid(jnp.dot(h, wg[...], preferred_element_type=F32))
        term = gate * jnp.dot(o[...], p_ref[n], preferred_element_type=F32)
        acc = term if acc is None else acc + term
    out_ref[...] = acc.astype(BF16)


def _merge(h, w_gate, outs, w_branch, l):
    t = h.shape[0]
    tm, tn = _tile(t, 1024), 512
    nj = D_MODEL // tn
    gate_specs = [pl.BlockSpec((None, D_MODEL, tn), functools.partial(lambda i, j, n: (l, 0, n * nj + j), n=n))
                  for n in range(N_BRANCH)]
    o_specs = [pl.BlockSpec((tm, BRANCH), lambda i, j: (i, 0)) for _ in range(N_BRANCH)]
    return pl.pallas_call(
        _merge_kernel,
        out_shape=jax.ShapeDtypeStruct((t, D_MODEL), BF16),
        grid=(t // tm, nj),
        in_specs=[pl.BlockSpec((tm, D_MODEL), lambda i, j: (i, 0))] + gate_specs + o_specs
                 + [pl.BlockSpec((None, N_BRANCH, BRANCH, tn), lambda i, j: (l, 0, 0, j))],
        out_specs=pl.BlockSpec((tm, tn), lambda i, j: (i, j)),
        compiler_params=_cparams(("parallel", "arbitrary")),
        name="merge",
    )(h, w_gate, w_gate, w_gate, w_gate, *outs, w_branch)


def _proj_norm_res_kernel(a_ref, w_ref, x_ref, g_ref, gn_ref, o_ref, *h_ref, nk):
    k = pl.program_id(1)

    def accumulate(first):
        part = jnp.dot(a_ref[...], w_ref[...], preferred_element_type=F32)
        o_ref[...] = part if first else o_ref[...] + part

    def finish():
        sub = o_ref.shape[0] // PROJ_ROW_SPLIT
        for r in range(PROJ_ROW_SPLIT):
            rows = slice(r * sub, (r + 1) * sub)
            y = jnp.dot(a_ref[rows, :], w_ref[...], preferred_element_type=F32)
            if nk > 1:
                y = y + o_ref[rows, :]
            xn = x_ref[rows, :] + _rms(y, g_ref[...], RMS_EPS)
            o_ref[rows, :] = xn
            if h_ref:
                h_ref[0][rows, :] = _rms(xn, gn_ref[...], RMS_EPS).astype(BF16)

    if nk == 1:
        finish()
    else:
        pl.when(k == 0)(functools.partial(accumulate, True))
        if nk > 2:
            pl.when((k > 0) & (k < nk - 1))(functools.partial(accumulate, False))
        pl.when(k == nk - 1)(finish)


def _proj_norm_res(a, w, l, x2, gain, next_gain, tk):
    t, kdim = a.shape
    tm = _tile(t, 512)
    emit_h = next_gain is not None
    row_spec = pl.BlockSpec((tm, D_MODEL), lambda i, k: (i, 0))
    vec_spec = pl.BlockSpec((1, D_MODEL), lambda i, k: (0, 0))
    out_shape = [jax.ShapeDtypeStruct((t, D_MODEL), F32)]
    if emit_h:
        out_shape.append(jax.ShapeDtypeStruct((t, D_MODEL), BF16))
    assert kdim % tk == 0 and tm % (8 * PROJ_ROW_SPLIT) == 0
    res = pl.pallas_call(
        functools.partial(_proj_norm_res_kernel, nk=kdim // tk),
        out_shape=tuple(out_shape),
        grid=(t // tm, kdim // tk),
        in_specs=[pl.BlockSpec((tm, tk), lambda i, k: (i, k)),
                  pl.BlockSpec((None, tk, D_MODEL), lambda i, k: (l, k, 0)),
                  row_spec, vec_spec, vec_spec],
        out_specs=tuple([row_spec] * len(out_shape)),
        compiler_params=_cparams(("parallel", "arbitrary")),
        name="proj_norm_res",
    )(a, w, x2, gain, next_gain if emit_h else gain)
    return (res[0], res[1]) if emit_h else (res[0], None)


def _ffn_up_kernel(h_ref, wg_ref, wu_ref, a_ref, wg_b, wu_b):
    @pl.when(pl.program_id(1) == 0)
    def _():
        wg_b[...] = wg_ref[...].astype(BF16)
        wu_b[...] = wu_ref[...].astype(BF16)

    h = h_ref[...]
    gt = jnp.dot(h, wg_b[...], preferred_element_type=F32)
    up = jnp.dot(h, wu_b[...], preferred_element_type=F32)
    a_ref[...] = (gt * _sigmoid(gt) * up).astype(BF16)


def _ffn_up(h, wg, wu, l):
    t = h.shape[0]
    tm, tn = _tile(t, 1024), 512
    return pl.pallas_call(
        _ffn_up_kernel,
        out_shape=jax.ShapeDtypeStruct((t, D_FF), BF16),
        grid=(D_FF // tn, t // tm),
        in_specs=[pl.BlockSpec((tm, D_MODEL), lambda j, i: (i, 0)),
                  pl.BlockSpec((None, D_MODEL, tn), lambda j, i: (l, 0, j)),
                  pl.BlockSpec((None, D_MODEL, tn), lambda j, i: (l, 0, j))],
        out_specs=pl.BlockSpec((tm, tn), lambda j, i: (i, j)),
        scratch_shapes=[pltpu.VMEM((D_MODEL, tn), BF16), pltpu.VMEM((D_MODEL, tn), BF16)],
        compiler_params=_cparams(("parallel", "arbitrary")),
        name="ffn_up",
    )(h, wg, wu)


def _conv_kernel(b_ref, c_ref, u_ref, w_ref, o_ref, carry):
    @pl.when(pl.program_id(1) == 0)
    def _():
        carry[...] = jnp.zeros_like(carry)

    cu = c_ref[0].astype(F32) * u_ref[0].astype(F32)
    ts = cu.shape[0]
    row = _iota2(cu.shape, 0)
    p1 = carry[7:8, :]
    p2 = carry[6:7, :]
    s1 = jnp.where(row == 0, p1, pltpu.roll(cu, 1, axis=0))
    s2 = jnp.where(row == 0, p2, jnp.where(row == 1, p1, pltpu.roll(cu, 2, axis=0)))
    w = w_ref[...]
    y = w[2:3, :] * cu + w[1:2, :] * s1 + w[0:1, :] * s2
    o_ref[0] = (b_ref[0].astype(F32) * y).astype(BF16)
    carry[...] = cu[ts - 8:, :]


def _conv_mixer(z3, conv_w):
    b, s = z3.shape[0], z3.shape[1]
    ts = _tile(s, 512)
    c0 = COL_CONV // BRANCH
    return pl.pallas_call(
        _conv_kernel,
        out_shape=jax.ShapeDtypeStruct((b, s, BRANCH), BF16),
        grid=(b, s // ts),
        in_specs=[pl.BlockSpec((1, ts, BRANCH), lambda i, j: (i, j, c0)),
                  pl.BlockSpec((1, ts, BRANCH), lambda i, j: (i, j, c0 + 1)),
                  pl.BlockSpec((1, ts, BRANCH), lambda i, j: (i, j, c0 + 2)),
                  pl.BlockSpec((3, BRANCH), lambda i, j: (0, 0))],
        out_specs=pl.BlockSpec((1, ts, BRANCH), lambda i, j: (i, j, 0)),
        scratch_shapes=[pltpu.VMEM((8, BRANCH), F32)],
        compiler_params=_cparams(("parallel", "arbitrary")),
        name="conv",
    )(z3, z3, z3, conv_w)


def _gla_kernel(q_ref, k_ref, v_ref, g_ref, wl_ref, wa2_ref, ba_ref, gn_ref, o_ref, state):
    @pl.when(pl.program_id(1) == 0)
    def _():
        state[...] = jnp.zeros_like(state)

    nb = q_ref.shape[0]
    causal = _iota2((CHUNK, GLA_KEY), 0) >= (_iota2((CHUNK, GLA_KEY), 1) % CHUNK)
    same_head = ((_iota2((GLA_HEADS * GLA_DV, GLA_KEY), 0) // GLA_DV)
                 == (_iota2((GLA_HEADS * GLA_DV, GLA_KEY), 1) // GLA_DK))
    tril = _tril_incl(CHUNK)
    q_in, k_in, q_st, k_st, v, ve, dec = [], [], [], [], [], [], []
    for b in range(nb):
        q = q_ref[b].astype(F32) * (GLA_DK ** -0.5)
        k = k_ref[b].astype(F32)
        z = _bdot(wl_ref[b], wa2_ref[...]) + ba_ref[...]
        log_a = (jnp.minimum(z, 0.0) - jnp.log(1.0 + jnp.exp(-jnp.abs(z)))) * (1.0 / GLA_LOGIT_NORM)
        cum = _split_dot_left(tril, log_a)
        mid = cum[CHUNK // 2 - 1:CHUNK // 2, :]
        last = cum[CHUNK - 1:CHUNK, :]
        q_in.append(q * jnp.exp(cum - mid))
        k_in.append(_expand(k * jnp.exp(mid - cum), GLA_DK).astype(BF16))
        q_st.append(q * jnp.exp(cum))
        k_st.append(k * jnp.exp(last - cum))
        v.append(v_ref[b])
        ve.append(_expand(v[b].astype(F32), GLA_DV).astype(BF16))
        dec.append(jnp.exp(last))
    scores = [jnp.where(causal, _bdot_nt(q_in[b], k_in[b]), 0.0) for b in range(nb)]
    st = [state[b] for b in range(nb)]
    o_l = [_bdot(scores[b], ve[b]) + _bdot_nt(q_st[b], st[b]) for b in range(nb)]
    for b in range(nb):
        state[b] = st[b] * dec[b] + jnp.where(same_head, _bdot_tn(v[b], k_st[b]), 0.0)
    gn = gn_ref[...]
    for b in range(nb):
        o = o_l[b]
        g = g_ref[b].astype(F32)
        gate = g * _sigmoid(g)
        outs = [_rms(o[:, h * GLA_DV:(h + 1) * GLA_DV], gn, HEAD_EPS) for h in range(GLA_HEADS)]
        o_ref[b] = (jnp.concatenate(outs, axis=1) * gate).astype(BF16)


def _gla_mixer(z3, wa2p, ba, gn):
    b, s = z3.shape[0], z3.shape[1]
    cq = COL_GLA // GLA_KEY
    cv = (COL_GLA + 2 * GLA_KEY) // BRANCH
    cw = COL_RW_A // LANE
    nb = RWKV_SEQS_PER_STEP if b % RWKV_SEQS_PER_STEP == 0 else 1
    return pl.pallas_call(
        _gla_kernel,
        out_shape=jax.ShapeDtypeStruct((b, s, BRANCH), BF16),
        grid=(b // nb, s // CHUNK),
        in_specs=[pl.BlockSpec((nb, CHUNK, GLA_KEY), lambda i, j: (i, j, cq)),
                  pl.BlockSpec((nb, CHUNK, GLA_KEY), lambda i, j: (i, j, cq + 1)),
                  pl.BlockSpec((nb, CHUNK, BRANCH), lambda i, j: (i, j, cv)),
                  pl.BlockSpec((nb, CHUNK, BRANCH), lambda i, j: (i, j, cv + 1)),
                  pl.BlockSpec((nb, CHUNK, LANE), lambda i, j: (i, j, cw)),
                  pl.BlockSpec((LANE, GLA_KEY), lambda i, j: (0, 0)),
                  pl.BlockSpec((1, GLA_KEY), lambda i, j: (0, 0)),
                  pl.BlockSpec((1, GLA_DV), lambda i, j: (0, 0))],
        out_specs=pl.BlockSpec((nb, CHUNK, BRANCH), lambda i, j: (i, j, 0)),
        scratch_shapes=[pltpu.VMEM((nb, GLA_HEADS * GLA_DV, GLA_KEY), F32)],
        compiler_params=_cparams(("parallel", "arbitrary")),
        name="gla",
    )(z3, z3, z3, z3, z3, wa2p, ba, gn)


VT_ROWS = 2 * DIFF_D + 16


def _alibi_slope(h):
    return 2.0 ** (-8.0 * (h + 1) / DIFF_HEADS)


def _diff_kernel(q_ref, k_ref, v_ref, lq1, lk1, lq2, lk2, gn_ref, o_ref, ka_scr, vb_scr, qa_scr, acc_scr,
                 m_scr, s_scr, rel_scr, *, lambda_init, tq):
    qi = pl.program_id(1)
    hw = 2 * DIFF_D
    seq = k_ref.shape[1]
    lane = _iota2((tq, hw), 1)

    @pl.when(qi == 0)
    def _():
        key_row = _iota2((tq, hw), 0).astype(F32)
        for h in range(DIFF_HEADS):
            k_aug = []
            for half in range(2):
                key_bias = (_alibi_slope(h) * LOG2E) * (key_row + float(half * tq))
                hi = key_bias.astype(BF16).astype(F32)
                mid = (key_bias - hi).astype(BF16).astype(F32)
                lo = key_bias - hi - mid
                aug = jnp.where(lane == 0, hi, jnp.where(lane == 1, mid, jnp.where(lane == 2, lo, 0.0)))
                k_aug.append(aug.astype(BF16))
            for c in range(seq // tq):
                rows = slice(c * tq, (c + 1) * tq)
                ka_scr[h, rows, 0:hw] = k_ref[0, rows, h * hw:(h + 1) * hw].astype(BF16)
                ka_scr[h, rows, hw:2 * hw] = k_aug[c % 2]
        k_row = _iota2((2 * tq, 2 * tq), 0)
        col = _iota2((2 * tq, 2 * tq), 1)
        for par in range(2):
            q_row = jnp.where(col >= tq, col - tq, col) + par * tq
            d = q_row - k_row
            rel_scr[par] = jnp.where((k_row // CHUNK) <= (q_row // CHUNK), (d - jnp.abs(d)).astype(F32), NEG_INF)
        sub = _iota2((VT_ROWS - hw, tq), 0)
        tail = jnp.where(sub == 0, 1.0, 0.0).astype(BF16)
        for c in range(seq // tq):
            rows = slice(c * tq, (c + 1) * tq)
            for h in range(DIFF_HEADS):
                vb_scr[h, 0:hw, rows] = v_ref[0, rows, h * hw:(h + 1) * hw].astype(F32).T.astype(BF16)
                vb_scr[h, hw:VT_ROWS, rows] = tail

    lane2 = _iota2((2 * tq, hw), 1)
    ones_aug = (lane2 < 3).astype(F32)
    for h in range(DIFF_HEADS):
        q = q_ref[0, :, h * hw:(h + 1) * hw].astype(F32) * (DIFF_D ** -0.5 * LOG2E)
        q2 = jnp.concatenate([jnp.where(lane < DIFF_D, q, 0.0), jnp.where(lane < DIFF_D, 0.0, q)],
                             axis=0)
        qa_scr[h] = jnp.concatenate([q2, ones_aug], axis=1).astype(BF16)
    acc_scr[...] = jnp.zeros_like(acc_scr)
    m_scr[...] = jnp.full_like(m_scr, NEG_INF)

    tk = 2 * tq
    last_tile = qi // 2

    def tile_scores(j):
        start = pl.multiple_of(j * tk, tk)
        return [lax.dot_general(ka_scr[h, pl.ds(start, tk), :], qa_scr[h], (((1,), (1,)), ((), ())),
                                preferred_element_type=F32) for h in range(DIFF_HEADS)]

    def update(scores, j):
        start = pl.multiple_of(j * tk, tk)
        alphas, probs = [], []
        for h, s in enumerate(scores):
            m_old = m_scr[h]
            m_new = jnp.maximum(m_old, jnp.max(s, axis=0, keepdims=True))
            alphas.append(jnp.exp2(m_old - m_new))
            probs.append(jnp.exp2((s - m_new).astype(BF16)))
            m_scr[h] = m_new - _alibi_slope(h) * LOG2E * tk
        pvs = [jnp.dot(vb_scr[h, :, pl.ds(start, tk)], probs[h], preferred_element_type=F32)
               for h in range(DIFF_HEADS)]
        for h in range(DIFF_HEADS):
            acc_scr[h] = alphas[h] * acc_scr[h] + pvs[h]

    for h, s in enumerate(tile_scores(0)):
        s_scr[h] = s

    def body(j, carry):
        nxt = tile_scores(j + 1)
        update([s_scr[h] for h in range(DIFF_HEADS)], j)
        for h in range(DIFF_HEADS):
            s_scr[h] = nxt[h]
        return carry

    lax.fori_loop(0, last_tile, body, 0)

    rel = rel_scr[qi % 2]
    lam = (jnp.exp(jnp.sum(lq1[...] * lk1[...], axis=-1, keepdims=True))
           - jnp.exp(jnp.sum(lq2[...] * lk2[...], axis=-1, keepdims=True))
           + lambda_init)
    update([s_scr[h] + (_alibi_slope(h) * LOG2E) * rel for h in range(DIFF_HEADS)], last_tile)
    for h in range(DIFF_HEADS):
        on = acc_scr[h, 0:hw, :] / acc_scr[h, hw:hw + 1, :]
        o = (on[:, :tq] - lam * on[:, tq:]).T
        o_ref[0, :, h * hw:(h + 1) * hw] = (_rms(o, gn_ref[...], HEAD_EPS) * (1.0 - lambda_init)).astype(BF16)


def _diff_mixer(z3, lq1, lk1, lq2, lk2, gn, lambda_init):
    b, s = z3.shape[0], z3.shape[1]
    tq = _tile(s // 2, 256)
    assert s % (2 * tq) == 0 and tq % CHUNK == 0
    hw = 2 * DIFF_D
    cq = COL_DIFF // BRANCH
    nh = DIFF_HEADS
    vec = pl.BlockSpec((1, DIFF_D), lambda i, j: (0, 0))
    return pl.pallas_call(
        functools.partial(_diff_kernel, lambda_init=lambda_init, tq=tq),
        out_shape=jax.ShapeDtypeStruct((b, s, BRANCH), BF16),
        grid=(b, s // tq),
        in_specs=[pl.BlockSpec((1, tq, BRANCH), lambda i, j: (i, j, cq)),
                  pl.BlockSpec((1, s, BRANCH), lambda i, j: (i, 0, cq + 1), pipeline_mode=pl.Buffered(1)),
                  pl.BlockSpec((1, s, BRANCH), lambda i, j: (i, 0, cq + 2), pipeline_mode=pl.Buffered(1)),
                  vec, vec, vec, vec,
                  pl.BlockSpec((1, hw), lambda i, j: (0, 0))],
        out_specs=pl.BlockSpec((1, tq, BRANCH), lambda i, j: (i, j, 0)),
        scratch_shapes=[pltpu.VMEM((nh, s, 2 * hw), BF16), pltpu.VMEM((nh, VT_ROWS, s), BF16),
                        pltpu.VMEM((nh, 2 * tq, 2 * hw), BF16), pltpu.VMEM((nh, VT_ROWS, 2 * tq), F32),
                        pltpu.VMEM((nh, 1, 2 * tq), F32), pltpu.VMEM((nh, 2 * tq, 2 * tq), F32),
                        pltpu.VMEM((2, 2 * tq, 2 * tq), F32)],
        compiler_params=_cparams(("parallel", "arbitrary")),
        name="diff_attn",
    )(z3, z3, z3, lq1, lk1, lq2, lk2, gn)


def _shift(x, prev_rows):
    row = _iota2(x.shape, 0)
    return jnp.where(row == 0, prev_rows[7:8, :], pltpu.roll(x, 1, axis=0))


def _rwkv_kernel(*refs, has_vres, nb):
    if has_vres:
        (rkv_ref, gl_ref, wb_ref, ab_ref, vf_ref, mu_rkv, mu_g, mu_w, mu_a, w0, w2, a0, a2, g2, kk_s, ka_s,
         rk_s, lnw, lnb, v0, v1, v2, o_ref, p_rkv, p_g, p_w, p_a, state) = refs
    else:
        (rkv_ref, gl_ref, wb_ref, ab_ref, mu_rkv, mu_g, mu_w, mu_a, w0, w2, a0, a2, g2, kk_s, ka_s,
         rk_s, lnw, lnb, o_ref, vf_out, p_rkv, p_g, p_w, p_a, state) = refs

    @pl.when(pl.program_id(1) == 0)
    def _():
        for r in (p_rkv, p_g, p_w, p_a, state):
            r[...] = jnp.zeros_like(r)

    def stack(parts):
        return parts[0] if nb == 1 else jnp.concatenate(parts, axis=0)

    def mixed(x_ref, p_ref, mu_ref):
        parts = []
        for b in range(nb):
            x = x_ref[b].astype(F32)
            parts.append(x + (_shift(x, p_ref[b]) - x) * mu_ref[...])
            p_ref[b] = x[CHUNK - 8:, :]
        return stack(parts)

    rkv = mixed(rkv_ref, p_rkv, mu_rkv)
    g_lr = mixed(gl_ref, p_g, mu_g)
    w_lr = mixed(wb_ref, p_w, mu_w)
    a_lr = mixed(ab_ref, p_a, mu_a)
    r = rkv[:, 0:BRANCH]
    k = rkv[:, BRANCH:2 * BRANCH]
    v = rkv[:, 2 * BRANCH:3 * BRANCH]
    nrow = nb * CHUNK

    y = w0[...] + _bdot(jnp.tanh(w_lr), w2[...])
    lw = -math.exp(-0.5) * _sigmoid(y)
    a = _sigmoid(a0[...] + _bdot(a_lr, a2[...]))
    gate = _bdot(_sigmoid(g_lr), g2[...])
    if has_vres:
        vf = stack([vf_ref[b] for b in range(nb)])
        v = v + (vf - v) * _sigmoid(v0[...] + _bdot(_bdot(v, v1[...]), v2[...]))
    else:
        for b in range(nb):
            vf_out[b] = v[b * CHUNK:(b + 1) * CHUNK]

    ones = _head_ones(BRANCH, RWKV_HD)
    kk = k * kk_s[...]
    k = k * (1.0 + (a - 1.0) * ka_s[...])
    sums = _bdot(jnp.concatenate([kk * kk, r * k * rk_s[...]], axis=0), ones)
    kk = kk * lax.rsqrt(jnp.maximum(sums[:nrow], 1e-24))
    bonus = sums[nrow:] * v

    ri, ci = _iota2((nrow, nrow), 0), _iota2((nrow, nrow), 1)
    tril = ((ri // CHUNK == ci // CHUNK) & (ri >= ci)).astype(BF16)
    cum = _split_dot_left(tril, lw)
    lasts = [cum[(b + 1) * CHUNK - 1:(b + 1) * CHUNK, :] for b in range(nb)]
    to_end = stack([jnp.exp(lasts[b] - cum[b * CHUNK:(b + 1) * CHUNK]) for b in range(nb)])
    decay_all = [jnp.exp(last) for last in lasts]
    a_t = -kk * jnp.exp(cum - lw)
    r_t = r * jnp.exp(cum)
    inv = jnp.exp(-cum)
    kka = kk * a
    b_s = kka * inv
    k_s = k * inv
    b_e = kka * to_end
    k_e = k * to_end

    rows = HEAD_GROUP * CHUNK
    blk = ((_iota2((rows, GROUP_W), 0) // CHUNK) == (_iota2((rows, GROUP_W), 1) // RWKV_HD))
    blk_bf = blk.astype(F32).astype(BF16)
    t_idx = _iota2((CHUNK, GROUP_W), 0)
    s_idx = _iota2((CHUNK, GROUP_W), 1) % CHUNK
    strict = t_idx > s_idx
    incl = t_idx >= s_idx
    eye = (t_idx == s_idx).astype(F32)

    def blocks(x):
        return jnp.concatenate([x.astype(BF16)] * HEAD_GROUP, axis=0) * blk_bf

    ngroup = RWKV_HEADS // HEAD_GROUP
    probs = [(b, gi) for b in range(nb) for gi in range(ngroup)]
    nprob = len(probs)

    def part(t):
        return [t[b * CHUNK:(b + 1) * CHUNK, gi * GROUP_W:(gi + 1) * GROUP_W] for b, gi in probs]

    cat0 = lambda x, y_: jnp.concatenate([x, y_], axis=0)
    a_l, r_l, b_l, k_l, be_l, ke_l, v_l = (part(t) for t in (a_t, r_t, b_s, k_s, b_e, k_e, v))
    pm = [_bdot_nt(cat0(a_l[i], r_l[i]), cat0(blocks(b_l[i]), blocks(k_l[i]))) for i in range(nprob)]
    n_ab = [jnp.where(strict, p[:CHUNK, :GROUP_W], 0.0) for p in pm]
    a_ak = [jnp.where(strict, p[:CHUNK, GROUP_W:], 0.0) for p in pm]
    a_rb = [jnp.where(incl, p[CHUNK:, :GROUP_W], 0.0) for p in pm]
    a_rk = [jnp.where(incl, p[CHUNK:, GROUP_W:], 0.0) for p in pm]
    t_inv = [eye + n for n in n_ab]
    n_pow = n_ab
    n_blk = [blocks(n) for n in n_pow]
    for _ in range(int(math.log2(CHUNK)) - 1):
        n_pow = [_bdot(n_pow[i], n_blk[i]) for i in range(nprob)]
        n_blk = [blocks(n) for n in n_pow]
        t_inv = [t_inv[i] + _bdot(t_inv[i], n_blk[i]) for i in range(nprob)]
    av = [_bdot(cat0(a_ak[i], a_rk[i]), blocks(v_l[i])) for i in range(nprob)]
    y2 = [_bdot(t_inv[i], jnp.concatenate([blocks(a_l[i]), blocks(av[i][:CHUNK])], axis=1))
          for i in range(nprob)]
    st = [state[b, gi] for b, gi in probs]
    x2 = [_bdot_nt(cat0(y2[i][:, :GROUP_W], r_l[i]), st[i]) for i in range(nprob)]
    u_l = [x2[i][:CHUNK] + y2[i][:, GROUP_W:] for i in range(nprob)]
    o_l = [x2[i][CHUNK:] + _bdot(a_rb[i], blocks(u_l[i])) + av[i][CHUNK:] for i in range(nprob)]
    for i, (b, gi) in enumerate(probs):
        outer = _bdot_tn(cat0(u_l[i], v_l[i]), cat0(be_l[i], ke_l[i]))
        state[b, gi] = (st[i] * decay_all[b][:, gi * GROUP_W:(gi + 1) * GROUP_W]
                        + jnp.where(blk, outer, 0.0))
    o = stack([jnp.concatenate([o_l[b * ngroup + gi] for gi in range(ngroup)], axis=1)
               for b in range(nb)])

    mean = _bdot(o, ones) * (1.0 / RWKV_HD)
    cen = o - mean
    var = _bdot(cen * cen, ones) * (1.0 / RWKV_HD)
    o = cen * lax.rsqrt(var + RWKV_GN_EPS) * lnw[...] + lnb[...]
    o = ((o + bonus) * gate).astype(BF16)
    for b in range(nb):
        o_ref[b] = o[b * CHUNK:(b + 1) * CHUNK]


def _rwkv_mixer(z3, v_first, p):
    b, s = z3.shape[0], z3.shape[1]
    has_vres = v_first is not None
    nb = RWKV_SEQS_PER_STEP if b % RWKV_SEQS_PER_STEP == 0 else 1
    row = lambda w, c: pl.BlockSpec((nb, CHUNK, w), lambda i, j: (i, j, c))
    full = lambda a: pl.BlockSpec(a.shape, lambda i, j: (0,) * a.ndim)
    ins = [z3, z3, z3, z3]
    specs = [row(3 * BRANCH, COL_RWKV // (3 * BRANCH)), row(RWKV_GATE_LORA, COL_RW_G // RWKV_GATE_LORA),
             row(LANE, COL_RW_W // LANE), row(LANE, COL_RW_A // LANE)]
    if has_vres:
        ins.append(v_first)
        specs.append(row(BRANCH, 0))
    names = ["mu_rkv", "mu_g", "mu_w", "mu_a", "w0", "w2", "a0", "a2", "g2", "kk", "ka", "rk", "lnw", "lnb"]
    if has_vres:
        names += ["v0", "v1", "v2"]
    for nme in names:
        ins.append(p[nme])
        specs.append(full(p[nme]))
    o_spec = pl.BlockSpec((nb, CHUNK, BRANCH), lambda i, j: (i, j, 0))
    o_shape = jax.ShapeDtypeStruct((b, s, BRANCH), BF16)
    if has_vres:
        out_shape, out_specs = o_shape, o_spec
    else:
        out_shape = (o_shape, jax.ShapeDtypeStruct((b, s, BRANCH), F32))
        out_specs = (o_spec, pl.BlockSpec((nb, CHUNK, BRANCH), lambda i, j: (i, j, 0)))
    res = pl.pallas_call(
        functools.partial(_rwkv_kernel, has_vres=has_vres, nb=nb),
        out_shape=out_shape,
        grid=(b // nb, s // CHUNK),
        in_specs=specs,
        out_specs=out_specs,
        scratch_shapes=[pltpu.VMEM((nb, 8, 3 * BRANCH), F32), pltpu.VMEM((nb, 8, RWKV_GATE_LORA), F32),
                        pltpu.VMEM((nb, 8, LANE), F32), pltpu.VMEM((nb, 8, LANE), F32),
                        pltpu.VMEM((nb, RWKV_HEADS // HEAD_GROUP, GROUP_W, GROUP_W), F32)],
        compiler_params=_cparams(("parallel", "arbitrary")),
        name="rwkv7",
    )(*ins)
    if has_vres:
        return res, v_first
    return res[0], res[1]


def _pad_rows(w, rows, offset=0):
    return jnp.zeros((rows, w.shape[1]), F32).at[offset:offset + w.shape[0]].set(w)


N_IN = 14800
COL_GATES_SRC = 6608
MIX_BLOCKS = ((COL_CONV, ((0, 3072),)), (COL_DIFF, ((3088, 1536),)), (COL_RWKV, ((4624, 1536),)),
              (COL_RW_G, ((6352, 256),)),
              (COL_RW_W, ((6160, RWKV_DECAY_LORA), (None, LANE - RWKV_DECAY_LORA))),
              (COL_RW_A, ((6256, RWKV_AAA_LORA), (3072, GLA_LOW_RANK), (None, LANE - GLA_WLR_LANE - GLA_LOW_RANK))))
PACK_CHUNK = 512


def _pack_w_in_kernel(wt_ref, mix_ref, gate_ref):
    tc = wt_ref.shape[1]

    def rows(src, n):
        return jnp.zeros((n, tc), F32) if src is None else wt_ref[src:src + n, :]

    def put(out_ref, dst, block):
        out_ref[:, dst:dst + block.shape[0]] = block.T.astype(BF16)

    for dst, parts in MIX_BLOCKS:
        if len(parts) == 1:
            src, n = parts[0]
            for c in range(0, n, PACK_CHUNK):
                m = min(PACK_CHUNK, n - c)
                put(mix_ref, dst + c, rows(src + c, m))
        else:
            put(mix_ref, dst, jnp.concatenate([rows(src, n) for src, n in parts], axis=0))
    for c in range(0, N_IN - COL_GATES_SRC, PACK_CHUNK):
        put(gate_ref, c, rows(COL_GATES_SRC + c, PACK_CHUNK))


def _pack_w_in(w_in):
    depth = w_in.shape[0]
    tc = 256
    n_gate = N_IN - COL_GATES_SRC
    assert n_gate % PACK_CHUNK == 0
    return pl.pallas_call(
        _pack_w_in_kernel,
        out_shape=(jax.ShapeDtypeStruct((depth, D_MODEL, N_MIX), BF16),
                   jax.ShapeDtypeStruct((depth, D_MODEL, n_gate), BF16)),
        grid=(depth, D_MODEL // tc),
        in_specs=[pl.BlockSpec((None, N_IN, tc), lambda l, i: (l, 0, i))],
        out_specs=(pl.BlockSpec((None, tc, N_MIX), lambda l, i: (l, i, 0)),
                   pl.BlockSpec((None, tc, n_gate), lambda l, i: (l, i, 0))),
        compiler_params=_cparams(("parallel", "parallel")),
        name="pack_w_in",
    )(jnp.swapaxes(w_in, 1, 2))


def _pad_lanes(v, width, offset=0):
    return jnp.zeros((1, width), F32).at[0, offset:offset + v.shape[0]].set(v)


def kernel(x, norm_mix_pre, w_in, conv_w, gla_wa2, gla_ba, gla_norm, diff_lq1, diff_lk1, diff_lq2, diff_lk2,
           diff_norm, rw_mu, rw_w0, rw_w2, rw_a0, rw_a2, rw_g2, rw_kk, rw_ka, rw_rk, rw_lnw, rw_lnb, rw_v0,
           rw_v1, rw_v2, w_branch, w_out, norm_mix_post, norm_ffn_pre, w_gate, w_up, w_down, norm_ffn_post):
    bsz, seq = x.shape[0], x.shape[1]
    depth = w_in.shape[0]
    w_branch_b, w_out_b, w_down_b = (w.astype(BF16) for w in (w_branch, w_out, w_down))
    w_mix, w_gates = _pack_w_in(w_in)
    t = bsz * seq
    x2 = x.reshape(t, D_MODEL)
    row = lambda v: v.reshape(1, -1)
    v_first = None
    h = _norm(x2, row(norm_mix_pre[0]))
    for l in range(depth):
        z3 = _inproj(h, w_mix, l).reshape(bsz, seq, N_MIX)
        lambda_init = 0.8 - 0.6 * math.exp(-0.3 * l)

        o_conv = _conv_mixer(z3, conv_w[l])
        o_gla = _gla_mixer(z3, _pad_rows(gla_wa2[l], LANE, GLA_WLR_LANE), row(gla_ba[l]), row(gla_norm[l]))
        o_diff = _diff_mixer(z3, row(diff_lq1[l]), row(diff_lk1[l]), row(diff_lq2[l]),
                             row(diff_lk2[l]), row(diff_norm[l]), lambda_init)
        mu = rw_mu[l]
        rp = {
            "mu_rkv": row(mu[0:1536]),
            "mu_w": _pad_lanes(mu[1536:1632], LANE),
            "mu_a": _pad_lanes(mu[1632:1728], LANE),
            "mu_g": row(mu[1728:1984]),
            "w0": row(rw_w0[l]), "w2": _pad_rows(rw_w2[l], LANE),
            "a0": row(rw_a0[l]), "a2": _pad_rows(rw_a2[l], LANE),
            "g2": rw_g2[l], "kk": row(rw_kk[l]), "ka": row(rw_ka[l]), "rk": row(rw_rk[l]),
            "lnw": row(rw_lnw[l]), "lnb": row(rw_lnb[l]),
        }
        if l > 0:
            rp.update(v0=row(rw_v0[l - 1]), v1=rw_v1[l - 1], v2=rw_v2[l - 1])
        o_rwkv, v_first = _rwkv_mixer(z3, v_first if l > 0 else None, rp)

        outs = [o.reshape(t, BRANCH) for o in (o_conv, o_gla, o_diff, o_rwkv)]
        merged = _merge(h, w_gates, outs, w_branch_b, l)
        x2, h_ffn = _proj_norm_res(merged, w_out_b, l, x2, row(norm_mix_post[l]),
                                   row(norm_ffn_pre[l]), D_MODEL)
        act = _ffn_up(h_ffn, w_gate, w_up, l)
        next_gain = row(norm_mix_pre[l + 1]) if l + 1 < depth else None
        x2, h = _proj_norm_res(act, w_down_b, l, x2, row(norm_ffn_post[l]), next_gain, FFN_DOWN_TK)
    return x2.reshape(bsz, seq, D_MODEL)
```

```python
import functools
import math

import jax
import jax.numpy as jnp
from jax import lax
from jax.experimental import pallas as pl
from jax.experimental.pallas import tpu as pltpu

F32 = jnp.float32
BF16 = jnp.bfloat16

D_MODEL = 2048
CHUNK = 64
N_BRANCH = 4
BRANCH = 512
GLA_HEADS = 4
GLA_DK = 64
GLA_DV = 128
GLA_KEY = GLA_HEADS * GLA_DK
GLA_LOW_RANK = 16
GLA_LOGIT_NORM = 16.0
DIFF_HEADS = 4
DIFF_D = 64
RWKV_HEADS = 8
RWKV_HD = 64
RWKV_DECAY_LORA = 96
RWKV_AAA_LORA = 96
RWKV_MV_LORA = 64
RWKV_GATE_LORA = 256
D_FF = 5632
RMS_EPS = 1e-6
HEAD_EPS = 1e-5
RWKV_GN_EPS = 64e-5
NEG_INF = -1e30
LOG2E = 1.4426950408889634

COL_CONV = 0
COL_GLA = 1536
COL_DIFF = 3072
COL_RWKV = 4608
COL_RW_G = 6144
COL_RW_W = 6400
COL_RW_A = 6528
GLA_WLR_LANE = RWKV_AAA_LORA
N_MIX = 6656
LANE = 128
HEAD_GROUP = 4
GROUP_W = HEAD_GROUP * 64
RWKV_SEQS_PER_STEP = 4

VMEM_LIMIT = 56 * 1024 * 1024
PROJ_ROW_SPLIT = 4
FFN_DOWN_TK = D_FF // 2


def _tile(n, pref):
    t = min(n, pref)
    while n % t:
        t -= 8
    return t


def _cparams(sem):
    return pltpu.CompilerParams(dimension_semantics=sem, vmem_limit_bytes=VMEM_LIMIT)


def _bdot(a, b):
    return jnp.dot(a.astype(BF16), b.astype(BF16), preferred_element_type=F32)


def _bdot_nt(a, b):
    return lax.dot_general(a.astype(BF16), b.astype(BF16), (((1,), (1,)), ((), ())),
                           preferred_element_type=F32)


def _bdot_tn(a, b):
    return lax.dot_general(a.astype(BF16), b.astype(BF16), (((0,), (0,)), ((), ())),
                           preferred_element_type=F32)


def _split_dot_left(ones_bf16, x):
    hi = x.astype(BF16)
    lo = (x - hi.astype(F32)).astype(BF16)
    return (jnp.dot(ones_bf16, hi, preferred_element_type=F32)
            + jnp.dot(ones_bf16, lo, preferred_element_type=F32))


def _sigmoid(x):
    return 0.5 * jnp.tanh(0.5 * x) + 0.5


def _rms(x, gain, eps):
    return x * lax.rsqrt(jnp.mean(x * x, axis=-1, keepdims=True) + eps) * gain


def _iota2(shape, dim):
    return lax.broadcasted_iota(jnp.int32, shape, dim)


def _tril_incl(n):
    return (_iota2((n, n), 0) >= _iota2((n, n), 1)).astype(BF16)


def _head_ones(n, width):
    return ((_iota2((n, n), 0) // width) == (_iota2((n, n), 1) // width)).astype(BF16)


def _expand(x, col_group):
    rows = HEAD_GROUP * CHUNK
    xt = jnp.concatenate([x] * HEAD_GROUP, axis=0)
    keep = (_iota2((rows, x.shape[1]), 0) // CHUNK) == (_iota2((rows, x.shape[1]), 1) // col_group)
    return jnp.where(keep, xt, 0.0)


def _norm_kernel(x_ref, g_ref, h_ref):
    h_ref[...] = _rms(x_ref[...], g_ref[...], RMS_EPS).astype(BF16)


def _norm(x2, gain):
    t = x2.shape[0]
    tm = _tile(t, 512)
    return pl.pallas_call(
        _norm_kernel,
        out_shape=jax.ShapeDtypeStruct((t, D_MODEL), BF16),
        grid=(t // tm,),
        in_specs=[pl.BlockSpec((tm, D_MODEL), lambda i: (i, 0)),
                  pl.BlockSpec((1, D_MODEL), lambda i: (0, 0))],
        out_specs=pl.BlockSpec((tm, D_MODEL), lambda i: (i, 0)),
        compiler_params=_cparams(("parallel",)),
        name="norm",
    )(x2, gain)


def _inproj_kernel(h_ref, w_ref, z_ref):
    z_ref[...] = jnp.dot(h_ref[...], w_ref[...], preferred_element_type=F32).astype(BF16)


def _inproj(h, w, l):
    t, n = h.shape[0], w.shape[2]
    tm, tn = _tile(t, 1024), _tile(n, 1664)
    return pl.pallas_call(
        _inproj_kernel,
        out_shape=jax.ShapeDtypeStruct((t, n), BF16),
        grid=(t // tm, n // tn),
        in_specs=[pl.BlockSpec((tm, D_MODEL), lambda i, j: (i, 0)),
                  pl.BlockSpec((None, D_MODEL, tn), lambda i, j: (l, 0, j))],
        out_specs=pl.BlockSpec((tm, tn), lambda i, j: (i, j)),
        compiler_params=_cparams(("parallel", "arbitrary")),
        name="inproj",
    )(h, w)


def _merge_kernel(h_ref, wg0, wg1, wg2, wg3, o0, o1, o2, o3, p_ref, out_ref):
    h = h_ref[...]
    acc = None
    for n, (wg, o) in enumerate(((wg0, o0), (wg1, o1), (wg2, o2), (wg3, o3))):
        gate = _sigmoid(jnp.dot(h, wg[...], preferred_element_type=F32))
        term = gate * jnp.dot(o[...], p_ref[n], preferred_element_type=F32)
        acc = term if acc is None else acc + term
    out_ref[...] = acc.astype(BF16)


def _merge(h, w_gate, outs, w_branch, l):
    t = h.shape[0]
    tm, tn = _tile(t, 1024), 512
    nj = D_MODEL // tn
    gate_specs = [pl.BlockSpec((None, D_MODEL, tn), functools.partial(lambda i, j, n: (l, 0, n * nj + j), n=n))
                  for n in range(N_BRANCH)]
    o_specs = [pl.BlockSpec((tm, BRANCH), lambda i, j: (i, 0)) for _ in range(N_BRANCH)]
    return pl.pallas_call(
        _merge_kernel,
        out_shape=jax.ShapeDtypeStruct((t, D_MODEL), BF16),
        grid=(t // tm, nj),
        in_specs=[pl.BlockSpec((tm, D_MODEL), lambda i, j: (i, 0))] + gate_specs + o_specs
                 + [pl.BlockSpec((None, N_BRANCH, BRANCH, tn), lambda i, j: (l, 0, 0, j))],
        out_specs=pl.BlockSpec((tm, tn), lambda i, j: (i, j)),
        compiler_params=_cparams(("parallel", "arbitrary")),
        name="merge",
    )(h, w_gate, w_gate, w_gate, w_gate, *outs, w_branch)


def _proj_norm_res_kernel(a_ref, w_ref, x_ref, g_ref, gn_ref, o_ref, *h_ref, nk):
    k = pl.program_id(1)

    def accumulate(first):
        part = jnp.dot(a_ref[...], w_ref[...], preferred_element_type=F32)
        o_ref[...] = part if first else o_ref[...] + part

    def finish():
        sub = o_ref.shape[0] // PROJ_ROW_SPLIT
        for r in range(PROJ_ROW_SPLIT):
            rows = slice(r * sub, (r + 1) * sub)
            y = jnp.dot(a_ref[rows, :], w_ref[...], preferred_element_type=F32)
            if nk > 1:
                y = y + o_ref[rows, :]
            xn = x_ref[rows, :] + _rms(y, g_ref[...], RMS_EPS)
            o_ref[rows, :] = xn
            if h_ref:
                h_ref[0][rows, :] = _rms(xn, gn_ref[...], RMS_EPS).astype(BF16)

    if nk == 1:
        finish()
    else:
        pl.when(k == 0)(functools.partial(accumulate, True))
        if nk > 2:
            pl.when((k > 0) & (k < nk - 1))(functools.partial(accumulate, False))
        pl.when(k == nk - 1)(finish)


def _proj_norm_res(a, w, l, x2, gain, next_gain, tk):
    t, kdim = a.shape
    tm = _tile(t, 512)
    emit_h = next_gain is not None
    row_spec = pl.BlockSpec((tm, D_MODEL), lambda i, k: (i, 0))
    vec_spec = pl.BlockSpec((1, D_MODEL), lambda i, k: (0, 0))
    out_shape = [jax.ShapeDtypeStruct((t, D_MODEL), F32)]
    if emit_h:
        out_shape.append(jax.ShapeDtypeStruct((t, D_MODEL), BF16))
    assert kdim % tk == 0 and tm % (8 * PROJ_ROW_SPLIT) == 0
    res = pl.pallas_call(
        functools.partial(_proj_norm_res_kernel, nk=kdim // tk),
        out_shape=tuple(out_shape),
        grid=(t // tm, kdim // tk),
        in_specs=[pl.BlockSpec((tm, tk), lambda i, k: (i, k)),
                  pl.BlockSpec((None, tk, D_MODEL), lambda i, k: (l, k, 0)),
                  row_spec, vec_spec, vec_spec],
        out_specs=tuple([row_spec] * len(out_shape)),
        compiler_params=_cparams(("parallel", "arbitrary")),
        name="proj_norm_res",
    )(a, w, x2, gain, next_gain if emit_h else gain)
    return (res[0], res[1]) if emit_h else (res[0], None)


def _ffn_up_kernel(h_ref, wg_ref, wu_ref, a_ref, wg_b, wu_b):
    @pl.when(pl.program_id(1) == 0)
    def _():
        wg_b[...] = wg_ref[...].astype(BF16)
        wu_b[...] = wu_ref[...].astype(BF16)

    h = h_ref[...]
    gt = jnp.dot(h, wg_b[...], preferred_element_type=F32)
    up = jnp.dot(h, wu_b[...], preferred_element_type=F32)
    a_ref[...] = (gt * _sigmoid(gt) * up).astype(BF16)


def _ffn_up(h, wg, wu, l):
    t = h.shape[0]
    tm, tn = _tile(t, 1024), 512
    return pl.pallas_call(
        _ffn_up_kernel,
        out_shape=jax.ShapeDtypeStruct((t, D_FF), BF16),
        grid=(D_FF // tn, t // tm),
        in_specs=[pl.BlockSpec((tm, D_MODEL), lambda j, i: (i, 0)),
                  pl.BlockSpec((None, D_MODEL, tn), lambda j, i: (l, 0, j)),
                  pl.BlockSpec((None, D_MODEL, tn), lambda j, i: (l, 0, j))],
        out_specs=pl.BlockSpec((tm, tn), lambda j, i: (i, j)),
        scratch_shapes=[pltpu.VMEM((D_MODEL, tn), BF16), pltpu.VMEM((D_MODEL, tn), BF16)],
        compiler_params=_cparams(("parallel", "arbitrary")),
        name="ffn_up",
    )(h, wg, wu)


def _conv_kernel(b_ref, c_ref, u_ref, w_ref, o_ref, carry):
    @pl.when(pl.program_id(1) == 0)
    def _():
        carry[...] = jnp.zeros_like(carry)

    cu = c_ref[0].astype(F32) * u_ref[0].astype(F32)
    ts = cu.shape[0]
    row = _iota2(cu.shape, 0)
    p1 = carry[7:8, :]
    p2 = carry[6:7, :]
    s1 = jnp.where(row == 0, p1, pltpu.roll(cu, 1, axis=0))
    s2 = jnp.where(row == 0, p2, jnp.where(row == 1, p1, pltpu.roll(cu, 2, axis=0)))
    w = w_ref[...]
    y = w[2:3, :] * cu + w[1:2, :] * s1 + w[0:1, :] * s2
    o_ref[0] = (b_ref[0].astype(F32) * y).astype(BF16)
    carry[...] = cu[ts - 8:, :]


def _conv_mixer(z3, conv_w):
    b, s = z3.shape[0], z3.shape[1]
    ts = _tile(s, 512)
    c0 = COL_CONV // BRANCH
    return pl.pallas_call(
        _conv_kernel,
        out_shape=jax.ShapeDtypeStruct((b, s, BRANCH), BF16),
        grid=(b, s // ts),
        in_specs=[pl.BlockSpec((1, ts, BRANCH), lambda i, j: (i, j, c0)),
                  pl.BlockSpec((1, ts, BRANCH), lambda i, j: (i, j, c0 + 1)),
                  pl.BlockSpec((1, ts, BRANCH), lambda i, j: (i, j, c0 + 2)),
                  pl.BlockSpec((3, BRANCH), lambda i, j: (0, 0))],
        out_specs=pl.BlockSpec((1, ts, BRANCH), lambda i, j: (i, j, 0)),
        scratch_shapes=[pltpu.VMEM((8, BRANCH), F32)],
        compiler_params=_cparams(("parallel", "arbitrary")),
        name="conv",
    )(z3, z3, z3, conv_w)


def _gla_kernel(q_ref, k_ref, v_ref, g_ref, wl_ref, wa2_ref, ba_ref, gn_ref, o_ref, state):
    @pl.when(pl.program_id(1) == 0)
    def _():
        state[...] = jnp.zeros_like(state)

    nb = q_ref.shape[0]
    causal = _iota2((CHUNK, GLA_KEY), 0) >= (_iota2((CHUNK, GLA_KEY), 1) % CHUNK)
    same_head = ((_iota2((GLA_HEADS * GLA_DV, GLA_KEY), 0) // GLA_DV)
                 == (_iota2((GLA_HEADS * GLA_DV, GLA_KEY), 1) // GLA_DK))
    tril = _tril_incl(CHUNK)
    q_in, k_in, q_st, k_st, v, ve, dec = [], [], [], [], [], [], []
    for b in range(nb):
        q = q_ref[b].astype(F32) * (GLA_DK ** -0.5)
        k = k_ref[b].astype(F32)
        z = _bdot(wl_ref[b], wa2_ref[...]) + ba_ref[...]
        log_a = (jnp.minimum(z, 0.0) - jnp.log(1.0 + jnp.exp(-jnp.abs(z)))) * (1.0 / GLA_LOGIT_NORM)
        cum = _split_dot_left(tril, log_a)
        mid = cum[CHUNK // 2 - 1:CHUNK // 2, :]
        last = cum[CHUNK - 1:CHUNK, :]
        q_in.append(q * jnp.exp(cum - mid))
        k_in.append(_expand(k * jnp.exp(mid - cum), GLA_DK).astype(BF16))
        q_st.append(q * jnp.exp(cum))
        k_st.append(k * jnp.exp(last - cum))
        v.append(v_ref[b])
        ve.append(_expand(v[b].astype(F32), GLA_DV).astype(BF16))
        dec.append(jnp.exp(last))
    scores = [jnp.where(causal, _bdot_nt(q_in[b], k_in[b]), 0.0) for b in range(nb)]
    st = [state[b] for b in range(nb)]
    o_l = [_bdot(scores[b], ve[b]) + _bdot_nt(q_st[b], st[b]) for b in range(nb)]
    for b in range(nb):
        state[b] = st[b] * dec[b] + jnp.where(same_head, _bdot_tn(v[b], k_st[b]), 0.0)
    gn = gn_ref[...]
    for b in range(nb):
        o = o_l[b]
        g = g_ref[b].astype(F32)
        gate = g * _sigmoid(g)
        outs = [_rms(o[:, h * GLA_DV:(h + 1) * GLA_DV], gn, HEAD_EPS) for h in range(GLA_HEADS)]
        o_ref[b] = (jnp.concatenate(outs, axis=1) * gate).astype(BF16)


def _gla_mixer(z3, wa2p, ba, gn):
    b, s = z3.shape[0], z3.shape[1]
    cq = COL_GLA // GLA_KEY
    cv = (COL_GLA + 2 * GLA_KEY) // BRANCH
    cw = COL_RW_A // LANE
    nb = RWKV_SEQS_PER_STEP if b % RWKV_SEQS_PER_STEP == 0 else 1
    return pl.pallas_call(
        _gla_kernel,
        out_shape=jax.ShapeDtypeStruct((b, s, BRANCH), BF16),
        grid=(b // nb, s // CHUNK),
        in_specs=[pl.BlockSpec((nb, CHUNK, GLA_KEY), lambda i, j: (i, j, cq)),
                  pl.BlockSpec((nb, CHUNK, GLA_KEY), lambda i, j: (i, j, cq + 1)),
                  pl.BlockSpec((nb, CHUNK, BRANCH), lambda i, j: (i, j, cv)),
                  pl.BlockSpec((nb, CHUNK, BRANCH), lambda i, j: (i, j, cv + 1)),
                  pl.BlockSpec((nb, CHUNK, LANE), lambda i, j: (i, j, cw)),
                  pl.BlockSpec((LANE, GLA_KEY), lambda i, j: (0, 0)),
                  pl.BlockSpec((1, GLA_KEY), lambda i, j: (0, 0)),
                  pl.BlockSpec((1, GLA_DV), lambda i, j: (0, 0))],
        out_specs=pl.BlockSpec((nb, CHUNK, BRANCH), lambda i, j: (i, j, 0)),
        scratch_shapes=[pltpu.VMEM((nb, GLA_HEADS * GLA_DV, GLA_KEY), F32)],
        compiler_params=_cparams(("parallel", "arbitrary")),
        name="gla",
    )(z3, z3, z3, z3, z3, wa2p, ba, gn)


DIFF_KEY_TILE = 512
DIFF_QUERY_TILE = 512
VT_ROWS = 2 * DIFF_D + 16


def _alibi_slope(h):
    return 2.0 ** (-8.0 * (h + 1) / DIFF_HEADS)


def _diff_kernel(q_ref, k_ref, v_ref, lq1, lk1, lq2, lk2, gn_ref, o_ref, ka_scr, vb_scr, qa_scr, acc_scr,
                 m_scr, s_scr, rel_scr, *, lambda_init, tq, tk):
    qi = pl.program_id(1)
    hw = 2 * DIFF_D
    seq = k_ref.shape[1]
    lane = _iota2((tq, hw), 1)

    @pl.when(qi == 0)
    def _():
        key_row = _iota2((tq, hw), 0).astype(F32)
        for h in range(DIFF_HEADS):
            k_aug = []
            for part in range(tk // tq):
                key_bias = (_alibi_slope(h) * LOG2E) * (key_row + float(part * tq))
                hi = key_bias.astype(BF16).astype(F32)
                mid = (key_bias - hi).astype(BF16).astype(F32)
                lo = key_bias - hi - mid
                aug = jnp.where(lane == 0, hi, jnp.where(lane == 1, mid, jnp.where(lane == 2, lo, 0.0)))
                k_aug.append(aug.astype(BF16))
            for c in range(seq // tq):
                rows = slice(c * tq, (c + 1) * tq)
                ka_scr[h, rows, 0:hw] = k_ref[0, rows, h * hw:(h + 1) * hw].astype(BF16)
                ka_scr[h, rows, hw:2 * hw] = k_aug[c % (tk // tq)]
        k_row = _iota2((tk, 2 * tq), 0)
        col = _iota2((tk, 2 * tq), 1)
        for par in range(tk // tq):
            q_row = jnp.where(col >= tq, col - tq, col) + par * tq
            d = q_row - k_row
            rel_scr[par] = jnp.where((k_row // CHUNK) <= (q_row // CHUNK), (d - jnp.abs(d)).astype(F32), NEG_INF)
        sub = _iota2((VT_ROWS - hw, tq), 0)
        tail = jnp.where(sub == 0, 1.0, 0.0).astype(BF16)
        for c in range(seq // tq):
            rows = slice(c * tq, (c + 1) * tq)
            for h in range(DIFF_HEADS):
                vb_scr[h, 0:hw, rows] = v_ref[0, rows, h * hw:(h + 1) * hw].astype(F32).T.astype(BF16)
                vb_scr[h, hw:VT_ROWS, rows] = tail

    lane2 = _iota2((2 * tq, hw), 1)
    ones_aug = (lane2 < 3).astype(F32)
    for h in range(DIFF_HEADS):
        q = q_ref[0, :, h * hw:(h + 1) * hw].astype(F32) * (DIFF_D ** -0.5 * LOG2E)
        q2 = jnp.concatenate([jnp.where(lane < DIFF_D, q, 0.0), jnp.where(lane < DIFF_D, 0.0, q)],
                             axis=0)
        qa_scr[h] = jnp.concatenate([q2, ones_aug], axis=1).astype(BF16)
    acc_scr[...] = jnp.zeros_like(acc_scr)
    m_scr[...] = jnp.full_like(m_scr, NEG_INF)

    last_tile = qi // (tk // tq)

    def tile_scores(j):
        start = pl.multiple_of(j * tk, tk)
        return [lax.dot_general(ka_scr[h, pl.ds(start, tk), :], qa_scr[h], (((1,), (1,)), ((), ())),
                                preferred_element_type=F32) for h in range(DIFF_HEADS)]

    def update(scores, j):
        start = pl.multiple_of(j * tk, tk)
        alphas, probs = [], []
        for h, s in enumerate(scores):
            m_old = m_scr[h]
            m_new = jnp.maximum(m_old, jnp.max(s, axis=0, keepdims=True))
            alphas.append(jnp.exp2(m_old - m_new))
            probs.append(jnp.exp2((s - m_new).astype(BF16)))
            m_scr[h] = m_new - _alibi_slope(h) * LOG2E * tk
        pvs = [jnp.dot(vb_scr[h, :, pl.ds(start, tk)], probs[h], preferred_element_type=F32)
               for h in range(DIFF_HEADS)]
        for h in range(DIFF_HEADS):
            acc_scr[h] = alphas[h] * acc_scr[h] + pvs[h]

    for h, s in enumerate(tile_scores(0)):
        s_scr[h] = s

    def body(j, carry):
        nxt = tile_scores(j + 1)
        update([s_scr[h] for h in range(DIFF_HEADS)], j)
        for h in range(DIFF_HEADS):
            s_scr[h] = nxt[h]
        return carry

    lax.fori_loop(0, last_tile, body, 0)

    rel = rel_scr[qi % (tk // tq)]
    lam = (jnp.exp(jnp.sum(lq1[...] * lk1[...], axis=-1, keepdims=True))
           - jnp.exp(jnp.sum(lq2[...] * lk2[...], axis=-1, keepdims=True))
           + lambda_init)
    update([s_scr[h] + (_alibi_slope(h) * LOG2E) * rel for h in range(DIFF_HEADS)], last_tile)
    for h in range(DIFF_HEADS):
        on = acc_scr[h, 0:hw, :] / acc_scr[h, hw:hw + 1, :]
        o = (on[:, :tq] - lam * on[:, tq:]).T
        o_ref[0, :, h * hw:(h + 1) * hw] = (_rms(o, gn_ref[...], HEAD_EPS) * (1.0 - lambda_init)).astype(BF16)


def _diff_mixer(z3, lq1, lk1, lq2, lk2, gn, lambda_init):
    b, s = z3.shape[0], z3.shape[1]
    tk = _tile(s, DIFF_KEY_TILE)
    tq = _tile(tk, DIFF_QUERY_TILE)
    assert s % tk == 0 and tk % tq == 0 and tq % CHUNK == 0
    hw = 2 * DIFF_D
    cq = COL_DIFF // BRANCH
    nh = DIFF_HEADS
    vec = pl.BlockSpec((1, DIFF_D), lambda i, j: (0, 0))
    return pl.pallas_call(
        functools.partial(_diff_kernel, lambda_init=lambda_init, tq=tq, tk=tk),
        out_shape=jax.ShapeDtypeStruct((b, s, BRANCH), BF16),
        grid=(b, s // tq),
        in_specs=[pl.BlockSpec((1, tq, BRANCH), lambda i, j: (i, j, cq)),
                  pl.BlockSpec((1, s, BRANCH), lambda i, j: (i, 0, cq + 1), pipeline_mode=pl.Buffered(1)),
                  pl.BlockSpec((1, s, BRANCH), lambda i, j: (i, 0, cq + 2), pipeline_mode=pl.Buffered(1)),
                  vec, vec, vec, vec,
                  pl.BlockSpec((1, hw), lambda i, j: (0, 0))],
        out_specs=pl.BlockSpec((1, tq, BRANCH), lambda i, j: (i, j, 0)),
        scratch_shapes=[pltpu.VMEM((nh, s, 2 * hw), BF16), pltpu.VMEM((nh, VT_ROWS, s), BF16),
                        pltpu.VMEM((nh, 2 * tq, 2 * hw), BF16), pltpu.VMEM((nh, VT_ROWS, 2 * tq), F32),
                        pltpu.VMEM((nh, 1, 2 * tq), F32), pltpu.VMEM((nh, tk, 2 * tq), F32),
                        pltpu.VMEM((tk // tq, tk, 2 * tq), F32)],
        compiler_params=_cparams(("parallel", "arbitrary")),
        name="diff_attn",
    )(z3, z3, z3, lq1, lk1, lq2, lk2, gn)


def _shift(x, prev_rows):
    row = _iota2(x.shape, 0)
    return jnp.where(row == 0, prev_rows[7:8, :], pltpu.roll(x, 1, axis=0))


def _rwkv_kernel(*refs, has_vres, nb):
    if has_vres:
        (rkv_ref, gl_ref, wb_ref, ab_ref, vf_ref, mu_rkv, mu_g, mu_w, mu_a, w0, w2, a0, a2, g2, kk_s, ka_s,
         rk_s, lnw, lnb, v0, v1, v2, o_ref, p_rkv, p_g, p_w, p_a, state) = refs
    else:
        (rkv_ref, gl_ref, wb_ref, ab_ref, mu_rkv, mu_g, mu_w, mu_a, w0, w2, a0, a2, g2, kk_s, ka_s,
         rk_s, lnw, lnb, o_ref, vf_out, p_rkv, p_g, p_w, p_a, state) = refs

    @pl.when(pl.program_id(1) == 0)
    def _():
        for r in (p_rkv, p_g, p_w, p_a, state):
            r[...] = jnp.zeros_like(r)

    def stack(parts):
        return parts[0] if nb == 1 else jnp.concatenate(parts, axis=0)

    def mixed(x_ref, p_ref, mu_ref):
        parts = []
        for b in range(nb):
            x = x_ref[b].astype(F32)
            parts.append(x + (_shift(x, p_ref[b]) - x) * mu_ref[...])
            p_ref[b] = x[CHUNK - 8:, :]
        return stack(parts)

    rkv = mixed(rkv_ref, p_rkv, mu_rkv)
    g_lr = mixed(gl_ref, p_g, mu_g)
    w_lr = mixed(wb_ref, p_w, mu_w)
    a_lr = mixed(ab_ref, p_a, mu_a)
    r = rkv[:, 0:BRANCH]
    k = rkv[:, BRANCH:2 * BRANCH]
    v = rkv[:, 2 * BRANCH:3 * BRANCH]
    nrow = nb * CHUNK

    y = w0[...] + _bdot(jnp.tanh(w_lr), w2[...])
    lw = -math.exp(-0.5) * _sigmoid(y)
    a = _sigmoid(a0[...] + _bdot(a_lr, a2[...]))
    gate = _bdot(_sigmoid(g_lr), g2[...])
    if has_vres:
        vf = stack([vf_ref[b] for b in range(nb)])
        v = v + (vf - v) * _sigmoid(v0[...] + _bdot(_bdot(v, v1[...]), v2[...]))
    else:
        for b in range(nb):
            vf_out[b] = v[b * CHUNK:(b + 1) * CHUNK]

    ones = _head_ones(BRANCH, RWKV_HD)
    kk = k * kk_s[...]
    k = k * (1.0 + (a - 1.0) * ka_s[...])
    sums = _bdot(jnp.concatenate([kk * kk, r * k * rk_s[...]], axis=0), ones)
    kk = kk * lax.rsqrt(jnp.maximum(sums[:nrow], 1e-24))
    bonus = sums[nrow:] * v

    ri, ci = _iota2((nrow, nrow), 0), _iota2((nrow, nrow), 1)
    tril = ((ri // CHUNK == ci // CHUNK) & (ri >= ci)).astype(BF16)
    cum = _split_dot_left(tril, lw)
    lasts = [cum[(b + 1) * CHUNK - 1:(b + 1) * CHUNK, :] for b in range(nb)]
    to_end = stack([jnp.exp(lasts[b] - cum[b * CHUNK:(b + 1) * CHUNK]) for b in range(nb)])
    decay_all = [jnp.exp(last) for last in lasts]
    a_t = -kk * jnp.exp(cum - lw)
    r_t = r * jnp.exp(cum)
    inv = jnp.exp(-cum)
    kka = kk * a
    b_s = kka * inv
    k_s = k * inv
    b_e = kka * to_end
    k_e = k * to_end

    rows = HEAD_GROUP * CHUNK
    blk = ((_iota2((rows, GROUP_W), 0) // CHUNK) == (_iota2((rows, GROUP_W), 1) // RWKV_HD))
    blk_bf = blk.astype(F32).astype(BF16)
    t_idx = _iota2((CHUNK, GROUP_W), 0)
    s_idx = _iota2((CHUNK, GROUP_W), 1) % CHUNK
    strict = t_idx > s_idx
    incl = t_idx >= s_idx
    eye = (t_idx == s_idx).astype(F32)

    def blocks(x):
        return jnp.concatenate([x.astype(BF16)] * HEAD_GROUP, axis=0) * blk_bf

    ngroup = RWKV_HEADS // HEAD_GROUP
    probs = [(b, gi) for b in range(nb) for gi in range(ngroup)]
    nprob = len(probs)

    def part(t):
        return [t[b * CHUNK:(b + 1) * CHUNK, gi * GROUP_W:(gi + 1) * GROUP_W] for b, gi in probs]

    cat0 = lambda x, y_: jnp.concatenate([x, y_], axis=0)
    a_l, r_l, b_l, k_l, be_l, ke_l, v_l = (part(t) for t in (a_t, r_t, b_s, k_s, b_e, k_e, v))
    pm = [_bdot_nt(cat0(a_l[i], r_l[i]), cat0(blocks(b_l[i]), blocks(k_l[i]))) for i in range(nprob)]
    n_ab = [jnp.where(strict, p[:CHUNK, :GROUP_W], 0.0) for p in pm]
    a_ak = [jnp.where(strict, p[:CHUNK, GROUP_W:], 0.0) for p in pm]
    a_rb = [jnp.where(incl, p[CHUNK:, :GROUP_W], 0.0) for p in pm]
    a_rk = [jnp.where(incl, p[CHUNK:, GROUP_W:], 0.0) for p in pm]
    t_inv = [eye + n for n in n_ab]
    n_pow = n_ab
    n_blk = [blocks(n) for n in n_pow]
    for _ in range(int(math.log2(CHUNK)) - 1):
        n_pow = [_bdot(n_pow[i], n_blk[i]) for i in range(nprob)]
        n_blk = [blocks(n) for n in n_pow]
        t_inv = [t_inv[i] + _bdot(t_inv[i], n_blk[i]) for i in range(nprob)]
    av = [_bdot(cat0(a_ak[i], a_rk[i]), blocks(v_l[i])) for i in range(nprob)]
    y2 = [_bdot(t_inv[i], jnp.concatenate([blocks(a_l[i]), blocks(av[i][:CHUNK])], axis=1))
          for i in range(nprob)]
    st = [state[b, gi] for b, gi in probs]
    x2 = [_bdot_nt(cat0(y2[i][:, :GROUP_W], r_l[i]), st[i]) for i in range(nprob)]
    u_l = [x2[i][:CHUNK] + y2[i][:, GROUP_W:] for i in range(nprob)]
    o_l = [x2[i][CHUNK:] + _bdot(a_rb[i], blocks(u_l[i])) + av[i][CHUNK:] for i in range(nprob)]
    for i, (b, gi) in enumerate(probs):
        outer = _bdot_tn(cat0(u_l[i], v_l[i]), cat0(be_l[i], ke_l[i]))
        state[b, gi] = (st[i] * decay_all[b][:, gi * GROUP_W:(gi + 1) * GROUP_W]
                        + jnp.where(blk, outer, 0.0))
    o = stack([jnp.concatenate([o_l[b * ngroup + gi] for gi in range(ngroup)], axis=1)
               for b in range(nb)])

    mean = _bdot(o, ones) * (1.0 / RWKV_HD)
    cen = o - mean
    var = _bdot(cen * cen, ones) * (1.0 / RWKV_HD)
    o = cen * lax.rsqrt(var + RWKV_GN_EPS) * lnw[...] + lnb[...]
    o = ((o + bonus) * gate).astype(BF16)
    for b in range(nb):
        o_ref[b] = o[b * CHUNK:(b + 1) * CHUNK]


def _rwkv_mixer(z3, v_first, p):
    b, s = z3.shape[0], z3.shape[1]
    has_vres = v_first is not None
    nb = RWKV_SEQS_PER_STEP if b % RWKV_SEQS_PER_STEP == 0 else 1
    row = lambda w, c: pl.BlockSpec((nb, CHUNK, w), lambda i, j: (i, j, c))
    full = lambda a: pl.BlockSpec(a.shape, lambda i, j: (0,) * a.ndim)
    ins = [z3, z3, z3, z3]
    specs = [row(3 * BRANCH, COL_RWKV // (3 * BRANCH)), row(RWKV_GATE_LORA, COL_RW_G // RWKV_GATE_LORA),
             row(LANE, COL_RW_W // LANE), row(LANE, COL_RW_A // LANE)]
    if has_vres:
        ins.append(v_first)
        specs.append(row(BRANCH, 0))
    names = ["mu_rkv", "mu_g", "mu_w", "mu_a", "w0", "w2", "a0", "a2", "g2", "kk", "ka", "rk", "lnw", "lnb"]
    if has_vres:
        names += ["v0", "v1", "v2"]
    for nme in names:
        ins.append(p[nme])
        specs.append(full(p[nme]))
    o_spec = pl.BlockSpec((nb, CHUNK, BRANCH), lambda i, j: (i, j, 0))
    o_shape = jax.ShapeDtypeStruct((b, s, BRANCH), BF16)
    if has_vres:
        out_shape, out_specs = o_shape, o_spec
    else:
        out_shape = (o_shape, jax.ShapeDtypeStruct((b, s, BRANCH), F32))
        out_specs = (o_spec, pl.BlockSpec((nb, CHUNK, BRANCH), lambda i, j: (i, j, 0)))
    res = pl.pallas_call(
        functools.partial(_rwkv_kernel, has_vres=has_vres, nb=nb),
        out_shape=out_shape,
        grid=(b // nb, s // CHUNK),
        in_specs=specs,
        out_specs=out_specs,
        scratch_shapes=[pltpu.VMEM((nb, 8, 3 * BRANCH), F32), pltpu.VMEM((nb, 8, RWKV_GATE_LORA), F32),
                        pltpu.VMEM((nb, 8, LANE), F32), pltpu.VMEM((nb, 8, LANE), F32),
                        pltpu.VMEM((nb, RWKV_HEADS // HEAD_GROUP, GROUP_W, GROUP_W), F32)],
        compiler_params=_cparams(("parallel", "arbitrary")),
        name="rwkv7",
    )(*ins)
    if has_vres:
        return res, v_first
    return res[0], res[1]


def _pad_rows(w, rows, offset=0):
    return jnp.zeros((rows, w.shape[1]), F32).at[offset:offset + w.shape[0]].set(w)


N_IN = 14800
COL_GATES_SRC = 6608
MIX_BLOCKS = ((COL_CONV, ((0, 3072),)), (COL_DIFF, ((3088, 1536),)), (COL_RWKV, ((4624, 1536),)),
              (COL_RW_G, ((6352, 256),)),
              (COL_RW_W, ((6160, RWKV_DECAY_LORA), (None, LANE - RWKV_DECAY_LORA))),
              (COL_RW_A, ((6256, RWKV_AAA_LORA), (3072, GLA_LOW_RANK), (None, LANE - GLA_WLR_LANE - GLA_LOW_RANK))))
PACK_CHUNK = 512


def _pack_w_in_kernel(wt_ref, mix_ref, gate_ref):
    tc = wt_ref.shape[1]

    def rows(src, n):
        return jnp.zeros((n, tc), F32) if src is None else wt_ref[src:src + n, :]

    def put(out_ref, dst, block):
        out_ref[:, dst:dst + block.shape[0]] = block.T.astype(BF16)

    for dst, parts in MIX_BLOCKS:
        if len(parts) == 1:
            src, n = parts[0]
            for c in range(0, n, PACK_CHUNK):
                m = min(PACK_CHUNK, n - c)
                put(mix_ref, dst + c, rows(src + c, m))
        else:
            put(mix_ref, dst, jnp.concatenate([rows(src, n) for src, n in parts], axis=0))
    for c in range(0, N_IN - COL_GATES_SRC, PACK_CHUNK):
        put(gate_ref, c, rows(COL_GATES_SRC + c, PACK_CHUNK))


def _pack_w_in(w_in):
    depth = w_in.shape[0]
    tc = 256
    n_gate = N_IN - COL_GATES_SRC
    assert n_gate % PACK_CHUNK == 0
    return pl.pallas_call(
        _pack_w_in_kernel,
        out_shape=(jax.ShapeDtypeStruct((depth, D_MODEL, N_MIX), BF16),
                   jax.ShapeDtypeStruct((depth, D_MODEL, n_gate), BF16)),
        grid=(depth, D_MODEL // tc),
        in_specs=[pl.BlockSpec((None, N_IN, tc), lambda l, i: (l, 0, i))],
        out_specs=(pl.BlockSpec((None, tc, N_MIX), lambda l, i: (l, i, 0)),
                   pl.BlockSpec((None, tc, n_gate), lambda l, i: (l, i, 0))),
        compiler_params=_cparams(("parallel", "parallel")),
        name="pack_w_in",
    )(jnp.swapaxes(w_in, 1, 2))


def _pad_lanes(v, width, offset=0):
    return jnp.zeros((1, width), F32).at[0, offset:offset + v.shape[0]].set(v)


def kernel(x, norm_mix_pre, w_in, conv_w, gla_wa2, gla_ba, gla_norm, diff_lq1, diff_lk1, diff_lq2, diff_lk2,
           diff_norm, rw_mu, rw_w0, rw_w2, rw_a0, rw_a2, rw_g2, rw_kk, rw_ka, rw_rk, rw_lnw, rw_lnb, rw_v0,
           rw_v1, rw_v2, w_branch, w_out, norm_mix_post, norm_ffn_pre, w_gate, w_up, w_down, norm_ffn_post):
    bsz, seq = x.shape[0], x.shape[1]
    depth = w_in.shape[0]
    w_branch_b, w_out_b, w_down_b = (w.astype(BF16) for w in (w_branch, w_out, w_down))
    w_mix, w_gates = _pack_w_in(w_in)
    t = bsz * seq
    x2 = x.reshape(t, D_MODEL)
    row = lambda v: v.reshape(1, -1)
    v_first = None
    h = _norm(x2, row(norm_mix_pre[0]))
    for l in range(depth):
        z3 = _inproj(h, w_mix, l).reshape(bsz, seq, N_MIX)
        lambda_init = 0.8 - 0.6 * math.exp(-0.3 * l)

        o_conv = _conv_mixer(z3, conv_w[l])
        o_gla = _gla_mixer(z3, _pad_rows(gla_wa2[l], LANE, GLA_WLR_LANE), row(gla_ba[l]), row(gla_norm[l]))
        o_diff = _diff_mixer(z3, row(diff_lq1[l]), row(diff_lk1[l]), row(diff_lq2[l]),
                             row(diff_lk2[l]), row(diff_norm[l]), lambda_init)
        mu = rw_mu[l]
        rp = {
            "mu_rkv": row(mu[0:1536]),
            "mu_w": _pad_lanes(mu[1536:1632], LANE),
            "mu_a": _pad_lanes(mu[1632:1728], LANE),
            "mu_g": row(mu[1728:1984]),
            "w0": row(rw_w0[l]), "w2": _pad_rows(rw_w2[l], LANE),
            "a0": row(rw_a0[l]), "a2": _pad_rows(rw_a2[l], LANE),
            "g2": rw_g2[l], "kk": row(rw_kk[l]), "ka": row(rw_ka[l]), "rk": row(rw_rk[l]),
            "lnw": row(rw_lnw[l]), "lnb": row(rw_lnb[l]),
        }
        if l > 0:
            rp.update(v0=row(rw_v0[l - 1]), v1=rw_v1[l - 1], v2=rw_v2[l - 1])
        o_rwkv, v_first = _rwkv_mixer(z3, v_first if l > 0 else None, rp)

        outs = [o.reshape(t, BRANCH) for o in (o_conv, o_gla, o_diff, o_rwkv)]
        merged = _merge(h, w_gates, outs, w_branch_b, l)
        x2, h_ffn = _proj_norm_res(merged, w_out_b, l, x2, row(norm_mix_post[l]),
                                   row(norm_ffn_pre[l]), D_MODEL)
        act = _ffn_up(h_ffn, w_gate, w_up, l)
        next_gain = row(norm_mix_pre[l + 1]) if l + 1 < depth else None
        x2, h = _proj_norm_res(act, w_down_b, l, x2, row(norm_ffn_post[l]), next_gain, FFN_DOWN_TK)
    return x2.reshape(bsz, seq, D_MODEL)
```

```python
import functools
import math

import jax
import jax.numpy as jnp
from jax import lax
from jax.experimental import pallas as pl
from jax.experimental.pallas import tpu as pltpu

F32 = jnp.float32
BF16 = jnp.bfloat16

D_MODEL = 2048
CHUNK = 64
N_BRANCH = 4
BRANCH = 512
GLA_HEADS = 4
GLA_DK = 64
GLA_DV = 128
GLA_KEY = GLA_HEADS * GLA_DK
GLA_LOW_RANK = 16
GLA_LOGIT_NORM = 16.0
DIFF_HEADS = 4
DIFF_D = 64
RWKV_HEADS = 8
RWKV_HD = 64
RWKV_DECAY_LORA = 96
RWKV_AAA_LORA = 96
RWKV_MV_LORA = 64
RWKV_GATE_LORA = 256
D_FF = 5632
RMS_EPS = 1e-6
HEAD_EPS = 1e-5
RWKV_GN_EPS = 64e-5
NEG_INF = -1e30
LOG2E = 1.4426950408889634

COL_CONV = 0
COL_GLA = 1536
COL_DIFF = 3072
COL_RWKV = 4608
COL_RW_G = 6144
COL_RW_W = 6400
COL_RW_A = 6528
GLA_WLR_LANE = RWKV_AAA_LORA
LANE = 128
MXU_DIM = 256
N_MIX = COL_RW_A + LANE
HEAD_GROUP = 4
GROUP_W = HEAD_GROUP * RWKV_HD
SEQS_PER_STEP = 4

VMEM_LIMIT = 56 * 1024 * 1024
PROJ_ROW_SPLIT = 4
FFN_DOWN_TK = D_FF // 2
assert GROUP_W == MXU_DIM and FFN_DOWN_TK % MXU_DIM == 0


def _tile(n, pref):
    t = min(n, pref)
    while n % t:
        t -= 8
    return t


def _cparams(sem):
    return pltpu.CompilerParams(dimension_semantics=sem, vmem_limit_bytes=VMEM_LIMIT)


def _bdot(a, b):
    return jnp.dot(a.astype(BF16), b.astype(BF16), preferred_element_type=F32)


def _bdot_nt(a, b):
    return lax.dot_general(a.astype(BF16), b.astype(BF16), (((1,), (1,)), ((), ())),
                           preferred_element_type=F32)


def _bdot_tn(a, b):
    return lax.dot_general(a.astype(BF16), b.astype(BF16), (((0,), (0,)), ((), ())),
                           preferred_element_type=F32)


def _split_dot_left(ones_bf16, x):
    hi = x.astype(BF16)
    lo = (x - hi.astype(F32)).astype(BF16)
    return (jnp.dot(ones_bf16, hi, preferred_element_type=F32)
            + jnp.dot(ones_bf16, lo, preferred_element_type=F32))


def _sigmoid(x):
    return 0.5 * jnp.tanh(0.5 * x) + 0.5


def _rms(x, gain, eps):
    return x * lax.rsqrt(jnp.mean(x * x, axis=-1, keepdims=True) + eps) * gain


def _iota2(shape, dim):
    return lax.broadcasted_iota(jnp.int32, shape, dim)


def _tril_incl(n):
    return (_iota2((n, n), 0) >= _iota2((n, n), 1)).astype(BF16)


def _head_ones(n, width):
    return ((_iota2((n, n), 0) // width) == (_iota2((n, n), 1) // width)).astype(BF16)


def _expand(x, col_group):
    rows = HEAD_GROUP * CHUNK
    xt = jnp.concatenate([x] * HEAD_GROUP, axis=0)
    keep = (_iota2((rows, x.shape[1]), 0) // CHUNK) == (_iota2((rows, x.shape[1]), 1) // col_group)
    return jnp.where(keep, xt, 0.0)


def _inproj_kernel(*refs, fuse_norm):
    if fuse_norm:
        x_ref, g_ref, w_ref, z_ref, h_ref = refs

        @pl.when(pl.program_id(1) == 0)
        def _():
            h_ref[...] = _rms(x_ref[...], g_ref[...], RMS_EPS).astype(BF16)
    else:
        h_ref, w_ref, z_ref = refs
    z_ref[...] = jnp.dot(h_ref[...], w_ref[...], preferred_element_type=F32).astype(BF16)


def _inproj(h, w, l, gain=None):
    t, n = h.shape[0], w.shape[2]
    fuse_norm = gain is not None
    tm, tn = _tile(t, 1024), _tile(n, 1664)
    row_spec = pl.BlockSpec((tm, D_MODEL), lambda i, j: (i, 0))
    z_spec = pl.BlockSpec((tm, tn), lambda i, j: (i, j))
    w_spec = pl.BlockSpec((None, D_MODEL, tn), lambda i, j: (l, 0, j))
    z_shape = jax.ShapeDtypeStruct((t, n), BF16)
    if fuse_norm:
        ins = (h, gain, w)
        in_specs = [row_spec, pl.BlockSpec((1, D_MODEL), lambda i, j: (0, 0)), w_spec]
        out_shape, out_specs = (z_shape, jax.ShapeDtypeStruct((t, D_MODEL), BF16)), (z_spec, row_spec)
    else:
        ins, in_specs, out_shape, out_specs = (h, w), [row_spec, w_spec], z_shape, z_spec
    return pl.pallas_call(
        functools.partial(_inproj_kernel, fuse_norm=fuse_norm),
        out_shape=out_shape,
        grid=(t // tm, n // tn),
        in_specs=in_specs,
        out_specs=out_specs,
        compiler_params=_cparams(("parallel", "arbitrary")),
        name="inproj",
    )(*ins)


def _merge_kernel(h_ref, wg0, wg1, wg2, wg3, o0, o1, o2, o3, p_ref, out_ref):
    h = h_ref[...]
    acc = None
    for n, (wg, o) in enumerate(((wg0, o0), (wg1, o1), (wg2, o2), (wg3, o3))):
        gate = _sigmoid(jnp.dot(h, wg[...], preferred_element_type=F32))
        term = gate * jnp.dot(o[...], p_ref[n], preferred_element_type=F32)
        acc = term if acc is None else acc + term
    out_ref[...] = acc.astype(BF16)


def _merge(h, w_gate, outs, w_branch, l):
    t = h.shape[0]
    tm, tn = _tile(t, 1024), 512
    nj = D_MODEL // tn
    gate_specs = [pl.BlockSpec((None, D_MODEL, tn), functools.partial(lambda i, j, n: (l, 0, n * nj + j), n=n))
                  for n in range(N_BRANCH)]
    o_specs = [pl.BlockSpec((tm, BRANCH), lambda i, j: (i, 0)) for _ in range(N_BRANCH)]
    return pl.pallas_call(
        _merge_kernel,
        out_shape=jax.ShapeDtypeStruct((t, D_MODEL), BF16),
        grid=(t // tm, nj),
        in_specs=[pl.BlockSpec((tm, D_MODEL), lambda i, j: (i, 0))] + gate_specs + o_specs
                 + [pl.BlockSpec((None, N_BRANCH, BRANCH, tn), lambda i, j: (l, 0, 0, j))],
        out_specs=pl.BlockSpec((tm, tn), lambda i, j: (i, j)),
        compiler_params=_cparams(("parallel", "arbitrary")),
        name="merge",
    )(h, w_gate, w_gate, w_gate, w_gate, *outs, w_branch)


def _proj_norm_res_kernel(a_ref, w_ref, x_ref, g_ref, gn_ref, o_ref, *h_ref, nk):
    k = pl.program_id(1)

    def accumulate(first):
        part = jnp.dot(a_ref[...], w_ref[...], preferred_element_type=F32)
        o_ref[...] = part if first else o_ref[...] + part

    def finish():
        sub = o_ref.shape[0] // PROJ_ROW_SPLIT
        for r in range(PROJ_ROW_SPLIT):
            rows = slice(r * sub, (r + 1) * sub)
            y = jnp.dot(a_ref[rows, :], w_ref[...], preferred_element_type=F32)
            if nk > 1:
                y = y + o_ref[rows, :]
            xn = x_ref[rows, :] + _rms(y, g_ref[...], RMS_EPS)
            o_ref[rows, :] = xn
            if h_ref:
                h_ref[0][rows, :] = _rms(xn, gn_ref[...], RMS_EPS).astype(BF16)

    if nk == 1:
        finish()
    else:
        pl.when(k == 0)(functools.partial(accumulate, True))
        if nk > 2:
            pl.when((k > 0) & (k < nk - 1))(functools.partial(accumulate, False))
        pl.when(k == nk - 1)(finish)


def _proj_norm_res(a, w, l, x2, gain, next_gain, tk):
    t, kdim = a.shape
    tm = _tile(t, 512)
    emit_h = next_gain is not None
    row_spec = pl.BlockSpec((tm, D_MODEL), lambda i, k: (i, 0))
    vec_spec = pl.BlockSpec((1, D_MODEL), lambda i, k: (0, 0))
    out_shape = [jax.ShapeDtypeStruct((t, D_MODEL), F32)]
    if emit_h:
        out_shape.append(jax.ShapeDtypeStruct((t, D_MODEL), BF16))
    assert kdim % tk == 0 and tm % (8 * PROJ_ROW_SPLIT) == 0
    res = pl.pallas_call(
        functools.partial(_proj_norm_res_kernel, nk=kdim // tk),
        out_shape=tuple(out_shape),
        grid=(t // tm, kdim // tk),
        in_specs=[pl.BlockSpec((tm, tk), lambda i, k: (i, k)),
                  pl.BlockSpec((None, tk, D_MODEL), lambda i, k: (l, k, 0)),
                  row_spec, vec_spec, vec_spec],
        out_specs=tuple([row_spec] * len(out_shape)),
        compiler_params=_cparams(("parallel", "arbitrary")),
        name="proj_norm_res",
    )(a, w, x2, gain, next_gain if emit_h else gain)
    return (res[0], res[1]) if emit_h else (res[0], None)


def _ffn_up_kernel(h_ref, wg_ref, wu_ref, a_ref, wg_b, wu_b):
    @pl.when(pl.program_id(1) == 0)
    def _():
        wg_b[...] = wg_ref[...].astype(BF16)
        wu_b[...] = wu_ref[...].astype(BF16)

    h = h_ref[...]
    gt = jnp.dot(h, wg_b[...], preferred_element_type=F32)
    up = jnp.dot(h, wu_b[...], preferred_element_type=F32)
    a_ref[...] = (gt * _sigmoid(gt) * up).astype(BF16)


def _ffn_up(h, wg, wu, l):
    t = h.shape[0]
    tm, tn = _tile(t, 1024), 512
    return pl.pallas_call(
        _ffn_up_kernel,
        out_shape=jax.ShapeDtypeStruct((t, D_FF), BF16),
        grid=(D_FF // tn, t // tm),
        in_specs=[pl.BlockSpec((tm, D_MODEL), lambda j, i: (i, 0)),
                  pl.BlockSpec((None, D_MODEL, tn), lambda j, i: (l, 0, j)),
                  pl.BlockSpec((None, D_MODEL, tn), lambda j, i: (l, 0, j))],
        out_specs=pl.BlockSpec((tm, tn), lambda j, i: (i, j)),
        scratch_shapes=[pltpu.VMEM((D_MODEL, tn), BF16), pltpu.VMEM((D_MODEL, tn), BF16)],
        compiler_params=_cparams(("parallel", "arbitrary")),
        name="ffn_up",
    )(h, wg, wu)


def _conv_kernel(b_ref, c_ref, u_ref, w_ref, o_ref, carry):
    @pl.when(pl.program_id(1) == 0)
    def _():
        carry[...] = jnp.zeros_like(carry)

    cu = c_ref[0].astype(F32) * u_ref[0].astype(F32)
    ts = cu.shape[0]
    row = _iota2(cu.shape, 0)
    p1 = carry[7:8, :]
    p2 = carry[6:7, :]
    s1 = jnp.where(row == 0, p1, pltpu.roll(cu, 1, axis=0))
    s2 = jnp.where(row == 0, p2, jnp.where(row == 1, p1, pltpu.roll(cu, 2, axis=0)))
    w = w_ref[...]
    y = w[2:3, :] * cu + w[1:2, :] * s1 + w[0:1, :] * s2
    o_ref[0] = (b_ref[0].astype(F32) * y).astype(BF16)
    carry[...] = cu[ts - 8:, :]


def _conv_mixer(z3, conv_w):
    b, s = z3.shape[0], z3.shape[1]
    ts = _tile(s, 512)
    c0 = COL_CONV // BRANCH
    return pl.pallas_call(
        _conv_kernel,
        out_shape=jax.ShapeDtypeStruct((b, s, BRANCH), BF16),
        grid=(b, s // ts),
        in_specs=[pl.BlockSpec((1, ts, BRANCH), lambda i, j: (i, j, c0)),
                  pl.BlockSpec((1, ts, BRANCH), lambda i, j: (i, j, c0 + 1)),
                  pl.BlockSpec((1, ts, BRANCH), lambda i, j: (i, j, c0 + 2)),
                  pl.BlockSpec((3, BRANCH), lambda i, j: (0, 0))],
        out_specs=pl.BlockSpec((1, ts, BRANCH), lambda i, j: (i, j, 0)),
        scratch_shapes=[pltpu.VMEM((8, BRANCH), F32)],
        compiler_params=_cparams(("parallel", "arbitrary")),
        name="conv",
    )(z3, z3, z3, conv_w)


def _gla_kernel(q_ref, k_ref, v_ref, g_ref, wl_ref, wa2_ref, ba_ref, gn_ref, o_ref, state):
    @pl.when(pl.program_id(1) == 0)
    def _():
        state[...] = jnp.zeros_like(state)

    nb = q_ref.shape[0]
    causal = _iota2((CHUNK, GLA_KEY), 0) >= (_iota2((CHUNK, GLA_KEY), 1) % CHUNK)
    same_head = ((_iota2((GLA_HEADS * GLA_DV, GLA_KEY), 0) // GLA_DV)
                 == (_iota2((GLA_HEADS * GLA_DV, GLA_KEY), 1) // GLA_DK))
    tril = _tril_incl(CHUNK)
    q_in, k_in, q_st, k_st, v, ve, dec = [], [], [], [], [], [], []
    for b in range(nb):
        q = q_ref[b].astype(F32) * (GLA_DK ** -0.5)
        k = k_ref[b].astype(F32)
        z = _bdot(wl_ref[b], wa2_ref[...]) + ba_ref[...]
        log_a = (jnp.minimum(z, 0.0) - jnp.log(1.0 + jnp.exp(-jnp.abs(z)))) * (1.0 / GLA_LOGIT_NORM)
        cum = _split_dot_left(tril, log_a)
        mid = cum[CHUNK // 2 - 1:CHUNK // 2, :]
        last = cum[CHUNK - 1:CHUNK, :]
        q_in.append(q * jnp.exp(cum - mid))
        k_in.append(_expand(k * jnp.exp(mid - cum), GLA_DK).astype(BF16))
        q_st.append(q * jnp.exp(cum))
        k_st.append(k * jnp.exp(last - cum))
        v.append(v_ref[b])
        ve.append(_expand(v[b].astype(F32), GLA_DV).astype(BF16))
        dec.append(jnp.exp(last))
    scores = [jnp.where(causal, _bdot_nt(q_in[b], k_in[b]), 0.0) for b in range(nb)]
    st = [state[b] for b in range(nb)]
    o_l = [_bdot(scores[b], ve[b]) + _bdot_nt(q_st[b], st[b]) for b in range(nb)]
    for b in range(nb):
        state[b] = st[b] * dec[b] + jnp.where(same_head, _bdot_tn(v[b], k_st[b]), 0.0)
    gn = gn_ref[...]
    for b in range(nb):
        o = o_l[b]
        g = g_ref[b].astype(F32)
        gate = g * _sigmoid(g)
        outs = [_rms(o[:, h * GLA_DV:(h + 1) * GLA_DV], gn, HEAD_EPS) for h in range(GLA_HEADS)]
        o_ref[b] = (jnp.concatenate(outs, axis=1) * gate).astype(BF16)


def _gla_mixer(z3, wa2p, ba, gn):
    b, s = z3.shape[0], z3.shape[1]
    cq = COL_GLA // GLA_KEY
    cv = (COL_GLA + 2 * GLA_KEY) // BRANCH
    cw = COL_RW_A // LANE
    nb = SEQS_PER_STEP if b % SEQS_PER_STEP == 0 else 1
    return pl.pallas_call(
        _gla_kernel,
        out_shape=jax.ShapeDtypeStruct((b, s, BRANCH), BF16),
        grid=(b // nb, s // CHUNK),
        in_specs=[pl.BlockSpec((nb, CHUNK, GLA_KEY), lambda i, j: (i, j, cq)),
                  pl.BlockSpec((nb, CHUNK, GLA_KEY), lambda i, j: (i, j, cq + 1)),
                  pl.BlockSpec((nb, CHUNK, BRANCH), lambda i, j: (i, j, cv)),
                  pl.BlockSpec((nb, CHUNK, BRANCH), lambda i, j: (i, j, cv + 1)),
                  pl.BlockSpec((nb, CHUNK, LANE), lambda i, j: (i, j, cw)),
                  pl.BlockSpec((LANE, GLA_KEY), lambda i, j: (0, 0)),
                  pl.BlockSpec((1, GLA_KEY), lambda i, j: (0, 0)),
                  pl.BlockSpec((1, GLA_DV), lambda i, j: (0, 0))],
        out_specs=pl.BlockSpec((nb, CHUNK, BRANCH), lambda i, j: (i, j, 0)),
        scratch_shapes=[pltpu.VMEM((nb, GLA_HEADS * GLA_DV, GLA_KEY), F32)],
        compiler_params=_cparams(("parallel", "arbitrary")),
        name="gla",
    )(z3, z3, z3, z3, z3, wa2p, ba, gn)


DIFF_KEY_TILE = 512
DIFF_QUERY_TILE = 512
VT_ROWS = 2 * DIFF_D + 16


def _alibi_slope(h):
    return 2.0 ** (-8.0 * (h + 1) / DIFF_HEADS)


def _diff_kernel(q_ref, k_ref, v_ref, lq1, lk1, lq2, lk2, gn_ref, o_ref, ka_scr, vb_scr, qa_scr, acc_scr,
                 m_scr, s_scr, rel_scr, *, lambda_init, tq, tk):
    qi = pl.program_id(1)
    hw = 2 * DIFF_D
    seq = k_ref.shape[1]
    lane = _iota2((tq, hw), 1)

    @pl.when(qi == 0)
    def _():
        key_row = _iota2((tq, hw), 0).astype(F32)
        for h in range(DIFF_HEADS):
            k_aug = []
            for part in range(tk // tq):
                key_bias = (_alibi_slope(h) * LOG2E) * (key_row + float(part * tq))
                hi = key_bias.astype(BF16).astype(F32)
                mid = (key_bias - hi).astype(BF16).astype(F32)
                lo = key_bias - hi - mid
                aug = jnp.where(lane == 0, hi, jnp.where(lane == 1, mid, jnp.where(lane == 2, lo, 0.0)))
                k_aug.append(aug.astype(BF16))
            for c in range(seq // tq):
                rows = slice(c * tq, (c + 1) * tq)
                ka_scr[h, rows, 0:hw] = k_ref[0, rows, h * hw:(h + 1) * hw].astype(BF16)
                ka_scr[h, rows, hw:2 * hw] = k_aug[c % (tk // tq)]
        k_row = _iota2((tk, 2 * tq), 0)
        col = _iota2((tk, 2 * tq), 1)
        for par in range(tk // tq):
            q_row = jnp.where(col >= tq, col - tq, col) + par * tq
            d = q_row - k_row
            rel_scr[par] = jnp.where((k_row // CHUNK) <= (q_row // CHUNK), (d - jnp.abs(d)).astype(F32), NEG_INF)
        sub = _iota2((VT_ROWS - hw, tq), 0)
        tail = jnp.where(sub == 0, 1.0, 0.0).astype(BF16)
        for c in range(seq // tq):
            rows = slice(c * tq, (c + 1) * tq)
            for h in range(DIFF_HEADS):
                vb_scr[h, 0:hw, rows] = v_ref[0, rows, h * hw:(h + 1) * hw].astype(F32).T.astype(BF16)
                vb_scr[h, hw:VT_ROWS, rows] = tail

    lane2 = _iota2((2 * tq, hw), 1)
    ones_aug = (lane2 < 3).astype(F32)
    for h in range(DIFF_HEADS):
        q = q_ref[0, :, h * hw:(h + 1) * hw].astype(F32) * (DIFF_D ** -0.5 * LOG2E)
        q2 = jnp.concatenate([jnp.where(lane < DIFF_D, q, 0.0), jnp.where(lane < DIFF_D, 0.0, q)],
                             axis=0)
        qa_scr[h] = jnp.concatenate([q2, ones_aug], axis=1).astype(BF16)
    acc_scr[...] = jnp.zeros_like(acc_scr)
    m_scr[...] = jnp.full_like(m_scr, NEG_INF)

    last_tile = qi // (tk // tq)

    def tile_scores(j):
        start = pl.multiple_of(j * tk, tk)
        return [lax.dot_general(ka_scr[h, pl.ds(start, tk), :], qa_scr[h], (((1,), (1,)), ((), ())),
                                preferred_element_type=F32) for h in range(DIFF_HEADS)]

    def update(scores, j):
        start = pl.multiple_of(j * tk, tk)
        alphas, probs = [], []
        for h, s in enumerate(scores):
            m_old = m_scr[h]
            m_new = jnp.maximum(m_old, jnp.max(s, axis=0, keepdims=True))
            alphas.append(jnp.exp2(m_old - m_new))
            probs.append(jnp.exp2((s - m_new).astype(BF16)))
            m_scr[h] = m_new - _alibi_slope(h) * LOG2E * tk
        pvs = [jnp.dot(vb_scr[h, :, pl.ds(start, tk)], probs[h], preferred_element_type=F32)
               for h in range(DIFF_HEADS)]
        for h in range(DIFF_HEADS):
            acc_scr[h] = alphas[h] * acc_scr[h] + pvs[h]

    for h, s in enumerate(tile_scores(0)):
        s_scr[h] = s

    def body(j, carry):
        nxt = tile_scores(j + 1)
        update([s_scr[h] for h in range(DIFF_HEADS)], j)
        for h in range(DIFF_HEADS):
            s_scr[h] = nxt[h]
        return carry

    lax.fori_loop(0, last_tile, body, 0)

    rel = rel_scr[qi % (tk // tq)]
    lam = (jnp.exp(jnp.sum(lq1[...] * lk1[...], axis=-1, keepdims=True))
           - jnp.exp(jnp.sum(lq2[...] * lk2[...], axis=-1, keepdims=True))
           + lambda_init)
    update([s_scr[h] + (_alibi_slope(h) * LOG2E) * rel for h in range(DIFF_HEADS)], last_tile)
    for h in range(DIFF_HEADS):
        on = acc_scr[h, 0:hw, :] / acc_scr[h, hw:hw + 1, :]
        o = (on[:, :tq] - lam * on[:, tq:]).T
        o_ref[0, :, h * hw:(h + 1) * hw] = (_rms(o, gn_ref[...], HEAD_EPS) * (1.0 - lambda_init)).astype(BF16)


def _diff_mixer(z3, lq1, lk1, lq2, lk2, gn, lambda_init):
    b, s = z3.shape[0], z3.shape[1]
    tk = _tile(s, DIFF_KEY_TILE)
    tq = _tile(tk, DIFF_QUERY_TILE)
    assert s % tk == 0 and tk % tq == 0 and tq % CHUNK == 0
    hw = 2 * DIFF_D
    cq = COL_DIFF // BRANCH
    nh = DIFF_HEADS
    vec = pl.BlockSpec((1, DIFF_D), lambda i, j: (0, 0))
    return pl.pallas_call(
        functools.partial(_diff_kernel, lambda_init=lambda_init, tq=tq, tk=tk),
        out_shape=jax.ShapeDtypeStruct((b, s, BRANCH), BF16),
        grid=(b, s // tq),
        in_specs=[pl.BlockSpec((1, tq, BRANCH), lambda i, j: (i, j, cq)),
                  pl.BlockSpec((1, s, BRANCH), lambda i, j: (i, 0, cq + 1), pipeline_mode=pl.Buffered(1)),
                  pl.BlockSpec((1, s, BRANCH), lambda i, j: (i, 0, cq + 2), pipeline_mode=pl.Buffered(1)),
                  vec, vec, vec, vec,
                  pl.BlockSpec((1, hw), lambda i, j: (0, 0))],
        out_specs=pl.BlockSpec((1, tq, BRANCH), lambda i, j: (i, j, 0)),
        scratch_shapes=[pltpu.VMEM((nh, s, 2 * hw), BF16), pltpu.VMEM((nh, VT_ROWS, s), BF16),
                        pltpu.VMEM((nh, 2 * tq, 2 * hw), BF16), pltpu.VMEM((nh, VT_ROWS, 2 * tq), F32),
                        pltpu.VMEM((nh, 1, 2 * tq), F32), pltpu.VMEM((nh, tk, 2 * tq), F32),
                        pltpu.VMEM((tk // tq, tk, 2 * tq), F32)],
        compiler_params=_cparams(("parallel", "arbitrary")),
        name="diff_attn",
    )(z3, z3, z3, lq1, lk1, lq2, lk2, gn)


def _shift(x, prev_rows):
    row = _iota2(x.shape, 0)
    return jnp.where(row == 0, prev_rows[7:8, :], pltpu.roll(x, 1, axis=0))


def _rwkv_kernel(*refs, has_vres, nb):
    if has_vres:
        (rkv_ref, gl_ref, wb_ref, ab_ref, vf_ref, mu_rkv, mu_g, mu_w, mu_a, w0, w2, a0, a2, g2, kk_s, ka_s,
         rk_s, lnw, lnb, v0, v1, v2, o_ref, p_rkv, p_g, p_w, p_a, state) = refs
    else:
        (rkv_ref, gl_ref, wb_ref, ab_ref, mu_rkv, mu_g, mu_w, mu_a, w0, w2, a0, a2, g2, kk_s, ka_s,
         rk_s, lnw, lnb, o_ref, vf_out, p_rkv, p_g, p_w, p_a, state) = refs

    @pl.when(pl.program_id(1) == 0)
    def _():
        for r in (p_rkv, p_g, p_w, p_a, state):
            r[...] = jnp.zeros_like(r)

    def stack(parts):
        return parts[0] if nb == 1 else jnp.concatenate(parts, axis=0)

    def mixed(x_ref, p_ref, mu_ref):
        parts = []
        for b in range(nb):
            x = x_ref[b].astype(F32)
            parts.append(x + (_shift(x, p_ref[b]) - x) * mu_ref[...])
            p_ref[b] = x[CHUNK - 8:, :]
        return stack(parts)

    rkv = mixed(rkv_ref, p_rkv, mu_rkv)
    g_lr = mixed(gl_ref, p_g, mu_g)
    w_lr = mixed(wb_ref, p_w, mu_w)
    a_lr = mixed(ab_ref, p_a, mu_a)
    r = rkv[:, 0:BRANCH]
    k = rkv[:, BRANCH:2 * BRANCH]
    v = rkv[:, 2 * BRANCH:3 * BRANCH]
    nrow = nb * CHUNK

    y = w0[...] + _bdot(jnp.tanh(w_lr), w2[...])
    lw = -math.exp(-0.5) * _sigmoid(y)
    a = _sigmoid(a0[...] + _bdot(a_lr, a2[...]))
    gate = _bdot(_sigmoid(g_lr), g2[...])
    if has_vres:
        vf = stack([vf_ref[b] for b in range(nb)])
        v = v + (vf - v) * _sigmoid(v0[...] + _bdot(_bdot(v, v1[...]), v2[...]))
    else:
        for b in range(nb):
            vf_out[b] = v[b * CHUNK:(b + 1) * CHUNK]

    ones = _head_ones(BRANCH, RWKV_HD)
    kk = k * kk_s[...]
    k = k * (1.0 + (a - 1.0) * ka_s[...])
    sums = _bdot(jnp.concatenate([kk * kk, r * k * rk_s[...]], axis=0), ones)
    kk = kk * lax.rsqrt(jnp.maximum(sums[:nrow], 1e-24))
    bonus = sums[nrow:] * v

    ri, ci = _iota2((nrow, nrow), 0), _iota2((nrow, nrow), 1)
    tril = ((ri // CHUNK == ci // CHUNK) & (ri >= ci)).astype(BF16)
    cum = _split_dot_left(tril, lw)
    lasts = [cum[(b + 1) * CHUNK - 1:(b + 1) * CHUNK, :] for b in range(nb)]
    decay_all = [jnp.exp(last) for last in lasts]
    a_t = -kk * jnp.exp(cum - lw)
    r_t = r * jnp.exp(cum)
    inv = jnp.exp(-cum)
    to_end = stack([decay_all[b] * inv[b * CHUNK:(b + 1) * CHUNK] for b in range(nb)])
    kka = kk * a
    b_s = kka * inv
    k_s = k * inv
    b_e = kka * to_end
    k_e = k * to_end

    rows = HEAD_GROUP * CHUNK
    blk = ((_iota2((rows, GROUP_W), 0) // CHUNK) == (_iota2((rows, GROUP_W), 1) // RWKV_HD))
    blk_bf = blk.astype(F32).astype(BF16)
    t_idx = _iota2((CHUNK, GROUP_W), 0)
    s_idx = _iota2((CHUNK, GROUP_W), 1) % CHUNK
    strict = t_idx > s_idx
    incl = t_idx >= s_idx
    eye = (t_idx == s_idx).astype(F32)

    def blocks(x):
        return jnp.concatenate([x.astype(BF16)] * HEAD_GROUP, axis=0) * blk_bf

    ngroup = RWKV_HEADS // HEAD_GROUP
    probs = [(b, gi) for b in range(nb) for gi in range(ngroup)]
    nprob = len(probs)

    def part(t):
        return [t[b * CHUNK:(b + 1) * CHUNK, gi * GROUP_W:(gi + 1) * GROUP_W] for b, gi in probs]

    cat0 = lambda x, y_: jnp.concatenate([x, y_], axis=0)
    a_l, r_l, b_l, k_l, be_l, ke_l, v_l = (part(t) for t in (a_t, r_t, b_s, k_s, b_e, k_e, v))
    pm = [_bdot_nt(cat0(a_l[i], r_l[i]), cat0(blocks(b_l[i]), blocks(k_l[i]))) for i in range(nprob)]
    n_ab = [jnp.where(strict, p[:CHUNK, :GROUP_W], 0.0) for p in pm]
    a_ak = [jnp.where(strict, p[:CHUNK, GROUP_W:], 0.0) for p in pm]
    a_rb = [jnp.where(incl, p[CHUNK:, :GROUP_W], 0.0) for p in pm]
    a_rk = [jnp.where(incl, p[CHUNK:, GROUP_W:], 0.0) for p in pm]
    t_inv = [eye + n for n in n_ab]
    n_pow = n_ab
    n_blk = [blocks(n) for n in n_pow]
    for _ in range(int(math.log2(CHUNK)) - 1):
        n_pow = [_bdot(n_pow[i], n_blk[i]) for i in range(nprob)]
        n_blk = [blocks(n) for n in n_pow]
        t_inv = [t_inv[i] + _bdot(t_inv[i], n_blk[i]) for i in range(nprob)]
    av = [_bdot(cat0(a_ak[i], a_rk[i]), blocks(v_l[i])) for i in range(nprob)]
    y2 = [_bdot(t_inv[i], jnp.concatenate([blocks(a_l[i]), blocks(av[i][:CHUNK])], axis=1))
          for i in range(nprob)]
    st = [state[b, gi] for b, gi in probs]
    x2 = [_bdot_nt(cat0(y2[i][:, :GROUP_W], r_l[i]), st[i]) for i in range(nprob)]
    u_l = [x2[i][:CHUNK] + y2[i][:, GROUP_W:] for i in range(nprob)]
    o_l = [x2[i][CHUNK:] + _bdot(a_rb[i], blocks(u_l[i])) + av[i][CHUNK:] for i in range(nprob)]
    for i, (b, gi) in enumerate(probs):
        outer = _bdot_tn(cat0(u_l[i], v_l[i]), cat0(be_l[i], ke_l[i]))
        state[b, gi] = (st[i] * decay_all[b][:, gi * GROUP_W:(gi + 1) * GROUP_W]
                        + jnp.where(blk, outer, 0.0))
    o = stack([jnp.concatenate([o_l[b * ngroup + gi] for gi in range(ngroup)], axis=1)
               for b in range(nb)])

    mean = _bdot(o, ones) * (1.0 / RWKV_HD)
    cen = o - mean
    var = _bdot(cen * cen, ones) * (1.0 / RWKV_HD)
    o = cen * lax.rsqrt(var + RWKV_GN_EPS) * lnw[...] + lnb[...]
    o = ((o + bonus) * gate).astype(BF16)
    for b in range(nb):
        o_ref[b] = o[b * CHUNK:(b + 1) * CHUNK]


def _rwkv_mixer(z3, v_first, p):
    b, s = z3.shape[0], z3.shape[1]
    has_vres = v_first is not None
    nb = SEQS_PER_STEP if b % SEQS_PER_STEP == 0 else 1
    row = lambda w, c: pl.BlockSpec((nb, CHUNK, w), lambda i, j: (i, j, c))
    full = lambda a: pl.BlockSpec(a.shape, lambda i, j: (0,) * a.ndim)
    ins = [z3, z3, z3, z3]
    specs = [row(3 * BRANCH, COL_RWKV // (3 * BRANCH)), row(RWKV_GATE_LORA, COL_RW_G // RWKV_GATE_LORA),
             row(LANE, COL_RW_W // LANE), row(LANE, COL_RW_A // LANE)]
    if has_vres:
        ins.append(v_first)
        specs.append(row(BRANCH, 0))
    names = ["mu_rkv", "mu_g", "mu_w", "mu_a", "w0", "w2", "a0", "a2", "g2", "kk", "ka", "rk", "lnw", "lnb"]
    if has_vres:
        names += ["v0", "v1", "v2"]
    for nme in names:
        ins.append(p[nme])
        specs.append(full(p[nme]))
    o_spec = pl.BlockSpec((nb, CHUNK, BRANCH), lambda i, j: (i, j, 0))
    o_shape = jax.ShapeDtypeStruct((b, s, BRANCH), BF16)
    if has_vres:
        out_shape, out_specs = o_shape, o_spec
    else:
        out_shape = (o_shape, jax.ShapeDtypeStruct((b, s, BRANCH), F32))
        out_specs = (o_spec, pl.BlockSpec((nb, CHUNK, BRANCH), lambda i, j: (i, j, 0)))
    res = pl.pallas_call(
        functools.partial(_rwkv_kernel, has_vres=has_vres, nb=nb),
        out_shape=out_shape,
        grid=(b // nb, s // CHUNK),
        in_specs=specs,
        out_specs=out_specs,
        scratch_shapes=[pltpu.VMEM((nb, 8, 3 * BRANCH), F32), pltpu.VMEM((nb, 8, RWKV_GATE_LORA), F32),
                        pltpu.VMEM((nb, 8, LANE), F32), pltpu.VMEM((nb, 8, LANE), F32),
                        pltpu.VMEM((nb, RWKV_HEADS // HEAD_GROUP, GROUP_W, GROUP_W), F32)],
        compiler_params=_cparams(("parallel", "arbitrary")),
        name="rwkv7",
    )(*ins)
    if has_vres:
        return res, v_first
    return res[0], res[1]


def _pad_rows(w, rows, offset=0):
    return jnp.zeros((rows, w.shape[1]), F32).at[offset:offset + w.shape[0]].set(w)


N_IN = 14800
COL_GATES_SRC = 6608
MIX_BLOCKS = ((COL_CONV, ((0, 3072),)), (COL_DIFF, ((3088, 1536),)), (COL_RWKV, ((4624, 1536),)),
              (COL_RW_G, ((6352, 256),)),
              (COL_RW_W, ((6160, RWKV_DECAY_LORA), (None, LANE - RWKV_DECAY_LORA))),
              (COL_RW_A, ((6256, RWKV_AAA_LORA), (3072, GLA_LOW_RANK), (None, LANE - GLA_WLR_LANE - GLA_LOW_RANK))))
PACK_CHUNK = 512


def _pack_w_in_kernel(wt_ref, mix_ref, gate_ref):
    tc = wt_ref.shape[1]

    def rows(src, n):
        return jnp.zeros((n, tc), F32) if src is None else wt_ref[src:src + n, :]

    def put(out_ref, dst, block):
        out_ref[:, dst:dst + block.shape[0]] = block.T.astype(BF16)

    for dst, parts in MIX_BLOCKS:
        if len(parts) == 1:
            src, n = parts[0]
            for c in range(0, n, PACK_CHUNK):
                m = min(PACK_CHUNK, n - c)
                put(mix_ref, dst + c, rows(src + c, m))
        else:
            put(mix_ref, dst, jnp.concatenate([rows(src, n) for src, n in parts], axis=0))
    for c in range(0, N_IN - COL_GATES_SRC, PACK_CHUNK):
        put(gate_ref, c, rows(COL_GATES_SRC + c, PACK_CHUNK))


def _pack_w_in(w_in):
    depth = w_in.shape[0]
    tc = 256
    n_gate = N_IN - COL_GATES_SRC
    assert n_gate % PACK_CHUNK == 0
    return pl.pallas_call(
        _pack_w_in_kernel,
        out_shape=(jax.ShapeDtypeStruct((depth, D_MODEL, N_MIX), BF16),
                   jax.ShapeDtypeStruct((depth, D_MODEL, n_gate), BF16)),
        grid=(depth, D_MODEL // tc),
        in_specs=[pl.BlockSpec((None, N_IN, tc), lambda l, i: (l, 0, i))],
        out_specs=(pl.BlockSpec((None, tc, N_MIX), lambda l, i: (l, i, 0)),
                   pl.BlockSpec((None, tc, n_gate), lambda l, i: (l, i, 0))),
        compiler_params=_cparams(("parallel", "parallel")),
        name="pack_w_in",
    )(jnp.swapaxes(w_in, 1, 2))


def _pad_lanes(v, width, offset=0):
    return jnp.zeros((1, width), F32).at[0, offset:offset + v.shape[0]].set(v)


def kernel(x, norm_mix_pre, w_in, conv_w, gla_wa2, gla_ba, gla_norm, diff_lq1, diff_lk1, diff_lq2, diff_lk2,
           diff_norm, rw_mu, rw_w0, rw_w2, rw_a0, rw_a2, rw_g2, rw_kk, rw_ka, rw_rk, rw_lnw, rw_lnb, rw_v0,
           rw_v1, rw_v2, w_branch, w_out, norm_mix_post, norm_ffn_pre, w_gate, w_up, w_down, norm_ffn_post):
    bsz, seq = x.shape[0], x.shape[1]
    depth = w_in.shape[0]
    w_branch_b, w_out_b, w_down_b = (w.astype(BF16) for w in (w_branch, w_out, w_down))
    w_mix, w_gates = _pack_w_in(w_in)
    t = bsz * seq
    x2 = x.reshape(t, D_MODEL)
    row = lambda v: v.reshape(1, -1)
    v_first = None
    h = None
    for l in range(depth):
        if l == 0:
            z2, h = _inproj(x2, w_mix, l, gain=row(norm_mix_pre[l]))
        else:
            z2 = _inproj(h, w_mix, l)
        z3 = z2.reshape(bsz, seq, N_MIX)
        lambda_init = 0.8 - 0.6 * math.exp(-0.3 * l)

        o_conv = _conv_mixer(z3, conv_w[l])
        o_gla = _gla_mixer(z3, _pad_rows(gla_wa2[l], LANE, GLA_WLR_LANE), row(gla_ba[l]), row(gla_norm[l]))
        o_diff = _diff_mixer(z3, row(diff_lq1[l]), row(diff_lk1[l]), row(diff_lq2[l]),
                             row(diff_lk2[l]), row(diff_norm[l]), lambda_init)
        mu = rw_mu[l]
        rp = {
            "mu_rkv": row(mu[0:1536]),
            "mu_w": _pad_lanes(mu[1536:1632], LANE),
            "mu_a": _pad_lanes(mu[1632:1728], LANE),
            "mu_g": row(mu[1728:1984]),
            "w0": row(rw_w0[l]), "w2": _pad_rows(rw_w2[l], LANE),
            "a0": row(rw_a0[l]), "a2": _pad_rows(rw_a2[l], LANE),
            "g2": rw_g2[l], "kk": row(rw_kk[l]), "ka": row(rw_ka[l]), "rk": row(rw_rk[l]),
            "lnw": row(rw_lnw[l]), "lnb": row(rw_lnb[l]),
        }
        if l > 0:
            rp.update(v0=row(rw_v0[l - 1]), v1=rw_v1[l - 1], v2=rw_v2[l - 1])
        o_rwkv, v_first = _rwkv_mixer(z3, v_first if l > 0 else None, rp)

        outs = [o.reshape(t, BRANCH) for o in (o_conv, o_gla, o_diff, o_rwkv)]
        merged = _merge(h, w_gates, outs, w_branch_b, l)
        x2, h_ffn = _proj_norm_res(merged, w_out_b, l, x2, row(norm_mix_post[l]),
                                   row(norm_ffn_pre[l]), D_MODEL)
        act = _ffn_up(h_ffn, w_gate, w_up, l)
        next_gain = row(norm_mix_pre[l + 1]) if l + 1 < depth else None
        x2, h = _proj_norm_res(act, w_down_b, l, x2, row(norm_ffn_post[l]), next_gain, FFN_DOWN_TK)
    return x2.reshape(bsz, seq, D_MODEL)
```

```python
import functools
import math

import jax
import jax.numpy as jnp
from jax import lax
from jax.experimental import pallas as pl
from jax.experimental.pallas import tpu as pltpu

F32 = jnp.float32
BF16 = jnp.bfloat16

D_MODEL = 2048
CHUNK = 64
N_BRANCH = 4
BRANCH = 512
GLA_HEADS = 4
GLA_DK = 64
GLA_DV = 128
GLA_KEY = GLA_HEADS * GLA_DK
GLA_LOW_RANK = 16
GLA_LOGIT_NORM = 16.0
DIFF_HEADS = 4
DIFF_D = 64
RWKV_HEADS = 8
RWKV_HD = 64
RWKV_DECAY_LORA = 96
RWKV_AAA_LORA = 96
RWKV_MV_LORA = 64
RWKV_GATE_LORA = 256
D_FF = 5632
RMS_EPS = 1e-6
HEAD_EPS = 1e-5
RWKV_GN_EPS = 64e-5
NEG_INF = -1e30
LOG2E = 1.4426950408889634

COL_CONV = 0
COL_GLA = 1536
COL_DIFF = 3072
COL_RWKV = 4608
COL_RW_G = 6144
COL_RW_W = 6400
COL_RW_A = 6528
GLA_WLR_LANE = RWKV_AAA_LORA
LANE = 128
MXU_DIM = 256
N_MIX = COL_RW_A + LANE
HEAD_GROUP = 4
GROUP_W = HEAD_GROUP * RWKV_HD
SEQS_PER_STEP = 4

VMEM_LIMIT = 56 * 1024 * 1024
PROJ_SUB_ROWS = 128
OUT_PROJ_TM = 512
FFN_DOWN_TM = 256
assert GROUP_W == MXU_DIM


def _tile(n, pref):
    t = min(n, pref)
    while n % t:
        t -= 8
    return t


def _cparams(sem):
    return pltpu.CompilerParams(dimension_semantics=sem, vmem_limit_bytes=VMEM_LIMIT)


def _bdot(a, b):
    return jnp.dot(a.astype(BF16), b.astype(BF16), preferred_element_type=F32)


def _bdot_nt(a, b):
    return lax.dot_general(a.astype(BF16), b.astype(BF16), (((1,), (1,)), ((), ())),
                           preferred_element_type=F32)


def _bdot_tn(a, b):
    return lax.dot_general(a.astype(BF16), b.astype(BF16), (((0,), (0,)), ((), ())),
                           preferred_element_type=F32)


def _split_dot_left(ones_bf16, x):
    hi = x.astype(BF16)
    lo = (x - hi.astype(F32)).astype(BF16)
    return (jnp.dot(ones_bf16, hi, preferred_element_type=F32)
            + jnp.dot(ones_bf16, lo, preferred_element_type=F32))


def _sigmoid(x):
    return 0.5 * jnp.tanh(0.5 * x) + 0.5


def _rms(x, gain, eps):
    return x * lax.rsqrt(jnp.mean(x * x, axis=-1, keepdims=True) + eps) * gain


def _iota2(shape, dim):
    return lax.broadcasted_iota(jnp.int32, shape, dim)


def _tril_incl(n):
    return (_iota2((n, n), 0) >= _iota2((n, n), 1)).astype(BF16)


def _head_ones(n, width):
    return ((_iota2((n, n), 0) // width) == (_iota2((n, n), 1) // width)).astype(BF16)


def _expand(x, col_group):
    rows = HEAD_GROUP * CHUNK
    xt = jnp.concatenate([x] * HEAD_GROUP, axis=0)
    keep = (_iota2((rows, x.shape[1]), 0) // CHUNK) == (_iota2((rows, x.shape[1]), 1) // col_group)
    return jnp.where(keep, xt, 0.0)


def _inproj_kernel(*refs, fuse_norm):
    if fuse_norm:
        x_ref, g_ref, w_ref, z_ref, h_ref = refs

        @pl.when(pl.program_id(1) == 0)
        def _():
            h_ref[...] = _rms(x_ref[...], g_ref[...], RMS_EPS).astype(BF16)
    else:
        h_ref, w_ref, z_ref = refs
    z_ref[...] = jnp.dot(h_ref[...], w_ref[...], preferred_element_type=F32).astype(BF16)


def _inproj(h, w, l, gain=None):
    t, n = h.shape[0], w.shape[2]
    fuse_norm = gain is not None
    tm, tn = _tile(t, 1024), _tile(n, 1664)
    row_spec = pl.BlockSpec((tm, D_MODEL), lambda i, j: (i, 0))
    z_spec = pl.BlockSpec((tm, tn), lambda i, j: (i, j))
    w_spec = pl.BlockSpec((None, D_MODEL, tn), lambda i, j: (l, 0, j))
    z_shape = jax.ShapeDtypeStruct((t, n), BF16)
    if fuse_norm:
        ins = (h, gain, w)
        in_specs = [row_spec, pl.BlockSpec((1, D_MODEL), lambda i, j: (0, 0)), w_spec]
        out_shape, out_specs = (z_shape, jax.ShapeDtypeStruct((t, D_MODEL), BF16)), (z_spec, row_spec)
    else:
        ins, in_specs, out_shape, out_specs = (h, w), [row_spec, w_spec], z_shape, z_spec
    return pl.pallas_call(
        functools.partial(_inproj_kernel, fuse_norm=fuse_norm),
        out_shape=out_shape,
        grid=(t // tm, n // tn),
        in_specs=in_specs,
        out_specs=out_specs,
        compiler_params=_cparams(("parallel", "arbitrary")),
        name="inproj",
    )(*ins)


def _merge_kernel(h_ref, wg0, wg1, wg2, wg3, o0, o1, o2, o3, p_ref, out_ref):
    h = h_ref[...]
    acc = None
    for n, (wg, o) in enumerate(((wg0, o0), (wg1, o1), (wg2, o2), (wg3, o3))):
        gate = _sigmoid(jnp.dot(h, wg[...], preferred_element_type=F32))
        term = gate * jnp.dot(o[...], p_ref[n], preferred_element_type=F32)
        acc = term if acc is None else acc + term
    out_ref[...] = acc.astype(BF16)


def _merge(h, w_gate, outs, w_branch, l):
    t = h.shape[0]
    tm, tn = _tile(t, 1024), 512
    nj = D_MODEL // tn
    gate_specs = [pl.BlockSpec((None, D_MODEL, tn), functools.partial(lambda i, j, n: (l, 0, n * nj + j), n=n))
                  for n in range(N_BRANCH)]
    o_specs = [pl.BlockSpec((tm, BRANCH), lambda i, j: (i, 0)) for _ in range(N_BRANCH)]
    return pl.pallas_call(
        _merge_kernel,
        out_shape=jax.ShapeDtypeStruct((t, D_MODEL), BF16),
        grid=(t // tm, nj),
        in_specs=[pl.BlockSpec((tm, D_MODEL), lambda i, j: (i, 0))] + gate_specs + o_specs
                 + [pl.BlockSpec((None, N_BRANCH, BRANCH, tn), lambda i, j: (l, 0, 0, j))],
        out_specs=pl.BlockSpec((tm, tn), lambda i, j: (i, j)),
        compiler_params=_cparams(("parallel", "arbitrary")),
        name="merge",
    )(h, w_gate, w_gate, w_gate, w_gate, *outs, w_branch)


def _proj_norm_res_kernel(a_ref, w_ref, x_ref, g_ref, gn_ref, o_ref, *h_ref, nk):
    k = pl.program_id(1)

    def accumulate(first):
        part = jnp.dot(a_ref[...], w_ref[...], preferred_element_type=F32)
        o_ref[...] = part if first else o_ref[...] + part

    def finish():
        sub = PROJ_SUB_ROWS
        for r in range(o_ref.shape[0] // sub):
            rows = slice(r * sub, (r + 1) * sub)
            y = jnp.dot(a_ref[rows, :], w_ref[...], preferred_element_type=F32)
            if nk > 1:
                y = y + o_ref[rows, :]
            xn = x_ref[rows, :] + _rms(y, g_ref[...], RMS_EPS)
            o_ref[rows, :] = xn
            if h_ref:
                h_ref[0][rows, :] = _rms(xn, gn_ref[...], RMS_EPS).astype(BF16)

    if nk == 1:
        finish()
    else:
        pl.when(k == 0)(functools.partial(accumulate, True))
        if nk > 2:
            pl.when((k > 0) & (k < nk - 1))(functools.partial(accumulate, False))
        pl.when(k == nk - 1)(finish)


def _proj_norm_res(a, w, l, x2, gain, next_gain, tm, tk):
    t, kdim = a.shape
    emit_h = next_gain is not None
    row_spec = pl.BlockSpec((tm, D_MODEL), lambda i, k: (i, 0))
    vec_spec = pl.BlockSpec((1, D_MODEL), lambda i, k: (0, 0))
    out_shape = [jax.ShapeDtypeStruct((t, D_MODEL), F32)]
    if emit_h:
        out_shape.append(jax.ShapeDtypeStruct((t, D_MODEL), BF16))
    assert kdim % tk == 0 and t % tm == 0 and tm % PROJ_SUB_ROWS == 0
    w_mode = dict(pipeline_mode=pl.Buffered(1)) if tk == kdim else {}
    res = pl.pallas_call(
        functools.partial(_proj_norm_res_kernel, nk=kdim // tk),
        out_shape=tuple(out_shape),
        grid=(t // tm, kdim // tk),
        in_specs=[pl.BlockSpec((tm, tk), lambda i, k: (i, k)),
                  pl.BlockSpec((None, tk, D_MODEL), lambda i, k: (l, k, 0), **w_mode),
                  row_spec, vec_spec, vec_spec],
        out_specs=tuple([row_spec] * len(out_shape)),
        compiler_params=_cparams(("parallel", "arbitrary")),
        name="proj_norm_res",
    )(a, w, x2, gain, next_gain if emit_h else gain)
    return (res[0], res[1]) if emit_h else (res[0], None)


def _ffn_up_kernel(h_ref, wg_ref, wu_ref, a_ref, wg_b, wu_b):
    @pl.when(pl.program_id(1) == 0)
    def _():
        wg_b[...] = wg_ref[...].astype(BF16)
        wu_b[...] = wu_ref[...].astype(BF16)

    h = h_ref[...]
    gt = jnp.dot(h, wg_b[...], preferred_element_type=F32)
    up = jnp.dot(h, wu_b[...], preferred_element_type=F32)
    a_ref[...] = (gt * _sigmoid(gt) * up).astype(BF16)


def _ffn_up(h, wg, wu, l):
    t = h.shape[0]
    tm, tn = _tile(t, 1024), 512
    return pl.pallas_call(
        _ffn_up_kernel,
        out_shape=jax.ShapeDtypeStruct((t, D_FF), BF16),
        grid=(D_FF // tn, t // tm),
        in_specs=[pl.BlockSpec((tm, D_MODEL), lambda j, i: (i, 0)),
                  pl.BlockSpec((None, D_MODEL, tn), lambda j, i: (l, 0, j)),
                  pl.BlockSpec((None, D_MODEL, tn), lambda j, i: (l, 0, j))],
        out_specs=pl.BlockSpec((tm, tn), lambda j, i: (i, j)),
        scratch_shapes=[pltpu.VMEM((D_MODEL, tn), BF16), pltpu.VMEM((D_MODEL, tn), BF16)],
        compiler_params=_cparams(("parallel", "arbitrary")),
        name="ffn_up",
    )(h, wg, wu)


def _conv_kernel(b_ref, c_ref, u_ref, w_ref, o_ref, carry):
    @pl.when(pl.program_id(1) == 0)
    def _():
        carry[...] = jnp.zeros_like(carry)

    cu = c_ref[0].astype(F32) * u_ref[0].astype(F32)
    ts = cu.shape[0]
    row = _iota2(cu.shape, 0)
    p1 = carry[7:8, :]
    p2 = carry[6:7, :]
    s1 = jnp.where(row == 0, p1, pltpu.roll(cu, 1, axis=0))
    s2 = jnp.where(row == 0, p2, jnp.where(row == 1, p1, pltpu.roll(cu, 2, axis=0)))
    w = w_ref[...]
    y = w[2:3, :] * cu + w[1:2, :] * s1 + w[0:1, :] * s2
    o_ref[0] = (b_ref[0].astype(F32) * y).astype(BF16)
    carry[...] = cu[ts - 8:, :]


def _conv_mixer(z3, conv_w):
    b, s = z3.shape[0], z3.shape[1]
    ts = _tile(s, 512)
    c0 = COL_CONV // BRANCH
    return pl.pallas_call(
        _conv_kernel,
        out_shape=jax.ShapeDtypeStruct((b, s, BRANCH), BF16),
        grid=(b, s // ts),
        in_specs=[pl.BlockSpec((1, ts, BRANCH), lambda i, j: (i, j, c0)),
                  pl.BlockSpec((1, ts, BRANCH), lambda i, j: (i, j, c0 + 1)),
                  pl.BlockSpec((1, ts, BRANCH), lambda i, j: (i, j, c0 + 2)),
                  pl.BlockSpec((3, BRANCH), lambda i, j: (0, 0))],
        out_specs=pl.BlockSpec((1, ts, BRANCH), lambda i, j: (i, j, 0)),
        scratch_shapes=[pltpu.VMEM((8, BRANCH), F32)],
        compiler_params=_cparams(("parallel", "arbitrary")),
        name="conv",
    )(z3, z3, z3, conv_w)


def _gla_kernel(q_ref, k_ref, v_ref, g_ref, wl_ref, wa2_ref, ba_ref, gn_ref, o_ref, state):
    @pl.when(pl.program_id(1) == 0)
    def _():
        state[...] = jnp.zeros_like(state)

    nb = q_ref.shape[0]
    causal = _iota2((CHUNK, GLA_KEY), 0) >= (_iota2((CHUNK, GLA_KEY), 1) % CHUNK)
    same_head = ((_iota2((GLA_HEADS * GLA_DV, GLA_KEY), 0) // GLA_DV)
                 == (_iota2((GLA_HEADS * GLA_DV, GLA_KEY), 1) // GLA_DK))
    tril = _tril_incl(CHUNK)
    q_in, k_in, q_st, k_st, v, ve, dec = [], [], [], [], [], [], []
    for b in range(nb):
        q = q_ref[b].astype(F32) * (GLA_DK ** -0.5)
        k = k_ref[b].astype(F32)
        z = _bdot(wl_ref[b], wa2_ref[...]) + ba_ref[...]
        log_a = (jnp.minimum(z, 0.0) - jnp.log(1.0 + jnp.exp(-jnp.abs(z)))) * (1.0 / GLA_LOGIT_NORM)
        cum = _split_dot_left(tril, log_a)
        mid = cum[CHUNK // 2 - 1:CHUNK // 2, :]
        last = cum[CHUNK - 1:CHUNK, :]
        q_in.append(q * jnp.exp(cum - mid))
        k_in.append(_expand(k * jnp.exp(mid - cum), GLA_DK).astype(BF16))
        q_st.append(q * jnp.exp(cum))
        k_st.append(k * jnp.exp(last - cum))
        v.append(v_ref[b])
        ve.append(_expand(v[b].astype(F32), GLA_DV).astype(BF16))
        dec.append(jnp.exp(last))
    scores = [jnp.where(causal, _bdot_nt(q_in[b], k_in[b]), 0.0) for b in range(nb)]
    st = [state[b] for b in range(nb)]
    o_l = [_bdot(scores[b], ve[b]) + _bdot_nt(q_st[b], st[b]) for b in range(nb)]
    for b in range(nb):
        state[b] = st[b] * dec[b] + jnp.where(same_head, _bdot_tn(v[b], k_st[b]), 0.0)
    gn = gn_ref[...]
    for b in range(nb):
        o = o_l[b]
        g = g_ref[b].astype(F32)
        gate = g * _sigmoid(g)
        outs = [_rms(o[:, h * GLA_DV:(h + 1) * GLA_DV], gn, HEAD_EPS) for h in range(GLA_HEADS)]
        o_ref[b] = (jnp.concatenate(outs, axis=1) * gate).astype(BF16)


def _gla_mixer(z3, wa2p, ba, gn):
    b, s = z3.shape[0], z3.shape[1]
    cq = COL_GLA // GLA_KEY
    cv = (COL_GLA + 2 * GLA_KEY) // BRANCH
    cw = COL_RW_A // LANE
    nb = SEQS_PER_STEP if b % SEQS_PER_STEP == 0 else 1
    return pl.pallas_call(
        _gla_kernel,
        out_shape=jax.ShapeDtypeStruct((b, s, BRANCH), BF16),
        grid=(b // nb, s // CHUNK),
        in_specs=[pl.BlockSpec((nb, CHUNK, GLA_KEY), lambda i, j: (i, j, cq)),
                  pl.BlockSpec((nb, CHUNK, GLA_KEY), lambda i, j: (i, j, cq + 1)),
                  pl.BlockSpec((nb, CHUNK, BRANCH), lambda i, j: (i, j, cv)),
                  pl.BlockSpec((nb, CHUNK, BRANCH), lambda i, j: (i, j, cv + 1)),
                  pl.BlockSpec((nb, CHUNK, LANE), lambda i, j: (i, j, cw)),
                  pl.BlockSpec((LANE, GLA_KEY), lambda i, j: (0, 0)),
                  pl.BlockSpec((1, GLA_KEY), lambda i, j: (0, 0)),
                  pl.BlockSpec((1, GLA_DV), lambda i, j: (0, 0))],
        out_specs=pl.BlockSpec((nb, CHUNK, BRANCH), lambda i, j: (i, j, 0)),
        scratch_shapes=[pltpu.VMEM((nb, GLA_HEADS * GLA_DV, GLA_KEY), F32)],
        compiler_params=_cparams(("parallel", "arbitrary")),
        name="gla",
    )(z3, z3, z3, z3, z3, wa2p, ba, gn)


DIFF_KEY_TILE = 512
DIFF_QUERY_TILE = 512
VT_ROWS = 2 * DIFF_D + 16


def _alibi_slope(h):
    return 2.0 ** (-8.0 * (h + 1) / DIFF_HEADS)


def _diff_kernel(q_ref, k_ref, v_ref, lq1, lk1, lq2, lk2, gn_ref, o_ref, ka_scr, vb_scr, qa_scr, acc_scr,
                 m_scr, s_scr, rel_scr, *, lambda_init, tq, tk):
    qi = pl.program_id(1)
    hw = 2 * DIFF_D
    seq = k_ref.shape[1]
    lane = _iota2((tq, hw), 1)

    @pl.when(qi == 0)
    def _():
        key_row = _iota2((tq, hw), 0).astype(F32)
        for h in range(DIFF_HEADS):
            k_aug = []
            for part in range(tk // tq):
                key_bias = (_alibi_slope(h) * LOG2E) * (key_row + float(part * tq))
                hi = key_bias.astype(BF16).astype(F32)
                mid = (key_bias - hi).astype(BF16).astype(F32)
                lo = key_bias - hi - mid
                aug = jnp.where(lane == 0, hi, jnp.where(lane == 1, mid, jnp.where(lane == 2, lo, 0.0)))
                k_aug.append(aug.astype(BF16))
            for c in range(seq // tq):
                rows = slice(c * tq, (c + 1) * tq)
                ka_scr[h, rows, 0:hw] = k_ref[0, rows, h * hw:(h + 1) * hw].astype(BF16)
                ka_scr[h, rows, hw:2 * hw] = k_aug[c % (tk // tq)]
        k_row = _iota2((tk, 2 * tq), 0)
        col = _iota2((tk, 2 * tq), 1)
        for par in range(tk // tq):
            q_row = jnp.where(col >= tq, col - tq, col) + par * tq
            d = q_row - k_row
            rel_scr[par] = jnp.where((k_row // CHUNK) <= (q_row // CHUNK), (d - jnp.abs(d)).astype(F32), NEG_INF)
        sub = _iota2((VT_ROWS - hw, tq), 0)
        tail = jnp.where(sub == 0, 1.0, 0.0).astype(BF16)
        for c in range(seq // tq):
            rows = slice(c * tq, (c + 1) * tq)
            for h in range(DIFF_HEADS):
                vb_scr[h, 0:hw, rows] = v_ref[0, rows, h * hw:(h + 1) * hw].astype(F32).T.astype(BF16)
                vb_scr[h, hw:VT_ROWS, rows] = tail

    lane2 = _iota2((2 * tq, hw), 1)
    ones_aug = (lane2 < 3).astype(F32)
    for h in range(DIFF_HEADS):
        q = q_ref[0, :, h * hw:(h + 1) * hw].astype(F32) * (DIFF_D ** -0.5 * LOG2E)
        q2 = jnp.concatenate([jnp.where(lane < DIFF_D, q, 0.0), jnp.where(lane < DIFF_D, 0.0, q)],
                             axis=0)
        qa_scr[h] = jnp.concatenate([q2, ones_aug], axis=1).astype(BF16)
    acc_scr[...] = jnp.zeros_like(acc_scr)
    m_scr[...] = jnp.full_like(m_scr, NEG_INF)

    last_tile = qi // (tk // tq)

    def tile_scores(j):
        start = pl.multiple_of(j * tk, tk)
        return [lax.dot_general(ka_scr[h, pl.ds(start, tk), :], qa_scr[h], (((1,), (1,)), ((), ())),
                                preferred_element_type=F32) for h in range(DIFF_HEADS)]

    def update(scores, j):
        start = pl.multiple_of(j * tk, tk)
        alphas, probs = [], []
        for h, s in enumerate(scores):
            m_old = m_scr[h]
            m_new = jnp.maximum(m_old, jnp.max(s, axis=0, keepdims=True))
            alphas.append(jnp.exp2(m_old - m_new))
            probs.append(jnp.exp2((s - m_new).astype(BF16)))
            m_scr[h] = m_new - _alibi_slope(h) * LOG2E * tk
        pvs = [jnp.dot(vb_scr[h, :, pl.ds(start, tk)], probs[h], preferred_element_type=F32)
               for h in range(DIFF_HEADS)]
        for h in range(DIFF_HEADS):
            acc_scr[h] = alphas[h] * acc_scr[h] + pvs[h]

    for h, s in enumerate(tile_scores(0)):
        s_scr[h] = s

    def body(j, carry):
        nxt = tile_scores(j + 1)
        update([s_scr[h] for h in range(DIFF_HEADS)], j)
        for h in range(DIFF_HEADS):
            s_scr[h] = nxt[h]
        return carry

    lax.fori_loop(0, last_tile, body, 0)

    rel = rel_scr[qi % (tk // tq)]
    lam = (jnp.exp(jnp.sum(lq1[...] * lk1[...], axis=-1, keepdims=True))
           - jnp.exp(jnp.sum(lq2[...] * lk2[...], axis=-1, keepdims=True))
           + lambda_init)
    update([s_scr[h] + (_alibi_slope(h) * LOG2E) * rel for h in range(DIFF_HEADS)], last_tile)
    for h in range(DIFF_HEADS):
        on = acc_scr[h, 0:hw, :] / acc_scr[h, hw:hw + 1, :]
        o = (on[:, :tq] - lam * on[:, tq:]).T
        o_ref[0, :, h * hw:(h + 1) * hw] = (_rms(o, gn_ref[...], HEAD_EPS) * (1.0 - lambda_init)).astype(BF16)


def _diff_mixer(z3, lq1, lk1, lq2, lk2, gn, lambda_init):
    b, s = z3.shape[0], z3.shape[1]
    tk = _tile(s, DIFF_KEY_TILE)
    tq = _tile(tk, DIFF_QUERY_TILE)
    assert s % tk == 0 and tk % tq == 0 and tq % CHUNK == 0
    hw = 2 * DIFF_D
    cq = COL_DIFF // BRANCH
    nh = DIFF_HEADS
    vec = pl.BlockSpec((1, DIFF_D), lambda i, j: (0, 0))
    return pl.pallas_call(
        functools.partial(_diff_kernel, lambda_init=lambda_init, tq=tq, tk=tk),
        out_shape=jax.ShapeDtypeStruct((b, s, BRANCH), BF16),
        grid=(b, s // tq),
        in_specs=[pl.BlockSpec((1, tq, BRANCH), lambda i, j: (i, j, cq)),
                  pl.BlockSpec((1, s, BRANCH), lambda i, j: (i, 0, cq + 1), pipeline_mode=pl.Buffered(1)),
                  pl.BlockSpec((1, s, BRANCH), lambda i, j: (i, 0, cq + 2), pipeline_mode=pl.Buffered(1)),
                  vec, vec, vec, vec,
                  pl.BlockSpec((1, hw), lambda i, j: (0, 0))],
        out_specs=pl.BlockSpec((1, tq, BRANCH), lambda i, j: (i, j, 0)),
        scratch_shapes=[pltpu.VMEM((nh, s, 2 * hw), BF16), pltpu.VMEM((nh, VT_ROWS, s), BF16),
                        pltpu.VMEM((nh, 2 * tq, 2 * hw), BF16), pltpu.VMEM((nh, VT_ROWS, 2 * tq), F32),
                        pltpu.VMEM((nh, 1, 2 * tq), F32), pltpu.VMEM((nh, tk, 2 * tq), F32),
                        pltpu.VMEM((tk // tq, tk, 2 * tq), F32)],
        compiler_params=_cparams(("parallel", "arbitrary")),
        name="diff_attn",
    )(z3, z3, z3, lq1, lk1, lq2, lk2, gn)


def _shift(x, prev_rows):
    row = _iota2(x.shape, 0)
    return jnp.where(row == 0, prev_rows[7:8, :], pltpu.roll(x, 1, axis=0))


def _rwkv_kernel(*refs, has_vres, nb):
    if has_vres:
        (rkv_ref, gl_ref, wb_ref, ab_ref, vf_ref, mu_rkv, mu_g, mu_w, mu_a, w0, w2, a0, a2, g2, kk_s, ka_s,
         rk_s, lnw, lnb, v0, v1, v2, o_ref, p_rkv, p_g, p_w, p_a, state) = refs
    else:
        (rkv_ref, gl_ref, wb_ref, ab_ref, mu_rkv, mu_g, mu_w, mu_a, w0, w2, a0, a2, g2, kk_s, ka_s,
         rk_s, lnw, lnb, o_ref, vf_out, p_rkv, p_g, p_w, p_a, state) = refs

    @pl.when(pl.program_id(1) == 0)
    def _():
        for r in (p_rkv, p_g, p_w, p_a, state):
            r[...] = jnp.zeros_like(r)

    def stack(parts):
        return parts[0] if nb == 1 else jnp.concatenate(parts, axis=0)

    def mixed(x_ref, p_ref, mu_ref):
        parts = []
        for b in range(nb):
            x = x_ref[b].astype(F32)
            parts.append(x + (_shift(x, p_ref[b]) - x) * mu_ref[...])
            p_ref[b] = x[CHUNK - 8:, :]
        return stack(parts)

    rkv = mixed(rkv_ref, p_rkv, mu_rkv)
    g_lr = mixed(gl_ref, p_g, mu_g)
    w_lr = mixed(wb_ref, p_w, mu_w)
    a_lr = mixed(ab_ref, p_a, mu_a)
    r = rkv[:, 0:BRANCH]
    k = rkv[:, BRANCH:2 * BRANCH]
    v = rkv[:, 2 * BRANCH:3 * BRANCH]
    nrow = nb * CHUNK

    y = w0[...] + _bdot(jnp.tanh(w_lr), w2[...])
    lw = -math.exp(-0.5) * _sigmoid(y)
    a = _sigmoid(a0[...] + _bdot(a_lr, a2[...]))
    gate = _bdot(_sigmoid(g_lr), g2[...])
    if has_vres:
        vf = stack([vf_ref[b] for b in range(nb)])
        v = v + (vf - v) * _sigmoid(v0[...] + _bdot(_bdot(v, v1[...]), v2[...]))
    else:
        for b in range(nb):
            vf_out[b] = v[b * CHUNK:(b + 1) * CHUNK]

    ones = _head_ones(BRANCH, RWKV_HD)
    kk = k * kk_s[...]
    k = k * (1.0 + (a - 1.0) * ka_s[...])
    sums = _bdot(jnp.concatenate([kk * kk, r * k * rk_s[...]], axis=0), ones)
    kk = kk * lax.rsqrt(jnp.maximum(sums[:nrow], 1e-24))
    bonus = sums[nrow:] * v

    ri, ci = _iota2((nrow, nrow), 0), _iota2((nrow, nrow), 1)
    tril = ((ri // CHUNK == ci // CHUNK) & (ri >= ci)).astype(BF16)
    cum = _split_dot_left(tril, lw)
    lasts = [cum[(b + 1) * CHUNK - 1:(b + 1) * CHUNK, :] for b in range(nb)]
    decay_all = [jnp.exp(last) for last in lasts]
    a_t = -kk * jnp.exp(cum - lw)
    r_t = r * jnp.exp(cum)
    inv = jnp.exp(-cum)
    to_end = stack([decay_all[b] * inv[b * CHUNK:(b + 1) * CHUNK] for b in range(nb)])
    kka = kk * a
    b_s = kka * inv
    k_s = k * inv
    b_e = kka * to_end
    k_e = k * to_end

    rows = HEAD_GROUP * CHUNK
    blk = ((_iota2((rows, GROUP_W), 0) // CHUNK) == (_iota2((rows, GROUP_W), 1) // RWKV_HD))
    blk_bf = blk.astype(F32).astype(BF16)
    t_idx = _iota2((CHUNK, GROUP_W), 0)
    s_idx = _iota2((CHUNK, GROUP_W), 1) % CHUNK
    strict = t_idx > s_idx
    incl = t_idx >= s_idx
    eye = (t_idx == s_idx).astype(F32)

    def blocks(x):
        return jnp.concatenate([x.astype(BF16)] * HEAD_GROUP, axis=0) * blk_bf

    ngroup = RWKV_HEADS // HEAD_GROUP
    probs = [(b, gi) for b in range(nb) for gi in range(ngroup)]
    nprob = len(probs)

    def part(t):
        return [t[b * CHUNK:(b + 1) * CHUNK, gi * GROUP_W:(gi + 1) * GROUP_W] for b, gi in probs]

    cat0 = lambda x, y_: jnp.concatenate([x, y_], axis=0)
    a_l, r_l, b_l, k_l, be_l, ke_l, v_l = (part(t) for t in (a_t, r_t, b_s, k_s, b_e, k_e, v))
    pm = [_bdot_nt(cat0(a_l[i], r_l[i]), cat0(blocks(b_l[i]), blocks(k_l[i]))) for i in range(nprob)]
    n_ab = [jnp.where(strict, p[:CHUNK, :GROUP_W], 0.0) for p in pm]
    a_ak = [jnp.where(strict, p[:CHUNK, GROUP_W:], 0.0) for p in pm]
    a_rb = [jnp.where(incl, p[CHUNK:, :GROUP_W], 0.0) for p in pm]
    a_rk = [jnp.where(incl, p[CHUNK:, GROUP_W:], 0.0) for p in pm]
    t_inv = [eye + n for n in n_ab]
    n_pow = n_ab
    n_blk = [blocks(n) for n in n_pow]
    for _ in range(int(math.log2(CHUNK)) - 1):
        n_pow = [_bdot(n_pow[i], n_blk[i]) for i in range(nprob)]
        n_blk = [blocks(n) for n in n_pow]
        t_inv = [t_inv[i] + _bdot(t_inv[i], n_blk[i]) for i in range(nprob)]
    av = [_bdot(cat0(a_ak[i], a_rk[i]), blocks(v_l[i])) for i in range(nprob)]
    y2 = [_bdot(t_inv[i], jnp.concatenate([blocks(a_l[i]), blocks(av[i][:CHUNK])], axis=1))
          for i in range(nprob)]
    st = [state[b, gi] for b, gi in probs]
    x2 = [_bdot_nt(cat0(y2[i][:, :GROUP_W], r_l[i]), st[i]) for i in range(nprob)]
    u_l = [x2[i][:CHUNK] + y2[i][:, GROUP_W:] for i in range(nprob)]
    o_l = [x2[i][CHUNK:] + _bdot(a_rb[i], blocks(u_l[i])) + av[i][CHUNK:] for i in range(nprob)]
    for i, (b, gi) in enumerate(probs):
        outer = _bdot_tn(cat0(u_l[i], v_l[i]), cat0(be_l[i], ke_l[i]))
        state[b, gi] = (st[i] * decay_all[b][:, gi * GROUP_W:(gi + 1) * GROUP_W]
                        + jnp.where(blk, outer, 0.0))
    o = stack([jnp.concatenate([o_l[b * ngroup + gi] for gi in range(ngroup)], axis=1)
               for b in range(nb)])

    mean = _bdot(o, ones) * (1.0 / RWKV_HD)
    cen = o - mean
    var = _bdot(cen * cen, ones) * (1.0 / RWKV_HD)
    o = cen * lax.rsqrt(var + RWKV_GN_EPS) * lnw[...] + lnb[...]
    o = ((o + bonus) * gate).astype(BF16)
    for b in range(nb):
        o_ref[b] = o[b * CHUNK:(b + 1) * CHUNK]


def _rwkv_mixer(z3, v_first, p):
    b, s = z3.shape[0], z3.shape[1]
    has_vres = v_first is not None
    nb = SEQS_PER_STEP if b % SEQS_PER_STEP == 0 else 1
    row = lambda w, c: pl.BlockSpec((nb, CHUNK, w), lambda i, j: (i, j, c))
    full = lambda a: pl.BlockSpec(a.shape, lambda i, j: (0,) * a.ndim)
    ins = [z3, z3, z3, z3]
    specs = [row(3 * BRANCH, COL_RWKV // (3 * BRANCH)), row(RWKV_GATE_LORA, COL_RW_G // RWKV_GATE_LORA),
             row(LANE, COL_RW_W // LANE), row(LANE, COL_RW_A // LANE)]
    if has_vres:
        ins.append(v_first)
        specs.append(row(BRANCH, 0))
    names = ["mu_rkv", "mu_g", "mu_w", "mu_a", "w0", "w2", "a0", "a2", "g2", "kk", "ka", "rk", "lnw", "lnb"]
    if has_vres:
        names += ["v0", "v1", "v2"]
    for nme in names:
        ins.append(p[nme])
        specs.append(full(p[nme]))
    o_spec = pl.BlockSpec((nb, CHUNK, BRANCH), lambda i, j: (i, j, 0))
    o_shape = jax.ShapeDtypeStruct((b, s, BRANCH), BF16)
    if has_vres:
        out_shape, out_specs = o_shape, o_spec
    else:
        out_shape = (o_shape, jax.ShapeDtypeStruct((b, s, BRANCH), F32))
        out_specs = (o_spec, pl.BlockSpec((nb, CHUNK, BRANCH), lambda i, j: (i, j, 0)))
    res = pl.pallas_call(
        functools.partial(_rwkv_kernel, has_vres=has_vres, nb=nb),
        out_shape=out_shape,
        grid=(b // nb, s // CHUNK),
        in_specs=specs,
        out_specs=out_specs,
        scratch_shapes=[pltpu.VMEM((nb, 8, 3 * BRANCH), F32), pltpu.VMEM((nb, 8, RWKV_GATE_LORA), F32),
                        pltpu.VMEM((nb, 8, LANE), F32), pltpu.VMEM((nb, 8, LANE), F32),
                        pltpu.VMEM((nb, RWKV_HEADS // HEAD_GROUP, GROUP_W, GROUP_W), F32)],
        compiler_params=_cparams(("parallel", "arbitrary")),
        name="rwkv7",
    )(*ins)
    if has_vres:
        return res, v_first
    return res[0], res[1]


def _pad_rows(w, rows, offset=0):
    return jnp.zeros((rows, w.shape[1]), F32).at[offset:offset + w.shape[0]].set(w)


N_IN = 14800
COL_GATES_SRC = 6608
MIX_BLOCKS = ((COL_CONV, ((0, 3072),)), (COL_DIFF, ((3088, 1536),)), (COL_RWKV, ((4624, 1536),)),
              (COL_RW_G, ((6352, 256),)),
              (COL_RW_W, ((6160, RWKV_DECAY_LORA), (None, LANE - RWKV_DECAY_LORA))),
              (COL_RW_A, ((6256, RWKV_AAA_LORA), (3072, GLA_LOW_RANK), (None, LANE - GLA_WLR_LANE - GLA_LOW_RANK))))
PACK_CHUNK = 512


def _pack_w_in_kernel(wt_ref, mix_ref, gate_ref):
    tc = wt_ref.shape[1]

    def rows(src, n):
        return jnp.zeros((n, tc), F32) if src is None else wt_ref[src:src + n, :]

    def put(out_ref, dst, block):
        out_ref[:, dst:dst + block.shape[0]] = block.T.astype(BF16)

    for dst, parts in MIX_BLOCKS:
        if len(parts) == 1:
            src, n = parts[0]
            for c in range(0, n, PACK_CHUNK):
                m = min(PACK_CHUNK, n - c)
                put(mix_ref, dst + c, rows(src + c, m))
        else:
            put(mix_ref, dst, jnp.concatenate([rows(src, n) for src, n in parts], axis=0))
    for c in range(0, N_IN - COL_GATES_SRC, PACK_CHUNK):
        put(gate_ref, c, rows(COL_GATES_SRC + c, PACK_CHUNK))


def _pack_w_in(w_in):
    depth = w_in.shape[0]
    tc = 256
    n_gate = N_IN - COL_GATES_SRC
    assert n_gate % PACK_CHUNK == 0
    return pl.pallas_call(
        _pack_w_in_kernel,
        out_shape=(jax.ShapeDtypeStruct((depth, D_MODEL, N_MIX), BF16),
                   jax.ShapeDtypeStruct((depth, D_MODEL, n_gate), BF16)),
        grid=(depth, D_MODEL // tc),
        in_specs=[pl.BlockSpec((None, N_IN, tc), lambda l, i: (l, 0, i))],
        out_specs=(pl.BlockSpec((None, tc, N_MIX), lambda l, i: (l, i, 0)),
                   pl.BlockSpec((None, tc, n_gate), lambda l, i: (l, i, 0))),
        compiler_params=_cparams(("parallel", "parallel")),
        name="pack_w_in",
    )(jnp.swapaxes(w_in, 1, 2))


def _pad_lanes(v, width, offset=0):
    return jnp.zeros((1, width), F32).at[0, offset:offset + v.shape[0]].set(v)


def kernel(x, norm_mix_pre, w_in, conv_w, gla_wa2, gla_ba, gla_norm, diff_lq1, diff_lk1, diff_lq2, diff_lk2,
           diff_norm, rw_mu, rw_w0, rw_w2, rw_a0, rw_a2, rw_g2, rw_kk, rw_ka, rw_rk, rw_lnw, rw_lnb, rw_v0,
           rw_v1, rw_v2, w_branch, w_out, norm_mix_post, norm_ffn_pre, w_gate, w_up, w_down, norm_ffn_post):
    bsz, seq = x.shape[0], x.shape[1]
    depth = w_in.shape[0]
    w_branch_b, w_out_b, w_down_b = (w.astype(BF16) for w in (w_branch, w_out, w_down))
    w_mix, w_gates = _pack_w_in(w_in)
    t = bsz * seq
    x2 = x.reshape(t, D_MODEL)
    row = lambda v: v.reshape(1, -1)
    v_first = None
    h = None
    for l in range(depth):
        if l == 0:
            z2, h = _inproj(x2, w_mix, l, gain=row(norm_mix_pre[l]))
        else:
            z2 = _inproj(h, w_mix, l)
        z3 = z2.reshape(bsz, seq, N_MIX)
        lambda_init = 0.8 - 0.6 * math.exp(-0.3 * l)

        o_conv = _conv_mixer(z3, conv_w[l])
        o_gla = _gla_mixer(z3, _pad_rows(gla_wa2[l], LANE, GLA_WLR_LANE), row(gla_ba[l]), row(gla_norm[l]))
        o_diff = _diff_mixer(z3, row(diff_lq1[l]), row(diff_lk1[l]), row(diff_lq2[l]),
                             row(diff_lk2[l]), row(diff_norm[l]), lambda_init)
        mu = rw_mu[l]
        rp = {
            "mu_rkv": row(mu[0:1536]),
            "mu_w": _pad_lanes(mu[1536:1632], LANE),
            "mu_a": _pad_lanes(mu[1632:1728], LANE),
            "mu_g": row(mu[1728:1984]),
            "w0": row(rw_w0[l]), "w2": _pad_rows(rw_w2[l], LANE),
            "a0": row(rw_a0[l]), "a2": _pad_rows(rw_a2[l], LANE),
            "g2": rw_g2[l], "kk": row(rw_kk[l]), "ka": row(rw_ka[l]), "rk": row(rw_rk[l]),
            "lnw": row(rw_lnw[l]), "lnb": row(rw_lnb[l]),
        }
        if l > 0:
            rp.update(v0=row(rw_v0[l - 1]), v1=rw_v1[l - 1], v2=rw_v2[l - 1])
        o_rwkv, v_first = _rwkv_mixer(z3, v_first if l > 0 else None, rp)

        outs = [o.reshape(t, BRANCH) for o in (o_conv, o_gla, o_diff, o_rwkv)]
        merged = _merge(h, w_gates, outs, w_branch_b, l)
        x2, h_ffn = _proj_norm_res(merged, w_out_b, l, x2, row(norm_mix_post[l]),
                                   row(norm_ffn_pre[l]), _tile(t, OUT_PROJ_TM), D_MODEL)
        act = _ffn_up(h_ffn, w_gate, w_up, l)
        next_gain = row(norm_mix_pre[l + 1]) if l + 1 < depth else None
        x2, h = _proj_norm_res(act, w_down_b, l, x2, row(norm_ffn_post[l]), next_gain,
                               _tile(t, FFN_DOWN_TM), D_FF)
    return x2.reshape(bsz, seq, D_MODEL)
```

```python
import functools
import math

import jax
import jax.numpy as jnp
from jax import lax
from jax.experimental import pallas as pl
from jax.experimental.pallas import tpu as pltpu

F32 = jnp.float32
BF16 = jnp.bfloat16

D_MODEL = 2048
CHUNK = 64
N_BRANCH = 4
BRANCH = 512
GLA_HEADS = 4
GLA_DK = 64
GLA_DV = 128
GLA_KEY = GLA_HEADS * GLA_DK
GLA_LOW_RANK = 16
GLA_LOGIT_NORM = 16.0
DIFF_HEADS = 4
DIFF_D = 64
RWKV_HEADS = 8
RWKV_HD = 64
RWKV_DECAY_LORA = 96
RWKV_AAA_LORA = 96
RWKV_MV_LORA = 64
RWKV_GATE_LORA = 256
D_FF = 5632
RMS_EPS = 1e-6
HEAD_EPS = 1e-5
RWKV_GN_EPS = 64e-5
NEG_INF = -1e30
LOG2E = 1.4426950408889634

COL_CONV = 0
COL_GLA = 1536
COL_DIFF = 3072
COL_RWKV = 4608
COL_RW_G = 6144
COL_RW_W = 6400
COL_RW_A = 6528
GLA_WLR_LANE = RWKV_AAA_LORA
LANE = 128
MXU_DIM = 256
N_MIX = COL_RW_A + LANE
HEAD_GROUP = 4
GROUP_W = HEAD_GROUP * RWKV_HD
SEQS_PER_STEP = 4
RWKV_ROWS_PER_STEP = 256

VMEM_LIMIT = 56 * 1024 * 1024
PROJ_SUB_ROWS = 128
OUT_PROJ_TM = 512
FFN_DOWN_TM = 256
assert GROUP_W == MXU_DIM


def _tile(n, pref):
    t = min(n, pref)
    while n % t:
        t -= 8
    return t


def _cparams(sem):
    return pltpu.CompilerParams(dimension_semantics=sem, vmem_limit_bytes=VMEM_LIMIT)


def _bdot(a, b):
    return jnp.dot(a.astype(BF16), b.astype(BF16), preferred_element_type=F32)


def _bdot_nt(a, b):
    return lax.dot_general(a.astype(BF16), b.astype(BF16), (((1,), (1,)), ((), ())),
                           preferred_element_type=F32)


def _bdot_tn(a, b):
    return lax.dot_general(a.astype(BF16), b.astype(BF16), (((0,), (0,)), ((), ())),
                           preferred_element_type=F32)


def _split_dot_left(ones_bf16, x):
    hi = x.astype(BF16)
    lo = (x - hi.astype(F32)).astype(BF16)
    return (jnp.dot(ones_bf16, hi, preferred_element_type=F32)
            + jnp.dot(ones_bf16, lo, preferred_element_type=F32))


def _sigmoid(x):
    return 0.5 * jnp.tanh(0.5 * x) + 0.5


def _rms(x, gain, eps):
    return x * lax.rsqrt(jnp.mean(x * x, axis=-1, keepdims=True) + eps) * gain


def _iota2(shape, dim):
    return lax.broadcasted_iota(jnp.int32, shape, dim)


def _tril_incl(n):
    return (_iota2((n, n), 0) >= _iota2((n, n), 1)).astype(BF16)


def _head_ones(n, width):
    return ((_iota2((n, n), 0) // width) == (_iota2((n, n), 1) // width)).astype(BF16)


def _expand(x, col_group):
    rows = HEAD_GROUP * CHUNK
    xt = jnp.concatenate([x] * HEAD_GROUP, axis=0)
    keep = (_iota2((rows, x.shape[1]), 0) // CHUNK) == (_iota2((rows, x.shape[1]), 1) // col_group)
    return jnp.where(keep, xt, 0.0)


def _inproj_kernel(*refs, fuse_norm):
    if fuse_norm:
        x_ref, g_ref, w_ref, z_ref, h_ref = refs

        @pl.when(pl.program_id(1) == 0)
        def _():
            h_ref[...] = _rms(x_ref[...], g_ref[...], RMS_EPS).astype(BF16)
    else:
        h_ref, w_ref, z_ref = refs
    z_ref[...] = jnp.dot(h_ref[...], w_ref[...], preferred_element_type=F32).astype(BF16)


def _inproj(h, w, l, gain=None):
    t, n = h.shape[0], w.shape[2]
    fuse_norm = gain is not None
    tm, tn = _tile(t, 1024), _tile(n, 1664)
    row_spec = pl.BlockSpec((tm, D_MODEL), lambda i, j: (i, 0))
    z_spec = pl.BlockSpec((tm, tn), lambda i, j: (i, j))
    w_spec = pl.BlockSpec((None, D_MODEL, tn), lambda i, j: (l, 0, j))
    z_shape = jax.ShapeDtypeStruct((t, n), BF16)
    if fuse_norm:
        ins = (h, gain, w)
        in_specs = [row_spec, pl.BlockSpec((1, D_MODEL), lambda i, j: (0, 0)), w_spec]
        out_shape, out_specs = (z_shape, jax.ShapeDtypeStruct((t, D_MODEL), BF16)), (z_spec, row_spec)
    else:
        ins, in_specs, out_shape, out_specs = (h, w), [row_spec, w_spec], z_shape, z_spec
    return pl.pallas_call(
        functools.partial(_inproj_kernel, fuse_norm=fuse_norm),
        out_shape=out_shape,
        grid=(t // tm, n // tn),
        in_specs=in_specs,
        out_specs=out_specs,
        compiler_params=_cparams(("parallel", "arbitrary")),
        name="inproj",
    )(*ins)


def _merge_kernel(h_ref, wg0, wg1, wg2, wg3, o0, o1, o2, o3, p_ref, out_ref):
    h = h_ref[...]
    acc = None
    for n, (wg, o) in enumerate(((wg0, o0), (wg1, o1), (wg2, o2), (wg3, o3))):
        gate = _sigmoid(jnp.dot(h, wg[...], preferred_element_type=F32))
        term = gate * jnp.dot(o[...], p_ref[n], preferred_element_type=F32)
        acc = term if acc is None else acc + term
    out_ref[...] = acc.astype(BF16)


def _merge(h, w_gate, outs, w_branch, l):
    t = h.shape[0]
    tm, tn = _tile(t, 1024), 512
    nj = D_MODEL // tn
    gate_specs = [pl.BlockSpec((None, D_MODEL, tn), functools.partial(lambda i, j, n: (l, 0, n * nj + j), n=n))
                  for n in range(N_BRANCH)]
    o_specs = [pl.BlockSpec((tm, BRANCH), lambda i, j: (i, 0)) for _ in range(N_BRANCH)]
    return pl.pallas_call(
        _merge_kernel,
        out_shape=jax.ShapeDtypeStruct((t, D_MODEL), BF16),
        grid=(t // tm, nj),
        in_specs=[pl.BlockSpec((tm, D_MODEL), lambda i, j: (i, 0))] + gate_specs + o_specs
                 + [pl.BlockSpec((None, N_BRANCH, BRANCH, tn), lambda i, j: (l, 0, 0, j))],
        out_specs=pl.BlockSpec((tm, tn), lambda i, j: (i, j)),
        compiler_params=_cparams(("parallel", "arbitrary")),
        name="merge",
    )(h, w_gate, w_gate, w_gate, w_gate, *outs, w_branch)


def _proj_norm_res_kernel(a_ref, w_ref, x_ref, g_ref, gn_ref, o_ref, *h_ref, nk):
    k = pl.program_id(1)

    def accumulate(first):
        part = jnp.dot(a_ref[...], w_ref[...], preferred_element_type=F32)
        o_ref[...] = part if first else o_ref[...] + part

    def finish():
        sub = PROJ_SUB_ROWS
        for r in range(o_ref.shape[0] // sub):
            rows = slice(r * sub, (r + 1) * sub)
            y = jnp.dot(a_ref[rows, :], w_ref[...], preferred_element_type=F32)
            if nk > 1:
                y = y + o_ref[rows, :]
            xn = x_ref[rows, :] + _rms(y, g_ref[...], RMS_EPS)
            o_ref[rows, :] = xn
            if h_ref:
                h_ref[0][rows, :] = _rms(xn, gn_ref[...], RMS_EPS).astype(BF16)

    if nk == 1:
        finish()
    else:
        pl.when(k == 0)(functools.partial(accumulate, True))
        if nk > 2:
            pl.when((k > 0) & (k < nk - 1))(functools.partial(accumulate, False))
        pl.when(k == nk - 1)(finish)


def _proj_norm_res(a, w, l, x2, gain, next_gain, tm, tk):
    t, kdim = a.shape
    emit_h = next_gain is not None
    row_spec = pl.BlockSpec((tm, D_MODEL), lambda i, k: (i, 0))
    vec_spec = pl.BlockSpec((1, D_MODEL), lambda i, k: (0, 0))
    out_shape = [jax.ShapeDtypeStruct((t, D_MODEL), F32)]
    if emit_h:
        out_shape.append(jax.ShapeDtypeStruct((t, D_MODEL), BF16))
    assert kdim % tk == 0 and t % tm == 0 and tm % PROJ_SUB_ROWS == 0
    w_mode = dict(pipeline_mode=pl.Buffered(1)) if tk == kdim else {}
    res = pl.pallas_call(
        functools.partial(_proj_norm_res_kernel, nk=kdim // tk),
        out_shape=tuple(out_shape),
        grid=(t // tm, kdim // tk),
        in_specs=[pl.BlockSpec((tm, tk), lambda i, k: (i, k)),
                  pl.BlockSpec((None, tk, D_MODEL), lambda i, k: (l, k, 0), **w_mode),
                  row_spec, vec_spec, vec_spec],
        out_specs=tuple([row_spec] * len(out_shape)),
        compiler_params=_cparams(("parallel", "arbitrary")),
        name="proj_norm_res",
    )(a, w, x2, gain, next_gain if emit_h else gain)
    return (res[0], res[1]) if emit_h else (res[0], None)


def _ffn_up_kernel(h_ref, wg_ref, wu_ref, a_ref, wg_b, wu_b):
    @pl.when(pl.program_id(1) == 0)
    def _():
        wg_b[...] = wg_ref[...].astype(BF16)
        wu_b[...] = wu_ref[...].astype(BF16)

    h = h_ref[...]
    gt = jnp.dot(h, wg_b[...], preferred_element_type=F32)
    up = jnp.dot(h, wu_b[...], preferred_element_type=F32)
    a_ref[...] = (gt * _sigmoid(gt) * up).astype(BF16)


def _ffn_up(h, wg, wu, l):
    t = h.shape[0]
    tm, tn = _tile(t, 1024), 512
    return pl.pallas_call(
        _ffn_up_kernel,
        out_shape=jax.ShapeDtypeStruct((t, D_FF), BF16),
        grid=(D_FF // tn, t // tm),
        in_specs=[pl.BlockSpec((tm, D_MODEL), lambda j, i: (i, 0)),
                  pl.BlockSpec((None, D_MODEL, tn), lambda j, i: (l, 0, j)),
                  pl.BlockSpec((None, D_MODEL, tn), lambda j, i: (l, 0, j))],
        out_specs=pl.BlockSpec((tm, tn), lambda j, i: (i, j)),
        scratch_shapes=[pltpu.VMEM((D_MODEL, tn), BF16), pltpu.VMEM((D_MODEL, tn), BF16)],
        compiler_params=_cparams(("parallel", "arbitrary")),
        name="ffn_up",
    )(h, wg, wu)


def _conv_kernel(b_ref, c_ref, u_ref, w_ref, o_ref, carry):
    @pl.when(pl.program_id(1) == 0)
    def _():
        carry[...] = jnp.zeros_like(carry)

    cu = c_ref[0].astype(F32) * u_ref[0].astype(F32)
    ts = cu.shape[0]
    row = _iota2(cu.shape, 0)
    p1 = carry[7:8, :]
    p2 = carry[6:7, :]
    s1 = jnp.where(row == 0, p1, pltpu.roll(cu, 1, axis=0))
    s2 = jnp.where(row == 0, p2, jnp.where(row == 1, p1, pltpu.roll(cu, 2, axis=0)))
    w = w_ref[...]
    y = w[2:3, :] * cu + w[1:2, :] * s1 + w[0:1, :] * s2
    o_ref[0] = (b_ref[0].astype(F32) * y).astype(BF16)
    carry[...] = cu[ts - 8:, :]


def _conv_mixer(z3, conv_w):
    b, s = z3.shape[0], z3.shape[1]
    ts = _tile(s, 512)
    c0 = COL_CONV // BRANCH
    return pl.pallas_call(
        _conv_kernel,
        out_shape=jax.ShapeDtypeStruct((b, s, BRANCH), BF16),
        grid=(b, s // ts),
        in_specs=[pl.BlockSpec((1, ts, BRANCH), lambda i, j: (i, j, c0)),
                  pl.BlockSpec((1, ts, BRANCH), lambda i, j: (i, j, c0 + 1)),
                  pl.BlockSpec((1, ts, BRANCH), lambda i, j: (i, j, c0 + 2)),
                  pl.BlockSpec((3, BRANCH), lambda i, j: (0, 0))],
        out_specs=pl.BlockSpec((1, ts, BRANCH), lambda i, j: (i, j, 0)),
        scratch_shapes=[pltpu.VMEM((8, BRANCH), F32)],
        compiler_params=_cparams(("parallel", "arbitrary")),
        name="conv",
    )(z3, z3, z3, conv_w)


def _gla_kernel(q_ref, k_ref, v_ref, g_ref, wl_ref, wa2_ref, ba_ref, gn_ref, o_ref, state):
    @pl.when(pl.program_id(1) == 0)
    def _():
        state[...] = jnp.zeros_like(state)

    nb = q_ref.shape[0]
    causal = _iota2((CHUNK, GLA_KEY), 0) >= (_iota2((CHUNK, GLA_KEY), 1) % CHUNK)
    same_head = ((_iota2((GLA_HEADS * GLA_DV, GLA_KEY), 0) // GLA_DV)
                 == (_iota2((GLA_HEADS * GLA_DV, GLA_KEY), 1) // GLA_DK))
    tril = _tril_incl(CHUNK)
    q_in, k_in, q_st, k_st, v, ve, dec = [], [], [], [], [], [], []
    for b in range(nb):
        q = q_ref[b].astype(F32) * (GLA_DK ** -0.5)
        k = k_ref[b].astype(F32)
        z = _bdot(wl_ref[b], wa2_ref[...]) + ba_ref[...]
        log_a = (jnp.minimum(z, 0.0) - jnp.log(1.0 + jnp.exp(-jnp.abs(z)))) * (1.0 / GLA_LOGIT_NORM)
        cum = _split_dot_left(tril, log_a)
        mid = cum[CHUNK // 2 - 1:CHUNK // 2, :]
        last = cum[CHUNK - 1:CHUNK, :]
        q_in.append(q * jnp.exp(cum - mid))
        k_in.append(_expand(k * jnp.exp(mid - cum), GLA_DK).astype(BF16))
        q_st.append(q * jnp.exp(cum))
        k_st.append(k * jnp.exp(last - cum))
        v.append(v_ref[b])
        ve.append(_expand(v[b].astype(F32), GLA_DV).astype(BF16))
        dec.append(jnp.exp(last))
    scores = [jnp.where(causal, _bdot_nt(q_in[b], k_in[b]), 0.0) for b in range(nb)]
    st = [state[b] for b in range(nb)]
    o_l = [_bdot(scores[b], ve[b]) + _bdot_nt(q_st[b], st[b]) for b in range(nb)]
    for b in range(nb):
        state[b] = st[b] * dec[b] + jnp.where(same_head, _bdot_tn(v[b], k_st[b]), 0.0)
    gn = gn_ref[...]
    for b in range(nb):
        o = o_l[b]
        g = g_ref[b].astype(F32)
        gate = g * _sigmoid(g)
        outs = [_rms(o[:, h * GLA_DV:(h + 1) * GLA_DV], gn, HEAD_EPS) for h in range(GLA_HEADS)]
        o_ref[b] = (jnp.concatenate(outs, axis=1) * gate).astype(BF16)


def _gla_mixer(z3, wa2p, ba, gn):
    b, s = z3.shape[0], z3.shape[1]
    cq = COL_GLA // GLA_KEY
    cv = (COL_GLA + 2 * GLA_KEY) // BRANCH
    cw = COL_RW_A // LANE
    nb = SEQS_PER_STEP if b % SEQS_PER_STEP == 0 else 1
    return pl.pallas_call(
        _gla_kernel,
        out_shape=jax.ShapeDtypeStruct((b, s, BRANCH), BF16),
        grid=(b // nb, s // CHUNK),
        in_specs=[pl.BlockSpec((nb, CHUNK, GLA_KEY), lambda i, j: (i, j, cq)),
                  pl.BlockSpec((nb, CHUNK, GLA_KEY), lambda i, j: (i, j, cq + 1)),
                  pl.BlockSpec((nb, CHUNK, BRANCH), lambda i, j: (i, j, cv)),
                  pl.BlockSpec((nb, CHUNK, BRANCH), lambda i, j: (i, j, cv + 1)),
                  pl.BlockSpec((nb, CHUNK, LANE), lambda i, j: (i, j, cw)),
                  pl.BlockSpec((LANE, GLA_KEY), lambda i, j: (0, 0)),
                  pl.BlockSpec((1, GLA_KEY), lambda i, j: (0, 0)),
                  pl.BlockSpec((1, GLA_DV), lambda i, j: (0, 0))],
        out_specs=pl.BlockSpec((nb, CHUNK, BRANCH), lambda i, j: (i, j, 0)),
        scratch_shapes=[pltpu.VMEM((nb, GLA_HEADS * GLA_DV, GLA_KEY), F32)],
        compiler_params=_cparams(("parallel", "arbitrary")),
        name="gla",
    )(z3, z3, z3, z3, z3, wa2p, ba, gn)


DIFF_KEY_TILE = 512
DIFF_QUERY_TILE = 512
VT_ROWS = 2 * DIFF_D + 16


def _alibi_slope(h):
    return 2.0 ** (-8.0 * (h + 1) / DIFF_HEADS)


def _diff_kernel(q_ref, k_ref, v_ref, lq1, lk1, lq2, lk2, gn_ref, o_ref, ka_scr, vb_scr, qa_scr, acc_scr,
                 m_scr, s_scr, rel_scr, *, lambda_init, tq, tk):
    qi = pl.program_id(1)
    hw = 2 * DIFF_D
    seq = k_ref.shape[1]
    lane = _iota2((tq, hw), 1)

    @pl.when(qi == 0)
    def _():
        key_row = _iota2((tq, hw), 0).astype(F32)
        for h in range(DIFF_HEADS):
            k_aug = []
            for part in range(tk // tq):
                key_bias = (_alibi_slope(h) * LOG2E) * (key_row + float(part * tq))
                hi = key_bias.astype(BF16).astype(F32)
                mid = (key_bias - hi).astype(BF16).astype(F32)
                lo = key_bias - hi - mid
                aug = jnp.where(lane == 0, hi, jnp.where(lane == 1, mid, jnp.where(lane == 2, lo, 0.0)))
                k_aug.append(aug.astype(BF16))
            for c in range(seq // tq):
                rows = slice(c * tq, (c + 1) * tq)
                ka_scr[h, rows, 0:hw] = k_ref[0, rows, h * hw:(h + 1) * hw].astype(BF16)
                ka_scr[h, rows, hw:2 * hw] = k_aug[c % (tk // tq)]
        k_row = _iota2((tk, 2 * tq), 0)
        col = _iota2((tk, 2 * tq), 1)
        for par in range(tk // tq):
            q_row = jnp.where(col >= tq, col - tq, col) + par * tq
            d = q_row - k_row
            rel_scr[par] = jnp.where((k_row // CHUNK) <= (q_row // CHUNK), (d - jnp.abs(d)).astype(F32), NEG_INF)
        sub = _iota2((VT_ROWS - hw, tq), 0)
        tail = jnp.where(sub == 0, 1.0, 0.0).astype(BF16)
        for c in range(seq // tq):
            rows = slice(c * tq, (c + 1) * tq)
            for h in range(DIFF_HEADS):
                vb_scr[h, 0:hw, rows] = v_ref[0, rows, h * hw:(h + 1) * hw].astype(F32).T.astype(BF16)
                vb_scr[h, hw:VT_ROWS, rows] = tail

    lane2 = _iota2((2 * tq, hw), 1)
    ones_aug = (lane2 < 3).astype(F32)
    for h in range(DIFF_HEADS):
        q = q_ref[0, :, h * hw:(h + 1) * hw].astype(F32) * (DIFF_D ** -0.5 * LOG2E)
        q2 = jnp.concatenate([jnp.where(lane < DIFF_D, q, 0.0), jnp.where(lane < DIFF_D, 0.0, q)],
                             axis=0)
        qa_scr[h] = jnp.concatenate([q2, ones_aug], axis=1).astype(BF16)
    acc_scr[...] = jnp.zeros_like(acc_scr)
    m_scr[...] = jnp.full_like(m_scr, NEG_INF)

    last_tile = qi // (tk // tq)

    def tile_scores(j):
        start = pl.multiple_of(j * tk, tk)
        return [lax.dot_general(ka_scr[h, pl.ds(start, tk), :], qa_scr[h], (((1,), (1,)), ((), ())),
                                preferred_element_type=F32) for h in range(DIFF_HEADS)]

    def update(scores, j):
        start = pl.multiple_of(j * tk, tk)
        alphas, probs = [], []
        for h, s in enumerate(scores):
            m_old = m_scr[h]
            m_new = jnp.maximum(m_old, jnp.max(s, axis=0, keepdims=True))
            alphas.append(jnp.exp2(m_old - m_new))
            probs.append(jnp.exp2((s - m_new).astype(BF16)))
            m_scr[h] = m_new - _alibi_slope(h) * LOG2E * tk
        pvs = [jnp.dot(vb_scr[h, :, pl.ds(start, tk)], probs[h], preferred_element_type=F32)
               for h in range(DIFF_HEADS)]
        for h in range(DIFF_HEADS):
            acc_scr[h] = alphas[h] * acc_scr[h] + pvs[h]

    for h, s in enumerate(tile_scores(0)):
        s_scr[h] = s

    def body(j, carry):
        nxt = tile_scores(j + 1)
        update([s_scr[h] for h in range(DIFF_HEADS)], j)
        for h in range(DIFF_HEADS):
            s_scr[h] = nxt[h]
        return carry

    lax.fori_loop(0, last_tile, body, 0)

    rel = rel_scr[qi % (tk // tq)]
    lam = (jnp.exp(jnp.sum(lq1[...] * lk1[...], axis=-1, keepdims=True))
           - jnp.exp(jnp.sum(lq2[...] * lk2[...], axis=-1, keepdims=True))
           + lambda_init)
    update([s_scr[h] + (_alibi_slope(h) * LOG2E) * rel for h in range(DIFF_HEADS)], last_tile)
    for h in range(DIFF_HEADS):
        on = acc_scr[h, 0:hw, :] / acc_scr[h, hw:hw + 1, :]
        o = (on[:, :tq] - lam * on[:, tq:]).T
        o_ref[0, :, h * hw:(h + 1) * hw] = (_rms(o, gn_ref[...], HEAD_EPS) * (1.0 - lambda_init)).astype(BF16)


def _diff_mixer(z3, lq1, lk1, lq2, lk2, gn, lambda_init):
    b, s = z3.shape[0], z3.shape[1]
    tk = _tile(s, DIFF_KEY_TILE)
    tq = _tile(tk, DIFF_QUERY_TILE)
    assert s % tk == 0 and tk % tq == 0 and tq % CHUNK == 0
    hw = 2 * DIFF_D
    cq = COL_DIFF // BRANCH
    nh = DIFF_HEADS
    vec = pl.BlockSpec((1, DIFF_D), lambda i, j: (0, 0))
    return pl.pallas_call(
        functools.partial(_diff_kernel, lambda_init=lambda_init, tq=tq, tk=tk),
        out_shape=jax.ShapeDtypeStruct((b, s, BRANCH), BF16),
        grid=(b, s // tq),
        in_specs=[pl.BlockSpec((1, tq, BRANCH), lambda i, j: (i, j, cq)),
                  pl.BlockSpec((1, s, BRANCH), lambda i, j: (i, 0, cq + 1), pipeline_mode=pl.Buffered(1)),
                  pl.BlockSpec((1, s, BRANCH), lambda i, j: (i, 0, cq + 2), pipeline_mode=pl.Buffered(1)),
                  vec, vec, vec, vec,
                  pl.BlockSpec((1, hw), lambda i, j: (0, 0))],
        out_specs=pl.BlockSpec((1, tq, BRANCH), lambda i, j: (i, j, 0)),
        scratch_shapes=[pltpu.VMEM((nh, s, 2 * hw), BF16), pltpu.VMEM((nh, VT_ROWS, s), BF16),
                        pltpu.VMEM((nh, 2 * tq, 2 * hw), BF16), pltpu.VMEM((nh, VT_ROWS, 2 * tq), F32),
                        pltpu.VMEM((nh, 1, 2 * tq), F32), pltpu.VMEM((nh, tk, 2 * tq), F32),
                        pltpu.VMEM((tk // tq, tk, 2 * tq), F32)],
        compiler_params=_cparams(("parallel", "arbitrary")),
        name="diff_attn",
    )(z3, z3, z3, lq1, lk1, lq2, lk2, gn)


def _shift(x, prev_rows):
    row = _iota2(x.shape, 0)
    return jnp.where(row == 0, prev_rows[7:8, :], pltpu.roll(x, 1, axis=0))


def _rwkv_kernel(*refs, has_vres, nb):
    scratch = refs[-5:]

    @pl.when(pl.program_id(1) == 0)
    def _():
        for r in scratch:
            r[...] = jnp.zeros_like(r)

    def chunk(c, carry):
        _rwkv_chunk(pl.multiple_of(c * CHUNK, CHUNK), *refs, has_vres=has_vres, nb=nb)
        return carry

    lax.fori_loop(0, refs[0].shape[1] // CHUNK, chunk, 0)


def _rwkv_chunk(off, *refs, has_vres, nb):
    if has_vres:
        (rkv_ref, gl_ref, wb_ref, ab_ref, vf_ref, mu_rkv, mu_g, mu_w, mu_a, w0, w2, a0, a2, g2, kk_s, ka_s,
         rk_s, lnw, lnb, v0, v1, v2, o_ref, p_rkv, p_g, p_w, p_a, state) = refs
    else:
        (rkv_ref, gl_ref, wb_ref, ab_ref, mu_rkv, mu_g, mu_w, mu_a, w0, w2, a0, a2, g2, kk_s, ka_s,
         rk_s, lnw, lnb, o_ref, vf_out, p_rkv, p_g, p_w, p_a, state) = refs
    chunk_rows = pl.ds(off, CHUNK)

    def stack(parts):
        return parts[0] if nb == 1 else jnp.concatenate(parts, axis=0)

    def mixed(x_ref, p_ref, mu_ref):
        parts = []
        for b in range(nb):
            x = x_ref[b, chunk_rows, :].astype(F32)
            parts.append(x + (_shift(x, p_ref[b]) - x) * mu_ref[...])
            p_ref[b] = x[CHUNK - 8:, :]
        return stack(parts)

    rkv = mixed(rkv_ref, p_rkv, mu_rkv)
    g_lr = mixed(gl_ref, p_g, mu_g)
    w_lr = mixed(wb_ref, p_w, mu_w)
    a_lr = mixed(ab_ref, p_a, mu_a)
    r = rkv[:, 0:BRANCH]
    k = rkv[:, BRANCH:2 * BRANCH]
    v = rkv[:, 2 * BRANCH:3 * BRANCH]
    nrow = nb * CHUNK

    y = w0[...] + _bdot(jnp.tanh(w_lr), w2[...])
    lw = -math.exp(-0.5) * _sigmoid(y)
    a = _sigmoid(a0[...] + _bdot(a_lr, a2[...]))
    gate = _bdot(_sigmoid(g_lr), g2[...])
    if has_vres:
        vf = stack([vf_ref[b, chunk_rows, :] for b in range(nb)])
        v = v + (vf - v) * _sigmoid(v0[...] + _bdot(_bdot(v, v1[...]), v2[...]))
    else:
        for b in range(nb):
            vf_out[b, chunk_rows, :] = v[b * CHUNK:(b + 1) * CHUNK]

    ones = _head_ones(BRANCH, RWKV_HD)
    kk = k * kk_s[...]
    k = k * (1.0 + (a - 1.0) * ka_s[...])
    sums = _bdot(jnp.concatenate([kk * kk, r * k * rk_s[...]], axis=0), ones)
    kk = kk * lax.rsqrt(jnp.maximum(sums[:nrow], 1e-24))
    bonus = sums[nrow:] * v

    ri, ci = _iota2((nrow, nrow), 0), _iota2((nrow, nrow), 1)
    tril = ((ri // CHUNK == ci // CHUNK) & (ri >= ci)).astype(BF16)
    cum = _split_dot_left(tril, lw)
    lasts = [cum[(b + 1) * CHUNK - 1:(b + 1) * CHUNK, :] for b in range(nb)]
    decay_all = [jnp.exp(last) for last in lasts]
    a_t = -kk * jnp.exp(cum - lw)
    r_t = r * jnp.exp(cum)
    inv = jnp.exp(-cum)
    to_end = stack([decay_all[b] * inv[b * CHUNK:(b + 1) * CHUNK] for b in range(nb)])
    kka = kk * a
    b_s = kka * inv
    k_s = k * inv
    b_e = kka * to_end
    k_e = k * to_end

    rows = HEAD_GROUP * CHUNK
    blk = ((_iota2((rows, GROUP_W), 0) // CHUNK) == (_iota2((rows, GROUP_W), 1) // RWKV_HD))
    blk_bf = blk.astype(F32).astype(BF16)
    t_idx = _iota2((CHUNK, GROUP_W), 0)
    s_idx = _iota2((CHUNK, GROUP_W), 1) % CHUNK
    strict = t_idx > s_idx
    incl = t_idx >= s_idx
    eye = (t_idx == s_idx).astype(F32)

    def blocks(x):
        return jnp.concatenate([x.astype(BF16)] * HEAD_GROUP, axis=0) * blk_bf

    ngroup = RWKV_HEADS // HEAD_GROUP
    probs = [(b, gi) for b in range(nb) for gi in range(ngroup)]
    nprob = len(probs)

    def part(t):
        return [t[b * CHUNK:(b + 1) * CHUNK, gi * GROUP_W:(gi + 1) * GROUP_W] for b, gi in probs]

    cat0 = lambda x, y_: jnp.concatenate([x, y_], axis=0)
    a_l, r_l, b_l, k_l, be_l, ke_l, v_l = (part(t) for t in (a_t, r_t, b_s, k_s, b_e, k_e, v))
    pm = [_bdot_nt(cat0(a_l[i], r_l[i]), cat0(blocks(b_l[i]), blocks(k_l[i]))) for i in range(nprob)]
    n_ab = [jnp.where(strict, p[:CHUNK, :GROUP_W], 0.0) for p in pm]
    a_ak = [jnp.where(strict, p[:CHUNK, GROUP_W:], 0.0) for p in pm]
    a_rb = [jnp.where(incl, p[CHUNK:, :GROUP_W], 0.0) for p in pm]
    a_rk = [jnp.where(incl, p[CHUNK:, GROUP_W:], 0.0) for p in pm]
    t_inv = [eye + n for n in n_ab]
    n_pow = n_ab
    n_blk = [blocks(n) for n in n_pow]
    for _ in range(int(math.log2(CHUNK)) - 1):
        n_pow = [_bdot(n_pow[i], n_blk[i]) for i in range(nprob)]
        n_blk = [blocks(n) for n in n_pow]
        t_inv = [t_inv[i] + _bdot(t_inv[i], n_blk[i]) for i in range(nprob)]
    av = [_bdot(cat0(a_ak[i], a_rk[i]), blocks(v_l[i])) for i in range(nprob)]
    y2 = [_bdot(t_inv[i], jnp.concatenate([blocks(a_l[i]), blocks(av[i][:CHUNK])], axis=1))
          for i in range(nprob)]
    st = [state[b, gi] for b, gi in probs]
    x2 = [_bdot_nt(cat0(y2[i][:, :GROUP_W], r_l[i]), st[i]) for i in range(nprob)]
    u_l = [x2[i][:CHUNK] + y2[i][:, GROUP_W:] for i in range(nprob)]
    o_l = [x2[i][CHUNK:] + _bdot(a_rb[i], blocks(u_l[i])) + av[i][CHUNK:] for i in range(nprob)]
    for i, (b, gi) in enumerate(probs):
        outer = _bdot_tn(cat0(u_l[i], v_l[i]), cat0(be_l[i], ke_l[i]))
        state[b, gi] = (st[i] * decay_all[b][:, gi * GROUP_W:(gi + 1) * GROUP_W]
                        + jnp.where(blk, outer, 0.0))
    o = stack([jnp.concatenate([o_l[b * ngroup + gi] for gi in range(ngroup)], axis=1)
               for b in range(nb)])

    mean = _bdot(o, ones) * (1.0 / RWKV_HD)
    cen = o - mean
    var = _bdot(cen * cen, ones) * (1.0 / RWKV_HD)
    o = cen * lax.rsqrt(var + RWKV_GN_EPS) * lnw[...] + lnb[...]
    o = ((o + bonus) * gate).astype(BF16)
    for b in range(nb):
        o_ref[b, chunk_rows, :] = o[b * CHUNK:(b + 1) * CHUNK]


def _rwkv_mixer(z3, v_first, p):
    b, s = z3.shape[0], z3.shape[1]
    has_vres = v_first is not None
    nb = SEQS_PER_STEP if b % SEQS_PER_STEP == 0 else 1
    ts = _tile(s, RWKV_ROWS_PER_STEP)
    assert ts % CHUNK == 0
    row = lambda w, c: pl.BlockSpec((nb, ts, w), lambda i, j: (i, j, c))
    full = lambda a: pl.BlockSpec(a.shape, lambda i, j: (0,) * a.ndim)
    ins = [z3, z3, z3, z3]
    specs = [row(3 * BRANCH, COL_RWKV // (3 * BRANCH)), row(RWKV_GATE_LORA, COL_RW_G // RWKV_GATE_LORA),
             row(LANE, COL_RW_W // LANE), row(LANE, COL_RW_A // LANE)]
    if has_vres:
        ins.append(v_first)
        specs.append(row(BRANCH, 0))
    names = ["mu_rkv", "mu_g", "mu_w", "mu_a", "w0", "w2", "a0", "a2", "g2", "kk", "ka", "rk", "lnw", "lnb"]
    if has_vres:
        names += ["v0", "v1", "v2"]
    for nme in names:
        ins.append(p[nme])
        specs.append(full(p[nme]))
    o_spec = pl.BlockSpec((nb, ts, BRANCH), lambda i, j: (i, j, 0))
    o_shape = jax.ShapeDtypeStruct((b, s, BRANCH), BF16)
    if has_vres:
        out_shape, out_specs = o_shape, o_spec
    else:
        out_shape = (o_shape, jax.ShapeDtypeStruct((b, s, BRANCH), F32))
        out_specs = (o_spec, pl.BlockSpec((nb, ts, BRANCH), lambda i, j: (i, j, 0)))
    res = pl.pallas_call(
        functools.partial(_rwkv_kernel, has_vres=has_vres, nb=nb),
        out_shape=out_shape,
        grid=(b // nb, s // ts),
        in_specs=specs,
        out_specs=out_specs,
        scratch_shapes=[pltpu.VMEM((nb, 8, 3 * BRANCH), F32), pltpu.VMEM((nb, 8, RWKV_GATE_LORA), F32),
                        pltpu.VMEM((nb, 8, LANE), F32), pltpu.VMEM((nb, 8, LANE), F32),
                        pltpu.VMEM((nb, RWKV_HEADS // HEAD_GROUP, GROUP_W, GROUP_W), F32)],
        compiler_params=_cparams(("parallel", "arbitrary")),
        name="rwkv7",
    )(*ins)
    if has_vres:
        return res, v_first
    return res[0], res[1]


def _pad_rows(w, rows, offset=0):
    return jnp.zeros((rows, w.shape[1]), F32).at[offset:offset + w.shape[0]].set(w)


N_IN = 14800
COL_GATES_SRC = 6608
MIX_BLOCKS = ((COL_CONV, ((0, 3072),)), (COL_DIFF, ((3088, 1536),)), (COL_RWKV, ((4624, 1536),)),
              (COL_RW_G, ((6352, 256),)),
              (COL_RW_W, ((6160, RWKV_DECAY_LORA), (None, LANE - RWKV_DECAY_LORA))),
              (COL_RW_A, ((6256, RWKV_AAA_LORA), (3072, GLA_LOW_RANK), (None, LANE - GLA_WLR_LANE - GLA_LOW_RANK))))
PACK_CHUNK = 512


def _pack_w_in_kernel(wt_ref, mix_ref, gate_ref):
    tc = wt_ref.shape[1]

    def rows(src, n):
        return jnp.zeros((n, tc), F32) if src is None else wt_ref[src:src + n, :]

    def put(out_ref, dst, block):
        out_ref[:, dst:dst + block.shape[0]] = block.T.astype(BF16)

    for dst, parts in MIX_BLOCKS:
        if len(parts) == 1:
            src, n = parts[0]
            for c in range(0, n, PACK_CHUNK):
                m = min(PACK_CHUNK, n - c)
                put(mix_ref, dst + c, rows(src + c, m))
        else:
            put(mix_ref, dst, jnp.concatenate([rows(src, n) for src, n in parts], axis=0))
    for c in range(0, N_IN - COL_GATES_SRC, PACK_CHUNK):
        put(gate_ref, c, rows(COL_GATES_SRC + c, PACK_CHUNK))


def _pack_w_in(w_in):
    depth = w_in.shape[0]
    tc = 256
    n_gate = N_IN - COL_GATES_SRC
    assert n_gate % PACK_CHUNK == 0
    return pl.pallas_call(
        _pack_w_in_kernel,
        out_shape=(jax.ShapeDtypeStruct((depth, D_MODEL, N_MIX), BF16),
                   jax.ShapeDtypeStruct((depth, D_MODEL, n_gate), BF16)),
        grid=(depth, D_MODEL // tc),
        in_specs=[pl.BlockSpec((None, N_IN, tc), lambda l, i: (l, 0, i))],
        out_specs=(pl.BlockSpec((None, tc, N_MIX), lambda l, i: (l, i, 0)),
                   pl.BlockSpec((None, tc, n_gate), lambda l, i: (l, i, 0))),
        compiler_params=_cparams(("parallel", "parallel")),
        name="pack_w_in",
    )(jnp.swapaxes(w_in, 1, 2))


def _pad_lanes(v, width, offset=0):
    return jnp.zeros((1, width), F32).at[0, offset:offset + v.shape[0]].set(v)


def kernel(x, norm_mix_pre, w_in, conv_w, gla_wa2, gla_ba, gla_norm, diff_lq1, diff_lk1, diff_lq2, diff_lk2,
           diff_norm, rw_mu, rw_w0, rw_w2, rw_a0, rw_a2, rw_g2, rw_kk, rw_ka, rw_rk, rw_lnw, rw_lnb, rw_v0,
           rw_v1, rw_v2, w_branch, w_out, norm_mix_post, norm_ffn_pre, w_gate, w_up, w_down, norm_ffn_post):
    bsz, seq = x.shape[0], x.shape[1]
    depth = w_in.shape[0]
    w_branch_b, w_out_b, w_down_b = (w.astype(BF16) for w in (w_branch, w_out, w_down))
    w_mix, w_gates = _pack_w_in(w_in)
    t = bsz * seq
    x2 = x.reshape(t, D_MODEL)
    row = lambda v: v.reshape(1, -1)
    v_first = None
    h = None
    for l in range(depth):
        if l == 0:
            z2, h = _inproj(x2, w_mix, l, gain=row(norm_mix_pre[l]))
        else:
            z2 = _inproj(h, w_mix, l)
        z3 = z2.reshape(bsz, seq, N_MIX)
        lambda_init = 0.8 - 0.6 * math.exp(-0.3 * l)

        o_conv = _conv_mixer(z3, conv_w[l])
        o_gla = _gla_mixer(z3, _pad_rows(gla_wa2[l], LANE, GLA_WLR_LANE), row(gla_ba[l]), row(gla_norm[l]))
        o_diff = _diff_mixer(z3, row(diff_lq1[l]), row(diff_lk1[l]), row(diff_lq2[l]),
                             row(diff_lk2[l]), row(diff_norm[l]), lambda_init)
        mu = rw_mu[l]
        rp = {
            "mu_rkv": row(mu[0:1536]),
            "mu_w": _pad_lanes(mu[1536:1632], LANE),
            "mu_a": _pad_lanes(mu[1632:1728], LANE),
            "mu_g": row(mu[1728:1984]),
            "w0": row(rw_w0[l]), "w2": _pad_rows(rw_w2[l], LANE),
            "a0": row(rw_a0[l]), "a2": _pad_rows(rw_a2[l], LANE),
            "g2": rw_g2[l], "kk": row(rw_kk[l]), "ka": row(rw_ka[l]), "rk": row(rw_rk[l]),
            "lnw": row(rw_lnw[l]), "lnb": row(rw_lnb[l]),
        }
        if l > 0:
            rp.update(v0=row(rw_v0[l - 1]), v1=rw_v1[l - 1], v2=rw_v2[l - 1])
        o_rwkv, v_first = _rwkv_mixer(z3, v_first if l > 0 else None, rp)

        outs = [o.reshape(t, BRANCH) for o in (o_conv, o_gla, o_diff, o_rwkv)]
        merged = _merge(h, w_gates, outs, w_branch_b, l)
        x2, h_ffn = _proj_norm_res(merged, w_out_b, l, x2, row(norm_mix_post[l]),
                                   row(norm_ffn_pre[l]), _tile(t, OUT_PROJ_TM), D_MODEL)
        act = _ffn_up(h_ffn, w_gate, w_up, l)
        next_gain = row(norm_mix_pre[l + 1]) if l + 1 < depth else None
        x2, h = _proj_norm_res(act, w_down_b, l, x2, row(norm_ffn_post[l]), next_gain,
                               _tile(t, FFN_DOWN_TM), D_FF)
    return x2.reshape(bsz, seq, D_MODEL)
```

```python
import functools
import math

import jax
import jax.numpy as jnp
from jax import lax
from jax.experimental import pallas as pl
from jax.experimental.pallas import tpu as pltpu

F32 = jnp.float32
BF16 = jnp.bfloat16

D_MODEL = 2048
CHUNK = 64
N_BRANCH = 4
BRANCH = 512
GLA_HEADS = 4
GLA_DK = 64
GLA_DV = 128
GLA_KEY = GLA_HEADS * GLA_DK
GLA_LOW_RANK = 16
GLA_LOGIT_NORM = 16.0
DIFF_HEADS = 4
DIFF_D = 64
RWKV_HEADS = 8
RWKV_HD = 64
RWKV_DECAY_LORA = 96
RWKV_AAA_LORA = 96
RWKV_MV_LORA = 64
RWKV_GATE_LORA = 256
D_FF = 5632
RMS_EPS = 1e-6
HEAD_EPS = 1e-5
RWKV_GN_EPS = 64e-5
NEG_INF = -1e30
LOG2E = 1.4426950408889634

COL_CONV = 0
COL_GLA = 1536
COL_DIFF = 3072
COL_RWKV = 4608
COL_RW_G = 6144
COL_RW_W = 6400
COL_RW_A = 6528
GLA_WLR_LANE = RWKV_AAA_LORA
LANE = 128
MXU_DIM = 256
N_MIX = COL_RW_A + LANE
HEAD_GROUP = 4
GROUP_W = HEAD_GROUP * RWKV_HD
SEQS_PER_STEP = 4
RWKV_ROWS_PER_STEP = 256

VMEM_LIMIT = 56 * 1024 * 1024
PROJ_SUB_ROWS = 128
OUT_PROJ_TM = 512
FFN_DOWN_TM = 256
assert GROUP_W == MXU_DIM


def _tile(n, pref):
    t = min(n, pref)
    while n % t:
        t -= 8
    return t


def _cparams(sem):
    return pltpu.CompilerParams(dimension_semantics=sem, vmem_limit_bytes=VMEM_LIMIT)


def _bdot(a, b):
    return jnp.dot(a.astype(BF16), b.astype(BF16), preferred_element_type=F32)


def _bdot_nt(a, b):
    return lax.dot_general(a.astype(BF16), b.astype(BF16), (((1,), (1,)), ((), ())),
                           preferred_element_type=F32)


def _bdot_tn(a, b):
    return lax.dot_general(a.astype(BF16), b.astype(BF16), (((0,), (0,)), ((), ())),
                           preferred_element_type=F32)


def _split_dot_left(ones_bf16, x):
    hi = x.astype(BF16)
    lo = (x - hi.astype(F32)).astype(BF16)
    return (jnp.dot(ones_bf16, hi, preferred_element_type=F32)
            + jnp.dot(ones_bf16, lo, preferred_element_type=F32))


def _sigmoid(x):
    return 0.5 * jnp.tanh(0.5 * x) + 0.5


def _rms(x, gain, eps):
    return x * lax.rsqrt(jnp.mean(x * x, axis=-1, keepdims=True) + eps) * gain


def _iota2(shape, dim):
    return lax.broadcasted_iota(jnp.int32, shape, dim)


def _tril_incl(n):
    return (_iota2((n, n), 0) >= _iota2((n, n), 1)).astype(BF16)


def _head_ones(n, width):
    return ((_iota2((n, n), 0) // width) == (_iota2((n, n), 1) // width)).astype(BF16)


def _expand(x, col_group):
    rows = HEAD_GROUP * CHUNK
    xt = jnp.concatenate([x] * HEAD_GROUP, axis=0)
    keep = (_iota2((rows, x.shape[1]), 0) // CHUNK) == (_iota2((rows, x.shape[1]), 1) // col_group)
    return jnp.where(keep, xt, 0.0)


def _inproj_kernel(*refs, fuse_norm):
    if fuse_norm:
        x_ref, g_ref, w_ref, z_ref, h_ref = refs

        @pl.when(pl.program_id(1) == 0)
        def _():
            h_ref[...] = _rms(x_ref[...], g_ref[...], RMS_EPS).astype(BF16)
    else:
        h_ref, w_ref, z_ref = refs
    h = h_ref[...]
    tn = z_ref.shape[1]
    mid = (tn // (2 * MXU_DIM)) * MXU_DIM
    for cols in ((slice(0, mid), slice(mid, tn)) if 0 < mid < tn else (slice(0, tn),)):
        z_ref[:, cols] = jnp.dot(h, w_ref[:, cols], preferred_element_type=F32).astype(BF16)


def _inproj(h, w, l, gain=None):
    t, n = h.shape[0], w.shape[2]
    fuse_norm = gain is not None
    tm = _tile(t, 1024)
    z_shape = jax.ShapeDtypeStruct((t, n), BF16)
    if fuse_norm:
        tn = _tile(n, 1664)
        grid = (t // tm, n // tn)
        row_spec = pl.BlockSpec((tm, D_MODEL), lambda i, j: (i, 0))
        z_spec = pl.BlockSpec((tm, tn), lambda i, j: (i, j))
        ins = (h, gain, w)
        in_specs = [row_spec, pl.BlockSpec((1, D_MODEL), lambda i, j: (0, 0)),
                    pl.BlockSpec((None, D_MODEL, tn), lambda i, j: (l, 0, j))]
        out_shape, out_specs = (z_shape, jax.ShapeDtypeStruct((t, D_MODEL), BF16)), (z_spec, row_spec)
    else:
        tn = _tile(n, 3328)
        grid = (n // tn, t // tm)
        ins = (h, w)
        in_specs = [pl.BlockSpec((tm, D_MODEL), lambda j, i: (i, 0)),
                    pl.BlockSpec((None, D_MODEL, tn), lambda j, i: (l, 0, j), pipeline_mode=pl.Buffered(1))]
        out_shape, out_specs = z_shape, pl.BlockSpec((tm, tn), lambda j, i: (i, j))
    return pl.pallas_call(
        functools.partial(_inproj_kernel, fuse_norm=fuse_norm),
        out_shape=out_shape,
        grid=grid,
        in_specs=in_specs,
        out_specs=out_specs,
        compiler_params=_cparams(("parallel", "arbitrary")),
        name="inproj",
    )(*ins)


def _merge_kernel(h_ref, wg0, wg1, wg2, wg3, o0, o1, o2, o3, p_ref, out_ref):
    h = h_ref[...]
    acc = None
    for n, (wg, o) in enumerate(((wg0, o0), (wg1, o1), (wg2, o2), (wg3, o3))):
        gate = _sigmoid(jnp.dot(h, wg[...], preferred_element_type=F32))
        term = gate * jnp.dot(o[...], p_ref[n], preferred_element_type=F32)
        acc = term if acc is None else acc + term
    out_ref[...] = acc.astype(BF16)


def _merge(h, w_gate, outs, w_branch, l):
    t = h.shape[0]
    tm, tn = _tile(t, 1024), 512
    nj = D_MODEL // tn
    gate_specs = [pl.BlockSpec((None, D_MODEL, tn), functools.partial(lambda i, j, n: (l, 0, n * nj + j), n=n))
                  for n in range(N_BRANCH)]
    o_specs = [pl.BlockSpec((tm, BRANCH), lambda i, j: (i, 0)) for _ in range(N_BRANCH)]
    return pl.pallas_call(
        _merge_kernel,
        out_shape=jax.ShapeDtypeStruct((t, D_MODEL), BF16),
        grid=(t // tm, nj),
        in_specs=[pl.BlockSpec((tm, D_MODEL), lambda i, j: (i, 0))] + gate_specs + o_specs
                 + [pl.BlockSpec((None, N_BRANCH, BRANCH, tn), lambda i, j: (l, 0, 0, j))],
        out_specs=pl.BlockSpec((tm, tn), lambda i, j: (i, j)),
        compiler_params=_cparams(("parallel", "arbitrary")),
        name="merge",
    )(h, w_gate, w_gate, w_gate, w_gate, *outs, w_branch)


def _proj_norm_res_kernel(a_ref, w_ref, x_ref, g_ref, gn_ref, o_ref, *h_ref, nk):
    k = pl.program_id(1)

    def accumulate(first):
        part = jnp.dot(a_ref[...], w_ref[...], preferred_element_type=F32)
        o_ref[...] = part if first else o_ref[...] + part

    def finish():
        sub = PROJ_SUB_ROWS
        for r in range(o_ref.shape[0] // sub):
            rows = slice(r * sub, (r + 1) * sub)
            y = jnp.dot(a_ref[rows, :], w_ref[...], preferred_element_type=F32)
            if nk > 1:
                y = y + o_ref[rows, :]
            xn = x_ref[rows, :] + _rms(y, g_ref[...], RMS_EPS)
            o_ref[rows, :] = xn
            if h_ref:
                h_ref[0][rows, :] = _rms(xn, gn_ref[...], RMS_EPS).astype(BF16)

    if nk == 1:
        finish()
    else:
        pl.when(k == 0)(functools.partial(accumulate, True))
        if nk > 2:
            pl.when((k > 0) & (k < nk - 1))(functools.partial(accumulate, False))
        pl.when(k == nk - 1)(finish)


def _proj_norm_res(a, w, l, x2, gain, next_gain, tm, tk):
    t, kdim = a.shape
    emit_h = next_gain is not None
    row_spec = pl.BlockSpec((tm, D_MODEL), lambda i, k: (i, 0))
    vec_spec = pl.BlockSpec((1, D_MODEL), lambda i, k: (0, 0))
    out_shape = [jax.ShapeDtypeStruct((t, D_MODEL), F32)]
    if emit_h:
        out_shape.append(jax.ShapeDtypeStruct((t, D_MODEL), BF16))
    assert kdim % tk == 0 and t % tm == 0 and tm % PROJ_SUB_ROWS == 0
    w_mode = dict(pipeline_mode=pl.Buffered(1)) if tk == kdim else {}
    res = pl.pallas_call(
        functools.partial(_proj_norm_res_kernel, nk=kdim // tk),
        out_shape=tuple(out_shape),
        grid=(t // tm, kdim // tk),
        in_specs=[pl.BlockSpec((tm, tk), lambda i, k: (i, k)),
                  pl.BlockSpec((None, tk, D_MODEL), lambda i, k: (l, k, 0), **w_mode),
                  row_spec, vec_spec, vec_spec],
        out_specs=tuple([row_spec] * len(out_shape)),
        compiler_params=_cparams(("parallel", "arbitrary")),
        name="proj_norm_res",
    )(a, w, x2, gain, next_gain if emit_h else gain)
    return (res[0], res[1]) if emit_h else (res[0], None)


def _ffn_up_kernel(h_ref, wg_ref, wu_ref, a_ref, wg_b, wu_b):
    @pl.when(pl.program_id(1) == 0)
    def _():
        wg_b[...] = wg_ref[...].astype(BF16)
        wu_b[...] = wu_ref[...].astype(BF16)

    tm = h_ref.shape[0]
    mid = (tm // 16) * 8
    for rows in (slice(0, mid), slice(mid, tm)):
        h = h_ref[rows, :]
        gt = jnp.dot(h, wg_b[...], preferred_element_type=F32)
        up = jnp.dot(h, wu_b[...], preferred_element_type=F32)
        a_ref[rows, :] = (gt * _sigmoid(gt) * up).astype(BF16)


def _ffn_up(h, wg, wu, l):
    t = h.shape[0]
    tm, tn = _tile(t, 1024), 512
    return pl.pallas_call(
        _ffn_up_kernel,
        out_shape=jax.ShapeDtypeStruct((t, D_FF), BF16),
        grid=(D_FF // tn, t // tm),
        in_specs=[pl.BlockSpec((tm, D_MODEL), lambda j, i: (i, 0)),
                  pl.BlockSpec((None, D_MODEL, tn), lambda j, i: (l, 0, j)),
                  pl.BlockSpec((None, D_MODEL, tn), lambda j, i: (l, 0, j))],
        out_specs=pl.BlockSpec((tm, tn), lambda j, i: (i, j)),
        scratch_shapes=[pltpu.VMEM((D_MODEL, tn), BF16), pltpu.VMEM((D_MODEL, tn), BF16)],
        compiler_params=_cparams(("parallel", "arbitrary")),
        name="ffn_up",
    )(h, wg, wu)


def _conv_kernel(b_ref, c_ref, u_ref, w_ref, o_ref, carry):
    @pl.when(pl.program_id(1) == 0)
    def _():
        carry[...] = jnp.zeros_like(carry)

    cu = c_ref[0].astype(F32) * u_ref[0].astype(F32)
    ts = cu.shape[0]
    row = _iota2(cu.shape, 0)
    p1 = carry[7:8, :]
    p2 = carry[6:7, :]
    s1 = jnp.where(row == 0, p1, pltpu.roll(cu, 1, axis=0))
    s2 = jnp.where(row == 0, p2, jnp.where(row == 1, p1, pltpu.roll(cu, 2, axis=0)))
    w = w_ref[...]
    y = w[2:3, :] * cu + w[1:2, :] * s1 + w[0:1, :] * s2
    o_ref[0] = (b_ref[0].astype(F32) * y).astype(BF16)
    carry[...] = cu[ts - 8:, :]


def _conv_mixer(z3, conv_w):
    b, s = z3.shape[0], z3.shape[1]
    ts = _tile(s, 512)
    c0 = COL_CONV // BRANCH
    return pl.pallas_call(
        _conv_kernel,
        out_shape=jax.ShapeDtypeStruct((b, s, BRANCH), BF16),
        grid=(b, s // ts),
        in_specs=[pl.BlockSpec((1, ts, BRANCH), lambda i, j: (i, j, c0)),
                  pl.BlockSpec((1, ts, BRANCH), lambda i, j: (i, j, c0 + 1)),
                  pl.BlockSpec((1, ts, BRANCH), lambda i, j: (i, j, c0 + 2)),
                  pl.BlockSpec((3, BRANCH), lambda i, j: (0, 0))],
        out_specs=pl.BlockSpec((1, ts, BRANCH), lambda i, j: (i, j, 0)),
        scratch_shapes=[pltpu.VMEM((8, BRANCH), F32)],
        compiler_params=_cparams(("parallel", "arbitrary")),
        name="conv",
    )(z3, z3, z3, conv_w)


def _gla_kernel(q_ref, k_ref, v_ref, g_ref, wl_ref, wa2_ref, ba_ref, gn_ref, o_ref, state):
    @pl.when(pl.program_id(1) == 0)
    def _():
        state[...] = jnp.zeros_like(state)

    nb = q_ref.shape[0]
    causal = _iota2((CHUNK, GLA_KEY), 0) >= (_iota2((CHUNK, GLA_KEY), 1) % CHUNK)
    same_head = ((_iota2((GLA_HEADS * GLA_DV, GLA_KEY), 0) // GLA_DV)
                 == (_iota2((GLA_HEADS * GLA_DV, GLA_KEY), 1) // GLA_DK))
    tril = _tril_incl(CHUNK)
    q_in, k_in, q_st, k_st, v, ve, dec = [], [], [], [], [], [], []
    for b in range(nb):
        q = q_ref[b].astype(F32) * (GLA_DK ** -0.5)
        k = k_ref[b].astype(F32)
        z = _bdot(wl_ref[b], wa2_ref[...]) + ba_ref[...]
        log_a = (jnp.minimum(z, 0.0) - jnp.log(1.0 + jnp.exp(-jnp.abs(z)))) * (1.0 / GLA_LOGIT_NORM)
        cum = _split_dot_left(tril, log_a)
        mid = cum[CHUNK // 2 - 1:CHUNK // 2, :]
        last = cum[CHUNK - 1:CHUNK, :]
        q_in.append(q * jnp.exp(cum - mid))
        k_in.append(_expand(k * jnp.exp(mid - cum), GLA_DK).astype(BF16))
        q_st.append(q * jnp.exp(cum))
        k_st.append(k * jnp.exp(last - cum))
        v.append(v_ref[b])
        ve.append(_expand(v[b].astype(F32), GLA_DV).astype(BF16))
        dec.append(jnp.exp(last))
    scores = [jnp.where(causal, _bdot_nt(q_in[b], k_in[b]), 0.0) for b in range(nb)]
    st = [state[b] for b in range(nb)]
    o_l = [_bdot(scores[b], ve[b]) + _bdot_nt(q_st[b], st[b]) for b in range(nb)]
    for b in range(nb):
        state[b] = st[b] * dec[b] + jnp.where(same_head, _bdot_tn(v[b], k_st[b]), 0.0)
    gn = gn_ref[...]
    for b in range(nb):
        o = o_l[b]
        g = g_ref[b].astype(F32)
        gate = g * _sigmoid(g)
        outs = [_rms(o[:, h * GLA_DV:(h + 1) * GLA_DV], gn, HEAD_EPS) for h in range(GLA_HEADS)]
        o_ref[b] = (jnp.concatenate(outs, axis=1) * gate).astype(BF16)


def _gla_mixer(z3, wa2p, ba, gn):
    b, s = z3.shape[0], z3.shape[1]
    cq = COL_GLA // GLA_KEY
    cv = (COL_GLA + 2 * GLA_KEY) // BRANCH
    cw = COL_RW_A // LANE
    nb = SEQS_PER_STEP if b % SEQS_PER_STEP == 0 else 1
    return pl.pallas_call(
        _gla_kernel,
        out_shape=jax.ShapeDtypeStruct((b, s, BRANCH), BF16),
        grid=(b // nb, s // CHUNK),
        in_specs=[pl.BlockSpec((nb, CHUNK, GLA_KEY), lambda i, j: (i, j, cq)),
                  pl.BlockSpec((nb, CHUNK, GLA_KEY), lambda i, j: (i, j, cq + 1)),
                  pl.BlockSpec((nb, CHUNK, BRANCH), lambda i, j: (i, j, cv)),
                  pl.BlockSpec((nb, CHUNK, BRANCH), lambda i, j: (i, j, cv + 1)),
                  pl.BlockSpec((nb, CHUNK, LANE), lambda i, j: (i, j, cw)),
                  pl.BlockSpec((LANE, GLA_KEY), lambda i, j: (0, 0)),
                  pl.BlockSpec((1, GLA_KEY), lambda i, j: (0, 0)),
                  pl.BlockSpec((1, GLA_DV), lambda i, j: (0, 0))],
        out_specs=pl.BlockSpec((nb, CHUNK, BRANCH), lambda i, j: (i, j, 0)),
        scratch_shapes=[pltpu.VMEM((nb, GLA_HEADS * GLA_DV, GLA_KEY), F32)],
        compiler_params=_cparams(("parallel", "arbitrary")),
        name="gla",
    )(z3, z3, z3, z3, z3, wa2p, ba, gn)


DIFF_KEY_TILE = 512
DIFF_QUERY_TILE = 512
VT_ROWS = 2 * DIFF_D + 16


def _alibi_slope(h):
    return 2.0 ** (-8.0 * (h + 1) / DIFF_HEADS)


def _diff_kernel(q_ref, k_ref, v_ref, lq1, lk1, lq2, lk2, gn_ref, o_ref, ka_scr, vb_scr, qa_scr, acc_scr,
                 m_scr, s_scr, rel_scr, *, lambda_init, tq, tk):
    qi = pl.program_id(1)
    hw = 2 * DIFF_D
    seq = k_ref.shape[1]
    lane = _iota2((tq, hw), 1)

    @pl.when(qi == 0)
    def _():
        key_row = _iota2((tq, hw), 0).astype(F32)
        for h in range(DIFF_HEADS):
            k_aug = []
            for part in range(tk // tq):
                key_bias = (_alibi_slope(h) * LOG2E) * (key_row + float(part * tq))
                hi = key_bias.astype(BF16).astype(F32)
                mid = (key_bias - hi).astype(BF16).astype(F32)
                lo = key_bias - hi - mid
                aug = jnp.where(lane == 0, hi, jnp.where(lane == 1, mid, jnp.where(lane == 2, lo, 0.0)))
                k_aug.append(aug.astype(BF16))
            for c in range(seq // tq):
                rows = slice(c * tq, (c + 1) * tq)
                ka_scr[h, rows, 0:hw] = k_ref[0, rows, h * hw:(h + 1) * hw].astype(BF16)
                ka_scr[h, rows, hw:2 * hw] = k_aug[c % (tk // tq)]
        k_row = _iota2((tk, 2 * tq), 0)
        col = _iota2((tk, 2 * tq), 1)
        for par in range(tk // tq):
            q_row = jnp.where(col >= tq, col - tq, col) + par * tq
            d = q_row - k_row
            rel_scr[par] = jnp.where((k_row // CHUNK) <= (q_row // CHUNK), (d - jnp.abs(d)).astype(F32), NEG_INF)
        sub = _iota2((VT_ROWS - hw, tq), 0)
        tail = jnp.where(sub == 0, 1.0, 0.0).astype(BF16)
        for c in range(seq // tq):
            rows = slice(c * tq, (c + 1) * tq)
            for h in range(DIFF_HEADS):
                vb_scr[h, 0:hw, rows] = v_ref[0, rows, h * hw:(h + 1) * hw].astype(F32).T.astype(BF16)
                vb_scr[h, hw:VT_ROWS, rows] = tail

    lane2 = _iota2((2 * tq, hw), 1)
    ones_aug = (lane2 < 3).astype(F32)
    for h in range(DIFF_HEADS):
        q = q_ref[0, :, h * hw:(h + 1) * hw].astype(F32) * (DIFF_D ** -0.5 * LOG2E)
        q2 = jnp.concatenate([jnp.where(lane < DIFF_D, q, 0.0), jnp.where(lane < DIFF_D, 0.0, q)],
                             axis=0)
        qa_scr[h] = jnp.concatenate([q2, ones_aug], axis=1).astype(BF16)
    acc_scr[...] = jnp.zeros_like(acc_scr)
    m_scr[...] = jnp.full_like(m_scr, NEG_INF)

    last_tile = qi // (tk // tq)

    def tile_scores(j):
        start = pl.multiple_of(j * tk, tk)
        return [lax.dot_general(ka_scr[h, pl.ds(start, tk), :], qa_scr[h], (((1,), (1,)), ((), ())),
                                preferred_element_type=F32) for h in range(DIFF_HEADS)]

    def update(scores, j):
        start = pl.multiple_of(j * tk, tk)
        alphas, probs = [], []
        for h, s in enumerate(scores):
            m_old = m_scr[h]
            m_new = jnp.maximum(m_old, jnp.max(s, axis=0, keepdims=True))
            alphas.append(jnp.exp2(m_old - m_new))
            probs.append(jnp.exp2((s - m_new).astype(BF16)))
            m_scr[h] = m_new - _alibi_slope(h) * LOG2E * tk
        pvs = [jnp.dot(vb_scr[h, :, pl.ds(start, tk)], probs[h], preferred_element_type=F32)
               for h in range(DIFF_HEADS)]
        for h in range(DIFF_HEADS):
            acc_scr[h] = alphas[h] * acc_scr[h] + pvs[h]

    for h, s in enumerate(tile_scores(0)):
        s_scr[h] = s

    def body(j, carry):
        nxt = tile_scores(j + 1)
        update([s_scr[h] for h in range(DIFF_HEADS)], j)
        for h in range(DIFF_HEADS):
            s_scr[h] = nxt[h]
        return carry

    lax.fori_loop(0, last_tile, body, 0)

    rel = rel_scr[qi % (tk // tq)]
    lam = (jnp.exp(jnp.sum(lq1[...] * lk1[...], axis=-1, keepdims=True))
           - jnp.exp(jnp.sum(lq2[...] * lk2[...], axis=-1, keepdims=True))
           + lambda_init)
    update([s_scr[h] + (_alibi_slope(h) * LOG2E) * rel for h in range(DIFF_HEADS)], last_tile)
    for h in range(DIFF_HEADS):
        on = acc_scr[h, 0:hw, :] / acc_scr[h, hw:hw + 1, :]
        o = (on[:, :tq] - lam * on[:, tq:]).T
        o_ref[0, :, h * hw:(h + 1) * hw] = (_rms(o, gn_ref[...], HEAD_EPS) * (1.0 - lambda_init)).astype(BF16)


def _diff_mixer(z3, lq1, lk1, lq2, lk2, gn, lambda_init):
    b, s = z3.shape[0], z3.shape[1]
    tk = _tile(s, DIFF_KEY_TILE)
    tq = _tile(tk, DIFF_QUERY_TILE)
    assert s % tk == 0 and tk % tq == 0 and tq % CHUNK == 0
    hw = 2 * DIFF_D
    cq = COL_DIFF // BRANCH
    nh = DIFF_HEADS
    vec = pl.BlockSpec((1, DIFF_D), lambda i, j: (0, 0))
    return pl.pallas_call(
        functools.partial(_diff_kernel, lambda_init=lambda_init, tq=tq, tk=tk),
        out_shape=jax.ShapeDtypeStruct((b, s, BRANCH), BF16),
        grid=(b, s // tq),
        in_specs=[pl.BlockSpec((1, tq, BRANCH), lambda i, j: (i, j, cq)),
                  pl.BlockSpec((1, s, BRANCH), lambda i, j: (i, 0, cq + 1), pipeline_mode=pl.Buffered(1)),
                  pl.BlockSpec((1, s, BRANCH), lambda i, j: (i, 0, cq + 2), pipeline_mode=pl.Buffered(1)),
                  vec, vec, vec, vec,
                  pl.BlockSpec((1, hw), lambda i, j: (0, 0))],
        out_specs=pl.BlockSpec((1, tq, BRANCH), lambda i, j: (i, j, 0)),
        scratch_shapes=[pltpu.VMEM((nh, s, 2 * hw), BF16), pltpu.VMEM((nh, VT_ROWS, s), BF16),
                        pltpu.VMEM((nh, 2 * tq, 2 * hw), BF16), pltpu.VMEM((nh, VT_ROWS, 2 * tq), F32),
                        pltpu.VMEM((nh, 1, 2 * tq), F32), pltpu.VMEM((nh, tk, 2 * tq), F32),
                        pltpu.VMEM((tk // tq, tk, 2 * tq), F32)],
        compiler_params=_cparams(("parallel", "arbitrary")),
        name="diff_attn",
    )(z3, z3, z3, lq1, lk1, lq2, lk2, gn)


def _shift(x, prev_rows):
    row = _iota2(x.shape, 0)
    return jnp.where(row == 0, prev_rows[7:8, :], pltpu.roll(x, 1, axis=0))


def _rwkv_kernel(*refs, has_vres, nb):
    scratch = refs[-5:]

    @pl.when(pl.program_id(1) == 0)
    def _():
        for r in scratch:
            r[...] = jnp.zeros_like(r)

    def chunk(c, carry):
        _rwkv_chunk(pl.multiple_of(c * CHUNK, CHUNK), *refs, has_vres=has_vres, nb=nb)
        return carry

    lax.fori_loop(0, refs[0].shape[1] // CHUNK, chunk, 0)


def _rwkv_chunk(off, *refs, has_vres, nb):
    if has_vres:
        (rkv_ref, gl_ref, wb_ref, ab_ref, vf_ref, mu_rkv, mu_g, mu_w, mu_a, w0, w2, a0, a2, g2, kk_s, ka_s,
         rk_s, lnw, lnb, v0, v1, v2, o_ref, p_rkv, p_g, p_w, p_a, state) = refs
    else:
        (rkv_ref, gl_ref, wb_ref, ab_ref, mu_rkv, mu_g, mu_w, mu_a, w0, w2, a0, a2, g2, kk_s, ka_s,
         rk_s, lnw, lnb, o_ref, vf_out, p_rkv, p_g, p_w, p_a, state) = refs
    chunk_rows = pl.ds(off, CHUNK)

    def stack(parts):
        return parts[0] if nb == 1 else jnp.concatenate(parts, axis=0)

    def mixed(x_ref, p_ref, mu_ref):
        parts = []
        for b in range(nb):
            x = x_ref[b, chunk_rows, :].astype(F32)
            parts.append(x + (_shift(x, p_ref[b]) - x) * mu_ref[...])
            p_ref[b] = x[CHUNK - 8:, :]
        return stack(parts)

    rkv = mixed(rkv_ref, p_rkv, mu_rkv)
    g_lr = mixed(gl_ref, p_g, mu_g)
    w_lr = mixed(wb_ref, p_w, mu_w)
    a_lr = mixed(ab_ref, p_a, mu_a)
    r = rkv[:, 0:BRANCH]
    k = rkv[:, BRANCH:2 * BRANCH]
    v = rkv[:, 2 * BRANCH:3 * BRANCH]
    nrow = nb * CHUNK

    y = w0[...] + _bdot(jnp.tanh(w_lr), w2[...])
    lw = -math.exp(-0.5) * _sigmoid(y)
    a = _sigmoid(a0[...] + _bdot(a_lr, a2[...]))
    gate = _bdot(_sigmoid(g_lr), g2[...])
    if has_vres:
        vf = stack([vf_ref[b, chunk_rows, :] for b in range(nb)])
        v = v + (vf - v) * _sigmoid(v0[...] + _bdot(_bdot(v, v1[...]), v2[...]))
    else:
        for b in range(nb):
            vf_out[b, chunk_rows, :] = v[b * CHUNK:(b + 1) * CHUNK]

    ones = _head_ones(BRANCH, RWKV_HD)
    kk = k * kk_s[...]
    k = k * (1.0 + (a - 1.0) * ka_s[...])
    sums = _bdot(jnp.concatenate([kk * kk, r * k * rk_s[...]], axis=0), ones)
    kk = kk * lax.rsqrt(jnp.maximum(sums[:nrow], 1e-24))
    bonus = sums[nrow:] * v

    ri, ci = _iota2((nrow, nrow), 0), _iota2((nrow, nrow), 1)
    tril = ((ri // CHUNK == ci // CHUNK) & (ri >= ci)).astype(BF16)
    cum = _split_dot_left(tril, lw)
    lasts = [cum[(b + 1) * CHUNK - 1:(b + 1) * CHUNK, :] for b in range(nb)]
    decay_all = [jnp.exp(last) for last in lasts]
    a_t = -kk * jnp.exp(cum - lw)
    r_t = r * jnp.exp(cum)
    inv = jnp.exp(-cum)
    to_end = stack([decay_all[b] * inv[b * CHUNK:(b + 1) * CHUNK] for b in range(nb)])
    kka = kk * a
    b_s = kka * inv
    k_s = k * inv
    b_e = kka * to_end
    k_e = k * to_end

    rows = HEAD_GROUP * CHUNK
    blk = ((_iota2((rows, GROUP_W), 0) // CHUNK) == (_iota2((rows, GROUP_W), 1) // RWKV_HD))
    blk_bf = blk.astype(F32).astype(BF16)
    t_idx = _iota2((CHUNK, GROUP_W), 0)
    s_idx = _iota2((CHUNK, GROUP_W), 1) % CHUNK
    strict = t_idx > s_idx
    incl = t_idx >= s_idx
    eye = (t_idx == s_idx).astype(F32)

    def blocks(x):
        return jnp.concatenate([x.astype(BF16)] * HEAD_GROUP, axis=0) * blk_bf

    ngroup = RWKV_HEADS // HEAD_GROUP
    probs = [(b, gi) for b in range(nb) for gi in range(ngroup)]
    nprob = len(probs)

    def part(t):
        return [t[b * CHUNK:(b + 1) * CHUNK, gi * GROUP_W:(gi + 1) * GROUP_W] for b, gi in probs]

    cat0 = lambda x, y_: jnp.concatenate([x, y_], axis=0)
    a_l, r_l, b_l, k_l, be_l, ke_l, v_l = (part(t) for t in (a_t, r_t, b_s, k_s, b_e, k_e, v))
    pm = [_bdot_nt(cat0(a_l[i], r_l[i]), cat0(blocks(b_l[i]), blocks(k_l[i]))) for i in range(nprob)]
    n_ab = [jnp.where(strict, p[:CHUNK, :GROUP_W], 0.0) for p in pm]
    a_ak = [jnp.where(strict, p[:CHUNK, GROUP_W:], 0.0) for p in pm]
    a_rb = [jnp.where(incl, p[CHUNK:, :GROUP_W], 0.0) for p in pm]
    a_rk = [jnp.where(incl, p[CHUNK:, GROUP_W:], 0.0) for p in pm]
    t_inv = [eye + n for n in n_ab]
    n_pow = n_ab
    n_blk = [blocks(n) for n in n_pow]
    for _ in range(int(math.log2(CHUNK)) - 1):
        n_pow = [_bdot(n_pow[i], n_blk[i]) for i in range(nprob)]
        n_blk = [blocks(n) for n in n_pow]
        t_inv = [t_inv[i] + _bdot(t_inv[i], n_blk[i]) for i in range(nprob)]
    av = [_bdot(cat0(a_ak[i], a_rk[i]), blocks(v_l[i])) for i in range(nprob)]
    y2 = [_bdot(t_inv[i], jnp.concatenate([blocks(a_l[i]), blocks(av[i][:CHUNK])], axis=1))
          for i in range(nprob)]
    st = [state[b, gi] for b, gi in probs]
    x2 = [_bdot_nt(cat0(y2[i][:, :GROUP_W], r_l[i]), st[i]) for i in range(nprob)]
    u_l = [x2[i][:CHUNK] + y2[i][:, GROUP_W:] for i in range(nprob)]
    o_l = [x2[i][CHUNK:] + _bdot(a_rb[i], blocks(u_l[i])) + av[i][CHUNK:] for i in range(nprob)]
    for i, (b, gi) in enumerate(probs):
        outer = _bdot_tn(cat0(u_l[i], v_l[i]), cat0(be_l[i], ke_l[i]))
        state[b, gi] = (st[i] * decay_all[b][:, gi * GROUP_W:(gi + 1) * GROUP_W]
                        + jnp.where(blk, outer, 0.0))
    o = stack([jnp.concatenate([o_l[b * ngroup + gi] for gi in range(ngroup)], axis=1)
               for b in range(nb)])

    mean = _bdot(o, ones) * (1.0 / RWKV_HD)
    cen = o - mean
    var = _bdot(cen * cen, ones) * (1.0 / RWKV_HD)
    o = cen * lax.rsqrt(var + RWKV_GN_EPS) * lnw[...] + lnb[...]
    o = ((o + bonus) * gate).astype(BF16)
    for b in range(nb):
        o_ref[b, chunk_rows, :] = o[b * CHUNK:(b + 1) * CHUNK]


def _rwkv_mixer(z3, v_first, p):
    b, s = z3.shape[0], z3.shape[1]
    has_vres = v_first is not None
    nb = SEQS_PER_STEP if b % SEQS_PER_STEP == 0 else 1
    ts = _tile(s, RWKV_ROWS_PER_STEP)
    assert ts % CHUNK == 0
    row = lambda w, c: pl.BlockSpec((nb, ts, w), lambda i, j: (i, j, c))
    full = lambda a: pl.BlockSpec(a.shape, lambda i, j: (0,) * a.ndim)
    ins = [z3, z3, z3, z3]
    specs = [row(3 * BRANCH, COL_RWKV // (3 * BRANCH)), row(RWKV_GATE_LORA, COL_RW_G // RWKV_GATE_LORA),
             row(LANE, COL_RW_W // LANE), row(LANE, COL_RW_A // LANE)]
    if has_vres:
        ins.append(v_first)
        specs.append(row(BRANCH, 0))
    names = ["mu_rkv", "mu_g", "mu_w", "mu_a", "w0", "w2", "a0", "a2", "g2", "kk", "ka", "rk", "lnw", "lnb"]
    if has_vres:
        names += ["v0", "v1", "v2"]
    for nme in names:
        ins.append(p[nme])
        specs.append(full(p[nme]))
    o_spec = pl.BlockSpec((nb, ts, BRANCH), lambda i, j: (i, j, 0))
    o_shape = jax.ShapeDtypeStruct((b, s, BRANCH), BF16)
    if has_vres:
        out_shape, out_specs = o_shape, o_spec
    else:
        out_shape = (o_shape, jax.ShapeDtypeStruct((b, s, BRANCH), F32))
        out_specs = (o_spec, pl.BlockSpec((nb, ts, BRANCH), lambda i, j: (i, j, 0)))
    res = pl.pallas_call(
        functools.partial(_rwkv_kernel, has_vres=has_vres, nb=nb),
        out_shape=out_shape,
        grid=(b // nb, s // ts),
        in_specs=specs,
        out_specs=out_specs,
        scratch_shapes=[pltpu.VMEM((nb, 8, 3 * BRANCH), F32), pltpu.VMEM((nb, 8, RWKV_GATE_LORA), F32),
                        pltpu.VMEM((nb, 8, LANE), F32), pltpu.VMEM((nb, 8, LANE), F32),
                        pltpu.VMEM((nb, RWKV_HEADS // HEAD_GROUP, GROUP_W, GROUP_W), F32)],
        compiler_params=_cparams(("parallel", "arbitrary")),
        name="rwkv7",
    )(*ins)
    if has_vres:
        return res, v_first
    return res[0], res[1]


def _pad_rows(w, rows, offset=0):
    return jnp.zeros((rows, w.shape[1]), F32).at[offset:offset + w.shape[0]].set(w)


N_IN = 14800
COL_GATES_SRC = 6608
MIX_BLOCKS = ((COL_CONV, ((0, 3072),)), (COL_DIFF, ((3088, 1536),)), (COL_RWKV, ((4624, 1536),)),
              (COL_RW_G, ((6352, 256),)),
              (COL_RW_W, ((6160, RWKV_DECAY_LORA), (None, LANE - RWKV_DECAY_LORA))),
              (COL_RW_A, ((6256, RWKV_AAA_LORA), (3072, GLA_LOW_RANK), (None, LANE - GLA_WLR_LANE - GLA_LOW_RANK))))
PACK_CHUNK = 512


def _pack_w_in_kernel(wt_ref, mix_ref, gate_ref):
    tc = wt_ref.shape[1]

    def rows(src, n):
        return jnp.zeros((n, tc), F32) if src is None else wt_ref[src:src + n, :]

    def put(out_ref, dst, block):
        out_ref[:, dst:dst + block.shape[0]] = block.T.astype(BF16)

    for dst, parts in MIX_BLOCKS:
        if len(parts) == 1:
            src, n = parts[0]
            for c in range(0, n, PACK_CHUNK):
                m = min(PACK_CHUNK, n - c)
                put(mix_ref, dst + c, rows(src + c, m))
        else:
            put(mix_ref, dst, jnp.concatenate([rows(src, n) for src, n in parts], axis=0))
    for c in range(0, N_IN - COL_GATES_SRC, PACK_CHUNK):
        put(gate_ref, c, rows(COL_GATES_SRC + c, PACK_CHUNK))


def _pack_w_in(w_in):
    depth = w_in.shape[0]
    tc = 256
    n_gate = N_IN - COL_GATES_SRC
    assert n_gate % PACK_CHUNK == 0
    return pl.pallas_call(
        _pack_w_in_kernel,
        out_shape=(jax.ShapeDtypeStruct((depth, D_MODEL, N_MIX), BF16),
                   jax.ShapeDtypeStruct((depth, D_MODEL, n_gate), BF16)),
        grid=(depth, D_MODEL // tc),
        in_specs=[pl.BlockSpec((None, N_IN, tc), lambda l, i: (l, 0, i))],
        out_specs=(pl.BlockSpec((None, tc, N_MIX), lambda l, i: (l, i, 0)),
                   pl.BlockSpec((None, tc, n_gate), lambda l, i: (l, i, 0))),
        compiler_params=_cparams(("parallel", "parallel")),
        name="pack_w_in",
    )(jnp.swapaxes(w_in, 1, 2))


def _pad_lanes(v, width, offset=0):
    return jnp.zeros((1, width), F32).at[0, offset:offset + v.shape[0]].set(v)


def kernel(x, norm_mix_pre, w_in, conv_w, gla_wa2, gla_ba, gla_norm, diff_lq1, diff_lk1, diff_lq2, diff_lk2,
           diff_norm, rw_mu, rw_w0, rw_w2, rw_a0, rw_a2, rw_g2, rw_kk, rw_ka, rw_rk, rw_lnw, rw_lnb, rw_v0,
           rw_v1, rw_v2, w_branch, w_out, norm_mix_post, norm_ffn_pre, w_gate, w_up, w_down, norm_ffn_post):
    bsz, seq = x.shape[0], x.shape[1]
    depth = w_in.shape[0]
    w_branch_b, w_out_b, w_down_b = (w.astype(BF16) for w in (w_branch, w_out, w_down))
    w_mix, w_gates = _pack_w_in(w_in)
    t = bsz * seq
    x2 = x.reshape(t, D_MODEL)
    row = lambda v: v.reshape(1, -1)
    v_first = None
    h = None
    for l in range(depth):
        if l == 0:
            z2, h = _inproj(x2, w_mix, l, gain=row(norm_mix_pre[l]))
        else:
            z2 = _inproj(h, w_mix, l)
        z3 = z2.reshape(bsz, seq, N_MIX)
        lambda_init = 0.8 - 0.6 * math.exp(-0.3 * l)

        o_conv = _conv_mixer(z3, conv_w[l])
        o_gla = _gla_mixer(z3, _pad_rows(gla_wa2[l], LANE, GLA_WLR_LANE), row(gla_ba[l]), row(gla_norm[l]))
        o_diff = _diff_mixer(z3, row(diff_lq1[l]), row(diff_lk1[l]), row(diff_lq2[l]),
                             row(diff_lk2[l]), row(diff_norm[l]), lambda_init)
        mu = rw_mu[l]
        rp = {
            "mu_rkv": row(mu[0:1536]),
            "mu_w": _pad_lanes(mu[1536:1632], LANE),
            "mu_a": _pad_lanes(mu[1632:1728], LANE),
            "mu_g": row(mu[1728:1984]),
            "w0": row(rw_w0[l]), "w2": _pad_rows(rw_w2[l], LANE),
            "a0": row(rw_a0[l]), "a2": _pad_rows(rw_a2[l], LANE),
            "g2": rw_g2[l], "kk": row(rw_kk[l]), "ka": row(rw_ka[l]), "rk": row(rw_rk[l]),
            "lnw": row(rw_lnw[l]), "lnb": row(rw_lnb[l]),
        }
        if l > 0:
            rp.update(v0=row(rw_v0[l - 1]), v1=rw_v1[l - 1], v2=rw_v2[l - 1])
        o_rwkv, v_first = _rwkv_mixer(z3, v_first if l > 0 else None, rp)

        outs = [o.reshape(t, BRANCH) for o in (o_conv, o_gla, o_diff, o_rwkv)]
        merged = _merge(h, w_gates, outs, w_branch_b, l)
        x2, h_ffn = _proj_norm_res(merged, w_out_b, l, x2, row(norm_mix_post[l]),
                                   row(norm_ffn_pre[l]), _tile(t, OUT_PROJ_TM), D_MODEL)
        act = _ffn_up(h_ffn, w_gate, w_up, l)
        next_gain = row(norm_mix_pre[l + 1]) if l + 1 < depth else None
        x2, h = _proj_norm_res(act, w_down_b, l, x2, row(norm_ffn_post[l]), next_gain,
                               _tile(t, FFN_DOWN_TM), D_FF)
    return x2.reshape(bsz, seq, D_MODEL)
```

```python
import functools
import math

import jax
import jax.numpy as jnp
from jax import lax
from jax.experimental import pallas as pl
from jax.experimental.pallas import tpu as pltpu

F32 = jnp.float32
BF16 = jnp.bfloat16

D_MODEL = 2048
CHUNK = 64
N_BRANCH = 4
BRANCH = 512
GLA_HEADS = 4
GLA_DK = 64
GLA_DV = 128
GLA_KEY = GLA_HEADS * GLA_DK
GLA_LOW_RANK = 16
GLA_LOGIT_NORM = 16.0
DIFF_HEADS = 4
DIFF_D = 64
RWKV_HEADS = 8
RWKV_HD = 64
RWKV_DECAY_LORA = 96
RWKV_AAA_LORA = 96
RWKV_MV_LORA = 64
RWKV_GATE_LORA = 256
D_FF = 5632
RMS_EPS = 1e-6
HEAD_EPS = 1e-5
RWKV_GN_EPS = 64e-5
NEG_INF = -1e30
LOG2E = 1.4426950408889634

COL_CONV = 0
COL_GLA = 1536
COL_DIFF = 3072
COL_RWKV = 4608
COL_RW_G = 6144
COL_RW_W = 6400
COL_RW_A = 6528
GLA_WLR_LANE = RWKV_AAA_LORA
LANE = 128
MXU_DIM = 256
N_MIX = COL_RW_A + LANE
HEAD_GROUP = 4
GROUP_W = HEAD_GROUP * RWKV_HD
SEQS_PER_STEP = 4
RWKV_ROWS_PER_STEP = 256

VMEM_LIMIT = 56 * 1024 * 1024
PROJ_SUB_ROWS = 128
OUT_PROJ_TM = 512
FFN_DOWN_TM = 256
assert GROUP_W == MXU_DIM


def _tile(n, pref):
    t = min(n, pref)
    while n % t:
        t -= 8
    return t


def _cparams(sem):
    return pltpu.CompilerParams(dimension_semantics=sem, vmem_limit_bytes=VMEM_LIMIT)


def _bdot(a, b):
    return jnp.dot(a.astype(BF16), b.astype(BF16), preferred_element_type=F32)


def _bdot_nt(a, b):
    return lax.dot_general(a.astype(BF16), b.astype(BF16), (((1,), (1,)), ((), ())),
                           preferred_element_type=F32)


def _bdot_tn(a, b):
    return lax.dot_general(a.astype(BF16), b.astype(BF16), (((0,), (0,)), ((), ())),
                           preferred_element_type=F32)


def _split_dot_left(ones_bf16, x):
    hi = x.astype(BF16)
    lo = (x - hi.astype(F32)).astype(BF16)
    return (jnp.dot(ones_bf16, hi, preferred_element_type=F32)
            + jnp.dot(ones_bf16, lo, preferred_element_type=F32))


def _sigmoid(x):
    return 0.5 * jnp.tanh(0.5 * x) + 0.5


def _rms(x, gain, eps):
    return x * lax.rsqrt(jnp.mean(x * x, axis=-1, keepdims=True) + eps) * gain


def _iota2(shape, dim):
    return lax.broadcasted_iota(jnp.int32, shape, dim)


def _tril_incl(n):
    return (_iota2((n, n), 0) >= _iota2((n, n), 1)).astype(BF16)


def _head_ones(n, width):
    return ((_iota2((n, n), 0) // width) == (_iota2((n, n), 1) // width)).astype(BF16)


def _expand(x, col_group):
    rows = HEAD_GROUP * CHUNK
    xt = jnp.concatenate([x] * HEAD_GROUP, axis=0)
    keep = (_iota2((rows, x.shape[1]), 0) // CHUNK) == (_iota2((rows, x.shape[1]), 1) // col_group)
    return jnp.where(keep, xt, 0.0)


def _inproj_kernel(*refs, fuse_norm):
    if fuse_norm:
        x_ref, g_ref, w_ref, z_ref, h_ref = refs
        h = _rms(x_ref[...], g_ref[...], RMS_EPS).astype(BF16)

        @pl.when(pl.program_id(0) == 0)
        def _():
            h_ref[...] = h
    else:
        h_ref, w_ref, z_ref = refs
        h = h_ref[...]
    tn = z_ref.shape[1]
    mid = (tn // (2 * MXU_DIM)) * MXU_DIM
    for cols in ((slice(0, mid), slice(mid, tn)) if 0 < mid < tn else (slice(0, tn),)):
        z_ref[:, cols] = jnp.dot(h, w_ref[:, cols], preferred_element_type=F32).astype(BF16)


def _inproj(h, w, l, gain=None):
    t, n = h.shape[0], w.shape[2]
    fuse_norm = gain is not None
    tm = _tile(t, 512 if fuse_norm else 1024)
    tn = _tile(n, 3328)
    grid = (n // tn, t // tm)
    row_spec = pl.BlockSpec((tm, D_MODEL), lambda j, i: (i, 0))
    w_spec = pl.BlockSpec((None, D_MODEL, tn), lambda j, i: (l, 0, j), pipeline_mode=pl.Buffered(1))
    z_spec = pl.BlockSpec((tm, tn), lambda j, i: (i, j))
    z_shape = jax.ShapeDtypeStruct((t, n), BF16)
    if fuse_norm:
        last = t // tm - 1
        h_spec = pl.BlockSpec((tm, D_MODEL), lambda j, i: (jnp.where(j == 0, i, last), 0))
        ins = (h, gain, w)
        in_specs = [row_spec, pl.BlockSpec((1, D_MODEL), lambda j, i: (0, 0)), w_spec]
        out_shape, out_specs = (z_shape, jax.ShapeDtypeStruct((t, D_MODEL), BF16)), (z_spec, h_spec)
    else:
        ins, in_specs, out_shape, out_specs = (h, w), [row_spec, w_spec], z_shape, z_spec
    return pl.pallas_call(
        functools.partial(_inproj_kernel, fuse_norm=fuse_norm),
        out_shape=out_shape,
        grid=grid,
        in_specs=in_specs,
        out_specs=out_specs,
        compiler_params=_cparams(("arbitrary", "arbitrary")),
        name="inproj",
    )(*ins)


def _merge_kernel(h_ref, wg0, wg1, wg2, wg3, o0, o1, o2, o3, p_ref, out_ref):
    h = h_ref[...]
    acc = None
    for n, (wg, o) in enumerate(((wg0, o0), (wg1, o1), (wg2, o2), (wg3, o3))):
        gate = _sigmoid(jnp.dot(h, wg[...], preferred_element_type=F32))
        term = gate * jnp.dot(o[...], p_ref[n], preferred_element_type=F32)
        acc = term if acc is None else acc + term
    out_ref[...] = acc.astype(BF16)


def _merge(h, w_gate, outs, w_branch, l):
    t = h.shape[0]
    tm, tn = _tile(t, 1024), 512
    nj = D_MODEL // tn
    gate_specs = [pl.BlockSpec((None, D_MODEL, tn), functools.partial(lambda i, j, n: (l, 0, n * nj + j), n=n))
                  for n in range(N_BRANCH)]
    o_specs = [pl.BlockSpec((tm, BRANCH), lambda i, j: (i, 0)) for _ in range(N_BRANCH)]
    return pl.pallas_call(
        _merge_kernel,
        out_shape=jax.ShapeDtypeStruct((t, D_MODEL), BF16),
        grid=(t // tm, nj),
        in_specs=[pl.BlockSpec((tm, D_MODEL), lambda i, j: (i, 0))] + gate_specs + o_specs
                 + [pl.BlockSpec((None, N_BRANCH, BRANCH, tn), lambda i, j: (l, 0, 0, j))],
        out_specs=pl.BlockSpec((tm, tn), lambda i, j: (i, j)),
        compiler_params=_cparams(("parallel", "arbitrary")),
        name="merge",
    )(h, w_gate, w_gate, w_gate, w_gate, *outs, w_branch)


def _proj_norm_res_kernel(a_ref, w_ref, x_ref, g_ref, gn_ref, o_ref, *h_ref, nk):
    k = pl.program_id(1)

    def accumulate(first):
        part = jnp.dot(a_ref[...], w_ref[...], preferred_element_type=F32)
        o_ref[...] = part if first else o_ref[...] + part

    def finish():
        sub = PROJ_SUB_ROWS
        for r in range(o_ref.shape[0] // sub):
            rows = slice(r * sub, (r + 1) * sub)
            y = jnp.dot(a_ref[rows, :], w_ref[...], preferred_element_type=F32)
            if nk > 1:
                y = y + o_ref[rows, :]
            xn = x_ref[rows, :] + _rms(y, g_ref[...], RMS_EPS)
            o_ref[rows, :] = xn
            if h_ref:
                h_ref[0][rows, :] = _rms(xn, gn_ref[...], RMS_EPS).astype(BF16)

    if nk == 1:
        finish()
    else:
        pl.when(k == 0)(functools.partial(accumulate, True))
        if nk > 2:
            pl.when((k > 0) & (k < nk - 1))(functools.partial(accumulate, False))
        pl.when(k == nk - 1)(finish)


def _proj_norm_res(a, w, l, x2, gain, next_gain, tm, tk):
    t, kdim = a.shape
    emit_h = next_gain is not None
    row_spec = pl.BlockSpec((tm, D_MODEL), lambda i, k: (i, 0))
    vec_spec = pl.BlockSpec((1, D_MODEL), lambda i, k: (0, 0))
    out_shape = [jax.ShapeDtypeStruct((t, D_MODEL), F32)]
    if emit_h:
        out_shape.append(jax.ShapeDtypeStruct((t, D_MODEL), BF16))
    assert kdim % tk == 0 and t % tm == 0 and tm % PROJ_SUB_ROWS == 0
    w_mode = dict(pipeline_mode=pl.Buffered(1)) if tk == kdim else {}
    res = pl.pallas_call(
        functools.partial(_proj_norm_res_kernel, nk=kdim // tk),
        out_shape=tuple(out_shape),
        grid=(t // tm, kdim // tk),
        in_specs=[pl.BlockSpec((tm, tk), lambda i, k: (i, k)),
                  pl.BlockSpec((None, tk, D_MODEL), lambda i, k: (l, k, 0), **w_mode),
                  row_spec, vec_spec, vec_spec],
        out_specs=tuple([row_spec] * len(out_shape)),
        compiler_params=_cparams(("parallel", "arbitrary")),
        name="proj_norm_res",
    )(a, w, x2, gain, next_gain if emit_h else gain)
    return (res[0], res[1]) if emit_h else (res[0], None)


def _ffn_up_kernel(h_ref, wg_ref, wu_ref, a_ref, wg_b, wu_b):
    @pl.when(pl.program_id(1) == 0)
    def _():
        wg_b[...] = wg_ref[...].astype(BF16)
        wu_b[...] = wu_ref[...].astype(BF16)

    tm = h_ref.shape[0]
    mid = (tm // 16) * 8
    for rows in (slice(0, mid), slice(mid, tm)):
        h = h_ref[rows, :]
        gt = jnp.dot(h, wg_b[...], preferred_element_type=F32)
        up = jnp.dot(h, wu_b[...], preferred_element_type=F32)
        a_ref[rows, :] = (gt * _sigmoid(gt) * up).astype(BF16)


def _ffn_up(h, wg, wu, l):
    t = h.shape[0]
    tm, tn = _tile(t, 1024), 512
    return pl.pallas_call(
        _ffn_up_kernel,
        out_shape=jax.ShapeDtypeStruct((t, D_FF), BF16),
        grid=(D_FF // tn, t // tm),
        in_specs=[pl.BlockSpec((tm, D_MODEL), lambda j, i: (i, 0)),
                  pl.BlockSpec((None, D_MODEL, tn), lambda j, i: (l, 0, j)),
                  pl.BlockSpec((None, D_MODEL, tn), lambda j, i: (l, 0, j))],
        out_specs=pl.BlockSpec((tm, tn), lambda j, i: (i, j)),
        scratch_shapes=[pltpu.VMEM((D_MODEL, tn), BF16), pltpu.VMEM((D_MODEL, tn), BF16)],
        compiler_params=_cparams(("parallel", "arbitrary")),
        name="ffn_up",
    )(h, wg, wu)


def _conv_kernel(b_ref, c_ref, u_ref, w_ref, o_ref, carry):
    @pl.when(pl.program_id(1) == 0)
    def _():
        carry[...] = jnp.zeros_like(carry)

    cu = c_ref[0].astype(F32) * u_ref[0].astype(F32)
    ts = cu.shape[0]
    row = _iota2(cu.shape, 0)
    p1 = carry[7:8, :]
    p2 = carry[6:7, :]
    s1 = jnp.where(row == 0, p1, pltpu.roll(cu, 1, axis=0))
    s2 = jnp.where(row == 0, p2, jnp.where(row == 1, p1, pltpu.roll(cu, 2, axis=0)))
    w = w_ref[...]
    y = w[2:3, :] * cu + w[1:2, :] * s1 + w[0:1, :] * s2
    o_ref[0] = (b_ref[0].astype(F32) * y).astype(BF16)
    carry[...] = cu[ts - 8:, :]


def _conv_mixer(z3, conv_w):
    b, s = z3.shape[0], z3.shape[1]
    ts = _tile(s, 512)
    c0 = COL_CONV // BRANCH
    return pl.pallas_call(
        _conv_kernel,
        out_shape=jax.ShapeDtypeStruct((b, s, BRANCH), BF16),
        grid=(b, s // ts),
        in_specs=[pl.BlockSpec((1, ts, BRANCH), lambda i, j: (i, j, c0)),
                  pl.BlockSpec((1, ts, BRANCH), lambda i, j: (i, j, c0 + 1)),
                  pl.BlockSpec((1, ts, BRANCH), lambda i, j: (i, j, c0 + 2)),
                  pl.BlockSpec((3, BRANCH), lambda i, j: (0, 0))],
        out_specs=pl.BlockSpec((1, ts, BRANCH), lambda i, j: (i, j, 0)),
        scratch_shapes=[pltpu.VMEM((8, BRANCH), F32)],
        compiler_params=_cparams(("parallel", "arbitrary")),
        name="conv",
    )(z3, z3, z3, conv_w)


def _gla_kernel(q_ref, k_ref, v_ref, g_ref, wl_ref, wa2_ref, ba_ref, gn_ref, o_ref, state):
    @pl.when(pl.program_id(1) == 0)
    def _():
        state[...] = jnp.zeros_like(state)

    nb = q_ref.shape[0]
    causal = _iota2((CHUNK, GLA_KEY), 0) >= (_iota2((CHUNK, GLA_KEY), 1) % CHUNK)
    same_head = ((_iota2((GLA_HEADS * GLA_DV, GLA_KEY), 0) // GLA_DV)
                 == (_iota2((GLA_HEADS * GLA_DV, GLA_KEY), 1) // GLA_DK))
    tril = _tril_incl(CHUNK)
    q_in, k_in, q_st, k_st, v, ve, dec = [], [], [], [], [], [], []
    for b in range(nb):
        q = q_ref[b].astype(F32) * (GLA_DK ** -0.5)
        k = k_ref[b].astype(F32)
        z = _bdot(wl_ref[b], wa2_ref[...]) + ba_ref[...]
        log_a = (jnp.minimum(z, 0.0) - jnp.log(1.0 + jnp.exp(-jnp.abs(z)))) * (1.0 / GLA_LOGIT_NORM)
        cum = _split_dot_left(tril, log_a)
        mid = cum[CHUNK // 2 - 1:CHUNK // 2, :]
        last = cum[CHUNK - 1:CHUNK, :]
        q_in.append(q * jnp.exp(cum - mid))
        k_in.append(_expand(k * jnp.exp(mid - cum), GLA_DK).astype(BF16))
        q_st.append(q * jnp.exp(cum))
        k_st.append(k * jnp.exp(last - cum))
        v.append(v_ref[b])
        ve.append(_expand(v[b].astype(F32), GLA_DV).astype(BF16))
        dec.append(jnp.exp(last))
    scores = [jnp.where(causal, _bdot_nt(q_in[b], k_in[b]), 0.0) for b in range(nb)]
    st = [state[b] for b in range(nb)]
    o_l = [_bdot(scores[b], ve[b]) + _bdot_nt(q_st[b], st[b]) for b in range(nb)]
    for b in range(nb):
        state[b] = st[b] * dec[b] + jnp.where(same_head, _bdot_tn(v[b], k_st[b]), 0.0)
    gn = gn_ref[...]
    for b in range(nb):
        o = o_l[b]
        g = g_ref[b].astype(F32)
        gate = g * _sigmoid(g)
        outs = [_rms(o[:, h * GLA_DV:(h + 1) * GLA_DV], gn, HEAD_EPS) for h in range(GLA_HEADS)]
        o_ref[b] = (jnp.concatenate(outs, axis=1) * gate).astype(BF16)


def _gla_mixer(z3, wa2p, ba, gn):
    b, s = z3.shape[0], z3.shape[1]
    cq = COL_GLA // GLA_KEY
    cv = (COL_GLA + 2 * GLA_KEY) // BRANCH
    cw = COL_RW_A // LANE
    nb = SEQS_PER_STEP if b % SEQS_PER_STEP == 0 else 1
    return pl.pallas_call(
        _gla_kernel,
        out_shape=jax.ShapeDtypeStruct((b, s, BRANCH), BF16),
        grid=(b // nb, s // CHUNK),
        in_specs=[pl.BlockSpec((nb, CHUNK, GLA_KEY), lambda i, j: (i, j, cq)),
                  pl.BlockSpec((nb, CHUNK, GLA_KEY), lambda i, j: (i, j, cq + 1)),
                  pl.BlockSpec((nb, CHUNK, BRANCH), lambda i, j: (i, j, cv)),
                  pl.BlockSpec((nb, CHUNK, BRANCH), lambda i, j: (i, j, cv + 1)),
                  pl.BlockSpec((nb, CHUNK, LANE), lambda i, j: (i, j, cw)),
                  pl.BlockSpec((LANE, GLA_KEY), lambda i, j: (0, 0)),
                  pl.BlockSpec((1, GLA_KEY), lambda i, j: (0, 0)),
                  pl.BlockSpec((1, GLA_DV), lambda i, j: (0, 0))],
        out_specs=pl.BlockSpec((nb, CHUNK, BRANCH), lambda i, j: (i, j, 0)),
        scratch_shapes=[pltpu.VMEM((nb, GLA_HEADS * GLA_DV, GLA_KEY), F32)],
        compiler_params=_cparams(("parallel", "arbitrary")),
        name="gla",
    )(z3, z3, z3, z3, z3, wa2p, ba, gn)


DIFF_KEY_TILE = 512
DIFF_QUERY_TILE = 512
VT_ROWS = 2 * DIFF_D + 16


def _alibi_slope(h):
    return 2.0 ** (-8.0 * (h + 1) / DIFF_HEADS)


def _diff_kernel(q_ref, k_ref, v_ref, lq1, lk1, lq2, lk2, gn_ref, o_ref, ka_scr, vb_scr, qa_scr, acc_scr,
                 m_scr, s_scr, rel_scr, *, lambda_init, tq, tk):
    qi = pl.program_id(1)
    hw = 2 * DIFF_D
    seq = k_ref.shape[1]
    lane = _iota2((tq, hw), 1)

    @pl.when(qi == 0)
    def _():
        key_row = _iota2((tq, hw), 0).astype(F32)
        for h in range(DIFF_HEADS):
            k_aug = []
            for part in range(tk // tq):
                key_bias = (_alibi_slope(h) * LOG2E) * (key_row + float(part * tq))
                hi = key_bias.astype(BF16).astype(F32)
                mid = (key_bias - hi).astype(BF16).astype(F32)
                lo = key_bias - hi - mid
                aug = jnp.where(lane == 0, hi, jnp.where(lane == 1, mid, jnp.where(lane == 2, lo, 0.0)))
                k_aug.append(aug.astype(BF16))
            for c in range(seq // tq):
                rows = slice(c * tq, (c + 1) * tq)
                ka_scr[h, rows, 0:hw] = k_ref[0, rows, h * hw:(h + 1) * hw].astype(BF16)
                ka_scr[h, rows, hw:2 * hw] = k_aug[c % (tk // tq)]
        k_row = _iota2((tk, 2 * tq), 0)
        col = _iota2((tk, 2 * tq), 1)
        for par in range(tk // tq):
            q_row = jnp.where(col >= tq, col - tq, col) + par * tq
            d = q_row - k_row
            rel_scr[par] = jnp.where((k_row // CHUNK) <= (q_row // CHUNK), (d - jnp.abs(d)).astype(F32), NEG_INF)
        sub = _iota2((VT_ROWS - hw, tq), 0)
        tail = jnp.where(sub == 0, 1.0, 0.0).astype(BF16)
        for c in range(seq // tq):
            rows = slice(c * tq, (c + 1) * tq)
            for h in range(DIFF_HEADS):
                vb_scr[h, 0:hw, rows] = v_ref[0, rows, h * hw:(h + 1) * hw].astype(F32).T.astype(BF16)
                vb_scr[h, hw:VT_ROWS, rows] = tail

    lane2 = _iota2((2 * tq, hw), 1)
    ones_aug = (lane2 < 3).astype(F32)
    for h in range(DIFF_HEADS):
        q = q_ref[0, :, h * hw:(h + 1) * hw].astype(F32) * (DIFF_D ** -0.5 * LOG2E)
        q2 = jnp.concatenate([jnp.where(lane < DIFF_D, q, 0.0), jnp.where(lane < DIFF_D, 0.0, q)],
                             axis=0)
        qa_scr[h] = jnp.concatenate([q2, ones_aug], axis=1).astype(BF16)
    acc_scr[...] = jnp.zeros_like(acc_scr)
    m_scr[...] = jnp.full_like(m_scr, NEG_INF)

    last_tile = qi // (tk // tq)

    def tile_scores(j):
        start = pl.multiple_of(j * tk, tk)
        return [lax.dot_general(ka_scr[h, pl.ds(start, tk), :], qa_scr[h], (((1,), (1,)), ((), ())),
                                preferred_element_type=F32) for h in range(DIFF_HEADS)]

    def update(scores, j):
        start = pl.multiple_of(j * tk, tk)
        alphas, probs = [], []
        for h, s in enumerate(scores):
            m_old = m_scr[h]
            m_new = jnp.maximum(m_old, jnp.max(s, axis=0, keepdims=True))
            alphas.append(jnp.exp2(m_old - m_new))
            probs.append(jnp.exp2((s - m_new).astype(BF16)))
            m_scr[h] = m_new - _alibi_slope(h) * LOG2E * tk
        pvs = [jnp.dot(vb_scr[h, :, pl.ds(start, tk)], probs[h], preferred_element_type=F32)
               for h in range(DIFF_HEADS)]
        for h in range(DIFF_HEADS):
            acc_scr[h] = alphas[h] * acc_scr[h] + pvs[h]

    for h, s in enumerate(tile_scores(0)):
        s_scr[h] = s

    def body(j, carry):
        nxt = tile_scores(j + 1)
        update([s_scr[h] for h in range(DIFF_HEADS)], j)
        for h in range(DIFF_HEADS):
            s_scr[h] = nxt[h]
        return carry

    lax.fori_loop(0, last_tile, body, 0)

    rel = rel_scr[qi % (tk // tq)]
    lam = (jnp.exp(jnp.sum(lq1[...] * lk1[...], axis=-1, keepdims=True))
           - jnp.exp(jnp.sum(lq2[...] * lk2[...], axis=-1, keepdims=True))
           + lambda_init)
    update([s_scr[h] + (_alibi_slope(h) * LOG2E) * rel for h in range(DIFF_HEADS)], last_tile)
    for h in range(DIFF_HEADS):
        on = acc_scr[h, 0:hw, :] / acc_scr[h, hw:hw + 1, :]
        o = (on[:, :tq] - lam * on[:, tq:]).T
        o_ref[0, :, h * hw:(h + 1) * hw] = (_rms(o, gn_ref[...], HEAD_EPS) * (1.0 - lambda_init)).astype(BF16)


def _diff_mixer(z3, lq1, lk1, lq2, lk2, gn, lambda_init):
    b, s = z3.shape[0], z3.shape[1]
    tk = _tile(s, DIFF_KEY_TILE)
    tq = _tile(tk, DIFF_QUERY_TILE)
    assert s % tk == 0 and tk % tq == 0 and tq % CHUNK == 0
    hw = 2 * DIFF_D
    cq = COL_DIFF // BRANCH
    nh = DIFF_HEADS
    vec = pl.BlockSpec((1, DIFF_D), lambda i, j: (0, 0))
    return pl.pallas_call(
        functools.partial(_diff_kernel, lambda_init=lambda_init, tq=tq, tk=tk),
        out_shape=jax.ShapeDtypeStruct((b, s, BRANCH), BF16),
        grid=(b, s // tq),
        in_specs=[pl.BlockSpec((1, tq, BRANCH), lambda i, j: (i, j, cq)),
                  pl.BlockSpec((1, s, BRANCH), lambda i, j: (i, 0, cq + 1), pipeline_mode=pl.Buffered(1)),
                  pl.BlockSpec((1, s, BRANCH), lambda i, j: (i, 0, cq + 2), pipeline_mode=pl.Buffered(1)),
                  vec, vec, vec, vec,
                  pl.BlockSpec((1, hw), lambda i, j: (0, 0))],
        out_specs=pl.BlockSpec((1, tq, BRANCH), lambda i, j: (i, j, 0)),
        scratch_shapes=[pltpu.VMEM((nh, s, 2 * hw), BF16), pltpu.VMEM((nh, VT_ROWS, s), BF16),
                        pltpu.VMEM((nh, 2 * tq, 2 * hw), BF16), pltpu.VMEM((nh, VT_ROWS, 2 * tq), F32),
                        pltpu.VMEM((nh, 1, 2 * tq), F32), pltpu.VMEM((nh, tk, 2 * tq), F32),
                        pltpu.VMEM((tk // tq, tk, 2 * tq), F32)],
        compiler_params=_cparams(("parallel", "arbitrary")),
        name="diff_attn",
    )(z3, z3, z3, lq1, lk1, lq2, lk2, gn)


def _shift(x, prev_rows):
    row = _iota2(x.shape, 0)
    return jnp.where(row == 0, prev_rows[7:8, :], pltpu.roll(x, 1, axis=0))


def _rwkv_kernel(*refs, has_vres, nb):
    scratch = refs[-5:]

    @pl.when(pl.program_id(1) == 0)
    def _():
        for r in scratch:
            r[...] = jnp.zeros_like(r)

    def chunk(c, carry):
        _rwkv_chunk(pl.multiple_of(c * CHUNK, CHUNK), *refs, has_vres=has_vres, nb=nb)
        return carry

    lax.fori_loop(0, refs[0].shape[1] // CHUNK, chunk, 0)


def _rwkv_chunk(off, *refs, has_vres, nb):
    if has_vres:
        (rkv_ref, gl_ref, wb_ref, ab_ref, vf_ref, mu_rkv, mu_g, mu_w, mu_a, w0, w2, a0, a2, g2, kk_s, ka_s,
         rk_s, lnw, lnb, v0, v1, v2, o_ref, p_rkv, p_g, p_w, p_a, state) = refs
    else:
        (rkv_ref, gl_ref, wb_ref, ab_ref, mu_rkv, mu_g, mu_w, mu_a, w0, w2, a0, a2, g2, kk_s, ka_s,
         rk_s, lnw, lnb, o_ref, vf_out, p_rkv, p_g, p_w, p_a, state) = refs
    chunk_rows = pl.ds(off, CHUNK)

    def stack(parts):
        return parts[0] if nb == 1 else jnp.concatenate(parts, axis=0)

    def mixed(x_ref, p_ref, mu_ref):
        parts = []
        for b in range(nb):
            x = x_ref[b, chunk_rows, :].astype(F32)
            parts.append(x + (_shift(x, p_ref[b]) - x) * mu_ref[...])
            p_ref[b] = x[CHUNK - 8:, :]
        return stack(parts)

    rkv = mixed(rkv_ref, p_rkv, mu_rkv)
    g_lr = mixed(gl_ref, p_g, mu_g)
    w_lr = mixed(wb_ref, p_w, mu_w)
    a_lr = mixed(ab_ref, p_a, mu_a)
    r = rkv[:, 0:BRANCH]
    k = rkv[:, BRANCH:2 * BRANCH]
    v = rkv[:, 2 * BRANCH:3 * BRANCH]
    nrow = nb * CHUNK

    y = w0[...] + _bdot(jnp.tanh(w_lr), w2[...])
    lw = -math.exp(-0.5) * _sigmoid(y)
    a = _sigmoid(a0[...] + _bdot(a_lr, a2[...]))
    gate = _bdot(_sigmoid(g_lr), g2[...])
    if has_vres:
        vf = stack([vf_ref[b, chunk_rows, :] for b in range(nb)])
        v = v + (vf - v) * _sigmoid(v0[...] + _bdot(_bdot(v, v1[...]), v2[...]))
    else:
        for b in range(nb):
            vf_out[b, chunk_rows, :] = v[b * CHUNK:(b + 1) * CHUNK]

    ones = _head_ones(BRANCH, RWKV_HD)
    kk = k * kk_s[...]
    k = k * (1.0 + (a - 1.0) * ka_s[...])
    sums = _bdot(jnp.concatenate([kk * kk, r * k * rk_s[...]], axis=0), ones)
    kk = kk * lax.rsqrt(jnp.maximum(sums[:nrow], 1e-24))
    bonus = sums[nrow:] * v

    ri, ci = _iota2((nrow, nrow), 0), _iota2((nrow, nrow), 1)
    tril = ((ri // CHUNK == ci // CHUNK) & (ri >= ci)).astype(BF16)
    cum = _split_dot_left(tril, lw)
    lasts = [cum[(b + 1) * CHUNK - 1:(b + 1) * CHUNK, :] for b in range(nb)]
    decay_all = [jnp.exp(last) for last in lasts]
    a_t = -kk * jnp.exp(cum - lw)
    r_t = r * jnp.exp(cum)
    inv = jnp.exp(-cum)
    to_end = stack([decay_all[b] * inv[b * CHUNK:(b + 1) * CHUNK] for b in range(nb)])
    kka = kk * a
    b_s = kka * inv
    k_s = k * inv
    b_e = kka * to_end
    k_e = k * to_end

    rows = HEAD_GROUP * CHUNK
    blk = ((_iota2((rows, GROUP_W), 0) // CHUNK) == (_iota2((rows, GROUP_W), 1) // RWKV_HD))
    blk_bf = blk.astype(F32).astype(BF16)
    t_idx = _iota2((CHUNK, GROUP_W), 0)
    s_idx = _iota2((CHUNK, GROUP_W), 1) % CHUNK
    strict = t_idx > s_idx
    incl = t_idx >= s_idx
    eye = (t_idx == s_idx).astype(F32)

    def blocks(x):
        return jnp.concatenate([x.astype(BF16)] * HEAD_GROUP, axis=0) * blk_bf

    ngroup = RWKV_HEADS // HEAD_GROUP
    probs = [(b, gi) for b in range(nb) for gi in range(ngroup)]
    nprob = len(probs)

    def part(t):
        return [t[b * CHUNK:(b + 1) * CHUNK, gi * GROUP_W:(gi + 1) * GROUP_W] for b, gi in probs]

    cat0 = lambda x, y_: jnp.concatenate([x, y_], axis=0)
    a_l, r_l, b_l, k_l, be_l, ke_l, v_l = (part(t) for t in (a_t, r_t, b_s, k_s, b_e, k_e, v))
    pm = [_bdot_nt(cat0(a_l[i], r_l[i]), cat0(blocks(b_l[i]), blocks(k_l[i]))) for i in range(nprob)]
    n_ab = [jnp.where(strict, p[:CHUNK, :GROUP_W], 0.0) for p in pm]
    a_ak = [jnp.where(strict, p[:CHUNK, GROUP_W:], 0.0) for p in pm]
    a_rb = [jnp.where(incl, p[CHUNK:, :GROUP_W], 0.0) for p in pm]
    a_rk = [jnp.where(incl, p[CHUNK:, GROUP_W:], 0.0) for p in pm]
    t_inv = [eye + n for n in n_ab]
    n_pow = n_ab
    n_blk = [blocks(n) for n in n_pow]
    for _ in range(int(math.log2(CHUNK)) - 1):
        n_pow = [_bdot(n_pow[i], n_blk[i]) for i in range(nprob)]
        n_blk = [blocks(n) for n in n_pow]
        t_inv = [t_inv[i] + _bdot(t_inv[i], n_blk[i]) for i in range(nprob)]
    av = [_bdot(cat0(a_ak[i], a_rk[i]), blocks(v_l[i])) for i in range(nprob)]
    y2 = [_bdot(t_inv[i], jnp.concatenate([blocks(a_l[i]), blocks(av[i][:CHUNK])], axis=1))
          for i in range(nprob)]
    st = [state[b, gi] for b, gi in probs]
    x2 = [_bdot_nt(cat0(y2[i][:, :GROUP_W], r_l[i]), st[i]) for i in range(nprob)]
    u_l = [x2[i][:CHUNK] + y2[i][:, GROUP_W:] for i in range(nprob)]
    o_l = [x2[i][CHUNK:] + _bdot(a_rb[i], blocks(u_l[i])) + av[i][CHUNK:] for i in range(nprob)]
    for i, (b, gi) in enumerate(probs):
        outer = _bdot_tn(cat0(u_l[i], v_l[i]), cat0(be_l[i], ke_l[i]))
        state[b, gi] = (st[i] * decay_all[b][:, gi * GROUP_W:(gi + 1) * GROUP_W]
                        + jnp.where(blk, outer, 0.0))
    o = stack([jnp.concatenate([o_l[b * ngroup + gi] for gi in range(ngroup)], axis=1)
               for b in range(nb)])

    mean = _bdot(o, ones) * (1.0 / RWKV_HD)
    cen = o - mean
    var = _bdot(cen * cen, ones) * (1.0 / RWKV_HD)
    o = cen * lax.rsqrt(var + RWKV_GN_EPS) * lnw[...] + lnb[...]
    o = ((o + bonus) * gate).astype(BF16)
    for b in range(nb):
        o_ref[b, chunk_rows, :] = o[b * CHUNK:(b + 1) * CHUNK]


def _rwkv_mixer(z3, v_first, p):
    b, s = z3.shape[0], z3.shape[1]
    has_vres = v_first is not None
    nb = SEQS_PER_STEP if b % SEQS_PER_STEP == 0 else 1
    ts = _tile(s, RWKV_ROWS_PER_STEP)
    assert ts % CHUNK == 0
    row = lambda w, c: pl.BlockSpec((nb, ts, w), lambda i, j: (i, j, c))
    full = lambda a: pl.BlockSpec(a.shape, lambda i, j: (0,) * a.ndim)
    ins = [z3, z3, z3, z3]
    specs = [row(3 * BRANCH, COL_RWKV // (3 * BRANCH)), row(RWKV_GATE_LORA, COL_RW_G // RWKV_GATE_LORA),
             row(LANE, COL_RW_W // LANE), row(LANE, COL_RW_A // LANE)]
    if has_vres:
        ins.append(v_first)
        specs.append(row(BRANCH, 0))
    names = ["mu_rkv", "mu_g", "mu_w", "mu_a", "w0", "w2", "a0", "a2", "g2", "kk", "ka", "rk", "lnw", "lnb"]
    if has_vres:
        names += ["v0", "v1", "v2"]
    for nme in names:
        ins.append(p[nme])
        specs.append(full(p[nme]))
    o_spec = pl.BlockSpec((nb, ts, BRANCH), lambda i, j: (i, j, 0))
    o_shape = jax.ShapeDtypeStruct((b, s, BRANCH), BF16)
    if has_vres:
        out_shape, out_specs = o_shape, o_spec
    else:
        out_shape = (o_shape, jax.ShapeDtypeStruct((b, s, BRANCH), F32))
        out_specs = (o_spec, pl.BlockSpec((nb, ts, BRANCH), lambda i, j: (i, j, 0)))
    res = pl.pallas_call(
        functools.partial(_rwkv_kernel, has_vres=has_vres, nb=nb),
        out_shape=out_shape,
        grid=(b // nb, s // ts),
        in_specs=specs,
        out_specs=out_specs,
        scratch_shapes=[pltpu.VMEM((nb, 8, 3 * BRANCH), F32), pltpu.VMEM((nb, 8, RWKV_GATE_LORA), F32),
                        pltpu.VMEM((nb, 8, LANE), F32), pltpu.VMEM((nb, 8, LANE), F32),
                        pltpu.VMEM((nb, RWKV_HEADS // HEAD_GROUP, GROUP_W, GROUP_W), F32)],
        compiler_params=_cparams(("parallel", "arbitrary")),
        name="rwkv7",
    )(*ins)
    if has_vres:
        return res, v_first
    return res[0], res[1]


def _pad_rows(w, rows, offset=0):
    return jnp.zeros((rows, w.shape[1]), F32).at[offset:offset + w.shape[0]].set(w)


N_IN = 14800
COL_GATES_SRC = 6608
MIX_BLOCKS = ((COL_CONV, ((0, 3072),)), (COL_DIFF, ((3088, 1536),)), (COL_RWKV, ((4624, 1536),)),
              (COL_RW_G, ((6352, 256),)),
              (COL_RW_W, ((6160, RWKV_DECAY_LORA), (None, LANE - RWKV_DECAY_LORA))),
              (COL_RW_A, ((6256, RWKV_AAA_LORA), (3072, GLA_LOW_RANK), (None, LANE - GLA_WLR_LANE - GLA_LOW_RANK))))
PACK_CHUNK = 512


def _pack_w_in_kernel(wt_ref, mix_ref, gate_ref):
    tc = wt_ref.shape[1]

    def rows(src, n):
        return jnp.zeros((n, tc), F32) if src is None else wt_ref[src:src + n, :]

    def put(out_ref, dst, block):
        out_ref[:, dst:dst + block.shape[0]] = block.T.astype(BF16)

    for dst, parts in MIX_BLOCKS:
        if len(parts) == 1:
            src, n = parts[0]
            for c in range(0, n, PACK_CHUNK):
                m = min(PACK_CHUNK, n - c)
                put(mix_ref, dst + c, rows(src + c, m))
        else:
            put(mix_ref, dst, jnp.concatenate([rows(src, n) for src, n in parts], axis=0))
    for c in range(0, N_IN - COL_GATES_SRC, PACK_CHUNK):
        put(gate_ref, c, rows(COL_GATES_SRC + c, PACK_CHUNK))


def _pack_w_in(w_in):
    depth = w_in.shape[0]
    tc = 256
    n_gate = N_IN - COL_GATES_SRC
    assert n_gate % PACK_CHUNK == 0
    return pl.pallas_call(
        _pack_w_in_kernel,
        out_shape=(jax.ShapeDtypeStruct((depth, D_MODEL, N_MIX), BF16),
                   jax.ShapeDtypeStruct((depth, D_MODEL, n_gate), BF16)),
        grid=(depth, D_MODEL // tc),
        in_specs=[pl.BlockSpec((None, N_IN, tc), lambda l, i: (l, 0, i))],
        out_specs=(pl.BlockSpec((None, tc, N_MIX), lambda l, i: (l, i, 0)),
                   pl.BlockSpec((None, tc, n_gate), lambda l, i: (l, i, 0))),
        compiler_params=_cparams(("parallel", "parallel")),
        name="pack_w_in",
    )(jnp.swapaxes(w_in, 1, 2))


def _pad_lanes(v, width, offset=0):
    return jnp.zeros((1, width), F32).at[0, offset:offset + v.shape[0]].set(v)


def kernel(x, norm_mix_pre, w_in, conv_w, gla_wa2, gla_ba, gla_norm, diff_lq1, diff_lk1, diff_lq2, diff_lk2,
           diff_norm, rw_mu, rw_w0, rw_w2, rw_a0, rw_a2, rw_g2, rw_kk, rw_ka, rw_rk, rw_lnw, rw_lnb, rw_v0,
           rw_v1, rw_v2, w_branch, w_out, norm_mix_post, norm_ffn_pre, w_gate, w_up, w_down, norm_ffn_post):
    bsz, seq = x.shape[0], x.shape[1]
    depth = w_in.shape[0]
    w_branch_b, w_out_b, w_down_b = (w.astype(BF16) for w in (w_branch, w_out, w_down))
    w_mix, w_gates = _pack_w_in(w_in)
    t = bsz * seq
    x2 = x.reshape(t, D_MODEL)
    row = lambda v: v.reshape(1, -1)
    v_first = None
    h = None
    for l in range(depth):
        if l == 0:
            z2, h = _inproj(x2, w_mix, l, gain=row(norm_mix_pre[l]))
        else:
            z2 = _inproj(h, w_mix, l)
        z3 = z2.reshape(bsz, seq, N_MIX)
        lambda_init = 0.8 - 0.6 * math.exp(-0.3 * l)

        o_conv = _conv_mixer(z3, conv_w[l])
        o_gla = _gla_mixer(z3, _pad_rows(gla_wa2[l], LANE, GLA_WLR_LANE), row(gla_ba[l]), row(gla_norm[l]))
        o_diff = _diff_mixer(z3, row(diff_lq1[l]), row(diff_lk1[l]), row(diff_lq2[l]),
                             row(diff_lk2[l]), row(diff_norm[l]), lambda_init)
        mu = rw_mu[l]
        rp = {
            "mu_rkv": row(mu[0:1536]),
            "mu_w": _pad_lanes(mu[1536:1632], LANE),
            "mu_a": _pad_lanes(mu[1632:1728], LANE),
            "mu_g": row(mu[1728:1984]),
            "w0": row(rw_w0[l]), "w2": _pad_rows(rw_w2[l], LANE),
            "a0": row(rw_a0[l]), "a2": _pad_rows(rw_a2[l], LANE),
            "g2": rw_g2[l], "kk": row(rw_kk[l]), "ka": row(rw_ka[l]), "rk": row(rw_rk[l]),
            "lnw": row(rw_lnw[l]), "lnb": row(rw_lnb[l]),
        }
        if l > 0:
            rp.update(v0=row(rw_v0[l - 1]), v1=rw_v1[l - 1], v2=rw_v2[l - 1])
        o_rwkv, v_first = _rwkv_mixer(z3, v_first if l > 0 else None, rp)

        outs = [o.reshape(t, BRANCH) for o in (o_conv, o_gla, o_diff, o_rwkv)]
        merged = _merge(h, w_gates, outs, w_branch_b, l)
        x2, h_ffn = _proj_norm_res(merged, w_out_b, l, x2, row(norm_mix_post[l]),
                                   row(norm_ffn_pre[l]), _tile(t, OUT_PROJ_TM), D_MODEL)
        act = _ffn_up(h_ffn, w_gate, w_up, l)
        next_gain = row(norm_mix_pre[l + 1]) if l + 1 < depth else None
        x2, h = _proj_norm_res(act, w_down_b, l, x2, row(norm_ffn_post[l]), next_gain,
                               _tile(t, FFN_DOWN_TM), D_FF)
    return x2.reshape(bsz, seq, D_MODEL)
```

```python
import functools
import math

import jax
import jax.numpy as jnp
from jax import lax
from jax.experimental import pallas as pl
from jax.experimental.pallas import tpu as pltpu

F32 = jnp.float32
BF16 = jnp.bfloat16

D_MODEL = 2048
CHUNK = 64
N_BRANCH = 4
BRANCH = 512
GLA_HEADS = 4
GLA_DK = 64
GLA_DV = 128
GLA_KEY = GLA_HEADS * GLA_DK
GLA_LOW_RANK = 16
GLA_LOGIT_NORM = 16.0
DIFF_HEADS = 4
DIFF_D = 64
RWKV_HEADS = 8
RWKV_HD = 64
RWKV_DECAY_LORA = 96
RWKV_AAA_LORA = 96
RWKV_MV_LORA = 64
RWKV_GATE_LORA = 256
D_FF = 5632
RMS_EPS = 1e-6
HEAD_EPS = 1e-5
RWKV_GN_EPS = 64e-5
NEG_INF = -1e30
LOG2E = 1.4426950408889634

COL_CONV = 0
COL_GLA = 1536
COL_DIFF = 3072
COL_RWKV = 4608
COL_RW_G = 6144
COL_RW_W = 6400
COL_RW_A = 6528
GLA_WLR_LANE = RWKV_AAA_LORA
LANE = 128
MXU_DIM = 256
N_MIX = COL_RW_A + LANE
HEAD_GROUP = 4
GROUP_W = HEAD_GROUP * RWKV_HD
SEQS_PER_STEP = 4
RWKV_ROWS_PER_STEP = 256

VMEM_LIMIT = 56 * 1024 * 1024
PROJ_SUB_ROWS = 128
OUT_PROJ_TM = 512
FFN_DOWN_TM = 256
assert GROUP_W == MXU_DIM


def _tile(n, pref):
    t = min(n, pref)
    while n % t:
        t -= 8
    return t


def _cparams(sem):
    return pltpu.CompilerParams(dimension_semantics=sem, vmem_limit_bytes=VMEM_LIMIT)


def _bdot(a, b):
    return jnp.dot(a.astype(BF16), b.astype(BF16), preferred_element_type=F32)


def _bdot_nt(a, b):
    return lax.dot_general(a.astype(BF16), b.astype(BF16), (((1,), (1,)), ((), ())),
                           preferred_element_type=F32)


def _bdot_tn(a, b):
    return lax.dot_general(a.astype(BF16), b.astype(BF16), (((0,), (0,)), ((), ())),
                           preferred_element_type=F32)


def _split_dot_left(ones_bf16, x):
    hi = x.astype(BF16)
    lo = (x - hi.astype(F32)).astype(BF16)
    return (jnp.dot(ones_bf16, hi, preferred_element_type=F32)
            + jnp.dot(ones_bf16, lo, preferred_element_type=F32))


def _sigmoid(x):
    return 0.5 * jnp.tanh(0.5 * x) + 0.5


def _rms(x, gain, eps):
    return x * lax.rsqrt(jnp.mean(x * x, axis=-1, keepdims=True) + eps) * gain


def _iota2(shape, dim):
    return lax.broadcasted_iota(jnp.int32, shape, dim)


def _tril_incl(n):
    return (_iota2((n, n), 0) >= _iota2((n, n), 1)).astype(BF16)


def _head_ones(n, width):
    return ((_iota2((n, n), 0) // width) == (_iota2((n, n), 1) // width)).astype(BF16)


def _expand(x, col_group):
    rows = HEAD_GROUP * CHUNK
    xt = jnp.concatenate([x] * HEAD_GROUP, axis=0)
    keep = (_iota2((rows, x.shape[1]), 0) // CHUNK) == (_iota2((rows, x.shape[1]), 1) // col_group)
    return jnp.where(keep, xt, 0.0)


def _inproj_kernel(*refs, fuse_norm):
    if fuse_norm:
        x_ref, g_ref, w_ref, z_ref, h_ref = refs

        @pl.when(pl.program_id(1) == 0)
        def _():
            h_ref[...] = _rms(x_ref[...], g_ref[...], RMS_EPS).astype(BF16)
    else:
        h_ref, w_ref, z_ref = refs
    h = h_ref[...]
    tn = z_ref.shape[1]
    mid = (tn // (2 * MXU_DIM)) * MXU_DIM
    for cols in ((slice(0, mid), slice(mid, tn)) if 0 < mid < tn else (slice(0, tn),)):
        z_ref[:, cols] = jnp.dot(h, w_ref[:, cols], preferred_element_type=F32).astype(BF16)


def _inproj(h, w, l, gain=None):
    t, n = h.shape[0], w.shape[2]
    fuse_norm = gain is not None
    tm = _tile(t, 1024)
    z_shape = jax.ShapeDtypeStruct((t, n), BF16)
    if fuse_norm:
        tn = _tile(n, 1664)
        grid = (t // tm, n // tn)
        row_spec = pl.BlockSpec((tm, D_MODEL), lambda i, j: (i, 0))
        z_spec = pl.BlockSpec((tm, tn), lambda i, j: (i, j))
        ins = (h, gain, w)
        in_specs = [row_spec, pl.BlockSpec((1, D_MODEL), lambda i, j: (0, 0)),
                    pl.BlockSpec((None, D_MODEL, tn), lambda i, j: (l, 0, j))]
        out_shape, out_specs = (z_shape, jax.ShapeDtypeStruct((t, D_MODEL), BF16)), (z_spec, row_spec)
    else:
        tn = _tile(n, 3328)
        grid = (n // tn, t // tm)
        ins = (h, w)
        in_specs = [pl.BlockSpec((tm, D_MODEL), lambda j, i: (i, 0)),
                    pl.BlockSpec((None, D_MODEL, tn), lambda j, i: (l, 0, j), pipeline_mode=pl.Buffered(1))]
        out_shape, out_specs = z_shape, pl.BlockSpec((tm, tn), lambda j, i: (i, j))
    return pl.pallas_call(
        functools.partial(_inproj_kernel, fuse_norm=fuse_norm),
        out_shape=out_shape,
        grid=grid,
        in_specs=in_specs,
        out_specs=out_specs,
        compiler_params=_cparams(("parallel", "arbitrary")),
        name="inproj",
    )(*ins)


def _merge_kernel(h_ref, wg0, wg1, wg2, wg3, o0, o1, o2, o3, p_ref, out_ref):
    tm = h_ref.shape[0]
    mid = (tm // 16) * 8
    for rows in (slice(0, mid), slice(mid, tm)):
        h = h_ref[rows, :]
        acc = None
        for n, (wg, o) in enumerate(((wg0, o0), (wg1, o1), (wg2, o2), (wg3, o3))):
            gate = _sigmoid(jnp.dot(h, wg[...], preferred_element_type=F32))
            term = gate * jnp.dot(o[rows, :], p_ref[n], preferred_element_type=F32)
            acc = term if acc is None else acc + term
        out_ref[rows, :] = acc.astype(BF16)


def _merge(h, w_gate, outs, w_branch, l):
    t = h.shape[0]
    tm, tn = _tile(t, 1024), 512
    nj = D_MODEL // tn
    gate_specs = [pl.BlockSpec((None, D_MODEL, tn), functools.partial(lambda i, j, n: (l, 0, n * nj + j), n=n))
                  for n in range(N_BRANCH)]
    o_specs = [pl.BlockSpec((tm, BRANCH), lambda i, j: (i, 0)) for _ in range(N_BRANCH)]
    return pl.pallas_call(
        _merge_kernel,
        out_shape=jax.ShapeDtypeStruct((t, D_MODEL), BF16),
        grid=(t // tm, nj),
        in_specs=[pl.BlockSpec((tm, D_MODEL), lambda i, j: (i, 0))] + gate_specs + o_specs
                 + [pl.BlockSpec((None, N_BRANCH, BRANCH, tn), lambda i, j: (l, 0, 0, j))],
        out_specs=pl.BlockSpec((tm, tn), lambda i, j: (i, j)),
        compiler_params=_cparams(("parallel", "arbitrary")),
        name="merge",
    )(h, w_gate, w_gate, w_gate, w_gate, *outs, w_branch)


def _proj_norm_res_kernel(a_ref, w_ref, x_ref, g_ref, gn_ref, o_ref, *h_ref, nk):
    k = pl.program_id(1)

    def accumulate(first):
        part = jnp.dot(a_ref[...], w_ref[...], preferred_element_type=F32)
        o_ref[...] = part if first else o_ref[...] + part

    def finish():
        sub = PROJ_SUB_ROWS
        for r in range(o_ref.shape[0] // sub):
            rows = slice(r * sub, (r + 1) * sub)
            y = jnp.dot(a_ref[rows, :], w_ref[...], preferred_element_type=F32)
            if nk > 1:
                y = y + o_ref[rows, :]
            xn = x_ref[rows, :] + _rms(y, g_ref[...], RMS_EPS)
            o_ref[rows, :] = xn
            if h_ref:
                h_ref[0][rows, :] = _rms(xn, gn_ref[...], RMS_EPS).astype(BF16)

    if nk == 1:
        finish()
    else:
        pl.when(k == 0)(functools.partial(accumulate, True))
        if nk > 2:
            pl.when((k > 0) & (k < nk - 1))(functools.partial(accumulate, False))
        pl.when(k == nk - 1)(finish)


def _proj_norm_res(a, w, l, x2, gain, next_gain, tm, tk):
    t, kdim = a.shape
    emit_h = next_gain is not None
    row_spec = pl.BlockSpec((tm, D_MODEL), lambda i, k: (i, 0))
    vec_spec = pl.BlockSpec((1, D_MODEL), lambda i, k: (0, 0))
    out_shape = [jax.ShapeDtypeStruct((t, D_MODEL), F32)]
    if emit_h:
        out_shape.append(jax.ShapeDtypeStruct((t, D_MODEL), BF16))
    assert kdim % tk == 0 and t % tm == 0 and tm % PROJ_SUB_ROWS == 0
    w_mode = dict(pipeline_mode=pl.Buffered(1)) if tk == kdim else {}
    res = pl.pallas_call(
        functools.partial(_proj_norm_res_kernel, nk=kdim // tk),
        out_shape=tuple(out_shape),
        grid=(t // tm, kdim // tk),
        in_specs=[pl.BlockSpec((tm, tk), lambda i, k: (i, k)),
                  pl.BlockSpec((None, tk, D_MODEL), lambda i, k: (l, k, 0), **w_mode),
                  row_spec, vec_spec, vec_spec],
        out_specs=tuple([row_spec] * len(out_shape)),
        compiler_params=_cparams(("parallel", "arbitrary")),
        name="proj_norm_res",
    )(a, w, x2, gain, next_gain if emit_h else gain)
    return (res[0], res[1]) if emit_h else (res[0], None)


def _ffn_up_kernel(h_ref, wg_ref, wu_ref, a_ref, wg_b, wu_b):
    @pl.when(pl.program_id(1) == 0)
    def _():
        wg_b[...] = wg_ref[...].astype(BF16)
        wu_b[...] = wu_ref[...].astype(BF16)

    tm = h_ref.shape[0]
    mid = (tm // 16) * 8
    for rows in (slice(0, mid), slice(mid, tm)):
        h = h_ref[rows, :]
        gt = jnp.dot(h, wg_b[...], preferred_element_type=F32)
        up = jnp.dot(h, wu_b[...], preferred_element_type=F32)
        a_ref[rows, :] = (gt * _sigmoid(gt) * up).astype(BF16)


def _ffn_up(h, wg, wu, l):
    t = h.shape[0]
    tm, tn = _tile(t, 1024), 512
    return pl.pallas_call(
        _ffn_up_kernel,
        out_shape=jax.ShapeDtypeStruct((t, D_FF), BF16),
        grid=(D_FF // tn, t // tm),
        in_specs=[pl.BlockSpec((tm, D_MODEL), lambda j, i: (i, 0)),
                  pl.BlockSpec((None, D_MODEL, tn), lambda j, i: (l, 0, j)),
                  pl.BlockSpec((None, D_MODEL, tn), lambda j, i: (l, 0, j))],
        out_specs=pl.BlockSpec((tm, tn), lambda j, i: (i, j)),
        scratch_shapes=[pltpu.VMEM((D_MODEL, tn), BF16), pltpu.VMEM((D_MODEL, tn), BF16)],
        compiler_params=_cparams(("parallel", "arbitrary")),
        name="ffn_up",
    )(h, wg, wu)


def _conv_kernel(b_ref, c_ref, u_ref, w_ref, o_ref, carry):
    @pl.when(pl.program_id(1) == 0)
    def _():
        carry[...] = jnp.zeros_like(carry)

    cu = c_ref[0].astype(F32) * u_ref[0].astype(F32)
    ts = cu.shape[0]
    row = _iota2(cu.shape, 0)
    p1 = carry[7:8, :]
    p2 = carry[6:7, :]
    s1 = jnp.where(row == 0, p1, pltpu.roll(cu, 1, axis=0))
    s2 = jnp.where(row == 0, p2, jnp.where(row == 1, p1, pltpu.roll(cu, 2, axis=0)))
    w = w_ref[...]
    y = w[2:3, :] * cu + w[1:2, :] * s1 + w[0:1, :] * s2
    o_ref[0] = (b_ref[0].astype(F32) * y).astype(BF16)
    carry[...] = cu[ts - 8:, :]


def _conv_mixer(z3, conv_w):
    b, s = z3.shape[0], z3.shape[1]
    ts = _tile(s, 512)
    c0 = COL_CONV // BRANCH
    return pl.pallas_call(
        _conv_kernel,
        out_shape=jax.ShapeDtypeStruct((b, s, BRANCH), BF16),
        grid=(b, s // ts),
        in_specs=[pl.BlockSpec((1, ts, BRANCH), lambda i, j: (i, j, c0)),
                  pl.BlockSpec((1, ts, BRANCH), lambda i, j: (i, j, c0 + 1)),
                  pl.BlockSpec((1, ts, BRANCH), lambda i, j: (i, j, c0 + 2)),
                  pl.BlockSpec((3, BRANCH), lambda i, j: (0, 0))],
        out_specs=pl.BlockSpec((1, ts, BRANCH), lambda i, j: (i, j, 0)),
        scratch_shapes=[pltpu.VMEM((8, BRANCH), F32)],
        compiler_params=_cparams(("parallel", "arbitrary")),
        name="conv",
    )(z3, z3, z3, conv_w)


def _gla_kernel(q_ref, k_ref, v_ref, g_ref, wl_ref, wa2_ref, ba_ref, gn_ref, o_ref, state):
    @pl.when(pl.program_id(1) == 0)
    def _():
        state[...] = jnp.zeros_like(state)

    nb = q_ref.shape[0]
    causal = _iota2((CHUNK, GLA_KEY), 0) >= (_iota2((CHUNK, GLA_KEY), 1) % CHUNK)
    same_head = ((_iota2((GLA_HEADS * GLA_DV, GLA_KEY), 0) // GLA_DV)
                 == (_iota2((GLA_HEADS * GLA_DV, GLA_KEY), 1) // GLA_DK))
    tril = _tril_incl(CHUNK)
    q_in, k_in, q_st, k_st, v, ve, dec = [], [], [], [], [], [], []
    for b in range(nb):
        q = q_ref[b].astype(F32) * (GLA_DK ** -0.5)
        k = k_ref[b].astype(F32)
        z = _bdot(wl_ref[b], wa2_ref[...]) + ba_ref[...]
        log_a = (jnp.minimum(z, 0.0) - jnp.log(1.0 + jnp.exp(-jnp.abs(z)))) * (1.0 / GLA_LOGIT_NORM)
        cum = _split_dot_left(tril, log_a)
        mid = cum[CHUNK // 2 - 1:CHUNK // 2, :]
        last = cum[CHUNK - 1:CHUNK, :]
        q_in.append(q * jnp.exp(cum - mid))
        k_in.append(_expand(k * jnp.exp(mid - cum), GLA_DK).astype(BF16))
        q_st.append(q * jnp.exp(cum))
        k_st.append(k * jnp.exp(last - cum))
        v.append(v_ref[b])
        ve.append(_expand(v[b].astype(F32), GLA_DV).astype(BF16))
        dec.append(jnp.exp(last))
    scores = [jnp.where(causal, _bdot_nt(q_in[b], k_in[b]), 0.0) for b in range(nb)]
    st = [state[b] for b in range(nb)]
    o_l = [_bdot(scores[b], ve[b]) + _bdot_nt(q_st[b], st[b]) for b in range(nb)]
    for b in range(nb):
        state[b] = st[b] * dec[b] + jnp.where(same_head, _bdot_tn(v[b], k_st[b]), 0.0)
    gn = gn_ref[...]
    for b in range(nb):
        o = o_l[b]
        g = g_ref[b].astype(F32)
        gate = g * _sigmoid(g)
        outs = [_rms(o[:, h * GLA_DV:(h + 1) * GLA_DV], gn, HEAD_EPS) for h in range(GLA_HEADS)]
        o_ref[b] = (jnp.concatenate(outs, axis=1) * gate).astype(BF16)


def _gla_mixer(z3, wa2p, ba, gn):
    b, s = z3.shape[0], z3.shape[1]
    cq = COL_GLA // GLA_KEY
    cv = (COL_GLA + 2 * GLA_KEY) // BRANCH
    cw = COL_RW_A // LANE
    nb = SEQS_PER_STEP if b % SEQS_PER_STEP == 0 else 1
    return pl.pallas_call(
        _gla_kernel,
        out_shape=jax.ShapeDtypeStruct((b, s, BRANCH), BF16),
        grid=(b // nb, s // CHUNK),
        in_specs=[pl.BlockSpec((nb, CHUNK, GLA_KEY), lambda i, j: (i, j, cq)),
                  pl.BlockSpec((nb, CHUNK, GLA_KEY), lambda i, j: (i, j, cq + 1)),
                  pl.BlockSpec((nb, CHUNK, BRANCH), lambda i, j: (i, j, cv)),
                  pl.BlockSpec((nb, CHUNK, BRANCH), lambda i, j: (i, j, cv + 1)),
                  pl.BlockSpec((nb, CHUNK, LANE), lambda i, j: (i, j, cw)),
                  pl.BlockSpec((LANE, GLA_KEY), lambda i, j: (0, 0)),
                  pl.BlockSpec((1, GLA_KEY), lambda i, j: (0, 0)),
                  pl.BlockSpec((1, GLA_DV), lambda i, j: (0, 0))],
        out_specs=pl.BlockSpec((nb, CHUNK, BRANCH), lambda i, j: (i, j, 0)),
        scratch_shapes=[pltpu.VMEM((nb, GLA_HEADS * GLA_DV, GLA_KEY), F32)],
        compiler_params=_cparams(("parallel", "arbitrary")),
        name="gla",
    )(z3, z3, z3, z3, z3, wa2p, ba, gn)


DIFF_KEY_TILE = 512
DIFF_QUERY_TILE = 512
VT_ROWS = 2 * DIFF_D + 16


def _alibi_slope(h):
    return 2.0 ** (-8.0 * (h + 1) / DIFF_HEADS)


def _diff_kernel(q_ref, k_ref, v_ref, lq1, lk1, lq2, lk2, gn_ref, o_ref, ka_scr, vb_scr, qa_scr, acc_scr,
                 m_scr, s_scr, rel_scr, *, lambda_init, tq, tk):
    qi = pl.program_id(1)
    hw = 2 * DIFF_D
    seq = k_ref.shape[1]
    lane = _iota2((tq, hw), 1)

    @pl.when(qi == 0)
    def _():
        key_row = _iota2((tq, hw), 0).astype(F32)
        for h in range(DIFF_HEADS):
            k_aug = []
            for part in range(tk // tq):
                key_bias = (_alibi_slope(h) * LOG2E) * (key_row + float(part * tq))
                hi = key_bias.astype(BF16).astype(F32)
                mid = (key_bias - hi).astype(BF16).astype(F32)
                lo = key_bias - hi - mid
                aug = jnp.where(lane == 0, hi, jnp.where(lane == 1, mid, jnp.where(lane == 2, lo, 0.0)))
                k_aug.append(aug.astype(BF16))
            for c in range(seq // tq):
                rows = slice(c * tq, (c + 1) * tq)
                ka_scr[h, rows, 0:hw] = k_ref[0, rows, h * hw:(h + 1) * hw].astype(BF16)
                ka_scr[h, rows, hw:2 * hw] = k_aug[c % (tk // tq)]
        k_row = _iota2((tk, 2 * tq), 0)
        col = _iota2((tk, 2 * tq), 1)
        for par in range(tk // tq):
            q_row = jnp.where(col >= tq, col - tq, col) + par * tq
            d = q_row - k_row
            rel_scr[par] = jnp.where((k_row // CHUNK) <= (q_row // CHUNK), (d - jnp.abs(d)).astype(F32), NEG_INF)
        sub = _iota2((VT_ROWS - hw, tq), 0)
        tail = jnp.where(sub == 0, 1.0, 0.0).astype(BF16)
        for c in range(seq // tq):
            rows = slice(c * tq, (c + 1) * tq)
            for h in range(DIFF_HEADS):
                vb_scr[h, 0:hw, rows] = v_ref[0, rows, h * hw:(h + 1) * hw].astype(F32).T.astype(BF16)
                vb_scr[h, hw:VT_ROWS, rows] = tail

    lane2 = _iota2((2 * tq, hw), 1)
    ones_aug = (lane2 < 3).astype(F32)
    for h in range(DIFF_HEADS):
        q = q_ref[0, :, h * hw:(h + 1) * hw].astype(F32) * (DIFF_D ** -0.5 * LOG2E)
        q2 = jnp.concatenate([jnp.where(lane < DIFF_D, q, 0.0), jnp.where(lane < DIFF_D, 0.0, q)],
                             axis=0)
        qa_scr[h] = jnp.concatenate([q2, ones_aug], axis=1).astype(BF16)
    acc_scr[...] = jnp.zeros_like(acc_scr)
    m_scr[...] = jnp.full_like(m_scr, NEG_INF)

    last_tile = qi // (tk // tq)

    def tile_scores(j):
        start = pl.multiple_of(j * tk, tk)
        return [lax.dot_general(ka_scr[h, pl.ds(start, tk), :], qa_scr[h], (((1,), (1,)), ((), ())),
                                preferred_element_type=F32) for h in range(DIFF_HEADS)]

    def update(scores, j):
        start = pl.multiple_of(j * tk, tk)
        alphas, probs = [], []
        for h, s in enumerate(scores):
            m_old = m_scr[h]
            m_new = jnp.maximum(m_old, jnp.max(s, axis=0, keepdims=True))
            alphas.append(jnp.exp2(m_old - m_new))
            probs.append(jnp.exp2((s - m_new).astype(BF16)))
            m_scr[h] = m_new - _alibi_slope(h) * LOG2E * tk
        pvs = [jnp.dot(vb_scr[h, :, pl.ds(start, tk)], probs[h], preferred_element_type=F32)
               for h in range(DIFF_HEADS)]
        for h in range(DIFF_HEADS):
            acc_scr[h] = alphas[h] * acc_scr[h] + pvs[h]

    for h, s in enumerate(tile_scores(0)):
        s_scr[h] = s

    def body(j, carry):
        nxt = tile_scores(j + 1)
        update([s_scr[h] for h in range(DIFF_HEADS)], j)
        for h in range(DIFF_HEADS):
            s_scr[h] = nxt[h]
        return carry

    lax.fori_loop(0, last_tile, body, 0)

    rel = rel_scr[qi % (tk // tq)]
    lam = (jnp.exp(jnp.sum(lq1[...] * lk1[...], axis=-1, keepdims=True))
           - jnp.exp(jnp.sum(lq2[...] * lk2[...], axis=-1, keepdims=True))
           + lambda_init)
    update([s_scr[h] + (_alibi_slope(h) * LOG2E) * rel for h in range(DIFF_HEADS)], last_tile)
    for h in range(DIFF_HEADS):
        on = acc_scr[h, 0:hw, :] / acc_scr[h, hw:hw + 1, :]
        o = (on[:, :tq] - lam * on[:, tq:]).T
        o_ref[0, :, h * hw:(h + 1) * hw] = (_rms(o, gn_ref[...], HEAD_EPS) * (1.0 - lambda_init)).astype(BF16)


def _diff_mixer(z3, lq1, lk1, lq2, lk2, gn, lambda_init):
    b, s = z3.shape[0], z3.shape[1]
    tk = _tile(s, DIFF_KEY_TILE)
    tq = _tile(tk, DIFF_QUERY_TILE)
    assert s % tk == 0 and tk % tq == 0 and tq % CHUNK == 0
    hw = 2 * DIFF_D
    cq = COL_DIFF // BRANCH
    nh = DIFF_HEADS
    vec = pl.BlockSpec((1, DIFF_D), lambda i, j: (0, 0))
    return pl.pallas_call(
        functools.partial(_diff_kernel, lambda_init=lambda_init, tq=tq, tk=tk),
        out_shape=jax.ShapeDtypeStruct((b, s, BRANCH), BF16),
        grid=(b, s // tq),
        in_specs=[pl.BlockSpec((1, tq, BRANCH), lambda i, j: (i, j, cq)),
                  pl.BlockSpec((1, s, BRANCH), lambda i, j: (i, 0, cq + 1), pipeline_mode=pl.Buffered(1)),
                  pl.BlockSpec((1, s, BRANCH), lambda i, j: (i, 0, cq + 2), pipeline_mode=pl.Buffered(1)),
                  vec, vec, vec, vec,
                  pl.BlockSpec((1, hw), lambda i, j: (0, 0))],
        out_specs=pl.BlockSpec((1, tq, BRANCH), lambda i, j: (i, j, 0)),
        scratch_shapes=[pltpu.VMEM((nh, s, 2 * hw), BF16), pltpu.VMEM((nh, VT_ROWS, s), BF16),
                        pltpu.VMEM((nh, 2 * tq, 2 * hw), BF16), pltpu.VMEM((nh, VT_ROWS, 2 * tq), F32),
                        pltpu.VMEM((nh, 1, 2 * tq), F32), pltpu.VMEM((nh, tk, 2 * tq), F32),
                        pltpu.VMEM((tk // tq, tk, 2 * tq), F32)],
        compiler_params=_cparams(("parallel", "arbitrary")),
        name="diff_attn",
    )(z3, z3, z3, lq1, lk1, lq2, lk2, gn)


def _shift(x, prev_rows):
    row = _iota2(x.shape, 0)
    return jnp.where(row == 0, prev_rows[7:8, :], pltpu.roll(x, 1, axis=0))


def _rwkv_kernel(*refs, has_vres, nb):
    scratch = refs[-5:]

    @pl.when(pl.program_id(1) == 0)
    def _():
        for r in scratch:
            r[...] = jnp.zeros_like(r)

    def chunk(c, carry):
        _rwkv_chunk(pl.multiple_of(c * CHUNK, CHUNK), *refs, has_vres=has_vres, nb=nb)
        return carry

    lax.fori_loop(0, refs[0].shape[1] // CHUNK, chunk, 0)


def _rwkv_chunk(off, *refs, has_vres, nb):
    if has_vres:
        (rkv_ref, gl_ref, wb_ref, ab_ref, vf_ref, mu_rkv, mu_g, mu_w, mu_a, w0, w2, a0, a2, g2, kk_s, ka_s,
         rk_s, lnw, lnb, v0, v1, v2, o_ref, p_rkv, p_g, p_w, p_a, state) = refs
    else:
        (rkv_ref, gl_ref, wb_ref, ab_ref, mu_rkv, mu_g, mu_w, mu_a, w0, w2, a0, a2, g2, kk_s, ka_s,
         rk_s, lnw, lnb, o_ref, vf_out, p_rkv, p_g, p_w, p_a, state) = refs
    chunk_rows = pl.ds(off, CHUNK)

    def stack(parts):
        return parts[0] if nb == 1 else jnp.concatenate(parts, axis=0)

    def mixed(x_ref, p_ref, mu_ref):
        parts = []
        for b in range(nb):
            x = x_ref[b, chunk_rows, :].astype(F32)
            parts.append(x + (_shift(x, p_ref[b]) - x) * mu_ref[...])
            p_ref[b] = x[CHUNK - 8:, :]
        return stack(parts)

    rkv = mixed(rkv_ref, p_rkv, mu_rkv)
    g_lr = mixed(gl_ref, p_g, mu_g)
    w_lr = mixed(wb_ref, p_w, mu_w)
    a_lr = mixed(ab_ref, p_a, mu_a)
    r = rkv[:, 0:BRANCH]
    k = rkv[:, BRANCH:2 * BRANCH]
    v = rkv[:, 2 * BRANCH:3 * BRANCH]
    nrow = nb * CHUNK

    y = w0[...] + _bdot(jnp.tanh(w_lr), w2[...])
    lw = -math.exp(-0.5) * _sigmoid(y)
    a = _sigmoid(a0[...] + _bdot(a_lr, a2[...]))
    gate = _bdot(_sigmoid(g_lr), g2[...])
    if has_vres:
        vf = stack([vf_ref[b, chunk_rows, :] for b in range(nb)])
        v = v + (vf - v) * _sigmoid(v0[...] + _bdot(_bdot(v, v1[...]), v2[...]))
    else:
        for b in range(nb):
            vf_out[b, chunk_rows, :] = v[b * CHUNK:(b + 1) * CHUNK]

    ones = _head_ones(BRANCH, RWKV_HD)
    kk = k * kk_s[...]
    k = k * (1.0 + (a - 1.0) * ka_s[...])
    sums = _bdot(jnp.concatenate([kk * kk, r * k * rk_s[...]], axis=0), ones)
    kk = kk * lax.rsqrt(jnp.maximum(sums[:nrow], 1e-24))
    bonus = sums[nrow:] * v

    ri, ci = _iota2((nrow, nrow), 0), _iota2((nrow, nrow), 1)
    tril = ((ri // CHUNK == ci // CHUNK) & (ri >= ci)).astype(BF16)
    cum = _split_dot_left(tril, lw)
    lasts = [cum[(b + 1) * CHUNK - 1:(b + 1) * CHUNK, :] for b in range(nb)]
    decay_all = [jnp.exp(last) for last in lasts]
    a_t = -kk * jnp.exp(cum - lw)
    r_t = r * jnp.exp(cum)
    inv = jnp.exp(-cum)
    to_end = stack([decay_all[b] * inv[b * CHUNK:(b + 1) * CHUNK] for b in range(nb)])
    kka = kk * a
    b_s = kka * inv
    k_s = k * inv
    b_e = kka * to_end
    k_e = k * to_end

    rows = HEAD_GROUP * CHUNK
    blk = ((_iota2((rows, GROUP_W), 0) // CHUNK) == (_iota2((rows, GROUP_W), 1) // RWKV_HD))
    blk_bf = blk.astype(F32).astype(BF16)
    t_idx = _iota2((CHUNK, GROUP_W), 0)
    s_idx = _iota2((CHUNK, GROUP_W), 1) % CHUNK
    strict = t_idx > s_idx
    incl = t_idx >= s_idx
    eye = (t_idx == s_idx).astype(F32)

    def blocks(x):
        return jnp.concatenate([x.astype(BF16)] * HEAD_GROUP, axis=0) * blk_bf

    ngroup = RWKV_HEADS // HEAD_GROUP
    probs = [(b, gi) for b in range(nb) for gi in range(ngroup)]
    nprob = len(probs)

    def part(t):
        return [t[b * CHUNK:(b + 1) * CHUNK, gi * GROUP_W:(gi + 1) * GROUP_W] for b, gi in probs]

    cat0 = lambda x, y_: jnp.concatenate([x, y_], axis=0)
    a_l, r_l, b_l, k_l, be_l, ke_l, v_l = (part(t) for t in (a_t, r_t, b_s, k_s, b_e, k_e, v))
    pm = [_bdot_nt(cat0(a_l[i], r_l[i]), cat0(blocks(b_l[i]), blocks(k_l[i]))) for i in range(nprob)]
    n_ab = [jnp.where(strict, p[:CHUNK, :GROUP_W], 0.0) for p in pm]
    a_ak = [jnp.where(strict, p[:CHUNK, GROUP_W:], 0.0) for p in pm]
    a_rb = [jnp.where(incl, p[CHUNK:, :GROUP_W], 0.0) for p in pm]
    a_rk = [jnp.where(incl, p[CHUNK:, GROUP_W:], 0.0) for p in pm]
    t_inv = [eye + n for n in n_ab]
    n_pow = n_ab
    n_blk = [blocks(n) for n in n_pow]
    for _ in range(int(math.log2(CHUNK)) - 1):
        n_pow = [_bdot(n_pow[i], n_blk[i]) for i in range(nprob)]
        n_blk = [blocks(n) for n in n_pow]
        t_inv = [t_inv[i] + _bdot(t_inv[i], n_blk[i]) for i in range(nprob)]
    av = [_bdot(cat0(a_ak[i], a_rk[i]), blocks(v_l[i])) for i in range(nprob)]
    y2 = [_bdot(t_inv[i], jnp.concatenate([blocks(a_l[i]), blocks(av[i][:CHUNK])], axis=1))
          for i in range(nprob)]
    st = [state[b, gi] for b, gi in probs]
    x2 = [_bdot_nt(cat0(y2[i][:, :GROUP_W], r_l[i]), st[i]) for i in range(nprob)]
    u_l = [x2[i][:CHUNK] + y2[i][:, GROUP_W:] for i in range(nprob)]
    o_l = [x2[i][CHUNK:] + _bdot(a_rb[i], blocks(u_l[i])) + av[i][CHUNK:] for i in range(nprob)]
    for i, (b, gi) in enumerate(probs):
        outer = _bdot_tn(cat0(u_l[i], v_l[i]), cat0(be_l[i], ke_l[i]))
        state[b, gi] = (st[i] * decay_all[b][:, gi * GROUP_W:(gi + 1) * GROUP_W]
                        + jnp.where(blk, outer, 0.0))
    o = stack([jnp.concatenate([o_l[b * ngroup + gi] for gi in range(ngroup)], axis=1)
               for b in range(nb)])

    mean = _bdot(o, ones) * (1.0 / RWKV_HD)
    cen = o - mean
    var = _bdot(cen * cen, ones) * (1.0 / RWKV_HD)
    o = cen * lax.rsqrt(var + RWKV_GN_EPS) * lnw[...] + lnb[...]
    o = ((o + bonus) * gate).astype(BF16)
    for b in range(nb):
        o_ref[b, chunk_rows, :] = o[b * CHUNK:(b + 1) * CHUNK]


def _rwkv_mixer(z3, v_first, p):
    b, s = z3.shape[0], z3.shape[1]
    has_vres = v_first is not None
    nb = SEQS_PER_STEP if b % SEQS_PER_STEP == 0 else 1
    ts = _tile(s, RWKV_ROWS_PER_STEP)
    assert ts % CHUNK == 0
    row = lambda w, c: pl.BlockSpec((nb, ts, w), lambda i, j: (i, j, c))
    full = lambda a: pl.BlockSpec(a.shape, lambda i, j: (0,) * a.ndim)
    ins = [z3, z3, z3, z3]
    specs = [row(3 * BRANCH, COL_RWKV // (3 * BRANCH)), row(RWKV_GATE_LORA, COL_RW_G // RWKV_GATE_LORA),
             row(LANE, COL_RW_W // LANE), row(LANE, COL_RW_A // LANE)]
    if has_vres:
        ins.append(v_first)
        specs.append(row(BRANCH, 0))
    names = ["mu_rkv", "mu_g", "mu_w", "mu_a", "w0", "w2", "a0", "a2", "g2", "kk", "ka", "rk", "lnw", "lnb"]
    if has_vres:
        names += ["v0", "v1", "v2"]
    for nme in names:
        ins.append(p[nme])
        specs.append(full(p[nme]))
    o_spec = pl.BlockSpec((nb, ts, BRANCH), lambda i, j: (i, j, 0))
    o_shape = jax.ShapeDtypeStruct((b, s, BRANCH), BF16)
    if has_vres:
        out_shape, out_specs = o_shape, o_spec
    else:
        out_shape = (o_shape, jax.ShapeDtypeStruct((b, s, BRANCH), F32))
        out_specs = (o_spec, pl.BlockSpec((nb, ts, BRANCH), lambda i, j: (i, j, 0)))
    res = pl.pallas_call(
        functools.partial(_rwkv_kernel, has_vres=has_vres, nb=nb),
        out_shape=out_shape,
        grid=(b // nb, s // ts),
        in_specs=specs,
        out_specs=out_specs,
        scratch_shapes=[pltpu.VMEM((nb, 8, 3 * BRANCH), F32), pltpu.VMEM((nb, 8, RWKV_GATE_LORA), F32),
                        pltpu.VMEM((nb, 8, LANE), F32), pltpu.VMEM((nb, 8, LANE), F32),
                        pltpu.VMEM((nb, RWKV_HEADS // HEAD_GROUP, GROUP_W, GROUP_W), F32)],
        compiler_params=_cparams(("parallel", "arbitrary")),
        name="rwkv7",
    )(*ins)
    if has_vres:
        return res, v_first
    return res[0], res[1]


def _pad_rows(w, rows, offset=0):
    return jnp.zeros((rows, w.shape[1]), F32).at[offset:offset + w.shape[0]].set(w)


N_IN = 14800
COL_GATES_SRC = 6608
MIX_BLOCKS = ((COL_CONV, ((0, 3072),)), (COL_DIFF, ((3088, 1536),)), (COL_RWKV, ((4624, 1536),)),
              (COL_RW_G, ((6352, 256),)),
              (COL_RW_W, ((6160, RWKV_DECAY_LORA), (None, LANE - RWKV_DECAY_LORA))),
              (COL_RW_A, ((6256, RWKV_AAA_LORA), (3072, GLA_LOW_RANK), (None, LANE - GLA_WLR_LANE - GLA_LOW_RANK))))
PACK_CHUNK = 512


def _pack_w_in_kernel(wt_ref, mix_ref, gate_ref):
    tc = wt_ref.shape[1]

    def rows(src, n):
        return jnp.zeros((n, tc), F32) if src is None else wt_ref[src:src + n, :]

    def put(out_ref, dst, block):
        out_ref[:, dst:dst + block.shape[0]] = block.T.astype(BF16)

    for dst, parts in MIX_BLOCKS:
        if len(parts) == 1:
            src, n = parts[0]
            for c in range(0, n, PACK_CHUNK):
                m = min(PACK_CHUNK, n - c)
                put(mix_ref, dst + c, rows(src + c, m))
        else:
            put(mix_ref, dst, jnp.concatenate([rows(src, n) for src, n in parts], axis=0))
    for c in range(0, N_IN - COL_GATES_SRC, PACK_CHUNK):
        put(gate_ref, c, rows(COL_GATES_SRC + c, PACK_CHUNK))


def _pack_w_in(w_in):
    depth = w_in.shape[0]
    tc = 256
    n_gate = N_IN - COL_GATES_SRC
    assert n_gate % PACK_CHUNK == 0
    return pl.pallas_call(
        _pack_w_in_kernel,
        out_shape=(jax.ShapeDtypeStruct((depth, D_MODEL, N_MIX), BF16),
                   jax.ShapeDtypeStruct((depth, D_MODEL, n_gate), BF16)),
        grid=(depth, D_MODEL // tc),
        in_specs=[pl.BlockSpec((None, N_IN, tc), lambda l, i: (l, 0, i))],
        out_specs=(pl.BlockSpec((None, tc, N_MIX), lambda l, i: (l, i, 0)),
                   pl.BlockSpec((None, tc, n_gate), lambda l, i: (l, i, 0))),
        compiler_params=_cparams(("parallel", "parallel")),
        name="pack_w_in",
    )(jnp.swapaxes(w_in, 1, 2))


def _pad_lanes(v, width, offset=0):
    return jnp.zeros((1, width), F32).at[0, offset:offset + v.shape[0]].set(v)


def kernel(x, norm_mix_pre, w_in, conv_w, gla_wa2, gla_ba, gla_norm, diff_lq1, diff_lk1, diff_lq2, diff_lk2,
           diff_norm, rw_mu, rw_w0, rw_w2, rw_a0, rw_a2, rw_g2, rw_kk, rw_ka, rw_rk, rw_lnw, rw_lnb, rw_v0,
           rw_v1, rw_v2, w_branch, w_out, norm_mix_post, norm_ffn_pre, w_gate, w_up, w_down, norm_ffn_post):
    bsz, seq = x.shape[0], x.shape[1]
    depth = w_in.shape[0]
    w_branch_b, w_out_b, w_down_b = (w.astype(BF16) for w in (w_branch, w_out, w_down))
    w_mix, w_gates = _pack_w_in(w_in)
    t = bsz * seq
    x2 = x.reshape(t, D_MODEL)
    row = lambda v: v.reshape(1, -1)
    v_first = None
    h = None
    for l in range(depth):
        if l == 0:
            z2, h = _inproj(x2, w_mix, l, gain=row(norm_mix_pre[l]))
        else:
            z2 = _inproj(h, w_mix, l)
        z3 = z2.reshape(bsz, seq, N_MIX)
        lambda_init = 0.8 - 0.6 * math.exp(-0.3 * l)

        o_conv = _conv_mixer(z3, conv_w[l])
        o_gla = _gla_mixer(z3, _pad_rows(gla_wa2[l], LANE, GLA_WLR_LANE), row(gla_ba[l]), row(gla_norm[l]))
        o_diff = _diff_mixer(z3, row(diff_lq1[l]), row(diff_lk1[l]), row(diff_lq2[l]),
                             row(diff_lk2[l]), row(diff_norm[l]), lambda_init)
        mu = rw_mu[l]
        rp = {
            "mu_rkv": row(mu[0:1536]),
            "mu_w": _pad_lanes(mu[1536:1632], LANE),
            "mu_a": _pad_lanes(mu[1632:1728], LANE),
            "mu_g": row(mu[1728:1984]),
            "w0": row(rw_w0[l]), "w2": _pad_rows(rw_w2[l], LANE),
            "a0": row(rw_a0[l]), "a2": _pad_rows(rw_a2[l], LANE),
            "g2": rw_g2[l], "kk": row(rw_kk[l]), "ka": row(rw_ka[l]), "rk": row(rw_rk[l]),
            "lnw": row(rw_lnw[l]), "lnb": row(rw_lnb[l]),
        }
        if l > 0:
            rp.update(v0=row(rw_v0[l - 1]), v1=rw_v1[l - 1], v2=rw_v2[l - 1])
        o_rwkv, v_first = _rwkv_mixer(z3, v_first if l > 0 else None, rp)

        outs = [o.reshape(t, BRANCH) for o in (o_conv, o_gla, o_diff, o_rwkv)]
        merged = _merge(h, w_gates, outs, w_branch_b, l)
        x2, h_ffn = _proj_norm_res(merged, w_out_b, l, x2, row(norm_mix_post[l]),
                                   row(norm_ffn_pre[l]), _tile(t, OUT_PROJ_TM), D_MODEL)
        act = _ffn_up(h_ffn, w_gate, w_up, l)
        next_gain = row(norm_mix_pre[l + 1]) if l + 1 < depth else None
        x2, h = _proj_norm_res(act, w_down_b, l, x2, row(norm_ffn_post[l]), next_gain,
                               _tile(t, FFN_DOWN_TM), D_FF)
    return x2.reshape(bsz, seq, D_MODEL)
```
